```python
import jax, jax.numpy as jnp
from jax import lax
import numpy as np

D_MODEL = 1024
BATCH = 4
SEQ = 4096
DEPTH = 2

GRID_W = 64
CTX_LEN = 256
HEAD_DIM = 64
AXIS_DIM = HEAD_DIM // 2
ROPE_THETA = 10000.0
EPS = 1e-6
Q_BLOCK = 128
N_MOD = 6

GLA_HEADS = 4
GLA_DK = 64
GLA_DV = 128
GLA_GATE_RANK = 16
GLA_GATE_NORM = 16.0
GLA_CHUNK = 64
ATT_HEADS = 8
ATT_KV_HEADS = 2
SWA_HEADS = 16
SWA_KV_HEADS = 2
SWA_WINDOW = 128
MOE_GROUPS = 4
MOE_EXPERTS_PER_GROUP = 4
MOE_TOPK = 2
D_EXPERT = 256

EVEN_WIDTHS = (GLA_HEADS * GLA_DK, GLA_HEADS * GLA_DK, GLA_HEADS * GLA_DV, GLA_HEADS * GLA_DV,
               2 * GLA_GATE_RANK, ATT_HEADS * HEAD_DIM, ATT_KV_HEADS * HEAD_DIM, ATT_KV_HEADS * HEAD_DIM)
EVEN_IN = sum(EVEN_WIDTHS)
EVEN_MIX = GLA_HEADS * GLA_DV + ATT_HEADS * HEAD_DIM
ODD_Q = SWA_HEADS * HEAD_DIM
ODD_KV = SWA_KV_HEADS * HEAD_DIM
ODD_IN = ODD_Q + 2 * ODD_KV
ODD_MIX = ODD_Q
N_EVEN = (DEPTH + 1) // 2
N_ODD = DEPTH // 2

kernel_name = 'hybrid_gla_gqa_swa_hmoe_diffusion_block'


def _split(z, widths):
    return jnp.split(z, np.cumsum(widths)[:-1].tolist(), axis=-1)


def rms_norm(x, gain):
    xf = x.astype(jnp.float32)
    y = xf * lax.rsqrt(jnp.mean(xf * xf, axis=-1, keepdims=True) + EPS)
    return (y * gain.astype(jnp.float32)).astype(x.dtype)


def modulate(x, gain, shift, scale):
    return rms_norm(x, gain) * (1 + scale) + shift


def axial_rope_tables(n):
    rows = n // GRID_W
    row = jnp.repeat(jnp.arange(rows, dtype=jnp.int32), GRID_W)
    col = jnp.tile(jnp.arange(GRID_W, dtype=jnp.int32), rows)
    inv_freq = ROPE_THETA ** (-jnp.arange(0, AXIS_DIM, 2, dtype=jnp.float32) / AXIS_DIM)
    ang = jnp.stack([row[:, None] * inv_freq, col[:, None] * inv_freq], axis=1)
    return jnp.cos(ang), jnp.sin(ang)


def apply_axial_rope(x, cos, sin):
    B, n, H, _ = x.shape
    xs = x.reshape(B, n, H, 2, 2, AXIS_DIM // 2)
    x1, x2 = xs[..., 0, :], xs[..., 1, :]
    c = cos[None, :, None].astype(x.dtype)
    s = sin[None, :, None].astype(x.dtype)
    out = jnp.stack([x1 * c - x2 * s, x2 * c + x1 * s], axis=-2)
    return out.reshape(B, n, H, HEAD_DIM)


def heads(z, n_heads, gain):
    B, L, _ = z.shape
    return rms_norm(z.reshape(B, L, n_heads, HEAD_DIM), gain)


def group_q(q, n_kv):
    B, L, H, Dh = q.shape
    return q.reshape(B, L, n_kv, H // n_kv, Dh)


def softmax_attend(q, k, v, mask=None, sink=None):
    B, Q, KV, G, Dh = q.shape
    s = jnp.einsum('bqhgd,bkhd->bhgqk', q, k).astype(jnp.float32) * (Dh ** -0.5)
    if mask is not None:
        s = jnp.where(mask, s, -jnp.inf)
    if sink is not None:
        sink_col = jnp.broadcast_to(sink.reshape(1, KV, G, 1, 1).astype(jnp.float32), s.shape[:-1] + (1,))
        p = jax.nn.softmax(jnp.concatenate([s, sink_col], axis=-1), axis=-1)[..., :-1]
    else:
        p = jax.nn.softmax(s, axis=-1)
    o = jnp.einsum('bhgqk,bkhd->bqhgd', p.astype(v.dtype), v)
    return o.reshape(B, Q, KV * G * Dh)


def blocked_dense_attention(q, k, v):
    B, S = q.shape[:2]
    nb = S // Q_BLOCK
    qb = jnp.moveaxis(q.reshape(B, nb, Q_BLOCK, *q.shape[2:]), 1, 0)
    ob = lax.map(lambda qi: softmax_attend(qi, k, v), qb)
    return jnp.moveaxis(ob, 0, 1).reshape(B, S, -1)


def banded_window_attention(q, k, v, k_ctx, v_ctx, sink):
    B, S = q.shape[:2]
    nb = S // Q_BLOCK
    span = Q_BLOCK + 2 * SWA_WINDOW
    pad = ((0, 0), (SWA_WINDOW, SWA_WINDOW), (0, 0), (0, 0))
    kp = jnp.pad(k, pad)
    vp = jnp.pad(v, pad)
    qb = jnp.moveaxis(q.reshape(B, nb, Q_BLOCK, *q.shape[2:]), 1, 0)
    ctx_ok = jnp.ones((Q_BLOCK, k_ctx.shape[1]), dtype=bool)

    def block(args):
        qi, bi = args
        start = bi * Q_BLOCK
        kb = lax.dynamic_slice_in_dim(kp, start, span, axis=1)
        vb = lax.dynamic_slice_in_dim(vp, start, span, axis=1)
        qpos = start + jnp.arange(Q_BLOCK, dtype=jnp.int32)
        kpos = start - SWA_WINDOW + jnp.arange(span, dtype=jnp.int32)
        band = (kpos[None, :] >= 0) & (kpos[None, :] < S) & (jnp.abs(kpos[None, :] - qpos[:, None]) <= SWA_WINDOW)
        mask = jnp.concatenate([band, ctx_ok], axis=1)
        return softmax_attend(qi, jnp.concatenate([kb, k_ctx], axis=1), jnp.concatenate([vb, v_ctx], axis=1), mask, sink)

    ob = lax.map(block, (qb, jnp.arange(nb, dtype=jnp.int32)))
    return jnp.moveaxis(ob, 0, 1).reshape(B, S, -1)


def gla_chunk_scan(q, k, v, g, s0):
    B, H, L, DK = q.shape
    DV = v.shape[-1]
    n = L // GLA_CHUNK

    def chunks(t):
        return jnp.moveaxis(t.reshape(B, H, n, GLA_CHUNK, t.shape[-1]), 2, 0)

    lower_tri = jnp.tril(jnp.ones((GLA_CHUNK, GLA_CHUNK), dtype=bool))[:, :, None]

    def step(s, blk):
        qc, kc, vc, gc = blk
        b = jnp.cumsum(gc, axis=2)
        o = jnp.einsum('bhtk,bhkv->bhtv', qc * jnp.exp(b), s)
        diff = b[:, :, :, None, :] - b[:, :, None, :, :]
        decay = jnp.exp(jnp.where(lower_tri, diff, -jnp.inf))
        att = jnp.einsum('bhtk,bhsk,bhtsk->bhts', qc, kc, decay)
        o = o + jnp.einsum('bhts,bhsv->bhtv', att, vc)
        b_end = b[:, :, -1:, :]
        s = jnp.exp(b_end[:, :, 0, :, None]) * s + jnp.einsum('bhsk,bhsv->bhkv', kc * jnp.exp(b_end - b), vc)
        return s, o

    s_fin, o = lax.scan(step, s0, (chunks(q), chunks(k), chunks(v), chunks(g)))
    return jnp.moveaxis(o, 0, 2).reshape(B, H, L, DV), s_fin


def gla_inputs(q, k, v, lr, gate_w, gate_b):
    B, L, _ = q.shape

    def hd(t, d):
        return t.reshape(B, L, GLA_HEADS, d).transpose(0, 2, 1, 3).astype(jnp.float32)

    z = jnp.einsum('blzr,zrk->zblk', lr.reshape(B, L, 2, GLA_GATE_RANK), gate_w) + gate_b[:, None, None, :]
    g = jax.nn.log_sigmoid(z.astype(jnp.float32)) / GLA_GATE_NORM
    g = g.reshape(2, B, L, GLA_HEADS, GLA_DK).transpose(0, 1, 3, 2, 4)
    return hd(q, GLA_DK) * (GLA_DK ** -0.5), hd(k, GLA_DK), hd(v, GLA_DV), g


def gla_bidirectional(lat, ctx):
    ql, kl, vl, gl = lat
    qc, kc, vc, gc = ctx
    B = ql.shape[0]
    s0 = jnp.zeros((B, GLA_HEADS, GLA_DK, GLA_DV), jnp.float32)

    def rev(t):
        return jnp.flip(t, axis=2)

    oc_f, sc_f = gla_chunk_scan(qc, kc, vc, gc[0], s0)
    oc_b, sc_b = gla_chunk_scan(rev(qc), rev(kc), rev(vc), rev(gc[1]), s0)
    ol_f, _ = gla_chunk_scan(ql, kl, vl, gl[0], sc_f)
    ol_b, _ = gla_chunk_scan(rev(ql), rev(kl), rev(vl), rev(gl[1]), sc_b)
    return ol_f + rev(ol_b), oc_f + rev(oc_b)


def gla_output(o, r, gain):
    B, H, L, _ = o.shape
    o = rms_norm(o.transpose(0, 2, 1, 3), gain).astype(r.dtype)
    return (o * jax.nn.silu(r.reshape(B, L, H, GLA_DV))).reshape(B, L, H * GLA_DV)


def even_mixer(h_lat, h_ctx, w_in, w_out, gate_w, gate_b, gla_norm, q_norm, k_norm, cos, sin):
    B, S, _ = h_lat.shape
    Lc = h_ctx.shape[1]
    z_l = _split(h_lat @ w_in, EVEN_WIDTHS)
    z_c = _split(h_ctx @ w_in, EVEN_WIDTHS)
    o_gla_l, o_gla_c = gla_bidirectional(gla_inputs(z_l[0], z_l[1], z_l[2], z_l[4], gate_w, gate_b),
                                         gla_inputs(z_c[0], z_c[1], z_c[2], z_c[4], gate_w, gate_b))
    a_l = gla_output(o_gla_l, z_l[3], gla_norm)
    a_c = gla_output(o_gla_c, z_c[3], gla_norm)
    q_l = group_q(apply_axial_rope(heads(z_l[5], ATT_HEADS, q_norm), cos, sin), ATT_KV_HEADS)
    k_l = apply_axial_rope(heads(z_l[6], ATT_KV_HEADS, k_norm), cos, sin)
    v_l = z_l[7].reshape(B, S, ATT_KV_HEADS, HEAD_DIM)
    q_c = group_q(heads(z_c[5], ATT_HEADS, q_norm), ATT_KV_HEADS)
    k_c = heads(z_c[6], ATT_KV_HEADS, k_norm)
    v_c = z_c[7].reshape(B, Lc, ATT_KV_HEADS, HEAD_DIM)
    b_l = blocked_dense_attention(q_l, jnp.concatenate([k_l, k_c], axis=1), jnp.concatenate([v_l, v_c], axis=1))
    b_c = softmax_attend(q_c, k_c, v_c)
    return jnp.concatenate([a_l, b_l], axis=-1) @ w_out, jnp.concatenate([a_c, b_c], axis=-1) @ w_out


def odd_mixer(h_lat, h_ctx, w_in, w_out, sink, q_norm, k_norm, cos, sin, need_ctx):
    B, S, _ = h_lat.shape
    Lc = h_ctx.shape[1]
    q_l, k_l, v_l = _split(h_lat @ w_in, (ODD_Q, ODD_KV, ODD_KV))
    k_c, v_c = _split(h_ctx @ w_in[:, ODD_Q:], (ODD_KV, ODD_KV))
    q = group_q(apply_axial_rope(heads(q_l, SWA_HEADS, q_norm), cos, sin), SWA_KV_HEADS)
    k = apply_axial_rope(heads(k_l, SWA_KV_HEADS, k_norm), cos, sin)
    v = v_l.reshape(B, S, SWA_KV_HEADS, HEAD_DIM)
    kc = heads(k_c, SWA_KV_HEADS, k_norm)
    vc = v_c.reshape(B, Lc, SWA_KV_HEADS, HEAD_DIM)
    m_l = banded_window_attention(q, k, v, kc, vc, sink) @ w_out
    if not need_ctx:
        return m_l, None
    qc = group_q(heads(h_ctx @ w_in[:, :ODD_Q], SWA_HEADS, q_norm), SWA_KV_HEADS)
    return m_l, softmax_attend(qc, kc, vc, None, sink) @ w_out


def hier_moe(h, wg, bg, we, be, w_gate, w_up, w_down):
    N = h.shape[0]
    g_logits = (h @ wg + bg).astype(jnp.float32)
    g_prob = jax.nn.softmax(g_logits, axis=-1)
    g_idx = jnp.argmax(g_logits, axis=-1)
    g_weight = jnp.take_along_axis(g_prob, g_idx[:, None], axis=-1)
    e_logits = (h @ we + be).astype(jnp.float32).reshape(N, MOE_GROUPS, MOE_EXPERTS_PER_GROUP)
    e_sel = jnp.take_along_axis(e_logits, g_idx[:, None, None], axis=1)[:, 0]
    top_p, top_i = lax.top_k(jax.nn.softmax(e_sel, axis=-1), MOE_TOPK)
    top_p = top_p / jnp.sum(top_p, axis=-1, keepdims=True)
    e_weight = jnp.sum(jax.nn.one_hot(top_i, MOE_EXPERTS_PER_GROUP, dtype=jnp.float32) * top_p[..., None], axis=1)
    combine = (jax.nn.one_hot(g_idx, MOE_GROUPS, dtype=jnp.float32)[:, :, None]
               * (g_weight * e_weight)[:, None, :]).astype(h.dtype)
    out = jnp.zeros_like(h)
    for g in range(MOE_GROUPS):
        a = jnp.einsum('nd,edf->nef', h, w_gate[g])
        u = jnp.einsum('nd,edf->nef', h, w_up[g])
        hid = jax.nn.silu(a) * u * combine[:, g, :, None]
        out = out + jnp.einsum('nef,efd->nd', hid, w_down[g])
    return out


def setup_inputs(seed: int = 0) -> dict:
    key = jax.random.key(seed)
    ks = iter(jax.random.split(key, 32))
    D = D_MODEL
    G, E = MOE_GROUPS, MOE_EXPERTS_PER_GROUP

    def nrm(shape, scale):
        return scale * jax.random.normal(next(ks), shape, jnp.float32)

    def gain(shape):
        return 1.0 + 0.02 * jax.random.normal(next(ks), shape, jnp.float32)

    return {
        'x': nrm((BATCH, SEQ, D), 1.0),
        'c': nrm((BATCH, D), 1.0),
        'ctx': nrm((BATCH, CTX_LEN, D), 1.0),
        'c_ctx': nrm((D,), 1.0),
        'mod_w': nrm((DEPTH, D, N_MOD * D), 0.5 * D ** -0.5),
        'mod_b': nrm((DEPTH, N_MOD * D), 0.02),
        'norm_mix': gain((DEPTH, D)),
        'norm_ffn': gain((DEPTH, D)),
        'ev_w_in': nrm((N_EVEN, D, EVEN_IN), D ** -0.5),
        'ev_w_out': nrm((N_EVEN, EVEN_MIX, D), EVEN_MIX ** -0.5),
        'gla_gate_w': nrm((N_EVEN, 2, GLA_GATE_RANK, GLA_HEADS * GLA_DK), GLA_GATE_RANK ** -0.5),
        'gla_gate_b': nrm((N_EVEN, 2, GLA_HEADS * GLA_DK), 0.1),
        'gla_out_norm': gain((N_EVEN, GLA_DV)),
        'att_q_norm': gain((N_EVEN, HEAD_DIM)),
        'att_k_norm': gain((N_EVEN, HEAD_DIM)),
        'od_w_in': nrm((N_ODD, D, ODD_IN), D ** -0.5),
        'od_w_out': nrm((N_ODD, ODD_MIX, D), ODD_MIX ** -0.5),
        'swa_sink': nrm((N_ODD, SWA_HEADS), 1.0),
        'swa_q_norm': gain((N_ODD, HEAD_DIM)),
        'swa_k_norm': gain((N_ODD, HEAD_DIM)),
        'router_group_w': nrm((DEPTH, D, G), D ** -0.5),
        'router_group_b': nrm((DEPTH, G), 0.01),
        'router_expert_w': nrm((DEPTH, D, G * E), D ** -0.5),
        'router_expert_b': nrm((DEPTH, G * E), 0.01),
        'exp_w_gate': nrm((DEPTH, G, E, D, D_EXPERT), D ** -0.5),
        'exp_w_up': nrm((DEPTH, G, E, D, D_EXPERT), D ** -0.5),
        'exp_w_down': nrm((DEPTH, G, E, D_EXPERT, D), D_EXPERT ** -0.5),
    }


def reference(x, c, ctx, c_ctx, mod_w, mod_b, norm_mix, norm_ffn, ev_w_in, ev_w_out, gla_gate_w, gla_gate_b,
              gla_out_norm, att_q_norm, att_k_norm, od_w_in, od_w_out, swa_sink, swa_q_norm, swa_k_norm,
              router_group_w, router_group_b, router_expert_w, router_expert_b, exp_w_gate, exp_w_up, exp_w_down):
    B, S, D = x.shape
    cos, sin = axial_rope_tables(S)
    sc = jax.nn.silu(c)
    scc = jax.nn.silu(c_ctx)[None]
    for i in range(DEPTH):
        last = i == DEPTH - 1
        j = i // 2
        mod_l = jnp.split((sc @ mod_w[i] + mod_b[i])[:, None, :], N_MOD, axis=-1)
        mod_c = jnp.split((scc @ mod_w[i] + mod_b[i])[:, None, :], N_MOD, axis=-1)
        h_l = modulate(x, norm_mix[i], mod_l[0], mod_l[1])
        h_c = modulate(ctx, norm_mix[i], mod_c[0], mod_c[1])
        if i % 2 == 0:
            m_l, m_c = even_mixer(h_l, h_c, ev_w_in[j], ev_w_out[j], gla_gate_w[j], gla_gate_b[j],
                                  gla_out_norm[j], att_q_norm[j], att_k_norm[j], cos, sin)
        else:
            m_l, m_c = odd_mixer(h_l, h_c, od_w_in[j], od_w_out[j], swa_sink[j], swa_q_norm[j],
                                 swa_k_norm[j], cos, sin, not last)
        x = x + mod_l[2] * m_l
        moe_p = (router_group_w[i], router_group_b[i], router_expert_w[i], router_expert_b[i],
                 exp_w_gate[i], exp_w_up[i], exp_w_down[i])
        if last:
            f_l = hier_moe(modulate(x, norm_ffn[i], mod_l[3], mod_l[4]).reshape(B * S, D), *moe_p)
            x = x + mod_l[5] * f_l.reshape(B, S, D)
        else:
            ctx = ctx + mod_c[2] * m_c
            tok = jnp.concatenate([modulate(x, norm_ffn[i], mod_l[3], mod_l[4]).reshape(B * S, D),
                                   modulate(ctx, norm_ffn[i], mod_c[3], mod_c[4]).reshape(-1, D)], axis=0)
            f = hier_moe(tok, *moe_p)
            x = x + mod_l[5] * f[:B * S].reshape(B, S, D)
            ctx = ctx + mod_c[5] * f[B * S:].reshape(ctx.shape)
    return x
```

```python
import functools

import numpy as np
import jax
import jax.numpy as jnp
from jax import lax
from jax.experimental import pallas as pl
from jax.experimental.pallas import tpu as pltpu

F32 = jnp.float32
BF16 = jnp.bfloat16

GRID_W = 64
HEAD_DIM = 64
AXIS_DIM = HEAD_DIM // 2
ROPE_THETA = 10000.0
EPS = 1e-6
N_MOD = 6
GLA_HEADS = 4
GLA_DK = 64
GLA_DV = 128
GLA_GATE_RANK = 16
GLA_GATE_NORM = 16.0
GLA_CHUNK = 64
ATT_HEADS = 8
ATT_KV_HEADS = 2
SWA_HEADS = 16
SWA_KV_HEADS = 2
SWA_WINDOW = 128
MOE_GROUPS = 4
MOE_EPG = 4
D_EXPERT = 256
N_PAIRS = 6
N_BUCKETS = MOE_GROUPS * N_PAIRS

LANES = 128
MXU_DIM = 256
TOK_TILE = 512
ATT_Q_TILE = 128
ATT_K_CHUNK = 512
ATT_COL = 4 * HEAD_DIM
MOE_TILE = 256
META_LANES = LANES
VMEM_LIMIT = 56 * 1024 * 1024
NEG_BIG = -1e30


def _bf(x):
    return x.astype(BF16)


def _split2(x):
    hi = _bf(x)
    lo = _bf(x - hi.astype(F32))
    return hi, lo


def _dot(a, b):
    return jnp.dot(a, b, preferred_element_type=F32)


def _dot_nt(a, b):
    return lax.dot_general(a, b, (((1,), (1,)), ((), ())), preferred_element_type=F32)


def _silu(x):
    return x / (1.0 + jnp.exp(-x))


def _rms(x):
    return x * lax.rsqrt(jnp.mean(x * x, axis=-1, keepdims=True) + EPS)


def _cparams(sem):
    return pltpu.CompilerParams(dimension_semantics=sem, vmem_limit_bytes=VMEM_LIMIT)


def _full(shape):
    n = len(shape)
    return pl.BlockSpec(shape, lambda *_: (0,) * n)


def _mod_kernel(c_ref, w_ref, b_ref, o_ref):
    c = c_ref[...]
    ch, cl = _split2(_silu(c))
    wh, wl = _split2(w_ref[0])
    o_ref[0] = _dot(ch, wh) + _dot(ch, wl) + _dot(cl, wh) + b_ref[0]


def _modulation(c_rows, mod_w, mod_b):
    depth, d, n = mod_w.shape
    tn = n // 4
    return pl.pallas_call(
        _mod_kernel,
        out_shape=jax.ShapeDtypeStruct((depth, 16, n), F32),
        grid=(depth, n // tn),
        in_specs=[pl.BlockSpec((16, d), lambda i, j: (0, 0)),
                  pl.BlockSpec((1, d, tn), lambda i, j: (i, 0, j)),
                  pl.BlockSpec((1, 1, tn), lambda i, j: (i, 0, j))],
        out_specs=pl.BlockSpec((1, 16, tn), lambda i, j: (i, 0, j)),
        compiler_params=_cparams(("arbitrary", "arbitrary")),
        name="modulation",
    )(c_rows, mod_w, mod_b.reshape(depth, 1, n))


def _mod_row(t, n_lat_tiles, tiles_per_batch, n_batch):
    return jnp.where(t < n_lat_tiles, t // tiles_per_batch, n_batch)


def _modulated(x, gain, mod_ref, row, k_shift, k_scale):
    d = x.shape[-1]
    shift = mod_ref[pl.ds(row, 1), k_shift * d:(k_shift + 1) * d]
    scale = mod_ref[pl.ds(row, 1), k_scale * d:(k_scale + 1) * d]
    return _rms(x) * gain * (1.0 + scale) + shift


def _rope_tables(seq):
    rows = seq // GRID_W
    row = jnp.repeat(jnp.arange(rows, dtype=jnp.int32), GRID_W)
    col = jnp.tile(jnp.arange(GRID_W, dtype=jnp.int32), rows)
    inv_freq = ROPE_THETA ** (-jnp.arange(0, AXIS_DIM, 2, dtype=F32) / AXIS_DIM)
    ang = jnp.stack([row[:, None] * inv_freq, col[:, None] * inv_freq], axis=1)
    cos, sin = jnp.cos(ang), jnp.sin(ang)
    zero = jnp.zeros_like(sin)
    cos64 = jnp.concatenate([cos[:, 0], cos[:, 0], cos[:, 1], cos[:, 1]], axis=-1)
    sa64 = jnp.concatenate([-sin[:, 0], zero[:, 0], -sin[:, 1], zero[:, 1]], axis=-1)
    sb64 = jnp.concatenate([zero[:, 0], sin[:, 0], zero[:, 1], sin[:, 1]], axis=-1)

    def widen(t, fill):
        t = jnp.concatenate([t, t], axis=-1)
        return jnp.concatenate([t, jnp.full((TOK_TILE, LANES), fill, F32)], axis=0)

    return widen(cos64, 1.0), widen(sa64, 0.0), widen(sb64, 0.0)


def _head_sumsq(y, bd):
    w = y.shape[-1]
    outs = []
    for s in range(0, w, MXU_DIM):
        e = min(s + MXU_DIM, w)
        hi, lo = _split2(y[:, s:e])
        b = bd[0:e - s, 0:e - s]
        outs.append(_dot(hi, b) + _dot(lo, b))
    return outs[0] if len(outs) == 1 else jnp.concatenate(outs, axis=-1)


def _qk_norm_rope(z, gain, bd, cos, sa, sb):
    w = z.shape[-1]
    rep = w // LANES
    ss = _head_sumsq(z * z, bd)
    y = z * lax.rsqrt(ss * (1.0 / HEAD_DIM) + EPS) * gain

    def wide(t):
        return t if rep == 1 else jnp.concatenate([t] * rep, axis=-1)

    return (y * wide(cos) + pltpu.roll(y, w - AXIS_DIM // 2, 1) * wide(sa)
            + pltpu.roll(y, AXIS_DIM // 2, 1) * wide(sb))


def _kv_rep(kv128):
    lane = lax.broadcasted_iota(jnp.int32, kv128.shape, 1)
    sw = pltpu.roll(kv128, HEAD_DIM, 1)
    a0 = jnp.where(lane < HEAD_DIM, kv128, sw)
    a1 = jnp.where(lane < HEAD_DIM, sw, kv128)
    return jnp.concatenate([a0, a0, a1, a1], axis=-1)


def _rope_block(t, n_lat_tiles, tiles_per_batch):
    return jnp.where(t < n_lat_tiles, t % tiles_per_batch, tiles_per_batch)


EV_GQ, EV_GK, EV_GV, EV_GR, EV_AQ, EV_AK, EV_AV, EV_LR, EV_END = 0, 256, 512, 1024, 1536, 2048, 2176, 2304, 2432


def _proj_even_kernel(geom, x_ref, mod_ref, gain_ref, w_ref, gw_ref, gb_ref, qg_ref, kg_ref,
                      cos_ref, sa_ref, sb_ref, bd_ref,
                      gq_ref, gk_ref, gv_ref, gr_ref, g_ref, aq_ref, ak_ref, av_ref):
    t = pl.program_id(0)
    row = _mod_row(t, *geom)
    hb = _bf(_modulated(x_ref[...], gain_ref[...], mod_ref, row, 0, 1))

    def seg(a, b):
        return _dot(hb, w_ref[:, a:b])

    gq_ref[...] = seg(EV_GQ, EV_GK) * (GLA_DK ** -0.5)
    gk_ref[...] = seg(EV_GK, EV_GV)
    gv_ref[...] = _bf(seg(EV_GV, EV_GR))
    gr_ref[...] = seg(EV_GR, EV_AQ)
    zg = _dot(_bf(seg(EV_LR, EV_END)), gw_ref[...]) + gb_ref[...]
    g_ref[...] = -(jnp.maximum(-zg, 0.0) + jnp.log1p(jnp.exp(-jnp.abs(zg)))) * (1.0 / GLA_GATE_NORM)
    bd = bd_ref[...]
    cos, sa, sb = cos_ref[...], sa_ref[...], sb_ref[...]
    aq = _qk_norm_rope(seg(EV_AQ, EV_AK), qg_ref[...], bd, cos, sa, sb)
    aq_ref[...] = _bf(aq * (HEAD_DIM ** -0.5))
    ak = _qk_norm_rope(seg(EV_AK, EV_AV), kg_ref[...], bd, cos, sa, sb)
    ak_ref[...] = _bf(_kv_rep(ak))
    av_ref[...] = _bf(_kv_rep(seg(EV_AV, EV_LR)))


def _proj_even(x_all, mod, gain, w, gw, gb, qg, kg, tables, bd, geom):
    n, d = x_all.shape
    n_lat_tiles, tiles_per_batch, _ = geom
    tt = TOK_TILE
    cos, sa, sb = tables
    tok = lambda w_: pl.BlockSpec((tt, w_), lambda t: (t, 0))
    rope = pl.BlockSpec((tt, LANES), lambda t: (_rope_block(t, n_lat_tiles, tiles_per_batch), 0))
    outs = [(256, F32), (256, F32), (512, BF16), (512, F32), (512, F32), (512, BF16), (512, BF16), (512, BF16)]
    return pl.pallas_call(
        functools.partial(_proj_even_kernel, geom),
        out_shape=[jax.ShapeDtypeStruct((n, w_), dt) for w_, dt in outs],
        grid=(n // tt,),
        in_specs=[tok(d), _full(mod.shape), _full(gain.shape), _full(w.shape), _full(gw.shape), _full(gb.shape),
                  _full(qg.shape), _full(kg.shape), rope, rope, rope, _full(bd.shape)],
        out_specs=[tok(w_) for w_, _ in outs],
        compiler_params=_cparams(("arbitrary",)),
        name="proj_even",
    )(x_all, mod, gain, w, gw, gb, qg, kg, cos, sa, sb, bd)


def _proj_odd_kernel(geom, x_ref, f_ref, mod_prev_ref, mod_ref, gain_ref, w_ref, qg_ref, kg_ref,
                     cos_ref, sa_ref, sb_ref, bd_ref, x1_ref, q_ref, k_ref, v_ref):
    t = pl.program_id(0)
    row = _mod_row(t, *geom)
    d = x_ref.shape[-1]
    gate = mod_prev_ref[pl.ds(row, 1), 5 * d:6 * d]
    x1 = x_ref[...] + gate * f_ref[...]
    x1_ref[...] = x1
    hb = _bf(_modulated(x1, gain_ref[...], mod_ref, row, 0, 1))
    bd = bd_ref[...]
    cos, sa, sb = cos_ref[...], sa_ref[...], sb_ref[...]
    q = _qk_norm_rope(_dot(hb, w_ref[:, 0:d]), qg_ref[...], bd, cos, sa, sb)
    q_ref[...] = _bf(q * (HEAD_DIM ** -0.5))
    k = _qk_norm_rope(_dot(hb, w_ref[:, d:d + LANES]), kg_ref[...], bd, cos, sa, sb)
    k_ref[...] = _bf(_kv_rep(k))
    v_ref[...] = _bf(_kv_rep(_dot(hb, w_ref[:, d + LANES:d + 2 * LANES])))


def _proj_odd(x_all, f_all, mod_prev, mod, gain, w, qg, kg, tables, bd, geom):
    n, d = x_all.shape
    n_lat_tiles, tiles_per_batch, _ = geom
    tt = TOK_TILE
    cos, sa, sb = tables
    tok = lambda w_: pl.BlockSpec((tt, w_), lambda t: (t, 0))
    rope = pl.BlockSpec((tt, LANES), lambda t: (_rope_block(t, n_lat_tiles, tiles_per_batch), 0))
    outs = [(d, F32), (d, BF16), (512, BF16), (512, BF16)]
    return pl.pallas_call(
        functools.partial(_proj_odd_kernel, geom),
        out_shape=[jax.ShapeDtypeStruct((n, w_), dt) for w_, dt in outs],
        grid=(n // tt,),
        in_specs=[tok(d), tok(d), _full(mod_prev.shape), _full(mod.shape), _full(gain.shape), _full(w.shape),
                  _full(qg.shape), _full(kg.shape), rope, rope, rope, _full(bd.shape)],
        out_specs=[tok(w_) for w_, _ in outs],
        compiler_params=_cparams(("arbitrary",)),
        name="proj_odd",
    )(x_all, f_all, mod_prev, mod, gain, w, qg, kg, cos, sa, sb, bd)


N_LEVELS = 6
GLA_MX_ROWS = (N_LEVELS + 2) * GLA_CHUNK


def _gla_constants():
    c = GLA_CHUNK
    mx = np.zeros((2, GLA_MX_ROWS, c), np.float32)
    pat = np.zeros((2, N_LEVELS + 1, GLA_HEADS * c, c), np.float32)
    r = np.arange(c)
    for lvl in range(N_LEVELS):
        h = 1 << lvl
        ref = (r // (2 * h)) * 2 * h + h - 1
        upper = (r % (2 * h)) >= h
        m = np.zeros((c, c), np.float32)
        for i in range(c):
            if upper[i]:
                m[i, ref[i] + 1:i + 1] = 1.0
            else:
                m[i, i + 1:ref[i] + 1] = 1.0
        mx[0, lvl * c:(lvl + 1) * c] = m
        same = (r[:, None] // (2 * h)) == (r[None, :] // (2 * h))
        p = same & upper[:, None] & (~upper)[None, :]
        pat[0, lvl] = np.tile(p.astype(np.float32), (GLA_HEADS, 1))
    mx[0, N_LEVELS * c:(N_LEVELS + 1) * c] = (r[None, :] <= r[:, None])
    mx[0, (N_LEVELS + 1) * c:(N_LEVELS + 2) * c] = (r[None, :] > r[:, None])
    pat[0, N_LEVELS] = np.tile(np.eye(c, dtype=np.float32), (GLA_HEADS, 1))
    for k in range(N_LEVELS + 2):
        mx[1, k * c:(k + 1) * c] = mx[0, k * c:(k + 1) * c][::-1, ::-1]
    for k in range(N_LEVELS + 1):
        for hd in range(GLA_HEADS):
            pat[1, k, hd * c:(hd + 1) * c] = pat[0, k, hd * c:(hd + 1) * c][::-1, ::-1]
    return mx, pat


def _gla_chunk(q, k, v, g, mx, pat_ref, s_ref):
    c = GLA_CHUNK
    ghi, glo = _split2(g)
    x_all = _dot(mx, ghi) + _dot(mx, glo)
    lane_head = lax.broadcasted_iota(jnp.int32, (c, GLA_HEADS * GLA_DK), 1) // GLA_DK

    def stack_heads(a):
        return jnp.concatenate([_bf(jnp.where(lane_head == h, a, 0.0)) for h in range(GLA_HEADS)], axis=0)

    att = jnp.where(pat_ref[0, N_LEVELS] > 0.0, _dot_nt(stack_heads(q), _bf(k)), 0.0)
    for lvl in range(N_LEVELS):
        e = jnp.exp(x_all[lvl * c:(lvl + 1) * c])
        p = _dot_nt(stack_heads(q * e), _bf(k * e))
        att = att + jnp.where(pat_ref[0, lvl] > 0.0, p, 0.0)
    bcum = x_all[N_LEVELS * c:(N_LEVELS + 1) * c]
    brem = x_all[(N_LEVELS + 1) * c:(N_LEVELS + 2) * c]
    qe = stack_heads(q * jnp.exp(bcum))
    s_old = s_ref[...]
    o_inter = _dot(qe, _bf(s_old))
    kt = jnp.transpose(k * jnp.exp(brem))
    tot = bcum[0:1] + brem[0:1]
    a_col = jnp.transpose(jnp.exp(jnp.broadcast_to(tot, (8, tot.shape[1]))))[:, 0:1]
    outs, news = [], []
    for h in range(GLA_HEADS):
        vh = v[:, h * GLA_DV:(h + 1) * GLA_DV]
        o_h = o_inter[h * c:(h + 1) * c] + _dot(_bf(att[h * c:(h + 1) * c]), vh)
        outs.append(o_h)
        news.append(_dot(_bf(kt[h * GLA_DK:(h + 1) * GLA_DK]), vh))
    s_ref[...] = a_col * s_old + jnp.concatenate(news, axis=0)
    return jnp.concatenate(outs, axis=-1)


def _gla_kernel(n_lat_chunks, n_ctx_chunks, ql_ref, kl_ref, vl_ref, gl_ref, qc_ref, kc_ref, vc_ref, gc_ref,
                mx_ref, pat_ref, ol_ref, oc_ref, s_ref):
    d = pl.program_id(1)
    c = GLA_CHUNK
    s_ref[...] = jnp.zeros_like(s_ref)
    mx = mx_ref[0]

    def run(n_chunks, q_ref, k_ref, v_ref, g_ref, o_ref):
        def body(i, carry):
            ci = jnp.where(d == 0, i, n_chunks - 1 - i)
            rows = pl.ds(pl.multiple_of(ci * c, c), c)
            o_ref[0, rows, :] = _gla_chunk(q_ref[rows, :], k_ref[rows, :], v_ref[rows, :], g_ref[rows, :],
                                           mx, pat_ref, s_ref)
            return carry
        lax.fori_loop(0, n_chunks, body, 0)

    run(n_ctx_chunks, qc_ref, kc_ref, vc_ref, gc_ref, oc_ref)
    run(n_lat_chunks, ql_ref, kl_ref, vl_ref, gl_ref, ol_ref)


def _gla(gq, gk, gv, g, n_batch, seq, lc):
    n = gq.shape[0]
    mx_np, pat_np = _gla_constants()
    mx = jnp.asarray(mx_np, BF16)
    pat = jnp.asarray(pat_np, F32)
    ctx0 = n_batch * seq // lc
    hk, hv = GLA_HEADS * GLA_DK, GLA_HEADS * GLA_DV
    lat = lambda w_, col: pl.BlockSpec((seq, w_), lambda b, d_: (b, col(d_)))
    ctx = lambda w_, col: pl.BlockSpec((lc, w_), lambda b, d_: (ctx0 + b, col(d_)))
    zero = lambda d_: 0
    same = lambda d_: d_
    o_lat, o_ctx = pl.pallas_call(
        functools.partial(_gla_kernel, seq // GLA_CHUNK, lc // GLA_CHUNK),
        out_shape=[jax.ShapeDtypeStruct((2, n_batch * seq, hv), F32),
                   jax.ShapeDtypeStruct((2, n_batch * lc, hv), F32)],
        grid=(n_batch, 2),
        in_specs=[lat(hk, zero), lat(hk, zero), lat(hv, zero), lat(hk, same),
                  ctx(hk, zero), ctx(hk, zero), ctx(hv, zero), ctx(hk, same),
                  pl.BlockSpec((1,) + mx.shape[1:], lambda b, d_: (d_, 0, 0)),
                  pl.BlockSpec((1,) + pat.shape[1:], lambda b, d_: (d_, 0, 0, 0))],
        out_specs=[pl.BlockSpec((1, seq, hv), lambda b, d_: (d_, b, 0)),
                   pl.BlockSpec((1, lc, hv), lambda b, d_: (d_, b, 0))],
        scratch_shapes=[pltpu.VMEM((GLA_HEADS * GLA_DK, GLA_DV), F32)],
        compiler_params=_cparams(("arbitrary", "arbitrary")),
        name="gla_scan",
    )(gq, gk, gv, g, gq, gk, gv, g, mx, pat)
    return jnp.concatenate([o_lat, o_ctx], axis=1)


def _attn_kernel(mode, seq, *refs):
    if mode == "ctx":
        q_ref, kc_ref, vc_ref, o_ref = refs
    elif mode == "dense":
        q_ref, kl_ref, vl_ref, kc_ref, vc_ref, o_ref = refs
    else:
        q_ref, kl_ref, vl_ref, kc_ref, vc_ref, sink_ref, o_ref = refs
    tq = q_ref.shape[0]
    q = q_ref[...]
    lane_head = lax.broadcasted_iota(jnp.int32, q.shape, 1) // HEAD_DIM
    q4 = jnp.concatenate([jnp.where(lane_head == h, q, jnp.zeros_like(q)) for h in range(4)], axis=0)
    rows = 4 * tq
    if mode == "window":
        m0 = jnp.concatenate([jnp.broadcast_to(sink_ref[0, :, h:h + 1], (tq, 1)) for h in range(4)], axis=0)
        l0 = jnp.ones((rows, 1), F32)
    else:
        m0 = jnp.full((rows, 1), NEG_BIG, F32)
        l0 = jnp.zeros((rows, 1), F32)
    acc0 = jnp.zeros((rows, ATT_COL), F32)

    def step(carry, k, v, mask=None):
        m, l, acc = carry
        s = _dot_nt(q4, k)
        if mask is not None:
            s = jnp.where(mask, s, NEG_BIG)
        m_new = jnp.maximum(m, jnp.max(s, axis=-1, keepdims=True))
        alpha = jnp.exp(m - m_new)
        p = jnp.exp(s - m_new)
        l = alpha * l + jnp.sum(p, axis=-1, keepdims=True)
        acc = alpha * acc + _dot(_bf(p), v)
        return m_new, l, acc

    carry = (m0, l0, acc0)
    if mode == "dense":
        def body(i, cr):
            r = pl.ds(pl.multiple_of(i * ATT_K_CHUNK, ATT_K_CHUNK), ATT_K_CHUNK)
            return step(cr, kl_ref[r, :], vl_ref[r, :])
        carry = lax.fori_loop(0, seq // ATT_K_CHUNK, body, carry)
    elif mode == "window":
        span = tq + 2 * SWA_WINDOW
        q0 = pl.program_id(2) * tq
        start = pl.multiple_of(jnp.clip(q0 - SWA_WINDOW, 0, seq - span), tq)
        kpos = start + lax.broadcasted_iota(jnp.int32, (rows, span), 1)
        qpos = q0 + lax.broadcasted_iota(jnp.int32, (rows, span), 0) % tq
        band = jnp.abs(kpos - qpos) <= SWA_WINDOW
        carry = step(carry, kl_ref[pl.ds(start, span), :], vl_ref[pl.ds(start, span), :], band)
    carry = step(carry, kc_ref[...], vc_ref[...])
    m, l, acc = carry
    out = acc / l
    o = jnp.zeros((tq, ATT_COL), F32)
    lane_head_o = lax.broadcasted_iota(jnp.int32, (tq, ATT_COL), 1) // HEAD_DIM
    for h in range(4):
        o = o + jnp.where(lane_head_o == h, out[h * tq:(h + 1) * tq], 0.0)
    o_ref[...] = _bf(o)


def _attention(mode, q, k_rep, v_rep, n_batch, seq, lc, n_heads, n_kv, sink=None):
    ncol = n_heads * HEAD_DIM // ATT_COL
    col_per_kv = ncol // n_kv
    tq = ATT_Q_TILE
    ctx0 = n_batch * seq // lc
    kv_lat = pl.BlockSpec((seq, ATT_COL), lambda b, j, i: (b, j // col_per_kv))
    kv_ctx = pl.BlockSpec((lc, ATT_COL), lambda b, j, i: (ctx0 + b, j // col_per_kv))
    if mode == "ctx":
        nq = lc // tq
        q0 = n_batch * seq // tq
        args = (q, k_rep, v_rep)
        in_specs = [pl.BlockSpec((tq, ATT_COL), lambda b, j, i: (q0 + b * nq + i, j)), kv_ctx, kv_ctx]
    else:
        nq = seq // tq
        args = (q, k_rep, v_rep, k_rep, v_rep)
        in_specs = [pl.BlockSpec((tq, ATT_COL), lambda b, j, i: (b * nq + i, j)), kv_lat, kv_lat, kv_ctx, kv_ctx]
        if mode == "window":
            args = args + (sink.reshape(ncol, 1, 4),)
            in_specs.append(pl.BlockSpec((1, 1, 4), lambda b, j, i: (j, 0, 0)))
    return pl.pallas_call(
        functools.partial(_attn_kernel, mode, seq),
        out_shape=jax.ShapeDtypeStruct((n_batch * nq * tq, n_heads * HEAD_DIM), BF16),
        grid=(n_batch, ncol, nq),
        in_specs=in_specs,
        out_specs=pl.BlockSpec((tq, ATT_COL), lambda b, j, i: (b * nq + i, j)),
        compiler_params=_cparams(("arbitrary", "arbitrary", "arbitrary")),
        name="attention_" + mode,
    )(*args)


def _route(h, wr_hi_ref, wr_lo_ref, br_ref):
    hh, hl = _split2(h)
    logits = _dot(hh, wr_hi_ref[...]) + _dot(hh, wr_lo_ref[...]) + _dot(hl, wr_hi_ref[...]) + br_ref[...]
    col = lambda i: logits[:, i:i + 1]
    gl = [col(i) for i in range(MOE_GROUPS)]
    gmax = functools.reduce(jnp.maximum, gl)
    gi = jnp.where(gl[0] == gmax, 0, jnp.where(gl[1] == gmax, 1, jnp.where(gl[2] == gmax, 2, 3)))
    g_weight = 1.0 / functools.reduce(lambda a, b: a + b, [jnp.exp(x - gmax) for x in gl])
    el = []
    for j in range(MOE_EPG):
        cand = [col(MOE_GROUPS + g * MOE_EPG + j) for g in range(MOE_GROUPS)]
        el.append(jnp.where(gi == 0, cand[0], jnp.where(gi == 1, cand[1], jnp.where(gi == 2, cand[2], cand[3]))))
    m1 = functools.reduce(jnp.maximum, el)
    i1 = jnp.where(el[0] == m1, 0, jnp.where(el[1] == m1, 1, jnp.where(el[2] == m1, 2, 3)))
    rest = [jnp.where(i1 == j, -jnp.inf, el[j]) for j in range(MOE_EPG)]
    m2 = functools.reduce(jnp.maximum, rest)
    i2 = jnp.where(rest[0] == m2, 0, jnp.where(rest[1] == m2, 1, jnp.where(rest[2] == m2, 2, 3)))
    e2 = jnp.exp(m2 - m1)
    w1 = g_weight / (1.0 + e2)
    w2 = g_weight * e2 / (1.0 + e2)
    lo = jnp.minimum(i1, i2)
    hi = jnp.maximum(i1, i2)
    w_lo = jnp.where(i1 == lo, w1, w2)
    w_hi = jnp.where(i1 == lo, w2, w1)
    pair = jnp.where(lo == 0, hi - 1, jnp.where(lo == 1, hi + 1, N_PAIRS - 1))
    bucket = (gi * N_PAIRS + pair).astype(F32)
    lane = lax.broadcasted_iota(jnp.int32, (h.shape[0], META_LANES), 1)
    return jnp.where(lane == 0, w_lo, jnp.where(lane == 1, w_hi, jnp.where(lane == 2, bucket, 0.0)))


def _out_tail(geom, m, x_ref, mod_ref, gain_ffn_ref, wr_hi_ref, wr_lo_ref, br_ref, x_new_ref, hrow_ref):
    t = pl.program_id(0)
    row = _mod_row(t, *geom)
    d = x_ref.shape[-1]
    x_new = x_ref[...] + mod_ref[pl.ds(row, 1), 2 * d:3 * d] * m
    x_new_ref[...] = x_new
    h = _modulated(x_new, gain_ffn_ref[...], mod_ref, row, 3, 4)
    hrow_ref[:, 0:d] = h
    hrow_ref[:, d:d + META_LANES] = _route(h, wr_hi_ref, wr_lo_ref, br_ref)


def _out_even_kernel(geom, o_ref, r_ref, att_ref, gn_ref, w_ref, x_ref, mod_ref, gain_ffn_ref,
                     wr_hi_ref, wr_lo_ref, br_ref, x_new_ref, hrow_ref):
    o = o_ref[0] + o_ref[1]
    r = r_ref[...]
    parts = []
    for h in range(GLA_HEADS):
        sl = slice(h * GLA_DV, (h + 1) * GLA_DV)
        parts.append(_rms(o[:, sl]) * gn_ref[...] * _silu(r[:, sl]))
    a = _bf(jnp.concatenate(parts, axis=-1))
    half = a.shape[-1]
    m = _dot(a, w_ref[0:half, :]) + _dot(att_ref[...], w_ref[half:, :])
    _out_tail(geom, m, x_ref, mod_ref, gain_ffn_ref, wr_hi_ref, wr_lo_ref, br_ref, x_new_ref, hrow_ref)


def _out_odd_kernel(geom, att_ref, w_ref, x_ref, mod_ref, gain_ffn_ref,
                    wr_hi_ref, wr_lo_ref, br_ref, x_new_ref, hrow_ref):
    m = _dot(att_ref[...], w_ref[...])
    _out_tail(geom, m, x_ref, mod_ref, gain_ffn_ref, wr_hi_ref, wr_lo_ref, br_ref, x_new_ref, hrow_ref)


def _out_proj(kernel, mix_args, mix_specs, w, x_all, mod, gain_ffn, wr_hi, wr_lo, br, n_rows, name):
    d = x_all.shape[-1]
    tt = TOK_TILE
    tok = lambda w_: pl.BlockSpec((tt, w_), lambda t: (t, 0))
    return pl.pallas_call(
        kernel,
        out_shape=[jax.ShapeDtypeStruct((n_rows, d), F32), jax.ShapeDtypeStruct((n_rows, d + META_LANES), F32)],
        grid=(n_rows // tt,),
        in_specs=mix_specs + [_full(w.shape), tok(d), _full(mod.shape), _full(gain_ffn.shape),
                              _full(wr_hi.shape), _full(wr_lo.shape), _full(br.shape)],
        out_specs=[tok(d), tok(d + META_LANES)],
        compiler_params=_cparams(("arbitrary",)),
        name=name,
    )(*mix_args, w, x_all, mod, gain_ffn, wr_hi, wr_lo, br)


def _moe_kernel(tlo_ref, thi_ref, nused_ref, src_ref, dst_ref, hrow_hbm, wg_lo, wu_lo, wd_lo, wg_hi, wu_hi, wd_hi,
                f_hbm, xbuf, obuf, sem_in, sem_out):
    t = pl.program_id(0)
    tm = xbuf.shape[0]
    d = obuf.shape[1]

    def gather(r):
        return pltpu.make_async_copy(hrow_hbm.at[pl.ds(src_ref[0, 0, r], 1)], xbuf.at[pl.ds(r, 1)], sem_in)

    def scatter(r):
        return pltpu.make_async_copy(obuf.at[pl.ds(r, 1)], f_hbm.at[pl.ds(dst_ref[0, 0, r], 1)], sem_out)

    @pl.when(t < nused_ref[0])
    def _():
        def start_in(r, c):
            gather(r).start()
            return c
        lax.fori_loop(0, tm, start_in, 0)

        def wait_in(r, c):
            gather(r).wait()
            return c
        lax.fori_loop(0, tm, wait_in, 0)

        xb = _bf(xbuf[:, 0:d])
        w_lo = xbuf[:, d:d + 1]
        w_hi = xbuf[:, d + 1:d + 2]
        hid_lo = _silu(_dot(xb, wg_lo[0])) * _dot(xb, wu_lo[0]) * w_lo
        hid_hi = _silu(_dot(xb, wg_hi[0])) * _dot(xb, wu_hi[0]) * w_hi
        obuf[...] = _dot(_bf(hid_lo), wd_lo[0]) + _dot(_bf(hid_hi), wd_hi[0])

        def start_out(r, c):
            @pl.when(dst_ref[0, 0, r] >= 0)
            def _():
                scatter(r).start()
            return c
        lax.fori_loop(0, tm, start_out, 0)

        def wait_out(r, c):
            @pl.when(dst_ref[0, 0, r] >= 0)
            def _():
                scatter(r).wait()
            return c
        lax.fori_loop(0, tm, wait_out, 0)


def _moe(hrow, w_gate, w_up, w_down):
    n, dw = hrow.shape
    d = dw - META_LANES
    tm = MOE_TILE
    n_tiles = n // tm + N_BUCKETS
    p = n_tiles * tm
    bucket = hrow[:, d + 2].astype(jnp.int32)
    onehot = (bucket[:, None] == jnp.arange(N_BUCKETS, dtype=jnp.int32)[None, :]).astype(jnp.int32)
    counts = jnp.sum(onehot, axis=0)
    tiles_b = (counts + tm - 1) // tm
    tile_end = jnp.cumsum(tiles_b)
    start_b = (tile_end - tiles_b) * tm
    rank = jnp.sum(jnp.cumsum(onehot, axis=0) * onehot, axis=1) - 1
    pos = start_b[bucket] + rank
    tok = jnp.arange(n, dtype=jnp.int32)
    src = jnp.zeros((p,), jnp.int32).at[pos].set(tok)
    dst = jnp.full((p,), -1, jnp.int32).at[pos].set(tok)
    n_used = tile_end[-1].astype(jnp.int32).reshape(1)
    tile_ids = jnp.arange(n_tiles, dtype=jnp.int32)
    tile_bucket = jnp.minimum(jnp.sum((tile_ids[:, None] >= tile_end[None, :]).astype(jnp.int32), axis=1),
                              N_BUCKETS - 1)
    last_bucket = tile_bucket[jnp.maximum(n_used[0] - 1, 0)]
    tile_bucket = jnp.where(tile_ids < n_used[0], tile_bucket, last_bucket)
    pair_lo = jnp.asarray([0, 0, 0, 1, 1, 2], jnp.int32)
    pair_hi = jnp.asarray([1, 2, 3, 2, 3, 3], jnp.int32)
    grp = tile_bucket // N_PAIRS
    t_lo = grp * MOE_EPG + pair_lo[tile_bucket % N_PAIRS]
    t_hi = grp * MOE_EPG + pair_hi[tile_bucket % N_PAIRS]

    idx_spec = pl.BlockSpec((1, 1, tm), lambda t, *_: (t, 0, 0), memory_space=pltpu.SMEM)
    f = D_EXPERT
    up_lo = pl.BlockSpec((1, d, f), lambda t, lo, hi, nu: (lo[t], 0, 0))
    up_hi = pl.BlockSpec((1, d, f), lambda t, lo, hi, nu: (hi[t], 0, 0))
    dn_lo = pl.BlockSpec((1, f, d), lambda t, lo, hi, nu: (lo[t], 0, 0))
    dn_hi = pl.BlockSpec((1, f, d), lambda t, lo, hi, nu: (hi[t], 0, 0))
    grid_spec = pltpu.PrefetchScalarGridSpec(
        num_scalar_prefetch=3,
        grid=(n_tiles,),
        in_specs=[idx_spec, idx_spec, pl.BlockSpec(memory_space=pl.ANY), up_lo, up_lo, dn_lo, up_hi, up_hi, dn_hi],
        out_specs=pl.BlockSpec(memory_space=pl.ANY),
        scratch_shapes=[pltpu.VMEM((tm, dw), F32), pltpu.VMEM((tm, d), F32),
                        pltpu.SemaphoreType.DMA(()), pltpu.SemaphoreType.DMA(())],
    )
    return pl.pallas_call(
        _moe_kernel,
        out_shape=jax.ShapeDtypeStruct((n, d), F32),
        grid_spec=grid_spec,
        compiler_params=_cparams(("arbitrary",)),
        name="moe_experts",
    )(t_lo, t_hi, n_used, src.reshape(n_tiles, 1, tm), dst.reshape(n_tiles, 1, tm), hrow,
      w_gate, w_up, w_down, w_gate, w_up, w_down)


def _final_kernel(tiles_per_batch, x_ref, f_ref, mod_ref, o_ref):
    row = pl.program_id(0) // tiles_per_batch
    d = x_ref.shape[-1]
    o_ref[...] = x_ref[...] + mod_ref[pl.ds(row, 1), 5 * d:6 * d] * f_ref[...]


def _final(x_lat, f_lat, mod, tiles_per_batch):
    n, d = x_lat.shape
    tok = pl.BlockSpec((TOK_TILE, d), lambda t: (t, 0))
    return pl.pallas_call(
        functools.partial(_final_kernel, tiles_per_batch),
        out_shape=jax.ShapeDtypeStruct((n, d), F32),
        grid=(n // TOK_TILE,),
        in_specs=[tok, tok, _full(mod.shape)],
        out_specs=tok,
        compiler_params=_cparams(("arbitrary",)),
        name="final_residual",
    )(x_lat, f_lat, mod)


def _block_diag_ones():
    r = np.arange(MXU_DIM) // HEAD_DIM
    return jnp.asarray((r[:, None] == r[None, :]).astype(np.float32), BF16)


def _router_weights(wg, bg, we, be):
    d = wg.shape[0]
    n = MOE_GROUPS + MOE_GROUPS * MOE_EPG
    w = jnp.concatenate([wg, we, jnp.zeros((d, LANES - n), F32)], axis=1)
    b = jnp.concatenate([bg, be, jnp.zeros((LANES - n,), F32)]).reshape(1, LANES)
    hi = _bf(w)
    return hi, _bf(w - hi.astype(F32)), b


def _expert_weights(w_gate, w_up, w_down):
    g, e, d, f = w_gate.shape
    return (_bf(w_gate).reshape(g * e, d, f), _bf(w_up).reshape(g * e, d, f), _bf(w_down).reshape(g * e, f, d))


def kernel(x, c, ctx, c_ctx, mod_w, mod_b, norm_mix, norm_ffn, ev_w_in, ev_w_out, gla_gate_w, gla_gate_b,
           gla_out_norm, att_q_norm, att_k_norm, od_w_in, od_w_out, swa_sink, swa_q_norm, swa_k_norm,
           router_group_w, router_group_b, router_expert_w, router_expert_b, exp_w_gate, exp_w_up, exp_w_down):
    n_batch, seq, d = x.shape
    lc = ctx.shape[1]
    depth = mod_w.shape[0]
    n_lat = n_batch * seq
    tiles_per_batch = seq // TOK_TILE
    geom = (n_lat // TOK_TILE, tiles_per_batch, n_batch)
    assert depth == 2 and seq % TOK_TILE == 0 and (n_batch * lc) % TOK_TILE == 0 and n_batch < 16

    x_all = jnp.concatenate([x.reshape(n_lat, d), ctx.reshape(n_batch * lc, d)], axis=0)
    c_rows = jnp.zeros((16, d), F32).at[:n_batch].set(c).at[n_batch].set(c_ctx)
    mod = _modulation(c_rows, mod_w, mod_b)
    tables = _rope_tables(seq)
    bd = _block_diag_ones()
    row2 = lambda v: v.reshape(1, -1)
    tile_gain = lambda gvec, reps: jnp.tile(gvec, reps).reshape(1, -1)

    w0 = ev_w_in[0]
    seg = np.cumsum([0, 256, 256, 512, 512, 32, 512, 128, 128])
    cols = lambda i: w0[:, seg[i]:seg[i + 1]]
    w_even = _bf(jnp.concatenate([cols(0), cols(1), cols(2), cols(3), cols(5), cols(6), cols(7), cols(4),
                                  jnp.zeros((d, EV_END - EV_LR - 2 * GLA_GATE_RANK), F32)], axis=1))
    hk = GLA_HEADS * GLA_DK
    gw = jnp.zeros((LANES, 2 * hk), F32)
    gw = gw.at[0:GLA_GATE_RANK, 0:hk].set(gla_gate_w[0, 0])
    gw = gw.at[GLA_GATE_RANK:2 * GLA_GATE_RANK, hk:2 * hk].set(gla_gate_w[0, 1])
    gb = gla_gate_b[0].reshape(1, 2 * hk)
    gq, gk, gv, gr, g, aq, ak, av = _proj_even(
        x_all, mod[0], row2(norm_mix[0]), w_even, _bf(gw), gb,
        tile_gain(att_q_norm[0], ATT_HEADS), tile_gain(att_k_norm[0], ATT_KV_HEADS), tables, bd, geom)
    o_gla = _gla(gq, gk, gv, g, n_batch, seq, lc)
    att_lat = _attention("dense", aq, ak, av, n_batch, seq, lc, ATT_HEADS, ATT_KV_HEADS)
    att_ctx = _attention("ctx", aq, ak, av, n_batch, seq, lc, ATT_HEADS, ATT_KV_HEADS)
    att = jnp.concatenate([att_lat, att_ctx], axis=0)
    wr_hi, wr_lo, br = _router_weights(router_group_w[0], router_group_b[0], router_expert_w[0], router_expert_b[0])
    tt = TOK_TILE
    n_all = x_all.shape[0]
    gn = row2(gla_out_norm[0])
    x_mid, hrow = _out_proj(
        functools.partial(_out_even_kernel, geom),
        (o_gla, gr, att, gn),
        [pl.BlockSpec((2, tt, GLA_HEADS * GLA_DV), lambda t: (0, t, 0)),
         pl.BlockSpec((tt, GLA_HEADS * GLA_DV), lambda t: (t, 0)),
         pl.BlockSpec((tt, ATT_HEADS * HEAD_DIM), lambda t: (t, 0)),
         _full(gn.shape)],
        _bf(ev_w_out[0]), x_all, mod[0], row2(norm_ffn[0]), wr_hi, wr_lo, br, x_all.shape[0], "out_even")
    f0 = _moe(hrow, *_expert_weights(exp_w_gate[0], exp_w_up[0], exp_w_down[0]))

    x1, q1, k1, v1 = _proj_odd(
        x_mid, f0, mod[0], mod[1], row2(norm_mix[1]), _bf(od_w_in[0]),
        tile_gain(swa_q_norm[0], SWA_HEADS), tile_gain(swa_k_norm[0], SWA_KV_HEADS), tables, bd, geom)
    att1 = _attention("window", q1, k1, v1, n_batch, seq, lc, SWA_HEADS, SWA_KV_HEADS, sink=swa_sink[0])
    wr_hi, wr_lo, br = _router_weights(router_group_w[1], router_group_b[1], router_expert_w[1], router_expert_b[1])
    x2, hrow1 = _out_proj(
        functools.partial(_out_odd_kernel, geom), (att1,),
        [pl.BlockSpec((tt, SWA_HEADS * HEAD_DIM), lambda t: (t, 0))],
        _bf(od_w_out[0]), x1, mod[1], row2(norm_ffn[1]), wr_hi, wr_lo, br, n_lat, "out_odd")
    f1 = _moe(hrow1, *_expert_weights(exp_w_gate[1], exp_w_up[1], exp_w_down[1]))
    out = _final(x2, f1, mod[1], tiles_per_batch)
    return out.reshape(n_batch, seq, d)
```

```python
import functools

import numpy as np
import jax
import jax.numpy as jnp
from jax import lax
from jax.experimental import pallas as pl
from jax.experimental.pallas import tpu as pltpu

F32 = jnp.float32
BF16 = jnp.bfloat16

GRID_W = 64
HEAD_DIM = 64
AXIS_DIM = HEAD_DIM // 2
ROPE_THETA = 10000.0
EPS = 1e-6
N_MOD = 6
GLA_HEADS = 4
GLA_DK = 64
GLA_DV = 128
GLA_GATE_RANK = 16
GLA_GATE_NORM = 16.0
GLA_CHUNK = 64
ATT_HEADS = 8
ATT_KV_HEADS = 2
SWA_HEADS = 16
SWA_KV_HEADS = 2
SWA_WINDOW = 128
MOE_GROUPS = 4
MOE_EPG = 4
D_EXPERT = 256
N_PAIRS = 6
N_BUCKETS = MOE_GROUPS * N_PAIRS

LANES = 128
MXU_DIM = 256
TOK_TILE = 512
ATT_Q_TILE = 128
ATT_K_CHUNK = 512
ATT_COL = 4 * HEAD_DIM
WIN_SUB = 2
MOE_TILE = 256
META_LANES = LANES
VMEM_LIMIT = 56 * 1024 * 1024
NEG_BIG = -1e30
LOG2E = 1.4426950408889634
Q_SCALE = HEAD_DIM ** -0.5 * LOG2E


def _bf(x):
    return x.astype(BF16)


def _split2(x):
    hi = _bf(x)
    lo = _bf(x - hi.astype(F32))
    return hi, lo


def _dot(a, b):
    return jnp.dot(a, b, preferred_element_type=F32)


def _dot_nt(a, b):
    return lax.dot_general(a, b, (((1,), (1,)), ((), ())), preferred_element_type=F32)


def _dot_tn(a, b):
    return lax.dot_general(a, b, (((0,), (0,)), ((), ())), preferred_element_type=F32)


def _silu(x):
    return x / (1.0 + jnp.exp(-x))


def _rms(x):
    return x * lax.rsqrt(jnp.mean(x * x, axis=-1, keepdims=True) + EPS)


def _cparams(sem):
    return pltpu.CompilerParams(dimension_semantics=sem, vmem_limit_bytes=VMEM_LIMIT)


def _full(shape):
    n = len(shape)
    return pl.BlockSpec(shape, lambda *_: (0,) * n)


def _mod_kernel(c_ref, w_ref, b_ref, o_ref):
    c = c_ref[...]
    ch, cl = _split2(_silu(c))
    wh, wl = _split2(w_ref[0])
    o_ref[0] = _dot(ch, wh) + _dot(ch, wl) + _dot(cl, wh) + b_ref[0]


def _modulation(c_rows, mod_w, mod_b):
    depth, d, n = mod_w.shape
    tn = n // 4
    return pl.pallas_call(
        _mod_kernel,
        out_shape=jax.ShapeDtypeStruct((depth, 16, n), F32),
        grid=(depth, n // tn),
        in_specs=[pl.BlockSpec((16, d), lambda i, j: (0, 0)),
                  pl.BlockSpec((1, d, tn), lambda i, j: (i, 0, j)),
                  pl.BlockSpec((1, 1, tn), lambda i, j: (i, 0, j))],
        out_specs=pl.BlockSpec((1, 16, tn), lambda i, j: (i, 0, j)),
        compiler_params=_cparams(("arbitrary", "arbitrary")),
        name="modulation",
    )(c_rows, mod_w, mod_b.reshape(depth, 1, n))


def _mod_row(t, n_lat_tiles, tiles_per_batch, n_batch):
    return jnp.where(t < n_lat_tiles, t // tiles_per_batch, n_batch)


def _modulated(x, gain, mod_ref, row, k_shift, k_scale):
    d = x.shape[-1]
    shift = mod_ref[pl.ds(row, 1), k_shift * d:(k_shift + 1) * d]
    scale = mod_ref[pl.ds(row, 1), k_scale * d:(k_scale + 1) * d]
    return _rms(x) * gain * (1.0 + scale) + shift


def _rope_tables(seq):
    rows = seq // GRID_W
    row = jnp.repeat(jnp.arange(rows, dtype=jnp.int32), GRID_W)
    col = jnp.tile(jnp.arange(GRID_W, dtype=jnp.int32), rows)
    inv_freq = ROPE_THETA ** (-jnp.arange(0, AXIS_DIM, 2, dtype=F32) / AXIS_DIM)
    ang = jnp.stack([row[:, None] * inv_freq, col[:, None] * inv_freq], axis=1)
    cos, sin = jnp.cos(ang), jnp.sin(ang)
    zero = jnp.zeros_like(sin)
    cos64 = jnp.concatenate([cos[:, 0], cos[:, 0], cos[:, 1], cos[:, 1]], axis=-1)
    sa64 = jnp.concatenate([-sin[:, 0], zero[:, 0], -sin[:, 1], zero[:, 1]], axis=-1)
    sb64 = jnp.concatenate([zero[:, 0], sin[:, 0], zero[:, 1], sin[:, 1]], axis=-1)

    def widen(t, fill):
        t = jnp.concatenate([t, t], axis=-1)
        return jnp.concatenate([t, jnp.full((TOK_TILE, LANES), fill, F32)], axis=0)

    return widen(cos64, 1.0), widen(sa64, 0.0), widen(sb64, 0.0)


def _head_sumsq(y, bd):
    w = y.shape[-1]
    outs = []
    for s in range(0, w, MXU_DIM):
        e = min(s + MXU_DIM, w)
        hi, lo = _split2(y[:, s:e])
        b = bd[0:e - s, 0:e - s]
        outs.append(_dot(hi, b) + _dot(lo, b))
    return outs[0] if len(outs) == 1 else jnp.concatenate(outs, axis=-1)


def _qk_norm_rope(z, gain, bd, cos, sa, sb):
    w = z.shape[-1]
    rep = w // LANES
    ss = _head_sumsq(z * z, bd)
    y = z * lax.rsqrt(ss * (1.0 / HEAD_DIM) + EPS) * gain

    def wide(t):
        return t if rep == 1 else jnp.concatenate([t] * rep, axis=-1)

    return (y * wide(cos) + pltpu.roll(y, w - AXIS_DIM // 2, 1) * wide(sa)
            + pltpu.roll(y, AXIS_DIM // 2, 1) * wide(sb))


def _kv_rep(kv128):
    lane = lax.broadcasted_iota(jnp.int32, kv128.shape, 1)
    sw = pltpu.roll(kv128, HEAD_DIM, 1)
    a0 = jnp.where(lane < HEAD_DIM, kv128, sw)
    a1 = jnp.where(lane < HEAD_DIM, sw, kv128)
    return jnp.concatenate([a0, a0, a1, a1], axis=-1)


def _rope_block(t, n_lat_tiles, tiles_per_batch):
    return jnp.where(t < n_lat_tiles, t % tiles_per_batch, tiles_per_batch)


EV_GQ, EV_GK, EV_GV, EV_GR, EV_AQ, EV_AK, EV_LR, EV_END = 0, 256, 512, 1024, 1536, 2048, 2176, 2304


def _proj_even_kernel(geom, x_ref, mod_ref, gain_ref, w_ref, wvt_ref, gw_ref, gb_ref, qg_ref, kg_ref,
                      cos_ref, sa_ref, sb_ref, bd_ref,
                      gq_ref, gk_ref, gv_ref, gr_ref, g_ref, aq_ref, ak_ref, avt_ref):
    t = pl.program_id(0)
    row = _mod_row(t, *geom)
    hb = _bf(_modulated(x_ref[...], gain_ref[...], mod_ref, row, 0, 1))

    def seg(a, b):
        return _dot(hb, w_ref[:, a:b])

    gq_ref[...] = seg(EV_GQ, EV_GK) * (GLA_DK ** -0.5)
    gk_ref[...] = seg(EV_GK, EV_GV)
    gv_ref[...] = _bf(seg(EV_GV, EV_GR))
    gr_ref[...] = seg(EV_GR, EV_AQ)
    zg = _dot(_bf(seg(EV_LR, EV_END)), gw_ref[...]) + gb_ref[...]
    g_ref[...] = -(jnp.maximum(-zg, 0.0) + jnp.log1p(jnp.exp(-jnp.abs(zg)))) * (1.0 / GLA_GATE_NORM)
    bd = bd_ref[...]
    cos, sa, sb = cos_ref[...], sa_ref[...], sb_ref[...]
    aq = _qk_norm_rope(seg(EV_AQ, EV_AK), qg_ref[...], bd, cos, sa, sb)
    aq_ref[...] = _bf(aq * Q_SCALE)
    ak = _qk_norm_rope(seg(EV_AK, EV_LR), kg_ref[...], bd, cos, sa, sb)
    ak_ref[...] = _bf(_kv_rep(ak))
    avt_ref[0] = _bf(_dot_nt(wvt_ref[...], hb))


def _vt_spec():
    return pl.BlockSpec((1, LANES, TOK_TILE), lambda t: (t, 0, 0))


def _proj_even(x_all, mod, gain, w, wvt, gw, gb, qg, kg, tables, bd, geom):
    n, d = x_all.shape
    n_lat_tiles, tiles_per_batch, _ = geom
    tt = TOK_TILE
    cos, sa, sb = tables
    tok = lambda w_: pl.BlockSpec((tt, w_), lambda t: (t, 0))
    rope = pl.BlockSpec((tt, LANES), lambda t: (_rope_block(t, n_lat_tiles, tiles_per_batch), 0))
    outs = [(256, F32), (256, F32), (512, BF16), (512, F32), (512, F32), (512, BF16), (512, BF16)]
    return pl.pallas_call(
        functools.partial(_proj_even_kernel, geom),
        out_shape=[jax.ShapeDtypeStruct((n, w_), dt) for w_, dt in outs]
        + [jax.ShapeDtypeStruct((n // tt, LANES, tt), BF16)],
        grid=(n // tt,),
        in_specs=[tok(d), _full(mod.shape), _full(gain.shape), _full(w.shape), _full(wvt.shape), _full(gw.shape),
                  _full(gb.shape), _full(qg.shape), _full(kg.shape), rope, rope, rope, _full(bd.shape)],
        out_specs=[tok(w_) for w_, _ in outs] + [_vt_spec()],
        compiler_params=_cparams(("arbitrary",)),
        name="proj_even",
    )(x_all, mod, gain, w, wvt, gw, gb, qg, kg, cos, sa, sb, bd)


def _proj_odd_kernel(geom, x_ref, f_ref, mod_prev_ref, mod_ref, gain_ref, w_ref, wvt_ref, qg_ref, kg_ref,
                     cos_ref, sa_ref, sb_ref, bd_ref, x1_ref, q_ref, k_ref, vt_ref):
    t = pl.program_id(0)
    row = _mod_row(t, *geom)
    d = x_ref.shape[-1]
    gate = mod_prev_ref[pl.ds(row, 1), 5 * d:6 * d]
    x1 = x_ref[...] + gate * f_ref[...]
    x1_ref[...] = x1
    hb = _bf(_modulated(x1, gain_ref[...], mod_ref, row, 0, 1))
    bd = bd_ref[...]
    cos, sa, sb = cos_ref[...], sa_ref[...], sb_ref[...]
    q = _qk_norm_rope(_dot(hb, w_ref[:, 0:d]), qg_ref[...], bd, cos, sa, sb)
    q_ref[...] = _bf(q * Q_SCALE)
    k = _qk_norm_rope(_dot(hb, w_ref[:, d:d + LANES]), kg_ref[...], bd, cos, sa, sb)
    k_ref[...] = _bf(_kv_rep(k))
    vt_ref[0] = _bf(_dot_nt(wvt_ref[...], hb))


def _proj_odd(x_all, f_all, mod_prev, mod, gain, w, wvt, qg, kg, tables, bd, geom):
    n, d = x_all.shape
    n_lat_tiles, tiles_per_batch, _ = geom
    tt = TOK_TILE
    cos, sa, sb = tables
    tok = lambda w_: pl.BlockSpec((tt, w_), lambda t: (t, 0))
    rope = pl.BlockSpec((tt, LANES), lambda t: (_rope_block(t, n_lat_tiles, tiles_per_batch), 0))
    outs = [(d, F32), (d, BF16), (512, BF16)]
    return pl.pallas_call(
        functools.partial(_proj_odd_kernel, geom),
        out_shape=[jax.ShapeDtypeStruct((n, w_), dt) for w_, dt in outs]
        + [jax.ShapeDtypeStruct((n // tt, LANES, tt), BF16)],
        grid=(n // tt,),
        in_specs=[tok(d), tok(d), _full(mod_prev.shape), _full(mod.shape), _full(gain.shape), _full(w.shape),
                  _full(wvt.shape), _full(qg.shape), _full(kg.shape), rope, rope, rope, _full(bd.shape)],
        out_specs=[tok(w_) for w_, _ in outs] + [_vt_spec()],
        compiler_params=_cparams(("arbitrary",)),
        name="proj_odd",
    )(x_all, f_all, mod_prev, mod, gain, w, wvt, qg, kg, cos, sa, sb, bd)


N_LEVELS = 6
GLA_MX_ROWS = (N_LEVELS + 2) * GLA_CHUNK


def _gla_constants():
    c = GLA_CHUNK
    mx = np.zeros((2, GLA_MX_ROWS, c), np.float32)
    pat = np.zeros((2, N_LEVELS + 1, GLA_HEADS * c, c), np.float32)
    r = np.arange(c)
    for lvl in range(N_LEVELS):
        h = 1 << lvl
        ref = (r // (2 * h)) * 2 * h + h - 1
        upper = (r % (2 * h)) >= h
        m = np.zeros((c, c), np.float32)
        for i in range(c):
            if upper[i]:
                m[i, ref[i] + 1:i + 1] = 1.0
            else:
                m[i, i + 1:ref[i] + 1] = 1.0
        mx[0, lvl * c:(lvl + 1) * c] = m
        same = (r[:, None] // (2 * h)) == (r[None, :] // (2 * h))
        p = same & upper[:, None] & (~upper)[None, :]
        pat[0, lvl] = np.tile(p.astype(np.float32), (GLA_HEADS, 1))
    mx[0, N_LEVELS * c:(N_LEVELS + 1) * c] = (r[None, :] <= r[:, None])
    mx[0, (N_LEVELS + 1) * c:(N_LEVELS + 2) * c] = (r[None, :] > r[:, None])
    pat[0, N_LEVELS] = np.tile(np.eye(c, dtype=np.float32), (GLA_HEADS, 1))
    for k in range(N_LEVELS + 2):
        mx[1, k * c:(k + 1) * c] = mx[0, k * c:(k + 1) * c][::-1, ::-1]
    for k in range(N_LEVELS + 1):
        for hd in range(GLA_HEADS):
            pat[1, k, hd * c:(hd + 1) * c] = pat[0, k, hd * c:(hd + 1) * c][::-1, ::-1]
    return mx, pat


def _gla_chunk(q, k, v, g, mx, pat_ref, s_ref):
    c = GLA_CHUNK
    ghi, glo = _split2(g)
    x_all = _dot(mx, ghi) + _dot(mx, glo)
    lane_head = lax.broadcasted_iota(jnp.int32, (c, GLA_HEADS * GLA_DK), 1) // GLA_DK

    def stack_heads(a):
        return jnp.concatenate([_bf(jnp.where(lane_head == h, a, 0.0)) for h in range(GLA_HEADS)], axis=0)

    att = jnp.where(pat_ref[0, N_LEVELS] > 0.0, _dot_nt(stack_heads(q), _bf(k)), 0.0)
    for lvl in range(N_LEVELS):
        e = jnp.exp(x_all[lvl * c:(lvl + 1) * c])
        p = _dot_nt(stack_heads(q * e), _bf(k * e))
        att = att + jnp.where(pat_ref[0, lvl] > 0.0, p, 0.0)
    bcum = x_all[N_LEVELS * c:(N_LEVELS + 1) * c]
    brem = x_all[(N_LEVELS + 1) * c:(N_LEVELS + 2) * c]
    qe = stack_heads(q * jnp.exp(bcum))
    s_old = s_ref[...]
    o_inter = _dot(qe, _bf(s_old))
    kt = jnp.transpose(k * jnp.exp(brem))
    tot = bcum[0:1] + brem[0:1]
    a_col = jnp.transpose(jnp.exp(jnp.broadcast_to(tot, (8, tot.shape[1]))))[:, 0:1]
    outs, news = [], []
    for h in range(GLA_HEADS):
        vh = v[:, h * GLA_DV:(h + 1) * GLA_DV]
        o_h = o_inter[h * c:(h + 1) * c] + _dot(_bf(att[h * c:(h + 1) * c]), vh)
        outs.append(o_h)
        news.append(_dot(_bf(kt[h * GLA_DK:(h + 1) * GLA_DK]), vh))
    s_ref[...] = a_col * s_old + jnp.concatenate(news, axis=0)
    return jnp.concatenate(outs, axis=-1)


def _gla_kernel(n_lat_chunks, n_ctx_chunks, ql_ref, kl_ref, vl_ref, gl_ref, qc_ref, kc_ref, vc_ref, gc_ref,
                mx_ref, pat_ref, ol_ref, oc_ref, s_ref):
    d = pl.program_id(1)
    c = GLA_CHUNK
    s_ref[...] = jnp.zeros_like(s_ref)
    mx = mx_ref[0]

    def run(n_chunks, q_ref, k_ref, v_ref, g_ref, o_ref):
        def body(i, carry):
            ci = jnp.where(d == 0, i, n_chunks - 1 - i)
            rows = pl.ds(pl.multiple_of(ci * c, c), c)
            o_ref[0, rows, :] = _gla_chunk(q_ref[rows, :], k_ref[rows, :], v_ref[rows, :], g_ref[rows, :],
                                           mx, pat_ref, s_ref)
            return carry
        lax.fori_loop(0, n_chunks, body, 0)

    run(n_ctx_chunks, qc_ref, kc_ref, vc_ref, gc_ref, oc_ref)
    run(n_lat_chunks, ql_ref, kl_ref, vl_ref, gl_ref, ol_ref)


def _gla(gq, gk, gv, g, n_batch, seq, lc):
    n = gq.shape[0]
    mx_np, pat_np = _gla_constants()
    mx = jnp.asarray(mx_np, BF16)
    pat = jnp.asarray(pat_np, F32)
    ctx0 = n_batch * seq // lc
    hk, hv = GLA_HEADS * GLA_DK, GLA_HEADS * GLA_DV
    lat = lambda w_, col: pl.BlockSpec((seq, w_), lambda b, d_: (b, col(d_)))
    ctx = lambda w_, col: pl.BlockSpec((lc, w_), lambda b, d_: (ctx0 + b, col(d_)))
    zero = lambda d_: 0
    same = lambda d_: d_
    o_lat, o_ctx = pl.pallas_call(
        functools.partial(_gla_kernel, seq // GLA_CHUNK, lc // GLA_CHUNK),
        out_shape=[jax.ShapeDtypeStruct((2, n_batch * seq, hv), F32),
                   jax.ShapeDtypeStruct((2, n_batch * lc, hv), F32)],
        grid=(n_batch, 2),
        in_specs=[lat(hk, zero), lat(hk, zero), lat(hv, zero), lat(hk, same),
                  ctx(hk, zero), ctx(hk, zero), ctx(hv, zero), ctx(hk, same),
                  pl.BlockSpec((1,) + mx.shape[1:], lambda b, d_: (d_, 0, 0)),
                  pl.BlockSpec((1,) + pat.shape[1:], lambda b, d_: (d_, 0, 0, 0))],
        out_specs=[pl.BlockSpec((1, seq, hv), lambda b, d_: (d_, b, 0)),
                   pl.BlockSpec((1, lc, hv), lambda b, d_: (d_, b, 0))],
        scratch_shapes=[pltpu.VMEM((GLA_HEADS * GLA_DK, GLA_DV), F32)],
        compiler_params=_cparams(("arbitrary", "arbitrary")),
        name="gla_scan",
    )(gq, gk, gv, g, gq, gk, gv, g, mx, pat)
    return jnp.concatenate([o_lat, o_ctx], axis=1)


def _stack_heads(q):
    lane_head = lax.broadcasted_iota(jnp.int32, q.shape, 1) // HEAD_DIM
    return jnp.concatenate([jnp.where(lane_head == h, q, jnp.zeros_like(q)) for h in range(4)], axis=0)


def _attn_store(acc, l, o_ref, tq):
    out = acc * (1.0 / l)
    for h in range(4):
        o_ref[h * HEAD_DIM:(h + 1) * HEAD_DIM, :] = _bf(out[:, h * tq:(h + 1) * tq])


def _attn_dense_kernel(n_chunks, *refs):
    if n_chunks:
        q_ref, kl_ref, vtl_ref, kc_ref, vtc_ref, o_ref = refs
    else:
        q_ref, kc_ref, vtc_ref, o_ref = refs
    tq = q_ref.shape[0]
    cols = 4 * tq
    q4 = _stack_heads(q_ref[...])

    def scores(c):
        if c < n_chunks:
            return _dot_nt(kl_ref[c * ATT_K_CHUNK:(c + 1) * ATT_K_CHUNK, :], q4)
        return _dot_nt(kc_ref[...], q4)

    def update(carry, st, vt):
        m, l, acc = carry
        m_new = jnp.maximum(m, jnp.max(st, axis=0, keepdims=True))
        alpha = jnp.exp2(m - m_new)
        p = jnp.exp2(st - m_new)
        l = alpha * l + jnp.sum(p, axis=0, keepdims=True)
        acc = alpha * acc + _dot(vt, _bf(p))
        return m_new, l, acc

    carry = (jnp.full((1, cols), NEG_BIG, F32), jnp.zeros((1, cols), F32), jnp.zeros((HEAD_DIM, cols), F32))
    st = scores(0)
    for c in range(n_chunks + 1):
        st_next = scores(c + 1) if c < n_chunks else None
        carry = update(carry, st, vtl_ref[c] if c < n_chunks else vtc_ref[0])
        st = st_next
    m, l, acc = carry
    _attn_store(acc, l, o_ref, tq)


def _attn_window_kernel(seq, q_ref, *refs):
    nk = WIN_SUB + 2
    k_refs, v_refs = refs[0:nk], refs[nk:2 * nk]
    kc_ref, vtc_ref, sink_ref, o_ref = refs[2 * nk:]
    tq = SWA_WINDOW
    i = pl.program_id(2)
    kb = jnp.concatenate([r[...] for r in k_refs], axis=0)
    vtb = jnp.concatenate([r[0] for r in v_refs], axis=1)
    kc, vtc, sink = kc_ref[...], vtc_ref[0], sink_ref[0]
    span = 3 * tq
    for u in range(WIN_SUB):
        q4 = _stack_heads(q_ref[u * tq:(u + 1) * tq, :])
        first = i * WIN_SUB + u - 1
        kpos = first * tq + lax.broadcasted_iota(jnp.int32, (span, tq), 0)
        qpos = (first + 1) * tq + lax.broadcasted_iota(jnp.int32, (span, tq), 1)
        ok = (kpos >= 0) & (kpos < seq) & (jnp.abs(kpos - qpos) <= SWA_WINDOW)
        bias = jnp.where(ok, 0.0, NEG_BIG)
        sb = _dot_nt(kb[u * tq:u * tq + span], q4) + jnp.concatenate([bias] * 4, axis=1)
        sc = _dot_nt(kc, q4)
        m = jnp.maximum(jnp.maximum(jnp.max(sb, axis=0, keepdims=True), jnp.max(sc, axis=0, keepdims=True)), sink)
        pb = jnp.exp2(sb - m)
        pc = jnp.exp2(sc - m)
        l = jnp.sum(pb, axis=0, keepdims=True) + jnp.sum(pc, axis=0, keepdims=True) + jnp.exp2(sink - m)
        acc = _dot(vtb[:, u * tq:u * tq + span], _bf(pb)) + _dot(vtc, _bf(pc))
        out = acc * (1.0 / l)
        for h in range(4):
            o_ref[h * HEAD_DIM:(h + 1) * HEAD_DIM, u * tq:(u + 1) * tq] = _bf(out[:, h * tq:(h + 1) * tq])


def _attention(mode, q, k_rep, vt, n_batch, seq, lc, n_heads, n_kv, sink=None):
    ncol = n_heads * HEAD_DIM // ATT_COL
    col_per_kv = ncol // n_kv
    tq = ATT_Q_TILE
    tpb = seq // TOK_TILE
    n_lat_tiles = n_batch * tpb
    ctx_per_tile = TOK_TILE // lc
    ctx0 = n_batch * seq // lc
    assert ATT_K_CHUNK == TOK_TILE and TOK_TILE % lc == 0 and tq == SWA_WINDOW
    kv = lambda j: j // col_per_kv
    k_ctx = pl.BlockSpec((lc, ATT_COL), lambda b, j, i: (ctx0 + b, kv(j)))
    vt_ctx = pl.BlockSpec((1, HEAD_DIM, lc), lambda b, j, i: (n_lat_tiles + b // ctx_per_tile, kv(j), b % ctx_per_tile))
    if mode == "ctx":
        nq = lc // tq
        q0 = n_batch * seq // tq
        kern = functools.partial(_attn_dense_kernel, 0)
        args = (q, k_rep, vt)
        in_specs = [pl.BlockSpec((tq, ATT_COL), lambda b, j, i: (q0 + b * nq + i, j)), k_ctx, vt_ctx]
    elif mode == "dense":
        nq = seq // tq
        kern = functools.partial(_attn_dense_kernel, tpb)
        args = (q, k_rep, vt, k_rep, vt)
        in_specs = [pl.BlockSpec((tq, ATT_COL), lambda b, j, i: (b * nq + i, j)),
                    pl.BlockSpec((seq, ATT_COL), lambda b, j, i: (b, kv(j))),
                    pl.BlockSpec((tpb, HEAD_DIM, TOK_TILE), lambda b, j, i: (b, kv(j), 0)), k_ctx, vt_ctx]
    else:
        wb = SWA_WINDOW
        tq = WIN_SUB * wb
        nq = seq // tq
        nkb = seq // wb
        per_tile = TOK_TILE // wb
        kern = functools.partial(_attn_window_kernel, seq)
        nb = lambda i, o: jnp.clip(i * WIN_SUB + o, 0, nkb - 1)
        k_nb = lambda o: pl.BlockSpec((wb, ATT_COL), lambda b, j, i: (b * nkb + nb(i, o), kv(j)))
        v_nb = lambda o: pl.BlockSpec(
            (1, HEAD_DIM, wb), lambda b, j, i: (b * tpb + nb(i, o) // per_tile, kv(j), nb(i, o) % per_tile))
        offs = range(-1, WIN_SUB + 1)
        sink_row = jnp.repeat(sink.reshape(ncol, 1, 4), wb, axis=2) * LOG2E
        args = (q,) + (k_rep,) * len(offs) + (vt,) * len(offs) + (k_rep, vt, sink_row)
        in_specs = ([pl.BlockSpec((tq, ATT_COL), lambda b, j, i: (b * nq + i, j))]
                    + [k_nb(o) for o in offs] + [v_nb(o) for o in offs]
                    + [k_ctx, vt_ctx, pl.BlockSpec((1, 1, 4 * wb), lambda b, j, i: (j, 0, 0))])
    return pl.pallas_call(
        kern,
        out_shape=jax.ShapeDtypeStruct((n_heads * HEAD_DIM, n_batch * nq * tq), BF16),
        grid=(n_batch, ncol, nq),
        in_specs=in_specs,
        out_specs=pl.BlockSpec((ATT_COL, tq), lambda b, j, i: (j, b * nq + i)),
        compiler_params=_cparams(("arbitrary", "arbitrary", "arbitrary")),
        name="attention_" + mode,
    )(*args)


def _route(h, wr_hi_ref, wr_lo_ref, br_ref):
    hh, hl = _split2(h)
    logits = _dot(hh, wr_hi_ref[...]) + _dot(hh, wr_lo_ref[...]) + _dot(hl, wr_hi_ref[...]) + br_ref[...]
    col = lambda i: logits[:, i:i + 1]
    gl = [col(i) for i in range(MOE_GROUPS)]
    gmax = functools.reduce(jnp.maximum, gl)
    gi = jnp.where(gl[0] == gmax, 0, jnp.where(gl[1] == gmax, 1, jnp.where(gl[2] == gmax, 2, 3)))
    g_weight = 1.0 / functools.reduce(lambda a, b: a + b, [jnp.exp(x - gmax) for x in gl])
    el = []
    for j in range(MOE_EPG):
        cand = [col(MOE_GROUPS + g * MOE_EPG + j) for g in range(MOE_GROUPS)]
        el.append(jnp.where(gi == 0, cand[0], jnp.where(gi == 1, cand[1], jnp.where(gi == 2, cand[2], cand[3]))))
    m1 = functools.reduce(jnp.maximum, el)
    i1 = jnp.where(el[0] == m1, 0, jnp.where(el[1] == m1, 1, jnp.where(el[2] == m1, 2, 3)))
    rest = [jnp.where(i1 == j, -jnp.inf, el[j]) for j in range(MOE_EPG)]
    m2 = functools.reduce(jnp.maximum, rest)
    i2 = jnp.where(rest[0] == m2, 0, jnp.where(rest[1] == m2, 1, jnp.where(rest[2] == m2, 2, 3)))
    e2 = jnp.exp(m2 - m1)
    w1 = g_weight / (1.0 + e2)
    w2 = g_weight * e2 / (1.0 + e2)
    lo = jnp.minimum(i1, i2)
    hi = jnp.maximum(i1, i2)
    w_lo = jnp.where(i1 == lo, w1, w2)
    w_hi = jnp.where(i1 == lo, w2, w1)
    pair = jnp.where(lo == 0, hi - 1, jnp.where(lo == 1, hi + 1, N_PAIRS - 1))
    bucket = (gi * N_PAIRS + pair).astype(F32)
    lane = lax.broadcasted_iota(jnp.int32, (h.shape[0], META_LANES), 1)
    return jnp.where(lane == 0, w_lo, jnp.where(lane == 1, w_hi, jnp.where(lane == 2, bucket, 0.0)))


def _out_tail(geom, m, x_ref, mod_ref, gain_ffn_ref, wr_hi_ref, wr_lo_ref, br_ref, x_new_ref, hrow_ref):
    t = pl.program_id(0)
    row = _mod_row(t, *geom)
    d = x_ref.shape[-1]
    x_new = x_ref[...] + mod_ref[pl.ds(row, 1), 2 * d:3 * d] * m
    x_new_ref[...] = x_new
    h = _modulated(x_new, gain_ffn_ref[...], mod_ref, row, 3, 4)
    hrow_ref[:, 0:d] = h
    hrow_ref[:, d:d + META_LANES] = _route(h, wr_hi_ref, wr_lo_ref, br_ref)


def _out_even_kernel(geom, o_ref, r_ref, att_ref, gn_ref, w_ref, x_ref, mod_ref, gain_ffn_ref,
                     wr_hi_ref, wr_lo_ref, br_ref, x_new_ref, hrow_ref):
    o = o_ref[0] + o_ref[1]
    r = r_ref[...]
    parts = []
    for h in range(GLA_HEADS):
        sl = slice(h * GLA_DV, (h + 1) * GLA_DV)
        parts.append(_rms(o[:, sl]) * gn_ref[...] * _silu(r[:, sl]))
    a = _bf(jnp.concatenate(parts, axis=-1))
    half = a.shape[-1]
    m = _dot(a, w_ref[0:half, :]) + _dot_tn(att_ref[...], w_ref[half:, :])
    _out_tail(geom, m, x_ref, mod_ref, gain_ffn_ref, wr_hi_ref, wr_lo_ref, br_ref, x_new_ref, hrow_ref)


def _out_odd_kernel(geom, att_ref, w_ref, x_ref, mod_ref, gain_ffn_ref,
                    wr_hi_ref, wr_lo_ref, br_ref, x_new_ref, hrow_ref):
    m = _dot_tn(att_ref[...], w_ref[...])
    _out_tail(geom, m, x_ref, mod_ref, gain_ffn_ref, wr_hi_ref, wr_lo_ref, br_ref, x_new_ref, hrow_ref)


def _out_proj(kernel, mix_args, mix_specs, w, x_all, mod, gain_ffn, wr_hi, wr_lo, br, n_rows, name):
    d = x_all.shape[-1]
    tt = TOK_TILE
    tok = lambda w_: pl.BlockSpec((tt, w_), lambda t: (t, 0))
    return pl.pallas_call(
        kernel,
        out_shape=[jax.ShapeDtypeStruct((n_rows, d), F32), jax.ShapeDtypeStruct((n_rows, d + META_LANES), F32)],
        grid=(n_rows // tt,),
        in_specs=mix_specs + [_full(w.shape), tok(d), _full(mod.shape), _full(gain_ffn.shape),
                              _full(wr_hi.shape), _full(wr_lo.shape), _full(br.shape)],
        out_specs=[tok(d), tok(d + META_LANES)],
        compiler_params=_cparams(("arbitrary",)),
        name=name,
    )(*mix_args, w, x_all, mod, gain_ffn, wr_hi, wr_lo, br)


def _moe_kernel(tlo_ref, thi_ref, nused_ref, src_ref, dst_ref, hrow_hbm, wg_lo, wu_lo, wd_lo, wg_hi, wu_hi, wd_hi,
                f_hbm, xbuf, obuf, sem_in, sem_out):
    t = pl.program_id(0)
    tm = xbuf.shape[0]
    d = obuf.shape[1]

    def gather(r):
        return pltpu.make_async_copy(hrow_hbm.at[pl.ds(src_ref[0, 0, r], 1)], xbuf.at[pl.ds(r, 1)], sem_in)

    def scatter(r):
        return pltpu.make_async_copy(obuf.at[pl.ds(r, 1)], f_hbm.at[pl.ds(dst_ref[0, 0, r], 1)], sem_out)

    @pl.when(t < nused_ref[0])
    def _():
        def start_in(r, c):
            gather(r).start()
            return c
        lax.fori_loop(0, tm, start_in, 0)

        def wait_in(r, c):
            gather(r).wait()
            return c
        lax.fori_loop(0, tm, wait_in, 0)

        xb = _bf(xbuf[:, 0:d])
        w_lo = xbuf[:, d:d + 1]
        w_hi = xbuf[:, d + 1:d + 2]
        hid_lo = _silu(_dot(xb, wg_lo[0])) * _dot(xb, wu_lo[0]) * w_lo
        hid_hi = _silu(_dot(xb, wg_hi[0])) * _dot(xb, wu_hi[0]) * w_hi
        obuf[...] = _dot(_bf(hid_lo), wd_lo[0]) + _dot(_bf(hid_hi), wd_hi[0])

        def start_out(r, c):
            @pl.when(dst_ref[0, 0, r] >= 0)
            def _():
                scatter(r).start()
            return c
        lax.fori_loop(0, tm, start_out, 0)

        def wait_out(r, c):
            @pl.when(dst_ref[0, 0, r] >= 0)
            def _():
                scatter(r).wait()
            return c
        lax.fori_loop(0, tm, wait_out, 0)


def _moe(hrow, w_gate, w_up, w_down):
    n, dw = hrow.shape
    d = dw - META_LANES
    tm = MOE_TILE
    n_tiles = n // tm + N_BUCKETS
    p = n_tiles * tm
    bucket = hrow[:, d + 2].astype(jnp.int32)
    onehot = (bucket[:, None] == jnp.arange(N_BUCKETS, dtype=jnp.int32)[None, :]).astype(jnp.int32)
    counts = jnp.sum(onehot, axis=0)
    tiles_b = (counts + tm - 1) // tm
    tile_end = jnp.cumsum(tiles_b)
    start_b = (tile_end - tiles_b) * tm
    rank = jnp.sum(jnp.cumsum(onehot, axis=0) * onehot, axis=1) - 1
    pos = start_b[bucket] + rank
    tok = jnp.arange(n, dtype=jnp.int32)
    src = jnp.zeros((p,), jnp.int32).at[pos].set(tok)
    dst = jnp.full((p,), -1, jnp.int32).at[pos].set(tok)
    n_used = tile_end[-1].astype(jnp.int32).reshape(1)
    tile_ids = jnp.arange(n_tiles, dtype=jnp.int32)
    tile_bucket = jnp.minimum(jnp.sum((tile_ids[:, None] >= tile_end[None, :]).astype(jnp.int32), axis=1),
                              N_BUCKETS - 1)
    last_bucket = tile_bucket[jnp.maximum(n_used[0] - 1, 0)]
    tile_bucket = jnp.where(tile_ids < n_used[0], tile_bucket, last_bucket)
    pair_lo = jnp.asarray([0, 0, 0, 1, 1, 2], jnp.int32)
    pair_hi = jnp.asarray([1, 2, 3, 2, 3, 3], jnp.int32)
    grp = tile_bucket // N_PAIRS
    t_lo = grp * MOE_EPG + pair_lo[tile_bucket % N_PAIRS]
    t_hi = grp * MOE_EPG + pair_hi[tile_bucket % N_PAIRS]

    idx_spec = pl.BlockSpec((1, 1, tm), lambda t, *_: (t, 0, 0), memory_space=pltpu.SMEM)
    f = D_EXPERT
    up_lo = pl.BlockSpec((1, d, f), lambda t, lo, hi, nu: (lo[t], 0, 0))
    up_hi = pl.BlockSpec((1, d, f), lambda t, lo, hi, nu: (hi[t], 0, 0))
    dn_lo = pl.BlockSpec((1, f, d), lambda t, lo, hi, nu: (lo[t], 0, 0))
    dn_hi = pl.BlockSpec((1, f, d), lambda t, lo, hi, nu: (hi[t], 0, 0))
    grid_spec = pltpu.PrefetchScalarGridSpec(
        num_scalar_prefetch=3,
        grid=(n_tiles,),
        in_specs=[idx_spec, idx_spec, pl.BlockSpec(memory_space=pl.ANY), up_lo, up_lo, dn_lo, up_hi, up_hi, dn_hi],
        out_specs=pl.BlockSpec(memory_space=pl.ANY),
        scratch_shapes=[pltpu.VMEM((tm, dw), F32), pltpu.VMEM((tm, d), F32),
                        pltpu.SemaphoreType.DMA(()), pltpu.SemaphoreType.DMA(())],
    )
    return pl.pallas_call(
        _moe_kernel,
        out_shape=jax.ShapeDtypeStruct((n, d), F32),
        grid_spec=grid_spec,
        compiler_params=_cparams(("arbitrary",)),
        name="moe_experts",
    )(t_lo, t_hi, n_used, src.reshape(n_tiles, 1, tm), dst.reshape(n_tiles, 1, tm), hrow,
      w_gate, w_up, w_down, w_gate, w_up, w_down)


def _final_kernel(tiles_per_batch, x_ref, f_ref, mod_ref, o_ref):
    row = pl.program_id(0) // tiles_per_batch
    d = x_ref.shape[-1]
    o_ref[...] = x_ref[...] + mod_ref[pl.ds(row, 1), 5 * d:6 * d] * f_ref[...]


def _final(x_lat, f_lat, mod, tiles_per_batch):
    n, d = x_lat.shape
    tok = pl.BlockSpec((TOK_TILE, d), lambda t: (t, 0))
    return pl.pallas_call(
        functools.partial(_final_kernel, tiles_per_batch),
        out_shape=jax.ShapeDtypeStruct((n, d), F32),
        grid=(n // TOK_TILE,),
        in_specs=[tok, tok, _full(mod.shape)],
        out_specs=tok,
        compiler_params=_cparams(("arbitrary",)),
        name="final_residual",
    )(x_lat, f_lat, mod)


def _block_diag_ones():
    r = np.arange(MXU_DIM) // HEAD_DIM
    return jnp.asarray((r[:, None] == r[None, :]).astype(np.float32), BF16)


def _router_weights(wg, bg, we, be):
    d = wg.shape[0]
    n = MOE_GROUPS + MOE_GROUPS * MOE_EPG
    w = jnp.concatenate([wg, we, jnp.zeros((d, LANES - n), F32)], axis=1)
    b = jnp.concatenate([bg, be, jnp.zeros((LANES - n,), F32)]).reshape(1, LANES)
    hi = _bf(w)
    return hi, _bf(w - hi.astype(F32)), b


def _expert_weights(w_gate, w_up, w_down):
    g, e, d, f = w_gate.shape
    return (_bf(w_gate).reshape(g * e, d, f), _bf(w_up).reshape(g * e, d, f), _bf(w_down).reshape(g * e, f, d))


def kernel(x, c, ctx, c_ctx, mod_w, mod_b, norm_mix, norm_ffn, ev_w_in, ev_w_out, gla_gate_w, gla_gate_b,
           gla_out_norm, att_q_norm, att_k_norm, od_w_in, od_w_out, swa_sink, swa_q_norm, swa_k_norm,
           router_group_w, router_group_b, router_expert_w, router_expert_b, exp_w_gate, exp_w_up, exp_w_down):
    n_batch, seq, d = x.shape
    lc = ctx.shape[1]
    depth = mod_w.shape[0]
    n_lat = n_batch * seq
    tiles_per_batch = seq // TOK_TILE
    geom = (n_lat // TOK_TILE, tiles_per_batch, n_batch)
    assert depth == 2 and seq % TOK_TILE == 0 and (n_batch * lc) % TOK_TILE == 0 and n_batch < 16

    x_all = jnp.concatenate([x.reshape(n_lat, d), ctx.reshape(n_batch * lc, d)], axis=0)
    c_rows = jnp.zeros((16, d), F32).at[:n_batch].set(c).at[n_batch].set(c_ctx)
    mod = _modulation(c_rows, mod_w, mod_b)
    tables = _rope_tables(seq)
    bd = _block_diag_ones()
    row2 = lambda v: v.reshape(1, -1)
    tile_gain = lambda gvec, reps: jnp.tile(gvec, reps).reshape(1, -1)

    w0 = ev_w_in[0]
    seg = np.cumsum([0, 256, 256, 512, 512, 32, 512, 128, 128])
    cols = lambda i: w0[:, seg[i]:seg[i + 1]]
    w_even = _bf(jnp.concatenate([cols(0), cols(1), cols(2), cols(3), cols(5), cols(6), cols(4),
                                  jnp.zeros((d, EV_END - EV_LR - 2 * GLA_GATE_RANK), F32)], axis=1))
    wvt_even = _bf(cols(7).T)
    hk = GLA_HEADS * GLA_DK
    gw = jnp.zeros((LANES, 2 * hk), F32)
    gw = gw.at[0:GLA_GATE_RANK, 0:hk].set(gla_gate_w[0, 0])
    gw = gw.at[GLA_GATE_RANK:2 * GLA_GATE_RANK, hk:2 * hk].set(gla_gate_w[0, 1])
    gb = gla_gate_b[0].reshape(1, 2 * hk)
    gq, gk, gv, gr, g, aq, ak, avt = _proj_even(
        x_all, mod[0], row2(norm_mix[0]), w_even, wvt_even, _bf(gw), gb,
        tile_gain(att_q_norm[0], ATT_HEADS), tile_gain(att_k_norm[0], ATT_KV_HEADS), tables, bd, geom)
    o_gla = _gla(gq, gk, gv, g, n_batch, seq, lc)
    att_lat = _attention("dense", aq, ak, avt, n_batch, seq, lc, ATT_HEADS, ATT_KV_HEADS)
    att_ctx = _attention("ctx", aq, ak, avt, n_batch, seq, lc, ATT_HEADS, ATT_KV_HEADS)
    att = jnp.concatenate([att_lat, att_ctx], axis=1)
    wr_hi, wr_lo, br = _router_weights(router_group_w[0], router_group_b[0], router_expert_w[0], router_expert_b[0])
    tt = TOK_TILE
    n_all = x_all.shape[0]
    gn = row2(gla_out_norm[0])
    x_mid, hrow = _out_proj(
        functools.partial(_out_even_kernel, geom),
        (o_gla, gr, att, gn),
        [pl.BlockSpec((2, tt, GLA_HEADS * GLA_DV), lambda t: (0, t, 0)),
         pl.BlockSpec((tt, GLA_HEADS * GLA_DV), lambda t: (t, 0)),
         pl.BlockSpec((ATT_HEADS * HEAD_DIM, tt), lambda t: (0, t)),
         _full(gn.shape)],
        _bf(ev_w_out[0]), x_all, mod[0], row2(norm_ffn[0]), wr_hi, wr_lo, br, x_all.shape[0], "out_even")
    f0 = _moe(hrow, *_expert_weights(exp_w_gate[0], exp_w_up[0], exp_w_down[0]))

    w_odd = od_w_in[0]
    x1, q1, k1, v1t = _proj_odd(
        x_mid, f0, mod[0], mod[1], row2(norm_mix[1]), _bf(w_odd[:, 0:d + LANES]), _bf(w_odd[:, d + LANES:].T),
        tile_gain(swa_q_norm[0], SWA_HEADS), tile_gain(swa_k_norm[0], SWA_KV_HEADS), tables, bd, geom)
    att1 = _attention("window", q1, k1, v1t, n_batch, seq, lc, SWA_HEADS, SWA_KV_HEADS, sink=swa_sink[0])
    wr_hi, wr_lo, br = _router_weights(router_group_w[1], router_group_b[1], router_expert_w[1], router_expert_b[1])
    x2, hrow1 = _out_proj(
        functools.partial(_out_odd_kernel, geom), (att1,),
        [pl.BlockSpec((SWA_HEADS * HEAD_DIM, tt), lambda t: (0, t))],
        _bf(od_w_out[0]), x1, mod[1], row2(norm_ffn[1]), wr_hi, wr_lo, br, n_lat, "out_odd")
    f1 = _moe(hrow1, *_expert_weights(exp_w_gate[1], exp_w_up[1], exp_w_down[1]))
    out = _final(x2, f1, mod[1], tiles_per_batch)
    return out.reshape(n_batch, seq, d)
```

```python
import functools

import numpy as np
import jax
import jax.numpy as jnp
from jax import lax
from jax.experimental import pallas as pl
from jax.experimental.pallas import tpu as pltpu

F32 = jnp.float32
BF16 = jnp.bfloat16

GRID_W = 64
HEAD_DIM = 64
AXIS_DIM = HEAD_DIM // 2
ROPE_THETA = 10000.0
EPS = 1e-6
N_MOD = 6
GLA_HEADS = 4
GLA_DK = 64
GLA_DV = 128
GLA_GATE_RANK = 16
GLA_GATE_NORM = 16.0
GLA_CHUNK = 64
ATT_HEADS = 8
ATT_KV_HEADS = 2
SWA_HEADS = 16
SWA_KV_HEADS = 2
SWA_WINDOW = 128
MOE_GROUPS = 4
MOE_EPG = 4
D_EXPERT = 256
N_PAIRS = 6
N_BUCKETS = MOE_GROUPS * N_PAIRS

LANES = 128
MXU_DIM = 256
TOK_TILE = 512
ATT_Q_TILE = 128
ATT_K_CHUNK = 512
ATT_COL = 4 * HEAD_DIM
WIN_SUB = 2
MOE_TILE = 256
META_LANES = LANES
VMEM_LIMIT = 56 * 1024 * 1024
NEG_BIG = -1e30
LOG2E = 1.4426950408889634
Q_SCALE = HEAD_DIM ** -0.5 * LOG2E


def _bf(x):
    return x.astype(BF16)


def _split2(x):
    hi = _bf(x)
    lo = _bf(x - hi.astype(F32))
    return hi, lo


def _dot(a, b):
    return jnp.dot(a, b, preferred_element_type=F32)


def _dot_nt(a, b):
    return lax.dot_general(a, b, (((1,), (1,)), ((), ())), preferred_element_type=F32)


def _dot_tn(a, b):
    return lax.dot_general(a, b, (((0,), (0,)), ((), ())), preferred_element_type=F32)


def _silu(x):
    return x / (1.0 + jnp.exp(-x))


def _rms(x):
    return x * lax.rsqrt(jnp.mean(x * x, axis=-1, keepdims=True) + EPS)


def _cparams(sem):
    return pltpu.CompilerParams(dimension_semantics=sem, vmem_limit_bytes=VMEM_LIMIT)


def _full(shape):
    n = len(shape)
    return pl.BlockSpec(shape, lambda *_: (0,) * n)


def _mod_kernel(c_ref, w_ref, b_ref, o_ref):
    c = c_ref[...]
    ch, cl = _split2(_silu(c))
    wh, wl = _split2(w_ref[0])
    o_ref[0] = _dot(ch, wh) + _dot(ch, wl) + _dot(cl, wh) + b_ref[0]


def _modulation(c_rows, mod_w, mod_b):
    depth, d, n = mod_w.shape
    tn = n // 4
    return pl.pallas_call(
        _mod_kernel,
        out_shape=jax.ShapeDtypeStruct((depth, 16, n), F32),
        grid=(depth, n // tn),
        in_specs=[pl.BlockSpec((16, d), lambda i, j: (0, 0)),
                  pl.BlockSpec((1, d, tn), lambda i, j: (i, 0, j)),
                  pl.BlockSpec((1, 1, tn), lambda i, j: (i, 0, j))],
        out_specs=pl.BlockSpec((1, 16, tn), lambda i, j: (i, 0, j)),
        compiler_params=_cparams(("arbitrary", "arbitrary")),
        name="modulation",
    )(c_rows, mod_w, mod_b.reshape(depth, 1, n))


def _mod_row(t, n_lat_tiles, tiles_per_batch, n_batch):
    return jnp.where(t < n_lat_tiles, t // tiles_per_batch, n_batch)


def _modulated(x, gain, mod_ref, row, k_shift, k_scale):
    d = x.shape[-1]
    shift = mod_ref[pl.ds(row, 1), k_shift * d:(k_shift + 1) * d]
    scale = mod_ref[pl.ds(row, 1), k_scale * d:(k_scale + 1) * d]
    return _rms(x) * gain * (1.0 + scale) + shift


def _rope_tables(seq):
    rows = seq // GRID_W
    row = jnp.repeat(jnp.arange(rows, dtype=jnp.int32), GRID_W)
    col = jnp.tile(jnp.arange(GRID_W, dtype=jnp.int32), rows)
    inv_freq = ROPE_THETA ** (-jnp.arange(0, AXIS_DIM, 2, dtype=F32) / AXIS_DIM)
    ang = jnp.stack([row[:, None] * inv_freq, col[:, None] * inv_freq], axis=1)
    cos, sin = jnp.cos(ang), jnp.sin(ang)
    zero = jnp.zeros_like(sin)
    cos64 = jnp.concatenate([cos[:, 0], cos[:, 0], cos[:, 1], cos[:, 1]], axis=-1)
    sa64 = jnp.concatenate([-sin[:, 0], zero[:, 0], -sin[:, 1], zero[:, 1]], axis=-1)
    sb64 = jnp.concatenate([zero[:, 0], sin[:, 0], zero[:, 1], sin[:, 1]], axis=-1)

    def widen(t, fill):
        t = jnp.concatenate([t, t], axis=-1)
        return jnp.concatenate([t, jnp.full((TOK_TILE, LANES), fill, F32)], axis=0)

    return widen(cos64, 1.0), widen(sa64, 0.0), widen(sb64, 0.0)


def _head_sumsq(y, bd):
    w = y.shape[-1]
    outs = []
    for s in range(0, w, MXU_DIM):
        e = min(s + MXU_DIM, w)
        hi, lo = _split2(y[:, s:e])
        b = bd[0:e - s, 0:e - s]
        outs.append(_dot(hi, b) + _dot(lo, b))
    return outs[0] if len(outs) == 1 else jnp.concatenate(outs, axis=-1)


def _qk_norm_rope(z, gain, bd, cos, sa, sb):
    w = z.shape[-1]
    rep = w // LANES
    ss = _head_sumsq(z * z, bd)
    y = z * lax.rsqrt(ss * (1.0 / HEAD_DIM) + EPS) * gain

    def wide(t):
        return t if rep == 1 else jnp.concatenate([t] * rep, axis=-1)

    return (y * wide(cos) + pltpu.roll(y, w - AXIS_DIM // 2, 1) * wide(sa)
            + pltpu.roll(y, AXIS_DIM // 2, 1) * wide(sb))


def _kv_rep(kv128):
    lane = lax.broadcasted_iota(jnp.int32, kv128.shape, 1)
    sw = pltpu.roll(kv128, HEAD_DIM, 1)
    a0 = jnp.where(lane < HEAD_DIM, kv128, sw)
    a1 = jnp.where(lane < HEAD_DIM, sw, kv128)
    return jnp.concatenate([a0, a0, a1, a1], axis=-1)


def _rope_block(t, n_lat_tiles, tiles_per_batch):
    return jnp.where(t < n_lat_tiles, t % tiles_per_batch, tiles_per_batch)


EV_GQ, EV_GK, EV_GV, EV_GR, EV_AQ, EV_AK, EV_LR, EV_END = 0, 256, 512, 1024, 1536, 2048, 2176, 2304


def _proj_even_kernel(geom, x_ref, mod_ref, gain_ref, w_ref, wvt_ref, gw_ref, gb_ref, qg_ref, kg_ref,
                      cos_ref, sa_ref, sb_ref, bd_ref,
                      gq_ref, gk_ref, gv_ref, gr_ref, g_ref, aq_ref, ak_ref, avt_ref):
    t = pl.program_id(0)
    row = _mod_row(t, *geom)
    hb = _bf(_modulated(x_ref[...], gain_ref[...], mod_ref, row, 0, 1))

    def seg(a, b):
        return _dot(hb, w_ref[:, a:b])

    gq_ref[...] = seg(EV_GQ, EV_GK) * (GLA_DK ** -0.5)
    gk_ref[...] = seg(EV_GK, EV_GV)
    gv_ref[...] = _bf(seg(EV_GV, EV_GR))
    gr_ref[...] = seg(EV_GR, EV_AQ)
    zg = _dot(_bf(seg(EV_LR, EV_END)), gw_ref[...]) + gb_ref[...]
    g_ref[...] = -(jnp.maximum(-zg, 0.0) + jnp.log1p(jnp.exp(-jnp.abs(zg)))) * (1.0 / GLA_GATE_NORM)
    bd = bd_ref[...]
    cos, sa, sb = cos_ref[...], sa_ref[...], sb_ref[...]
    aq = _qk_norm_rope(seg(EV_AQ, EV_AK), qg_ref[...], bd, cos, sa, sb)
    aq_ref[...] = _bf(aq * Q_SCALE)
    ak = _qk_norm_rope(seg(EV_AK, EV_LR), kg_ref[...], bd, cos, sa, sb)
    ak_ref[...] = _bf(_kv_rep(ak))
    avt_ref[0] = _bf(_dot_nt(wvt_ref[...], hb))


def _vt_spec():
    return pl.BlockSpec((1, LANES, TOK_TILE), lambda t: (t, 0, 0))


def _proj_even(x_all, mod, gain, w, wvt, gw, gb, qg, kg, tables, bd, geom):
    n, d = x_all.shape
    n_lat_tiles, tiles_per_batch, _ = geom
    tt = TOK_TILE
    cos, sa, sb = tables
    tok = lambda w_: pl.BlockSpec((tt, w_), lambda t: (t, 0))
    rope = pl.BlockSpec((tt, LANES), lambda t: (_rope_block(t, n_lat_tiles, tiles_per_batch), 0))
    outs = [(256, F32), (256, F32), (512, BF16), (512, F32), (512, F32), (512, BF16), (512, BF16)]
    return pl.pallas_call(
        functools.partial(_proj_even_kernel, geom),
        out_shape=[jax.ShapeDtypeStruct((n, w_), dt) for w_, dt in outs]
        + [jax.ShapeDtypeStruct((n // tt, LANES, tt), BF16)],
        grid=(n // tt,),
        in_specs=[tok(d), _full(mod.shape), _full(gain.shape), _full(w.shape), _full(wvt.shape), _full(gw.shape),
                  _full(gb.shape), _full(qg.shape), _full(kg.shape), rope, rope, rope, _full(bd.shape)],
        out_specs=[tok(w_) for w_, _ in outs] + [_vt_spec()],
        compiler_params=_cparams(("arbitrary",)),
        name="proj_even",
    )(x_all, mod, gain, w, wvt, gw, gb, qg, kg, cos, sa, sb, bd)


def _proj_odd_kernel(geom, x_ref, pos_ref, pos_next_ref, fs_hbm, mod_prev_ref, mod_ref, gain_ref, w_ref, wvt_ref,
                     qg_ref, kg_ref, cos_ref, sa_ref, sb_ref, bd_ref, x1_ref, q_ref, k_ref, vt_ref, fbuf, fsem):
    t = pl.program_id(0)
    row = _mod_row(t, *geom)
    d = x_ref.shape[-1]
    gate = mod_prev_ref[pl.ds(row, 1), 5 * d:6 * d]
    f = _gather_tile(t, pl.num_programs(0), pos_ref, pos_next_ref, fs_hbm, fbuf, fsem)
    x1 = x_ref[...] + gate * f
    x1_ref[...] = x1
    hb = _bf(_modulated(x1, gain_ref[...], mod_ref, row, 0, 1))
    bd = bd_ref[...]
    cos, sa, sb = cos_ref[...], sa_ref[...], sb_ref[...]
    q = _qk_norm_rope(_dot(hb, w_ref[:, 0:d]), qg_ref[...], bd, cos, sa, sb)
    q_ref[...] = _bf(q * Q_SCALE)
    k = _qk_norm_rope(_dot(hb, w_ref[:, d:d + LANES]), kg_ref[...], bd, cos, sa, sb)
    k_ref[...] = _bf(_kv_rep(k))
    vt_ref[0] = _bf(_dot_nt(wvt_ref[...], hb))


def _proj_odd(x_all, f_sorted, pos, mod_prev, mod, gain, w, wvt, qg, kg, tables, bd, geom):
    n, d = x_all.shape
    n_lat_tiles, tiles_per_batch, _ = geom
    tt = TOK_TILE
    cos, sa, sb = tables
    tok = lambda w_: pl.BlockSpec((tt, w_), lambda t: (t, 0))
    rope = pl.BlockSpec((tt, LANES), lambda t: (_rope_block(t, n_lat_tiles, tiles_per_batch), 0))
    outs = [(d, F32), (d, BF16), (512, BF16)]
    return pl.pallas_call(
        functools.partial(_proj_odd_kernel, geom),
        out_shape=[jax.ShapeDtypeStruct((n, w_), dt) for w_, dt in outs]
        + [jax.ShapeDtypeStruct((n // tt, LANES, tt), BF16)],
        grid=(n // tt,),
        in_specs=[tok(d)] + _gather_specs(n // tt)
        + [_full(mod_prev.shape), _full(mod.shape), _full(gain.shape), _full(w.shape),
           _full(wvt.shape), _full(qg.shape), _full(kg.shape), rope, rope, rope, _full(bd.shape)],
        out_specs=[tok(w_) for w_, _ in outs] + [_vt_spec()],
        scratch_shapes=_gather_scratch(d),
        compiler_params=_cparams(("arbitrary",)),
        name="proj_odd",
    )(x_all, pos, pos, f_sorted, mod_prev, mod, gain, w, wvt, qg, kg, cos, sa, sb, bd)


N_LEVELS = 6
GLA_MX_ROWS = (N_LEVELS + 2) * GLA_CHUNK


def _gla_constants():
    c = GLA_CHUNK
    mx = np.zeros((2, GLA_MX_ROWS, c), np.float32)
    pat = np.zeros((2, N_LEVELS + 1, c, GLA_HEADS * c), np.float32)
    r = np.arange(c)
    for lvl in range(N_LEVELS):
        h = 1 << lvl
        ref = (r // (2 * h)) * 2 * h + h - 1
        upper = (r % (2 * h)) >= h
        m = np.zeros((c, c), np.float32)
        for i in range(c):
            if upper[i]:
                m[i, ref[i] + 1:i + 1] = 1.0
            else:
                m[i, i + 1:ref[i] + 1] = 1.0
        mx[0, lvl * c:(lvl + 1) * c] = m
        same = (r[:, None] // (2 * h)) == (r[None, :] // (2 * h))
        p = same & upper[:, None] & (~upper)[None, :]
        pat[0, lvl] = np.tile(p.astype(np.float32), (1, GLA_HEADS))
    mx[0, N_LEVELS * c:(N_LEVELS + 1) * c] = (r[None, :] <= r[:, None])
    mx[0, (N_LEVELS + 1) * c:(N_LEVELS + 2) * c] = (r[None, :] > r[:, None])
    pat[0, N_LEVELS] = np.tile(np.eye(c, dtype=np.float32), (1, GLA_HEADS))
    for k in range(N_LEVELS + 2):
        mx[1, k * c:(k + 1) * c] = mx[0, k * c:(k + 1) * c][::-1, ::-1]
    for k in range(N_LEVELS + 1):
        pat[1, k] = np.tile(pat[0, k, :, 0:c][::-1, ::-1], (1, GLA_HEADS))
    return mx, pat


GLA_GROUP = 2


def _gla_chunks(chunks, mx, pat_ref, s_ref):
    c = GLA_CHUNK
    lane_head = lax.broadcasted_iota(jnp.int32, (c, GLA_HEADS * GLA_DK), 1) // GLA_DK

    def stack_heads(a):
        return jnp.concatenate([_bf(jnp.where(lane_head == h, a, 0.0)) for h in range(GLA_HEADS)], axis=0)

    xs = [_dot(mx, jnp.concatenate(_split2(g), axis=0)) for _, _, _, g in chunks]
    atts = [jnp.where(pat_ref[0, N_LEVELS] > 0.0, _dot_nt(_bf(q), stack_heads(k)), 0.0)
            for q, k, _, _ in chunks]
    for lvl in range(N_LEVELS):
        for j, (q, k, _, _) in enumerate(chunks):
            e = jnp.exp(xs[j][lvl * c:(lvl + 1) * c])
            atts[j] = atts[j] + jnp.where(pat_ref[0, lvl] > 0.0, _dot_nt(_bf(q * e), stack_heads(k * e)), 0.0)
    outs, qes, news, a_cols = [], [], [], []
    for j, (q, k, v, _) in enumerate(chunks):
        vhead = lax.broadcasted_iota(jnp.int32, v.shape, 1) // GLA_DV
        v_bd = jnp.concatenate([jnp.where(vhead == h, v, jnp.zeros_like(v)) for h in range(GLA_HEADS)], axis=0)
        outs.append(_dot(_bf(atts[j]), v_bd))
        bcum = xs[j][N_LEVELS * c:(N_LEVELS + 1) * c]
        brem = xs[j][(N_LEVELS + 1) * c:(N_LEVELS + 2) * c]
        qes.append(stack_heads(q * jnp.exp(bcum)))
        kt = jnp.transpose(k * jnp.exp(brem))
        news.append(jnp.concatenate(
            [_dot(_bf(kt[h * GLA_DK:(h + 1) * GLA_DK]), v[:, h * GLA_DV:(h + 1) * GLA_DV])
             for h in range(GLA_HEADS)], axis=0))
        tot = bcum[0:1] + brem[0:1]
        a_cols.append(jnp.transpose(jnp.exp(jnp.broadcast_to(tot, (8, tot.shape[1]))))[:, 0:1])
    s = s_ref[...]
    for j in range(len(chunks)):
        o_inter = _dot(qes[j], _bf(s))
        outs[j] = outs[j] + jnp.concatenate([o_inter[h * c:(h + 1) * c] for h in range(GLA_HEADS)], axis=-1)
        s = a_cols[j] * s + news[j]
    s_ref[...] = s
    return outs


def _gla_kernel(n_lat_chunks, n_ctx_chunks, ql_ref, kl_ref, vl_ref, gl_ref, qc_ref, kc_ref, vc_ref, gc_ref,
                mx_ref, pat_ref, ol_ref, oc_ref, s_ref):
    d = pl.program_id(1)
    c = GLA_CHUNK
    s_ref[...] = jnp.zeros_like(s_ref)
    mx = mx_ref[0]

    def run(n_chunks, q_ref, k_ref, v_ref, g_ref, o_ref):
        def body(i, carry):
            rows = []
            for j in range(GLA_GROUP):
                step = i * GLA_GROUP + j
                ci = jnp.where(d == 0, step, n_chunks - 1 - step)
                rows.append(pl.ds(pl.multiple_of(ci * c, c), c))
            outs = _gla_chunks([(q_ref[r, :], k_ref[r, :], v_ref[r, :], g_ref[r, :]) for r in rows],
                               mx, pat_ref, s_ref)
            for r, o in zip(rows, outs):
                o_ref[0, r, :] = o
            return carry
        lax.fori_loop(0, n_chunks // GLA_GROUP, body, 0)

    run(n_ctx_chunks, qc_ref, kc_ref, vc_ref, gc_ref, oc_ref)
    run(n_lat_chunks, ql_ref, kl_ref, vl_ref, gl_ref, ol_ref)


def _gla(gq, gk, gv, g, n_batch, seq, lc):
    n = gq.shape[0]
    mx_np, pat_np = _gla_constants()
    mx = jnp.asarray(np.concatenate([mx_np, mx_np], axis=2), BF16)
    pat = jnp.asarray(pat_np, F32)
    ctx0 = n_batch * seq // lc
    hk, hv = GLA_HEADS * GLA_DK, GLA_HEADS * GLA_DV
    lat = lambda w_, col: pl.BlockSpec((seq, w_), lambda b, d_: (b, col(d_)))
    ctx = lambda w_, col: pl.BlockSpec((lc, w_), lambda b, d_: (ctx0 + b, col(d_)))
    zero = lambda d_: 0
    same = lambda d_: d_
    o_lat, o_ctx = pl.pallas_call(
        functools.partial(_gla_kernel, seq // GLA_CHUNK, lc // GLA_CHUNK),
        out_shape=[jax.ShapeDtypeStruct((2, n_batch * seq, hv), F32),
                   jax.ShapeDtypeStruct((2, n_batch * lc, hv), F32)],
        grid=(n_batch, 2),
        in_specs=[lat(hk, zero), lat(hk, zero), lat(hv, zero), lat(hk, same),
                  ctx(hk, zero), ctx(hk, zero), ctx(hv, zero), ctx(hk, same),
                  pl.BlockSpec((1,) + mx.shape[1:], lambda b, d_: (d_, 0, 0)),
                  pl.BlockSpec((1,) + pat.shape[1:], lambda b, d_: (d_, 0, 0, 0))],
        out_specs=[pl.BlockSpec((1, seq, hv), lambda b, d_: (d_, b, 0)),
                   pl.BlockSpec((1, lc, hv), lambda b, d_: (d_, b, 0))],
        scratch_shapes=[pltpu.VMEM((GLA_HEADS * GLA_DK, GLA_DV), F32)],
        compiler_params=_cparams(("arbitrary", "arbitrary")),
        name="gla_scan",
    )(gq, gk, gv, g, gq, gk, gv, g, mx, pat)
    return jnp.concatenate([o_lat, o_ctx], axis=1)


def _stack_heads(q):
    lane_head = lax.broadcasted_iota(jnp.int32, q.shape, 1) // HEAD_DIM
    return jnp.concatenate([jnp.where(lane_head == h, q, jnp.zeros_like(q)) for h in range(4)], axis=0)


def _attn_store(acc, l, o_ref, tq):
    out = acc * (1.0 / l)
    for h in range(4):
        o_ref[h * HEAD_DIM:(h + 1) * HEAD_DIM, :] = _bf(out[:, h * tq:(h + 1) * tq])


def _attn_dense_kernel(n_chunks, *refs):
    if n_chunks:
        q_ref, kl_ref, vtl_ref, kc_ref, vtc_ref, o_ref = refs
    else:
        q_ref, kc_ref, vtc_ref, o_ref = refs
    tq = q_ref.shape[0]
    cols = 4 * tq
    q4 = _stack_heads(q_ref[...])

    def scores(c):
        if c < n_chunks:
            return _dot_nt(kl_ref[c * ATT_K_CHUNK:(c + 1) * ATT_K_CHUNK, :], q4)
        return _dot_nt(kc_ref[...], q4)

    def update(carry, st, vt):
        m, l, acc = carry
        m_new = jnp.maximum(m, jnp.max(st, axis=0, keepdims=True))
        alpha = jnp.exp2(m - m_new)
        p = jnp.exp2(st - m_new)
        l = alpha * l + jnp.sum(p, axis=0, keepdims=True)
        acc = alpha * acc + _dot(vt, _bf(p))
        return m_new, l, acc

    carry = (jnp.full((1, cols), NEG_BIG, F32), jnp.zeros((1, cols), F32), jnp.zeros((HEAD_DIM, cols), F32))
    st = scores(0)
    for c in range(n_chunks + 1):
        st_next = scores(c + 1) if c < n_chunks else None
        carry = update(carry, st, vtl_ref[c] if c < n_chunks else vtc_ref[0])
        st = st_next
    m, l, acc = carry
    _attn_store(acc, l, o_ref, tq)


def _attn_window_kernel(seq, q_ref, *refs):
    nk = WIN_SUB + 2
    k_refs, v_refs = refs[0:nk], refs[nk:2 * nk]
    kc_ref, vtc_ref, sink_ref, o_ref = refs[2 * nk:]
    tq = SWA_WINDOW
    i = pl.program_id(2)
    kb = jnp.concatenate([r[...] for r in k_refs], axis=0)
    vtb = jnp.concatenate([r[0] for r in v_refs], axis=1)
    kc, vtc, sink = kc_ref[...], vtc_ref[0], sink_ref[0]
    span = 3 * tq
    for u in range(WIN_SUB):
        q4 = _stack_heads(q_ref[u * tq:(u + 1) * tq, :])
        first = i * WIN_SUB + u - 1
        kpos = first * tq + lax.broadcasted_iota(jnp.int32, (span, tq), 0)
        qpos = (first + 1) * tq + lax.broadcasted_iota(jnp.int32, (span, tq), 1)
        ok = (kpos >= 0) & (kpos < seq) & (jnp.abs(kpos - qpos) <= SWA_WINDOW)
        bias = jnp.where(ok, 0.0, NEG_BIG)
        sb = _dot_nt(kb[u * tq:u * tq + span], q4) + jnp.concatenate([bias] * 4, axis=1)
        sc = _dot_nt(kc, q4)
        m = jnp.maximum(jnp.maximum(jnp.max(sb, axis=0, keepdims=True), jnp.max(sc, axis=0, keepdims=True)), sink)
        pb = jnp.exp2(sb - m)
        pc = jnp.exp2(sc - m)
        l = jnp.sum(pb, axis=0, keepdims=True) + jnp.sum(pc, axis=0, keepdims=True) + jnp.exp2(sink - m)
        acc = _dot(vtb[:, u * tq:u * tq + span], _bf(pb)) + _dot(vtc, _bf(pc))
        out = acc * (1.0 / l)
        for h in range(4):
            o_ref[h * HEAD_DIM:(h + 1) * HEAD_DIM, u * tq:(u + 1) * tq] = _bf(out[:, h * tq:(h + 1) * tq])


def _attention(mode, q, k_rep, vt, n_batch, seq, lc, n_heads, n_kv, sink=None):
    ncol = n_heads * HEAD_DIM // ATT_COL
    col_per_kv = ncol // n_kv
    tq = ATT_Q_TILE
    tpb = seq // TOK_TILE
    n_lat_tiles = n_batch * tpb
    ctx_per_tile = TOK_TILE // lc
    ctx0 = n_batch * seq // lc
    assert ATT_K_CHUNK == TOK_TILE and TOK_TILE % lc == 0 and tq == SWA_WINDOW
    kv = lambda j: j // col_per_kv
    k_ctx = pl.BlockSpec((lc, ATT_COL), lambda b, j, i: (ctx0 + b, kv(j)))
    vt_ctx = pl.BlockSpec((1, HEAD_DIM, lc), lambda b, j, i: (n_lat_tiles + b // ctx_per_tile, kv(j), b % ctx_per_tile))
    if mode == "ctx":
        nq = lc // tq
        q0 = n_batch * seq // tq
        kern = functools.partial(_attn_dense_kernel, 0)
        args = (q, k_rep, vt)
        in_specs = [pl.BlockSpec((tq, ATT_COL), lambda b, j, i: (q0 + b * nq + i, j)), k_ctx, vt_ctx]
    elif mode == "dense":
        nq = seq // tq
        kern = functools.partial(_attn_dense_kernel, tpb)
        args = (q, k_rep, vt, k_rep, vt)
        in_specs = [pl.BlockSpec((tq, ATT_COL), lambda b, j, i: (b * nq + i, j)),
                    pl.BlockSpec((seq, ATT_COL), lambda b, j, i: (b, kv(j))),
                    pl.BlockSpec((tpb, HEAD_DIM, TOK_TILE), lambda b, j, i: (b, kv(j), 0)), k_ctx, vt_ctx]
    else:
        wb = SWA_WINDOW
        tq = WIN_SUB * wb
        nq = seq // tq
        nkb = seq // wb
        per_tile = TOK_TILE // wb
        kern = functools.partial(_attn_window_kernel, seq)
        nb = lambda i, o: jnp.clip(i * WIN_SUB + o, 0, nkb - 1)
        k_nb = lambda o: pl.BlockSpec((wb, ATT_COL), lambda b, j, i: (b * nkb + nb(i, o), kv(j)))
        v_nb = lambda o: pl.BlockSpec(
            (1, HEAD_DIM, wb), lambda b, j, i: (b * tpb + nb(i, o) // per_tile, kv(j), nb(i, o) % per_tile))
        offs = range(-1, WIN_SUB + 1)
        sink_row = jnp.repeat(sink.reshape(ncol, 1, 4), wb, axis=2) * LOG2E
        args = (q,) + (k_rep,) * len(offs) + (vt,) * len(offs) + (k_rep, vt, sink_row)
        in_specs = ([pl.BlockSpec((tq, ATT_COL), lambda b, j, i: (b * nq + i, j))]
                    + [k_nb(o) for o in offs] + [v_nb(o) for o in offs]
                    + [k_ctx, vt_ctx, pl.BlockSpec((1, 1, 4 * wb), lambda b, j, i: (j, 0, 0))])
    return pl.pallas_call(
        kern,
        out_shape=jax.ShapeDtypeStruct((n_heads * HEAD_DIM, n_batch * nq * tq), BF16),
        grid=(n_batch, ncol, nq),
        in_specs=in_specs,
        out_specs=pl.BlockSpec((ATT_COL, tq), lambda b, j, i: (j, b * nq + i)),
        compiler_params=_cparams(("arbitrary", "arbitrary", "arbitrary")),
        name="attention_" + mode,
    )(*args)


def _route(h, wr_hi_ref, wr_lo_ref, br_ref):
    hh, hl = _split2(h)
    logits = _dot(hh, wr_hi_ref[...]) + _dot(hh, wr_lo_ref[...]) + _dot(hl, wr_hi_ref[...]) + br_ref[...]
    col = lambda i: logits[:, i:i + 1]
    gl = [col(i) for i in range(MOE_GROUPS)]
    gmax = functools.reduce(jnp.maximum, gl)
    gi = jnp.where(gl[0] == gmax, 0, jnp.where(gl[1] == gmax, 1, jnp.where(gl[2] == gmax, 2, 3)))
    g_weight = 1.0 / functools.reduce(lambda a, b: a + b, [jnp.exp(x - gmax) for x in gl])
    el = []
    for j in range(MOE_EPG):
        cand = [col(MOE_GROUPS + g * MOE_EPG + j) for g in range(MOE_GROUPS)]
        el.append(jnp.where(gi == 0, cand[0], jnp.where(gi == 1, cand[1], jnp.where(gi == 2, cand[2], cand[3]))))
    m1 = functools.reduce(jnp.maximum, el)
    i1 = jnp.where(el[0] == m1, 0, jnp.where(el[1] == m1, 1, jnp.where(el[2] == m1, 2, 3)))
    rest = [jnp.where(i1 == j, -jnp.inf, el[j]) for j in range(MOE_EPG)]
    m2 = functools.reduce(jnp.maximum, rest)
    i2 = jnp.where(rest[0] == m2, 0, jnp.where(rest[1] == m2, 1, jnp.where(rest[2] == m2, 2, 3)))
    e2 = jnp.exp(m2 - m1)
    w1 = g_weight / (1.0 + e2)
    w2 = g_weight * e2 / (1.0 + e2)
    lo = jnp.minimum(i1, i2)
    hi = jnp.maximum(i1, i2)
    w_lo = jnp.where(i1 == lo, w1, w2)
    w_hi = jnp.where(i1 == lo, w2, w1)
    pair = jnp.where(lo == 0, hi - 1, jnp.where(lo == 1, hi + 1, N_PAIRS - 1))
    return w_lo, w_hi, gi * N_PAIRS + pair


def _out_tail(geom, m, x_ref, mod_ref, gain_ffn_ref, wr_hi_ref, wr_lo_ref, br_ref, tri_ref,
              x_new_ref, hrow_ref, counts_ref, run_ref):
    t = pl.program_id(0)
    row = _mod_row(t, *geom)
    d = x_ref.shape[-1]

    @pl.when(t == 0)
    def _():
        run_ref[...] = jnp.zeros_like(run_ref)

    x_new = x_ref[...] + mod_ref[pl.ds(row, 1), 2 * d:3 * d] * m
    x_new_ref[...] = x_new
    h = _modulated(x_new, gain_ffn_ref[...], mod_ref, row, 3, 4)
    hrow_ref[:, 0:d] = h
    w_lo, w_hi, bucket = _route(h, wr_hi_ref, wr_lo_ref, br_ref)
    lane = lax.broadcasted_iota(jnp.int32, (h.shape[0], META_LANES), 1)
    onehot = lane == bucket
    ones = jnp.where(onehot, 1.0, 0.0)
    before = _dot(tri_ref[...], _bf(ones)) + run_ref[...]
    rank = jnp.sum(jnp.where(onehot, before, 0.0), axis=-1, keepdims=True)
    run_ref[...] = run_ref[...] + jnp.sum(ones, axis=0, keepdims=True)
    counts_ref[...] = run_ref[...]
    hrow_ref[:, d:d + META_LANES] = jnp.where(
        lane == 0, w_lo, jnp.where(lane == 1, w_hi, jnp.where(lane == 2, bucket.astype(F32),
                                                             jnp.where(lane == 3, rank, 0.0))))


def _out_even_kernel(geom, o_ref, r_ref, att_ref, gn_ref, w_ref, x_ref, mod_ref, gain_ffn_ref,
                     wr_hi_ref, wr_lo_ref, br_ref, tri_ref, x_new_ref, hrow_ref, counts_ref, run_ref):
    o = o_ref[0] + o_ref[1]
    r = r_ref[...]
    parts = []
    for h in range(GLA_HEADS):
        sl = slice(h * GLA_DV, (h + 1) * GLA_DV)
        parts.append(_rms(o[:, sl]) * gn_ref[...] * _silu(r[:, sl]))
    a = _bf(jnp.concatenate(parts, axis=-1))
    half = a.shape[-1]
    m = _dot(a, w_ref[0:half, :]) + _dot_tn(att_ref[...], w_ref[half:, :])
    _out_tail(geom, m, x_ref, mod_ref, gain_ffn_ref, wr_hi_ref, wr_lo_ref, br_ref, tri_ref,
              x_new_ref, hrow_ref, counts_ref, run_ref)


def _out_odd_kernel(geom, att_ref, w_ref, x_ref, mod_ref, gain_ffn_ref,
                    wr_hi_ref, wr_lo_ref, br_ref, tri_ref, x_new_ref, hrow_ref, counts_ref, run_ref):
    m = _dot_tn(att_ref[...], w_ref[...])
    _out_tail(geom, m, x_ref, mod_ref, gain_ffn_ref, wr_hi_ref, wr_lo_ref, br_ref, tri_ref,
              x_new_ref, hrow_ref, counts_ref, run_ref)


def _out_proj(kernel, mix_args, mix_specs, w, x_all, mod, gain_ffn, wr_hi, wr_lo, br, n_rows, name):
    d = x_all.shape[-1]
    tt = TOK_TILE
    tok = lambda w_: pl.BlockSpec((tt, w_), lambda t: (t, 0))
    r = np.arange(tt)
    tri = jnp.asarray((r[None, :] < r[:, None]).astype(np.float32), BF16)
    return pl.pallas_call(
        kernel,
        out_shape=[jax.ShapeDtypeStruct((n_rows, d), F32), jax.ShapeDtypeStruct((n_rows, d + META_LANES), F32),
                   jax.ShapeDtypeStruct((1, META_LANES), F32)],
        grid=(n_rows // tt,),
        in_specs=mix_specs + [_full(w.shape), tok(d), _full(mod.shape), _full(gain_ffn.shape),
                              _full(wr_hi.shape), _full(wr_lo.shape), _full(br.shape), _full(tri.shape)],
        out_specs=[tok(d), tok(d + META_LANES), _full((1, META_LANES))],
        scratch_shapes=[pltpu.VMEM((1, META_LANES), F32)],
        compiler_params=_cparams(("arbitrary",)),
        name=name,
    )(*mix_args, w, x_all, mod, gain_ffn, wr_hi, wr_lo, br, tri)


ROW_DMA_UNROLL = 8


def _dispatch_kernel(pos_ref, x_ref, xs_init_ref, xs_ref, sem):
    del xs_init_ref
    tt = x_ref.shape[0]

    def body(r, c):
        pltpu.make_async_copy(x_ref.at[pl.ds(r, 1)], xs_ref.at[pl.ds(pos_ref[0, 0, r], 1)], sem).start()
        return c

    lax.fori_loop(0, tt, body, 0, unroll=ROW_DMA_UNROLL)
    pltpu.make_async_copy(x_ref, xs_ref.at[pl.ds(0, tt)], sem).wait()


def _moe_mlp_kernel(tlo_ref, thi_ref, nused_ref, xs_ref, wg_lo, wu_lo, wd_lo, wg_hi, wu_hi, wd_hi, f_ref):
    del tlo_ref, thi_ref
    d = f_ref.shape[1]
    used = pl.program_id(0) < nused_ref[0]

    @pl.when(used)
    def _():
        xb = _bf(xs_ref[:, 0:d])
        w_lo = xs_ref[:, d:d + 1]
        w_hi = xs_ref[:, d + 1:d + 2]
        hid_lo = _silu(_dot(xb, wg_lo[0])) * _dot(xb, wu_lo[0]) * w_lo
        hid_hi = _silu(_dot(xb, wg_hi[0])) * _dot(xb, wu_hi[0]) * w_hi
        f_ref[...] = _dot(_bf(hid_lo), wd_lo[0]) + _dot(_bf(hid_hi), wd_hi[0])

    @pl.when(jnp.logical_not(used))
    def _():
        f_ref[...] = jnp.zeros_like(f_ref)


def _moe(hrow, counts, w_gate, w_up, w_down):
    n, dw = hrow.shape
    d = dw - META_LANES
    tm, tt = MOE_TILE, TOK_TILE
    n_tiles = n // tm + N_BUCKETS
    p = n_tiles * tm
    cnt = counts[0, 0:N_BUCKETS].astype(jnp.int32)
    tiles_b = (cnt + tm - 1) // tm
    tile_end = jnp.cumsum(tiles_b)
    start_b = (tile_end - tiles_b) * tm
    bucket = hrow[:, d + 2].astype(jnp.int32)
    rank = hrow[:, d + 3].astype(jnp.int32)
    pos = (jnp.take(start_b, bucket) + rank).reshape(n // tt, 1, tt)
    n_used = tile_end[-1].reshape(1)
    tile_ids = jnp.arange(n_tiles, dtype=jnp.int32)
    tile_bucket = jnp.sum((tile_ids[:, None] >= tile_end[None, :]).astype(jnp.int32), axis=1)
    tile_bucket = jnp.minimum(tile_bucket, jnp.sum((n_used[0] - 1 >= tile_end).astype(jnp.int32)))
    tile_bucket = jnp.minimum(tile_bucket, N_BUCKETS - 1)
    pair_lo = jnp.asarray([0, 0, 0, 1, 1, 2], jnp.int32)
    pair_hi = jnp.asarray([1, 2, 3, 2, 3, 3], jnp.int32)
    grp = tile_bucket // N_PAIRS
    t_lo = grp * MOE_EPG + pair_lo[tile_bucket % N_PAIRS]
    t_hi = grp * MOE_EPG + pair_hi[tile_bucket % N_PAIRS]

    xs = pl.pallas_call(
        _dispatch_kernel,
        out_shape=jax.ShapeDtypeStruct((p, dw), F32),
        grid=(n // tt,),
        in_specs=[pl.BlockSpec((1, 1, tt), lambda t: (t, 0, 0), memory_space=pltpu.SMEM),
                  pl.BlockSpec((tt, dw), lambda t: (t, 0)),
                  pl.BlockSpec(memory_space=pl.ANY)],
        out_specs=pl.BlockSpec(memory_space=pl.ANY),
        scratch_shapes=[pltpu.SemaphoreType.DMA(())],
        input_output_aliases={2: 0},
        compiler_params=_cparams(("arbitrary",)),
        name="moe_dispatch",
    )(pos, hrow, jnp.zeros((p, dw), F32))

    f = D_EXPERT
    up_lo = pl.BlockSpec((1, d, f), lambda t, lo, hi, nu: (lo[t], 0, 0))
    up_hi = pl.BlockSpec((1, d, f), lambda t, lo, hi, nu: (hi[t], 0, 0))
    dn_lo = pl.BlockSpec((1, f, d), lambda t, lo, hi, nu: (lo[t], 0, 0))
    dn_hi = pl.BlockSpec((1, f, d), lambda t, lo, hi, nu: (hi[t], 0, 0))
    grid_spec = pltpu.PrefetchScalarGridSpec(
        num_scalar_prefetch=3,
        grid=(n_tiles,),
        in_specs=[pl.BlockSpec((tm, dw), lambda t, *_: (t, 0)), up_lo, up_lo, dn_lo, up_hi, up_hi, dn_hi],
        out_specs=pl.BlockSpec((tm, d), lambda t, *_: (t, 0)),
    )
    f_sorted = pl.pallas_call(
        _moe_mlp_kernel,
        out_shape=jax.ShapeDtypeStruct((p, d), F32),
        grid_spec=grid_spec,
        compiler_params=_cparams(("arbitrary",)),
        name="moe_experts",
    )(t_lo, t_hi, n_used, xs, w_gate, w_up, w_down, w_gate, w_up, w_down)
    return f_sorted, pos


def _gather_tile(t, n_t, pos_ref, pos_next_ref, src_hbm, buf, sem):
    tt = buf.shape[1]

    def start(p_ref, slot):
        def body(r, c):
            pltpu.make_async_copy(src_hbm.at[pl.ds(p_ref[0, 0, r], 1)], buf.at[slot].at[pl.ds(r, 1)],
                                  sem.at[slot]).start()
            return c
        lax.fori_loop(0, tt, body, 0, unroll=ROW_DMA_UNROLL)

    slot = t % 2

    @pl.when(t == 0)
    def _():
        start(pos_ref, 0)

    pltpu.make_async_copy(src_hbm.at[pl.ds(0, tt)], buf.at[slot], sem.at[slot]).wait()

    @pl.when(t + 1 < n_t)
    def _():
        start(pos_next_ref, 1 - slot)

    return buf[slot]


def _gather_specs(n_t):
    tt = TOK_TILE
    return [pl.BlockSpec((1, 1, tt), lambda t: (t, 0, 0), memory_space=pltpu.SMEM),
            pl.BlockSpec((1, 1, tt), lambda t: (jnp.minimum(t + 1, n_t - 1), 0, 0), memory_space=pltpu.SMEM),
            pl.BlockSpec(memory_space=pl.ANY)]


def _gather_scratch(d):
    return [pltpu.VMEM((2, TOK_TILE, d), F32), pltpu.SemaphoreType.DMA((2,))]


def _final_kernel(tiles_per_batch, x_ref, pos_ref, pos_next_ref, fs_hbm, mod_ref, o_ref, fbuf, fsem):
    t = pl.program_id(0)
    row = t // tiles_per_batch
    d = x_ref.shape[-1]
    f = _gather_tile(t, pl.num_programs(0), pos_ref, pos_next_ref, fs_hbm, fbuf, fsem)
    o_ref[...] = x_ref[...] + mod_ref[pl.ds(row, 1), 5 * d:6 * d] * f


def _final(x_lat, f_sorted, pos, mod, tiles_per_batch):
    n, d = x_lat.shape
    tok = pl.BlockSpec((TOK_TILE, d), lambda t: (t, 0))
    return pl.pallas_call(
        functools.partial(_final_kernel, tiles_per_batch),
        out_shape=jax.ShapeDtypeStruct((n, d), F32),
        grid=(n // TOK_TILE,),
        in_specs=[tok] + _gather_specs(n // TOK_TILE) + [_full(mod.shape)],
        out_specs=tok,
        scratch_shapes=_gather_scratch(d),
        compiler_params=_cparams(("arbitrary",)),
        name="final_residual",
    )(x_lat, pos, pos, f_sorted, mod)


def _block_diag_ones():
    r = np.arange(MXU_DIM) // HEAD_DIM
    return jnp.asarray((r[:, None] == r[None, :]).astype(np.float32), BF16)


def _router_weights(wg, bg, we, be):
    d = wg.shape[0]
    n = MOE_GROUPS + MOE_GROUPS * MOE_EPG
    w = jnp.concatenate([wg, we, jnp.zeros((d, LANES - n), F32)], axis=1)
    b = jnp.concatenate([bg, be, jnp.zeros((LANES - n,), F32)]).reshape(1, LANES)
    hi = _bf(w)
    return hi, _bf(w - hi.astype(F32)), b


def _expert_weights(w_gate, w_up, w_down):
    g, e, d, f = w_gate.shape
    return (_bf(w_gate).reshape(g * e, d, f), _bf(w_up).reshape(g * e, d, f), _bf(w_down).reshape(g * e, f, d))


def kernel(x, c, ctx, c_ctx, mod_w, mod_b, norm_mix, norm_ffn, ev_w_in, ev_w_out, gla_gate_w, gla_gate_b,
           gla_out_norm, att_q_norm, att_k_norm, od_w_in, od_w_out, swa_sink, swa_q_norm, swa_k_norm,
           router_group_w, router_group_b, router_expert_w, router_expert_b, exp_w_gate, exp_w_up, exp_w_down):
    n_batch, seq, d = x.shape
    lc = ctx.shape[1]
    depth = mod_w.shape[0]
    n_lat = n_batch * seq
    tiles_per_batch = seq // TOK_TILE
    geom = (n_lat // TOK_TILE, tiles_per_batch, n_batch)
    assert depth == 2 and seq % TOK_TILE == 0 and (n_batch * lc) % TOK_TILE == 0 and n_batch < 16

    x_all = jnp.concatenate([x.reshape(n_lat, d), ctx.reshape(n_batch * lc, d)], axis=0)
    c_rows = jnp.zeros((16, d), F32).at[:n_batch].set(c).at[n_batch].set(c_ctx)
    mod = _modulation(c_rows, mod_w, mod_b)
    tables = _rope_tables(seq)
    bd = _block_diag_ones()
    row2 = lambda v: v.reshape(1, -1)
    tile_gain = lambda gvec, reps: jnp.tile(gvec, reps).reshape(1, -1)

    w0 = ev_w_in[0]
    seg = np.cumsum([0, 256, 256, 512, 512, 32, 512, 128, 128])
    cols = lambda i: w0[:, seg[i]:seg[i + 1]]
    w_even = _bf(jnp.concatenate([cols(0), cols(1), cols(2), cols(3), cols(5), cols(6), cols(4),
                                  jnp.zeros((d, EV_END - EV_LR - 2 * GLA_GATE_RANK), F32)], axis=1))
    wvt_even = _bf(cols(7).T)
    hk = GLA_HEADS * GLA_DK
    gw = jnp.zeros((LANES, 2 * hk), F32)
    gw = gw.at[0:GLA_GATE_RANK, 0:hk].set(gla_gate_w[0, 0])
    gw = gw.at[GLA_GATE_RANK:2 * GLA_GATE_RANK, hk:2 * hk].set(gla_gate_w[0, 1])
    gb = gla_gate_b[0].reshape(1, 2 * hk)
    gq, gk, gv, gr, g, aq, ak, avt = _proj_even(
        x_all, mod[0], row2(norm_mix[0]), w_even, wvt_even, _bf(gw), gb,
        tile_gain(att_q_norm[0], ATT_HEADS), tile_gain(att_k_norm[0], ATT_KV_HEADS), tables, bd, geom)
    o_gla = _gla(gq, gk, gv, g, n_batch, seq, lc)
    att_lat = _attention("dense", aq, ak, avt, n_batch, seq, lc, ATT_HEADS, ATT_KV_HEADS)
    att_ctx = _attention("ctx", aq, ak, avt, n_batch, seq, lc, ATT_HEADS, ATT_KV_HEADS)
    att = jnp.concatenate([att_lat, att_ctx], axis=1)
    wr_hi, wr_lo, br = _router_weights(router_group_w[0], router_group_b[0], router_expert_w[0], router_expert_b[0])
    tt = TOK_TILE
    n_all = x_all.shape[0]
    gn = row2(gla_out_norm[0])
    x_mid, hrow, counts = _out_proj(
        functools.partial(_out_even_kernel, geom),
        (o_gla, gr, att, gn),
        [pl.BlockSpec((2, tt, GLA_HEADS * GLA_DV), lambda t: (0, t, 0)),
         pl.BlockSpec((tt, GLA_HEADS * GLA_DV), lambda t: (t, 0)),
         pl.BlockSpec((ATT_HEADS * HEAD_DIM, tt), lambda t: (0, t)),
         _full(gn.shape)],
        _bf(ev_w_out[0]), x_all, mod[0], row2(norm_ffn[0]), wr_hi, wr_lo, br, x_all.shape[0], "out_even")
    f0, pos0 = _moe(hrow, counts, *_expert_weights(exp_w_gate[0], exp_w_up[0], exp_w_down[0]))

    w_odd = od_w_in[0]
    x1, q1, k1, v1t = _proj_odd(
        x_mid, f0, pos0, mod[0], mod[1], row2(norm_mix[1]), _bf(w_odd[:, 0:d + LANES]), _bf(w_odd[:, d + LANES:].T),
        tile_gain(swa_q_norm[0], SWA_HEADS), tile_gain(swa_k_norm[0], SWA_KV_HEADS), tables, bd, geom)
    att1 = _attention("window", q1, k1, v1t, n_batch, seq, lc, SWA_HEADS, SWA_KV_HEADS, sink=swa_sink[0])
    wr_hi, wr_lo, br = _router_weights(router_group_w[1], router_group_b[1], router_expert_w[1], router_expert_b[1])
    x2, hrow1, counts1 = _out_proj(
        functools.partial(_out_odd_kernel, geom), (att1,),
        [pl.BlockSpec((SWA_HEADS * HEAD_DIM, tt), lambda t: (0, t))],
        _bf(od_w_out[0]), x1, mod[1], row2(norm_ffn[1]), wr_hi, wr_lo, br, n_lat, "out_odd")
    f1, pos1 = _moe(hrow1, counts1, *_expert_weights(exp_w_gate[1], exp_w_up[1], exp_w_down[1]))
    out = _final(x2, f1, pos1, mod[1], tiles_per_batch)
    return out.reshape(n_batch, seq, d)
```

```python
import functools

import numpy as np
import jax
import jax.numpy as jnp
from jax import lax
from jax.experimental import pallas as pl
from jax.experimental.pallas import tpu as pltpu

F32 = jnp.float32
BF16 = jnp.bfloat16

GRID_W = 64
HEAD_DIM = 64
AXIS_DIM = HEAD_DIM // 2
ROPE_THETA = 10000.0
EPS = 1e-6
N_MOD = 6
GLA_HEADS = 4
GLA_DK = 64
GLA_DV = 128
GLA_GATE_RANK = 16
GLA_GATE_NORM = 16.0
GLA_CHUNK = 64
ATT_HEADS = 8
ATT_KV_HEADS = 2
SWA_HEADS = 16
SWA_KV_HEADS = 2
SWA_WINDOW = 128
MOE_GROUPS = 4
MOE_EPG = 4
D_EXPERT = 256
N_PAIRS = 6
N_BUCKETS = MOE_GROUPS * N_PAIRS

LANES = 128
MXU_DIM = 256
TOK_TILE = 512
ATT_Q_TILE = 128
ATT_K_CHUNK = 512
ATT_COL = 4 * HEAD_DIM
WIN_SUB = 2
DENSE_SUB = 2
MOE_TILE = 256
META_LANES = LANES
VMEM_LIMIT = 56 * 1024 * 1024
NEG_BIG = -1e30
LOG2E = 1.4426950408889634
Q_SCALE = HEAD_DIM ** -0.5 * LOG2E


def _bf(x):
    return x.astype(BF16)


def _split2(x):
    hi = _bf(x)
    lo = _bf(x - hi.astype(F32))
    return hi, lo


def _dot(a, b):
    return jnp.dot(a, b, preferred_element_type=F32)


def _dot_nt(a, b):
    return lax.dot_general(a, b, (((1,), (1,)), ((), ())), preferred_element_type=F32)


def _dot_tn(a, b):
    return lax.dot_general(a, b, (((0,), (0,)), ((), ())), preferred_element_type=F32)


def _silu(x):
    return x / (1.0 + jnp.exp(-x))


def _rms(x):
    return x * lax.rsqrt(jnp.mean(x * x, axis=-1, keepdims=True) + EPS)


def _cparams(sem):
    return pltpu.CompilerParams(dimension_semantics=sem, vmem_limit_bytes=VMEM_LIMIT)


def _full(shape):
    n = len(shape)
    return pl.BlockSpec(shape, lambda *_: (0,) * n)


def _mod_kernel(c_ref, w_ref, b_ref, o_ref):
    c = c_ref[...]
    ch, cl = _split2(_silu(c))
    wh, wl = _split2(w_ref[0])
    o_ref[0] = _dot(ch, wh) + _dot(ch, wl) + _dot(cl, wh) + b_ref[0]


def _modulation(c_rows, mod_w, mod_b):
    depth, d, n = mod_w.shape
    tn = n // 4
    return pl.pallas_call(
        _mod_kernel,
        out_shape=jax.ShapeDtypeStruct((depth, 16, n), F32),
        grid=(depth, n // tn),
        in_specs=[pl.BlockSpec((16, d), lambda i, j: (0, 0)),
                  pl.BlockSpec((1, d, tn), lambda i, j: (i, 0, j)),
                  pl.BlockSpec((1, 1, tn), lambda i, j: (i, 0, j))],
        out_specs=pl.BlockSpec((1, 16, tn), lambda i, j: (i, 0, j)),
        compiler_params=_cparams(("arbitrary", "arbitrary")),
        name="modulation",
    )(c_rows, mod_w, mod_b.reshape(depth, 1, n))


def _mod_row(t, n_lat_tiles, tiles_per_batch, n_batch):
    return jnp.where(t < n_lat_tiles, t // tiles_per_batch, n_batch)


def _modulated(x, gain, mod_ref, row, k_shift, k_scale):
    d = x.shape[-1]
    shift = mod_ref[pl.ds(row, 1), k_shift * d:(k_shift + 1) * d]
    scale = mod_ref[pl.ds(row, 1), k_scale * d:(k_scale + 1) * d]
    return _rms(x) * gain * (1.0 + scale) + shift


def _rope_tables(seq):
    rows = seq // GRID_W
    row = jnp.repeat(jnp.arange(rows, dtype=jnp.int32), GRID_W)
    col = jnp.tile(jnp.arange(GRID_W, dtype=jnp.int32), rows)
    inv_freq = ROPE_THETA ** (-jnp.arange(0, AXIS_DIM, 2, dtype=F32) / AXIS_DIM)
    ang = jnp.stack([row[:, None] * inv_freq, col[:, None] * inv_freq], axis=1)
    cos, sin = jnp.cos(ang), jnp.sin(ang)
    zero = jnp.zeros_like(sin)
    cos64 = jnp.concatenate([cos[:, 0], cos[:, 0], cos[:, 1], cos[:, 1]], axis=-1)
    sa64 = jnp.concatenate([-sin[:, 0], zero[:, 0], -sin[:, 1], zero[:, 1]], axis=-1)
    sb64 = jnp.concatenate([zero[:, 0], sin[:, 0], zero[:, 1], sin[:, 1]], axis=-1)

    def widen(t, fill):
        t = jnp.concatenate([t, t], axis=-1)
        return jnp.concatenate([t, jnp.full((TOK_TILE, LANES), fill, F32)], axis=0)

    return widen(cos64, 1.0), widen(sa64, 0.0), widen(sb64, 0.0)


def _head_sumsq(y, bd):
    w = y.shape[-1]
    outs = []
    for s in range(0, w, MXU_DIM):
        e = min(s + MXU_DIM, w)
        hi, lo = _split2(y[:, s:e])
        b = bd[0:e - s, 0:e - s]
        outs.append(_dot(hi, b) + _dot(lo, b))
    return outs[0] if len(outs) == 1 else jnp.concatenate(outs, axis=-1)


def _qk_norm_rope(z, gain, bd, cos, sa, sb):
    w = z.shape[-1]
    rep = w // LANES
    ss = _head_sumsq(z * z, bd)
    y = z * lax.rsqrt(ss * (1.0 / HEAD_DIM) + EPS) * gain

    def wide(t):
        return t if rep == 1 else jnp.concatenate([t] * rep, axis=-1)

    return (y * wide(cos) + pltpu.roll(y, w - AXIS_DIM // 2, 1) * wide(sa)
            + pltpu.roll(y, AXIS_DIM // 2, 1) * wide(sb))


def _kv_rep(kv128):
    lane = lax.broadcasted_iota(jnp.int32, kv128.shape, 1)
    sw = pltpu.roll(kv128, HEAD_DIM, 1)
    a0 = jnp.where(lane < HEAD_DIM, kv128, sw)
    a1 = jnp.where(lane < HEAD_DIM, sw, kv128)
    return jnp.concatenate([a0, a0, a1, a1], axis=-1)


def _rope_block(t, n_lat_tiles, tiles_per_batch):
    return jnp.where(t < n_lat_tiles, t % tiles_per_batch, tiles_per_batch)


EV_GQ, EV_GK, EV_GV, EV_GR, EV_AQ, EV_AK, EV_LR, EV_END = 0, 256, 512, 1024, 1536, 2048, 2176, 2304


def _proj_even_kernel(geom, xl_ref, xc_ref, mod_ref, gain_ref, w_ref, wvt_ref, gw_ref, gb_ref, qg_ref, kg_ref,
                      cos_ref, sa_ref, sb_ref, bd_ref,
                      gq_ref, gk_ref, gv_ref, gr_ref, g_ref, aq_ref, ak_ref, avt_ref):
    t = pl.program_id(0)
    row = _mod_row(t, *geom)
    x = jnp.where(t < geom[0], xl_ref[...], xc_ref[...])
    hb = _bf(_modulated(x, gain_ref[...], mod_ref, row, 0, 1))

    def seg(a, b):
        return _dot(hb, w_ref[:, a:b])

    gq_ref[...] = seg(EV_GQ, EV_GK) * (GLA_DK ** -0.5)
    gk_ref[...] = seg(EV_GK, EV_GV)
    gv_ref[...] = _bf(seg(EV_GV, EV_GR))
    gr_ref[...] = seg(EV_GR, EV_AQ)
    zg = _dot(_bf(seg(EV_LR, EV_END)), gw_ref[...]) + gb_ref[...]
    g_ref[...] = -(jnp.maximum(-zg, 0.0) + jnp.log1p(jnp.exp(-jnp.abs(zg)))) * (1.0 / GLA_GATE_NORM)
    bd = bd_ref[...]
    cos, sa, sb = cos_ref[...], sa_ref[...], sb_ref[...]
    aq = _qk_norm_rope(seg(EV_AQ, EV_AK), qg_ref[...], bd, cos, sa, sb)
    aq_ref[...] = _bf(aq * Q_SCALE)
    ak = _qk_norm_rope(seg(EV_AK, EV_LR), kg_ref[...], bd, cos, sa, sb)
    ak_ref[...] = _bf(_kv_rep(ak))
    avt_ref[0] = _bf(_dot_nt(wvt_ref[...], hb))


def _vt_spec():
    return pl.BlockSpec((1, LANES, TOK_TILE), lambda t: (t, 0, 0))


def _lat_ctx_specs(block, n_lat_tiles, lead=()):
    z = (0,) * len(lead)
    lat = pl.BlockSpec(lead + block, lambda t: z + (jnp.minimum(t, n_lat_tiles - 1), 0))
    ctx = pl.BlockSpec(lead + block, lambda t: z + (jnp.maximum(t - n_lat_tiles, 0), 0))
    return [lat, ctx]


def _proj_even(x_lat, x_ctx, mod, gain, w, wvt, gw, gb, qg, kg, tables, bd, geom):
    d = x_lat.shape[1]
    n = x_lat.shape[0] + x_ctx.shape[0]
    n_lat_tiles, tiles_per_batch, _ = geom
    tt = TOK_TILE
    cos, sa, sb = tables
    tok = lambda w_: pl.BlockSpec((tt, w_), lambda t: (t, 0))
    rope = pl.BlockSpec((tt, LANES), lambda t: (_rope_block(t, n_lat_tiles, tiles_per_batch), 0))
    outs = [(256, F32), (256, F32), (512, BF16), (512, F32), (512, F32), (512, BF16), (512, BF16)]
    return pl.pallas_call(
        functools.partial(_proj_even_kernel, geom),
        out_shape=[jax.ShapeDtypeStruct((n, w_), dt) for w_, dt in outs]
        + [jax.ShapeDtypeStruct((n // tt, LANES, tt), BF16)],
        grid=(n // tt,),
        in_specs=_lat_ctx_specs((tt, d), n_lat_tiles)
        + [_full(mod.shape), _full(gain.shape), _full(w.shape), _full(wvt.shape), _full(gw.shape),
           _full(gb.shape), _full(qg.shape), _full(kg.shape), rope, rope, rope, _full(bd.shape)],
        out_specs=[tok(w_) for w_, _ in outs] + [_vt_spec()],
        compiler_params=_cparams(("arbitrary",)),
        name="proj_even",
    )(x_lat, x_ctx, mod, gain, w, wvt, gw, gb, qg, kg, cos, sa, sb, bd)


def _proj_odd_kernel(geom, x_ref, pos_ref, pos_next_ref, fs_hbm, mod_prev_ref, mod_ref, gain_ref, w_ref, wvt_ref,
                     qg_ref, kg_ref, cos_ref, sa_ref, sb_ref, bd_ref, x1_ref, q_ref, k_ref, vt_ref, fbuf, fsem):
    t = pl.program_id(0)
    row = _mod_row(t, *geom)
    d = x_ref.shape[-1]
    gate = mod_prev_ref[pl.ds(row, 1), 5 * d:6 * d]
    f = _gather_tile(t, pl.num_programs(0), pos_ref, pos_next_ref, fs_hbm, fbuf, fsem)
    x1 = x_ref[...] + gate * f
    x1_ref[...] = x1
    hb = _bf(_modulated(x1, gain_ref[...], mod_ref, row, 0, 1))
    bd = bd_ref[...]
    cos, sa, sb = cos_ref[...], sa_ref[...], sb_ref[...]
    q = _qk_norm_rope(_dot(hb, w_ref[:, 0:d]), qg_ref[...], bd, cos, sa, sb)
    q_ref[...] = _bf(q * Q_SCALE)
    k = _qk_norm_rope(_dot(hb, w_ref[:, d:d + LANES]), kg_ref[...], bd, cos, sa, sb)
    k_ref[...] = _bf(_kv_rep(k))
    vt_ref[0] = _bf(_dot_nt(wvt_ref[...], hb))


def _proj_odd(x_all, f_sorted, pos, mod_prev, mod, gain, w, wvt, qg, kg, tables, bd, geom):
    n, d = x_all.shape
    n_lat_tiles, tiles_per_batch, _ = geom
    tt = TOK_TILE
    cos, sa, sb = tables
    tok = lambda w_: pl.BlockSpec((tt, w_), lambda t: (t, 0))
    rope = pl.BlockSpec((tt, LANES), lambda t: (_rope_block(t, n_lat_tiles, tiles_per_batch), 0))
    outs = [(d, F32), (d, BF16), (512, BF16)]
    return pl.pallas_call(
        functools.partial(_proj_odd_kernel, geom),
        out_shape=[jax.ShapeDtypeStruct((n, w_), dt) for w_, dt in outs]
        + [jax.ShapeDtypeStruct((n // tt, LANES, tt), BF16)],
        grid=(n // tt,),
        in_specs=[tok(d)] + _gather_specs(n // tt)
        + [_full(mod_prev.shape), _full(mod.shape), _full(gain.shape), _full(w.shape),
           _full(wvt.shape), _full(qg.shape), _full(kg.shape), rope, rope, rope, _full(bd.shape)],
        out_specs=[tok(w_) for w_, _ in outs] + [_vt_spec()],
        scratch_shapes=_gather_scratch(d),
        compiler_params=_cparams(("arbitrary",)),
        name="proj_odd",
    )(x_all, pos, pos, f_sorted, mod_prev, mod, gain, w, wvt, qg, kg, cos, sa, sb, bd)


N_LEVELS = 6
GLA_MX_ROWS = (N_LEVELS + 2) * GLA_CHUNK


def _gla_constants():
    c = GLA_CHUNK
    mx = np.zeros((2, GLA_MX_ROWS, c), np.float32)
    pat = np.zeros((2, N_LEVELS + 1, c, GLA_HEADS * c), np.float32)
    r = np.arange(c)
    for lvl in range(N_LEVELS):
        h = 1 << lvl
        ref = (r // (2 * h)) * 2 * h + h - 1
        upper = (r % (2 * h)) >= h
        m = np.zeros((c, c), np.float32)
        for i in range(c):
            if upper[i]:
                m[i, ref[i] + 1:i + 1] = 1.0
            else:
                m[i, i + 1:ref[i] + 1] = 1.0
        mx[0, lvl * c:(lvl + 1) * c] = m
        same = (r[:, None] // (2 * h)) == (r[None, :] // (2 * h))
        p = same & upper[:, None] & (~upper)[None, :]
        pat[0, lvl] = np.tile(p.astype(np.float32), (1, GLA_HEADS))
    mx[0, N_LEVELS * c:(N_LEVELS + 1) * c] = (r[None, :] <= r[:, None])
    mx[0, (N_LEVELS + 1) * c:(N_LEVELS + 2) * c] = (r[None, :] > r[:, None])
    pat[0, N_LEVELS] = np.tile(np.eye(c, dtype=np.float32), (1, GLA_HEADS))
    for k in range(N_LEVELS + 2):
        mx[1, k * c:(k + 1) * c] = mx[0, k * c:(k + 1) * c][::-1, ::-1]
    for k in range(N_LEVELS + 1):
        pat[1, k] = np.tile(pat[0, k, :, 0:c][::-1, ::-1], (1, GLA_HEADS))
    return mx, pat


GLA_GROUP = 2


def _gla_chunks(chunks, mx, pat_ref, s_ref):
    c = GLA_CHUNK
    lane_head = lax.broadcasted_iota(jnp.int32, (c, GLA_HEADS * GLA_DK), 1) // GLA_DK

    def stack_heads(a):
        return jnp.concatenate([_bf(jnp.where(lane_head == h, a, 0.0)) for h in range(GLA_HEADS)], axis=0)

    xs = [_dot(mx, jnp.concatenate(_split2(g), axis=0)) for _, _, _, g in chunks]
    atts = [jnp.where(pat_ref[0, N_LEVELS] > 0.0, _dot_nt(_bf(q), stack_heads(k)), 0.0)
            for q, k, _, _ in chunks]
    for lvl in range(N_LEVELS):
        for j, (q, k, _, _) in enumerate(chunks):
            e = jnp.exp(xs[j][lvl * c:(lvl + 1) * c])
            atts[j] = atts[j] + jnp.where(pat_ref[0, lvl] > 0.0, _dot_nt(_bf(q * e), stack_heads(k * e)), 0.0)
    outs, qes, news, a_cols = [], [], [], []
    for j, (q, k, v, _) in enumerate(chunks):
        vhead = lax.broadcasted_iota(jnp.int32, v.shape, 1) // GLA_DV
        v_bd = jnp.concatenate([jnp.where(vhead == h, v, jnp.zeros_like(v)) for h in range(GLA_HEADS)], axis=0)
        outs.append(_dot(_bf(atts[j]), v_bd))
        bcum = xs[j][N_LEVELS * c:(N_LEVELS + 1) * c]
        brem = xs[j][(N_LEVELS + 1) * c:(N_LEVELS + 2) * c]
        qes.append(stack_heads(q * jnp.exp(bcum)))
        kt = jnp.transpose(k * jnp.exp(brem))
        news.append(jnp.concatenate(
            [_dot(_bf(kt[h * GLA_DK:(h + 1) * GLA_DK]), v[:, h * GLA_DV:(h + 1) * GLA_DV])
             for h in range(GLA_HEADS)], axis=0))
        tot = bcum[0:1] + brem[0:1]
        a_cols.append(jnp.transpose(jnp.exp(jnp.broadcast_to(tot, (8, tot.shape[1]))))[:, 0:1])
    s = s_ref[...]
    for j in range(len(chunks)):
        o_inter = _dot(qes[j], _bf(s))
        outs[j] = outs[j] + jnp.concatenate([o_inter[h * c:(h + 1) * c] for h in range(GLA_HEADS)], axis=-1)
        s = a_cols[j] * s + news[j]
    s_ref[...] = s
    return outs


def _gla_kernel(n_lat_chunks, n_ctx_chunks, ql_ref, kl_ref, vl_ref, gl_ref, qc_ref, kc_ref, vc_ref, gc_ref,
                mx_ref, pat_ref, ol_ref, oc_ref, s_ref):
    d = pl.program_id(1)
    c = GLA_CHUNK
    s_ref[...] = jnp.zeros_like(s_ref)
    mx = mx_ref[0]

    def run(n_chunks, q_ref, k_ref, v_ref, g_ref, o_ref):
        def body(i, carry):
            rows = []
            for j in range(GLA_GROUP):
                step = i * GLA_GROUP + j
                ci = jnp.where(d == 0, step, n_chunks - 1 - step)
                rows.append(pl.ds(pl.multiple_of(ci * c, c), c))
            outs = _gla_chunks([(q_ref[r, :], k_ref[r, :], v_ref[r, :], g_ref[r, :]) for r in rows],
                               mx, pat_ref, s_ref)
            for r, o in zip(rows, outs):
                o_ref[0, r, :] = o
            return carry
        lax.fori_loop(0, n_chunks // GLA_GROUP, body, 0)

    run(n_ctx_chunks, qc_ref, kc_ref, vc_ref, gc_ref, oc_ref)
    run(n_lat_chunks, ql_ref, kl_ref, vl_ref, gl_ref, ol_ref)


def _gla(gq, gk, gv, g, n_batch, seq, lc):
    n = gq.shape[0]
    mx_np, pat_np = _gla_constants()
    mx = jnp.asarray(np.concatenate([mx_np, mx_np], axis=2), BF16)
    pat = jnp.asarray(pat_np, F32)
    ctx0 = n_batch * seq // lc
    hk, hv = GLA_HEADS * GLA_DK, GLA_HEADS * GLA_DV
    lat = lambda w_, col: pl.BlockSpec((seq, w_), lambda b, d_: (b, col(d_)))
    ctx = lambda w_, col: pl.BlockSpec((lc, w_), lambda b, d_: (ctx0 + b, col(d_)))
    zero = lambda d_: 0
    same = lambda d_: d_
    return pl.pallas_call(
        functools.partial(_gla_kernel, seq // GLA_CHUNK, lc // GLA_CHUNK),
        out_shape=[jax.ShapeDtypeStruct((2, n_batch * seq, hv), F32),
                   jax.ShapeDtypeStruct((2, n_batch * lc, hv), F32)],
        grid=(n_batch, 2),
        in_specs=[lat(hk, zero), lat(hk, zero), lat(hv, zero), lat(hk, same),
                  ctx(hk, zero), ctx(hk, zero), ctx(hv, zero), ctx(hk, same),
                  pl.BlockSpec((1,) + mx.shape[1:], lambda b, d_: (d_, 0, 0)),
                  pl.BlockSpec((1,) + pat.shape[1:], lambda b, d_: (d_, 0, 0, 0))],
        out_specs=[pl.BlockSpec((1, seq, hv), lambda b, d_: (d_, b, 0)),
                   pl.BlockSpec((1, lc, hv), lambda b, d_: (d_, b, 0))],
        scratch_shapes=[pltpu.VMEM((GLA_HEADS * GLA_DK, GLA_DV), F32)],
        compiler_params=_cparams(("arbitrary", "arbitrary")),
        name="gla_scan",
    )(gq, gk, gv, g, gq, gk, gv, g, mx, pat)


def _stack_heads(q):
    lane_head = lax.broadcasted_iota(jnp.int32, q.shape, 1) // HEAD_DIM
    return jnp.concatenate([jnp.where(lane_head == h, q, jnp.zeros_like(q)) for h in range(4)], axis=0)


def _attn_store(acc, l, o_ref, u, tq):
    out = acc * (1.0 / l)
    out = jnp.concatenate([out[:, h * tq:(h + 1) * tq] for h in range(4)], axis=0)
    o_ref[u * tq:(u + 1) * tq, :] = _bf(jnp.transpose(out))


def _attn_dense_kernel(n_chunks, n_sub, *refs):
    if n_chunks:
        q_ref, kl_ref, vtl_ref, kc_ref, vtc_ref, o_ref = refs
    else:
        q_ref, kc_ref, vtc_ref, o_ref = refs
    tq = q_ref.shape[0] // n_sub
    cols = 4 * tq
    q4 = [_stack_heads(q_ref[u * tq:(u + 1) * tq, :]) for u in range(n_sub)]

    def scores(c, u):
        if c < n_chunks:
            return _dot_nt(kl_ref[c * ATT_K_CHUNK:(c + 1) * ATT_K_CHUNK, :], q4[u])
        return _dot_nt(kc_ref[...], q4[u])

    def update(carry, st, vt):
        m, l, acc = carry
        m_new = jnp.maximum(m, jnp.max(st, axis=0, keepdims=True))
        alpha = jnp.exp2(m - m_new)
        p = jnp.exp2(st - m_new)
        l = alpha * l + jnp.sum(p, axis=0, keepdims=True)
        acc = alpha * acc + _dot(vt, _bf(p))
        return m_new, l, acc

    carry = [(jnp.full((1, cols), NEG_BIG, F32), jnp.zeros((1, cols), F32), jnp.zeros((HEAD_DIM, cols), F32))
             for _ in range(n_sub)]
    st = [scores(0, u) for u in range(n_sub)]
    for c in range(n_chunks + 1):
        st_next = [scores(c + 1, u) for u in range(n_sub)] if c < n_chunks else None
        vt = vtl_ref[c] if c < n_chunks else vtc_ref[0]
        carry = [update(carry[u], st[u], vt) for u in range(n_sub)]
        st = st_next
    for u in range(n_sub):
        _attn_store(carry[u][2], carry[u][1], o_ref, u, tq)


def _attn_window_kernel(seq, q_ref, *refs):
    nk = WIN_SUB + 2
    k_refs, v_refs = refs[0:nk], refs[nk:2 * nk]
    kc_ref, vtc_ref, sink_ref, o_ref = refs[2 * nk:]
    tq = SWA_WINDOW
    i = pl.program_id(2)
    kb = jnp.concatenate([r[...] for r in k_refs], axis=0)
    vtb = jnp.concatenate([r[0] for r in v_refs], axis=1)
    kc, vtc, sink = kc_ref[...], vtc_ref[0], sink_ref[0]
    span = 3 * tq
    for u in range(WIN_SUB):
        q4 = _stack_heads(q_ref[u * tq:(u + 1) * tq, :])
        first = i * WIN_SUB + u - 1
        kpos = first * tq + lax.broadcasted_iota(jnp.int32, (span, tq), 0)
        qpos = (first + 1) * tq + lax.broadcasted_iota(jnp.int32, (span, tq), 1)
        ok = (kpos >= 0) & (kpos < seq) & (jnp.abs(kpos - qpos) <= SWA_WINDOW)
        bias = jnp.where(ok, 0.0, NEG_BIG)
        sb = _dot_nt(kb[u * tq:u * tq + span], q4) + jnp.concatenate([bias] * 4, axis=1)
        sc = _dot_nt(kc, q4)
        m = jnp.maximum(jnp.maximum(jnp.max(sb, axis=0, keepdims=True), jnp.max(sc, axis=0, keepdims=True)), sink)
        pb = jnp.exp2(sb - m)
        pc = jnp.exp2(sc - m)
        l = jnp.sum(pb, axis=0, keepdims=True) + jnp.sum(pc, axis=0, keepdims=True) + jnp.exp2(sink - m)
        acc = _dot(vtb[:, u * tq:u * tq + span], _bf(pb)) + _dot(vtc, _bf(pc))
        _attn_store(acc, l, o_ref, u, tq)


def _attention(mode, q, k_rep, vt, n_batch, seq, lc, n_heads, n_kv, sink=None):
    ncol = n_heads * HEAD_DIM // ATT_COL
    col_per_kv = ncol // n_kv
    tq = ATT_Q_TILE
    tpb = seq // TOK_TILE
    n_lat_tiles = n_batch * tpb
    ctx_per_tile = TOK_TILE // lc
    ctx0 = n_batch * seq // lc
    assert ATT_K_CHUNK == TOK_TILE and TOK_TILE % lc == 0 and tq == SWA_WINDOW
    kv = lambda j: j // col_per_kv
    k_ctx = pl.BlockSpec((lc, ATT_COL), lambda b, j, i: (ctx0 + b, kv(j)))
    vt_ctx = pl.BlockSpec((1, HEAD_DIM, lc), lambda b, j, i: (n_lat_tiles + b // ctx_per_tile, kv(j), b % ctx_per_tile))
    if mode == "ctx":
        nq = lc // tq
        q0 = n_batch * seq // tq
        kern = functools.partial(_attn_dense_kernel, 0, 1)
        args = (q, k_rep, vt)
        in_specs = [pl.BlockSpec((tq, ATT_COL), lambda b, j, i: (q0 + b * nq + i, j)), k_ctx, vt_ctx]
    elif mode == "dense":
        tq = DENSE_SUB * ATT_Q_TILE
        nq = seq // tq
        kern = functools.partial(_attn_dense_kernel, tpb, DENSE_SUB)
        args = (q, k_rep, vt, k_rep, vt)
        in_specs = [pl.BlockSpec((tq, ATT_COL), lambda b, j, i: (b * nq + i, j)),
                    pl.BlockSpec((seq, ATT_COL), lambda b, j, i: (b, kv(j))),
                    pl.BlockSpec((tpb, HEAD_DIM, TOK_TILE), lambda b, j, i: (b, kv(j), 0)), k_ctx, vt_ctx]
    else:
        wb = SWA_WINDOW
        tq = WIN_SUB * wb
        nq = seq // tq
        nkb = seq // wb
        per_tile = TOK_TILE // wb
        kern = functools.partial(_attn_window_kernel, seq)
        nb = lambda i, o: jnp.clip(i * WIN_SUB + o, 0, nkb - 1)
        k_nb = lambda o: pl.BlockSpec((wb, ATT_COL), lambda b, j, i: (b * nkb + nb(i, o), kv(j)))
        v_nb = lambda o: pl.BlockSpec(
            (1, HEAD_DIM, wb), lambda b, j, i: (b * tpb + nb(i, o) // per_tile, kv(j), nb(i, o) % per_tile))
        offs = range(-1, WIN_SUB + 1)
        sink_row = jnp.repeat(sink.reshape(ncol, 1, 4), wb, axis=2) * LOG2E
        args = (q,) + (k_rep,) * len(offs) + (vt,) * len(offs) + (k_rep, vt, sink_row)
        in_specs = ([pl.BlockSpec((tq, ATT_COL), lambda b, j, i: (b * nq + i, j))]
                    + [k_nb(o) for o in offs] + [v_nb(o) for o in offs]
                    + [k_ctx, vt_ctx, pl.BlockSpec((1, 1, 4 * wb), lambda b, j, i: (j, 0, 0))])
    return pl.pallas_call(
        kern,
        out_shape=jax.ShapeDtypeStruct((n_batch * nq * tq, n_heads * HEAD_DIM), BF16),
        grid=(n_batch, ncol, nq),
        in_specs=in_specs,
        out_specs=pl.BlockSpec((tq, ATT_COL), lambda b, j, i: (b * nq + i, j)),
        compiler_params=_cparams(("arbitrary", "arbitrary", "arbitrary")),
        name="attention_" + mode,
    )(*args)


ROUTE_ROWS = 8


def _route(h, wrt_ref, wrt_hi_ref, brt_ref):
    hh, hl = _split2(h)
    a = _dot_nt(wrt_ref[...], hh)
    lt = a[0:LANES] + a[LANES:2 * LANES] + _dot_nt(wrt_hi_ref[...], hl) + brt_ref[...]
    col = lambda i: lt[i:i + 1, :]
    gl = [col(i) for i in range(MOE_GROUPS)]
    gmax = functools.reduce(jnp.maximum, gl)
    gi = jnp.where(gl[0] == gmax, 0, jnp.where(gl[1] == gmax, 1, jnp.where(gl[2] == gmax, 2, 3)))
    g_weight = 1.0 / functools.reduce(lambda a, b: a + b, [jnp.exp(x - gmax) for x in gl])
    el = []
    for j in range(MOE_EPG):
        cand = [col(MOE_GROUPS + g * MOE_EPG + j) for g in range(MOE_GROUPS)]
        el.append(jnp.where(gi == 0, cand[0], jnp.where(gi == 1, cand[1], jnp.where(gi == 2, cand[2], cand[3]))))
    m1 = functools.reduce(jnp.maximum, el)
    i1 = jnp.where(el[0] == m1, 0, jnp.where(el[1] == m1, 1, jnp.where(el[2] == m1, 2, 3)))
    rest = [jnp.where(i1 == j, -jnp.inf, el[j]) for j in range(MOE_EPG)]
    m2 = functools.reduce(jnp.maximum, rest)
    i2 = jnp.where(rest[0] == m2, 0, jnp.where(rest[1] == m2, 1, jnp.where(rest[2] == m2, 2, 3)))
    e2 = jnp.exp(m2 - m1)
    w1 = g_weight / (1.0 + e2)
    w2 = g_weight * e2 / (1.0 + e2)
    lo = jnp.minimum(i1, i2)
    hi = jnp.maximum(i1, i2)
    w_lo = jnp.where(i1 == lo, w1, w2)
    w_hi = jnp.where(i1 == lo, w2, w1)
    pair = jnp.where(lo == 0, hi - 1, jnp.where(lo == 1, hi + 1, N_PAIRS - 1))
    return w_lo, w_hi, gi * N_PAIRS + pair


def _out_tail(geom, m, x, mod_ref, gain_ffn_ref, wrt_ref, wrt_hi_ref, brt_ref, triu_ref,
              x_new_ref, hrow_ref, rt_ref, counts_ref, run_ref):
    t = pl.program_id(0)
    row = _mod_row(t, *geom)
    tt, d = x.shape

    @pl.when(t == 0)
    def _():
        run_ref[...] = jnp.zeros_like(run_ref)

    x_new = x + mod_ref[pl.ds(row, 1), 2 * d:3 * d] * m
    x_new_ref[...] = x_new
    h = _modulated(x_new, gain_ffn_ref[...], mod_ref, row, 3, 4)
    hrow_ref[:, 0:d] = h
    w_lo, w_hi, bucket = _route(h, wrt_ref, wrt_hi_ref, brt_ref)
    onehot = lax.broadcasted_iota(jnp.int32, (LANES, tt), 0) == bucket
    ones = jnp.where(onehot, 1.0, 0.0)
    before = _dot(_bf(ones), triu_ref[...]) + run_ref[...]
    rank = jnp.sum(jnp.where(onehot, before, 0.0), axis=0, keepdims=True)
    run = run_ref[...] + jnp.sum(ones, axis=1, keepdims=True)
    run_ref[...] = run
    counts_ref[...] = jnp.broadcast_to(run, counts_ref.shape)
    rec = jnp.concatenate([w_lo, w_hi, bucket.astype(F32), rank, jnp.zeros((ROUTE_ROWS - 4, tt), F32)], axis=0)
    rt_ref[0] = rec
    meta_t = jnp.concatenate([rec, jnp.zeros((META_LANES - ROUTE_ROWS, tt), F32)], axis=0)
    hrow_ref[:, d:d + META_LANES] = jnp.transpose(meta_t)


def _pick(t, n_lat_tiles, lat_ref, ctx_ref):
    return jnp.where(t < n_lat_tiles, lat_ref[...], ctx_ref[...])


def _out_even_kernel(geom, ol_ref, oc_ref, r_ref, attl_ref, attc_ref, gn_ref, w_ref, xl_ref, xc_ref, mod_ref,
                     gain_ffn_ref, wrt_ref, wrt_hi_ref, brt_ref, triu_ref,
                     x_new_ref, hrow_ref, rt_ref, counts_ref, run_ref):
    t = pl.program_id(0)
    o2 = _pick(t, geom[0], ol_ref, oc_ref)
    o = o2[0] + o2[1]
    r = r_ref[...]
    parts = []
    for h in range(GLA_HEADS):
        sl = slice(h * GLA_DV, (h + 1) * GLA_DV)
        parts.append(_rms(o[:, sl]) * gn_ref[...] * _silu(r[:, sl]))
    a = _bf(jnp.concatenate(parts, axis=-1))
    half = a.shape[-1]
    m = _dot(a, w_ref[0:half, :]) + _dot(_pick(t, geom[0], attl_ref, attc_ref), w_ref[half:, :])
    _out_tail(geom, m, _pick(t, geom[0], xl_ref, xc_ref), mod_ref, gain_ffn_ref, wrt_ref, wrt_hi_ref, brt_ref,
              triu_ref, x_new_ref, hrow_ref, rt_ref, counts_ref, run_ref)


def _out_odd_kernel(geom, att_ref, w_ref, x_ref, mod_ref, gain_ffn_ref, wrt_ref, wrt_hi_ref, brt_ref, triu_ref,
                    x_new_ref, hrow_ref, rt_ref, counts_ref, run_ref):
    m = _dot(att_ref[...], w_ref[...])
    _out_tail(geom, m, x_ref[...], mod_ref, gain_ffn_ref, wrt_ref, wrt_hi_ref, brt_ref, triu_ref,
              x_new_ref, hrow_ref, rt_ref, counts_ref, run_ref)


def _out_proj(kernel, lead_args, lead_specs, d, mod, gain_ffn, router, n_rows, name):
    tt = TOK_TILE
    tok = lambda w_: pl.BlockSpec((tt, w_), lambda t: (t, 0))
    r = np.arange(tt)
    triu = jnp.asarray((r[:, None] < r[None, :]).astype(np.float32), BF16)
    wrt, wrt_hi, br = router
    brt = jnp.broadcast_to(br.reshape(LANES, 1), (LANES, tt))
    return pl.pallas_call(
        kernel,
        out_shape=[jax.ShapeDtypeStruct((n_rows, d), F32), jax.ShapeDtypeStruct((n_rows, d + META_LANES), F32),
                   jax.ShapeDtypeStruct((n_rows // tt, ROUTE_ROWS, tt), F32),
                   jax.ShapeDtypeStruct((LANES, LANES), F32)],
        grid=(n_rows // tt,),
        in_specs=lead_specs + [_full(mod.shape), _full(gain_ffn.shape), _full(wrt.shape), _full(wrt_hi.shape),
                               _full(brt.shape), _full(triu.shape)],
        out_specs=[tok(d), tok(d + META_LANES), pl.BlockSpec((1, ROUTE_ROWS, tt), lambda t: (t, 0, 0)),
                   _full((LANES, LANES))],
        scratch_shapes=[pltpu.VMEM((LANES, 1), F32)],
        compiler_params=_cparams(("arbitrary",)),
        name=name,
    )(*lead_args, mod, gain_ffn, wrt, wrt_hi, brt, triu)


ROW_DMA_UNROLL = 8


def _pos_kernel(start_ref, rt_ref, pos_ref):
    bucket = rt_ref[:, 2, :].astype(jnp.int32)
    base = jnp.zeros_like(bucket)
    for b in range(N_BUCKETS):
        base = jnp.where(bucket == b, start_ref[b], base)
    pos_ref[:, 0, :] = base + rt_ref[:, 3, :].astype(jnp.int32)


def _dispatch_kernel(last_ref, pos_ref, x_ref, xs_ref, zbuf, sem, zsem):
    tt = x_ref.shape[0]
    tm = zbuf.shape[0]

    @pl.when(pl.program_id(0) == 0)
    def _():
        zbuf[...] = jnp.zeros_like(zbuf)

        def zero_copy(b):
            return pltpu.make_async_copy(zbuf, xs_ref.at[pl.ds(jnp.maximum(last_ref[b], 0) * tm, tm)], zsem)

        for b in range(2 * N_BUCKETS):
            @pl.when(last_ref[b] >= 0)
            def _():
                zero_copy(b).start()
        for b in range(2 * N_BUCKETS):
            @pl.when(last_ref[b] >= 0)
            def _():
                zero_copy(b).wait()

    def body(r, c):
        pltpu.make_async_copy(x_ref.at[pl.ds(r, 1)], xs_ref.at[pl.ds(pos_ref[0, 0, r], 1)], sem).start()
        return c

    lax.fori_loop(0, tt, body, 0, unroll=ROW_DMA_UNROLL)
    pltpu.make_async_copy(x_ref, xs_ref.at[pl.ds(0, tt)], sem).wait()


def _moe_mlp_kernel(tlo_ref, thi_ref, nused_ref, xs_ref, wg_lo, wu_lo, wd_lo, wg_hi, wu_hi, wd_hi, f_ref):
    del tlo_ref, thi_ref
    d = f_ref.shape[1]
    used = pl.program_id(0) < nused_ref[0]

    @pl.when(used)
    def _():
        xb = _bf(xs_ref[:, 0:d])
        w_lo = xs_ref[:, d:d + 1]
        w_hi = xs_ref[:, d + 1:d + 2]
        hid_lo = _silu(_dot(xb, wg_lo[0])) * _dot(xb, wu_lo[0]) * w_lo
        hid_hi = _silu(_dot(xb, wg_hi[0])) * _dot(xb, wu_hi[0]) * w_hi
        f_ref[...] = _dot(_bf(hid_lo), wd_lo[0]) + _dot(_bf(hid_hi), wd_hi[0])

    @pl.when(jnp.logical_not(used))
    def _():
        f_ref[...] = jnp.zeros_like(f_ref)


def _moe(hrow, rt, counts, w_gate, w_up, w_down):
    n, dw = hrow.shape
    d = dw - META_LANES
    tm, tt = MOE_TILE, TOK_TILE
    n_tiles = n // tm + N_BUCKETS
    p = n_tiles * tm
    cnt = counts[0:N_BUCKETS, 0].astype(jnp.int32)
    tiles_b = (cnt + tm - 1) // tm
    tile_end = jnp.cumsum(tiles_b)
    start_b = (tile_end - tiles_b) * tm
    n_used = tile_end[-1].reshape(1)
    spare = n_used[0] + jnp.arange(N_BUCKETS, dtype=jnp.int32)
    last_tile = jnp.concatenate([jnp.where(tiles_b > 0, tile_end - 1, -1), jnp.where(spare < n_tiles, spare, -1)])
    pos = pl.pallas_call(
        _pos_kernel,
        out_shape=jax.ShapeDtypeStruct((n // tt, 1, tt), jnp.int32),
        grid_spec=pltpu.PrefetchScalarGridSpec(
            num_scalar_prefetch=1, grid=(1,),
            in_specs=[pl.BlockSpec(rt.shape, lambda i, s: (0, 0, 0))],
            out_specs=pl.BlockSpec((n // tt, 1, tt), lambda i, s: (0, 0, 0))),
        compiler_params=_cparams(("arbitrary",)),
        name="moe_positions",
    )(start_b, rt)
    tile_ids = jnp.arange(n_tiles, dtype=jnp.int32)
    tile_bucket = jnp.sum((tile_ids[:, None] >= tile_end[None, :]).astype(jnp.int32), axis=1)
    tile_bucket = jnp.minimum(tile_bucket, jnp.sum((n_used[0] - 1 >= tile_end).astype(jnp.int32)))
    tile_bucket = jnp.minimum(tile_bucket, N_BUCKETS - 1)
    pair_lo = jnp.asarray([0, 0, 0, 1, 1, 2], jnp.int32)
    pair_hi = jnp.asarray([1, 2, 3, 2, 3, 3], jnp.int32)
    grp = tile_bucket // N_PAIRS
    t_lo = grp * MOE_EPG + pair_lo[tile_bucket % N_PAIRS]
    t_hi = grp * MOE_EPG + pair_hi[tile_bucket % N_PAIRS]

    xs = pl.pallas_call(
        _dispatch_kernel,
        out_shape=jax.ShapeDtypeStruct((p, dw), F32),
        grid_spec=pltpu.PrefetchScalarGridSpec(
            num_scalar_prefetch=1, grid=(n // tt,),
            in_specs=[pl.BlockSpec((1, 1, tt), lambda t, s: (t, 0, 0), memory_space=pltpu.SMEM),
                      pl.BlockSpec((tt, dw), lambda t, s: (t, 0))],
            out_specs=pl.BlockSpec(memory_space=pl.ANY),
            scratch_shapes=[pltpu.VMEM((tm, dw), F32), pltpu.SemaphoreType.DMA(()), pltpu.SemaphoreType.DMA(())]),
        compiler_params=_cparams(("arbitrary",)),
        name="moe_dispatch",
    )(last_tile, pos, hrow)

    f = D_EXPERT
    up_lo = pl.BlockSpec((1, d, f), lambda t, lo, hi, nu: (lo[t], 0, 0))
    up_hi = pl.BlockSpec((1, d, f), lambda t, lo, hi, nu: (hi[t], 0, 0))
    dn_lo = pl.BlockSpec((1, f, d), lambda t, lo, hi, nu: (lo[t], 0, 0))
    dn_hi = pl.BlockSpec((1, f, d), lambda t, lo, hi, nu: (hi[t], 0, 0))
    grid_spec = pltpu.PrefetchScalarGridSpec(
        num_scalar_prefetch=3,
        grid=(n_tiles,),
        in_specs=[pl.BlockSpec((tm, dw), lambda t, lo, hi, nu: (jnp.minimum(t, nu[0] - 1), 0)),
                  up_lo, up_lo, dn_lo, up_hi, up_hi, dn_hi],
        out_specs=pl.BlockSpec((tm, d), lambda t, *_: (t, 0)),
    )
    f_sorted = pl.pallas_call(
        _moe_mlp_kernel,
        out_shape=jax.ShapeDtypeStruct((p, d), F32),
        grid_spec=grid_spec,
        compiler_params=_cparams(("arbitrary",)),
        name="moe_experts",
    )(t_lo, t_hi, n_used, xs, w_gate, w_up, w_down, w_gate, w_up, w_down)
    return f_sorted, pos


def _gather_tile(t, n_t, pos_ref, pos_next_ref, src_hbm, buf, sem):
    tt = buf.shape[1]

    def start(p_ref, slot):
        def body(r, c):
            pltpu.make_async_copy(src_hbm.at[pl.ds(p_ref[0, 0, r], 1)], buf.at[slot].at[pl.ds(r, 1)],
                                  sem.at[slot]).start()
            return c
        lax.fori_loop(0, tt, body, 0, unroll=ROW_DMA_UNROLL)

    slot = t % 2

    @pl.when(t == 0)
    def _():
        start(pos_ref, 0)

    pltpu.make_async_copy(src_hbm.at[pl.ds(0, tt)], buf.at[slot], sem.at[slot]).wait()

    @pl.when(t + 1 < n_t)
    def _():
        start(pos_next_ref, 1 - slot)

    return buf[slot]


def _gather_specs(n_t):
    tt = TOK_TILE
    return [pl.BlockSpec((1, 1, tt), lambda t: (t, 0, 0), memory_space=pltpu.SMEM),
            pl.BlockSpec((1, 1, tt), lambda t: (jnp.minimum(t + 1, n_t - 1), 0, 0), memory_space=pltpu.SMEM),
            pl.BlockSpec(memory_space=pl.ANY)]


def _gather_scratch(d):
    return [pltpu.VMEM((2, TOK_TILE, d), F32), pltpu.SemaphoreType.DMA((2,))]


def _final_kernel(tiles_per_batch, x_ref, pos_ref, pos_next_ref, fs_hbm, mod_ref, o_ref, fbuf, fsem):
    t = pl.program_id(0)
    row = t // tiles_per_batch
    d = x_ref.shape[-1]
    f = _gather_tile(t, pl.num_programs(0), pos_ref, pos_next_ref, fs_hbm, fbuf, fsem)
    o_ref[...] = x_ref[...] + mod_ref[pl.ds(row, 1), 5 * d:6 * d] * f


def _final(x_lat, f_sorted, pos, mod, tiles_per_batch):
    n, d = x_lat.shape
    tok = pl.BlockSpec((TOK_TILE, d), lambda t: (t, 0))
    return pl.pallas_call(
        functools.partial(_final_kernel, tiles_per_batch),
        out_shape=jax.ShapeDtypeStruct((n, d), F32),
        grid=(n // TOK_TILE,),
        in_specs=[tok] + _gather_specs(n // TOK_TILE) + [_full(mod.shape)],
        out_specs=tok,
        scratch_shapes=_gather_scratch(d),
        compiler_params=_cparams(("arbitrary",)),
        name="final_residual",
    )(x_lat, pos, pos, f_sorted, mod)


def _block_diag_ones():
    r = np.arange(MXU_DIM) // HEAD_DIM
    return jnp.asarray((r[:, None] == r[None, :]).astype(np.float32), BF16)


def _router_weights(wg, bg, we, be):
    d = wg.shape[0]
    n = MOE_GROUPS + MOE_GROUPS * MOE_EPG
    wt = jnp.concatenate([wg, we, jnp.zeros((d, LANES - n), F32)], axis=1).T
    b = jnp.concatenate([bg, be, jnp.zeros((LANES - n,), F32)])
    hi = _bf(wt)
    lo = _bf(wt - hi.astype(F32))
    return jnp.concatenate([hi, lo], axis=0), hi, b


def _expert_weights(w_gate, w_up, w_down):
    g, e, d, f = w_gate.shape
    return (_bf(w_gate).reshape(g * e, d, f), _bf(w_up).reshape(g * e, d, f), _bf(w_down).reshape(g * e, f, d))


def kernel(x, c, ctx, c_ctx, mod_w, mod_b, norm_mix, norm_ffn, ev_w_in, ev_w_out, gla_gate_w, gla_gate_b,
           gla_out_norm, att_q_norm, att_k_norm, od_w_in, od_w_out, swa_sink, swa_q_norm, swa_k_norm,
           router_group_w, router_group_b, router_expert_w, router_expert_b, exp_w_gate, exp_w_up, exp_w_down):
    n_batch, seq, d = x.shape
    lc = ctx.shape[1]
    depth = mod_w.shape[0]
    n_lat = n_batch * seq
    tiles_per_batch = seq // TOK_TILE
    geom = (n_lat // TOK_TILE, tiles_per_batch, n_batch)
    assert depth == 2 and seq % TOK_TILE == 0 and (n_batch * lc) % TOK_TILE == 0 and n_batch < 16

    x_lat, x_ctx = x.reshape(n_lat, d), ctx.reshape(n_batch * lc, d)
    n_all = n_lat + n_batch * lc
    c_rows = jnp.zeros((16, d), F32).at[:n_batch].set(c).at[n_batch].set(c_ctx)
    mod = _modulation(c_rows, mod_w, mod_b)
    tables = _rope_tables(seq)
    bd = _block_diag_ones()
    row2 = lambda v: v.reshape(1, -1)
    tile_gain = lambda gvec, reps: jnp.tile(gvec, reps).reshape(1, -1)

    w0 = ev_w_in[0]
    seg = np.cumsum([0, 256, 256, 512, 512, 32, 512, 128, 128])
    cols = lambda i: w0[:, seg[i]:seg[i + 1]]
    w_even = _bf(jnp.concatenate([cols(0), cols(1), cols(2), cols(3), cols(5), cols(6), cols(4),
                                  jnp.zeros((d, EV_END - EV_LR - 2 * GLA_GATE_RANK), F32)], axis=1))
    wvt_even = _bf(cols(7).T)
    hk = GLA_HEADS * GLA_DK
    gw = jnp.zeros((LANES, 2 * hk), F32)
    gw = gw.at[0:GLA_GATE_RANK, 0:hk].set(gla_gate_w[0, 0])
    gw = gw.at[GLA_GATE_RANK:2 * GLA_GATE_RANK, hk:2 * hk].set(gla_gate_w[0, 1])
    gb = gla_gate_b[0].reshape(1, 2 * hk)
    gq, gk, gv, gr, g, aq, ak, avt = _proj_even(
        x_lat, x_ctx, mod[0], row2(norm_mix[0]), w_even, wvt_even, _bf(gw), gb,
        tile_gain(att_q_norm[0], ATT_HEADS), tile_gain(att_k_norm[0], ATT_KV_HEADS), tables, bd, geom)
    o_lat, o_ctx = _gla(gq, gk, gv, g, n_batch, seq, lc)
    att_lat = _attention("dense", aq, ak, avt, n_batch, seq, lc, ATT_HEADS, ATT_KV_HEADS)
    att_ctx = _attention("ctx", aq, ak, avt, n_batch, seq, lc, ATT_HEADS, ATT_KV_HEADS)
    router = _router_weights(router_group_w[0], router_group_b[0], router_expert_w[0], router_expert_b[0])
    tt = TOK_TILE
    nlt = geom[0]
    gn = row2(gla_out_norm[0])
    w_out0 = _bf(ev_w_out[0])
    hv, ha = GLA_HEADS * GLA_DV, ATT_HEADS * HEAD_DIM
    x_mid, hrow, rt, counts = _out_proj(
        functools.partial(_out_even_kernel, geom),
        (o_lat, o_ctx, gr, att_lat, att_ctx, gn, w_out0, x_lat, x_ctx),
        _lat_ctx_specs((tt, hv), nlt, lead=(2,)) + [pl.BlockSpec((tt, hv), lambda t: (t, 0))]
        + _lat_ctx_specs((tt, ha), nlt) + [_full(gn.shape), _full(w_out0.shape)] + _lat_ctx_specs((tt, d), nlt),
        d, mod[0], row2(norm_ffn[0]), router, n_all, "out_even")
    f0, pos0 = _moe(hrow, rt, counts, *_expert_weights(exp_w_gate[0], exp_w_up[0], exp_w_down[0]))

    w_odd = od_w_in[0]
    x1, q1, k1, v1t = _proj_odd(
        x_mid, f0, pos0, mod[0], mod[1], row2(norm_mix[1]), _bf(w_odd[:, 0:d + LANES]), _bf(w_odd[:, d + LANES:].T),
        tile_gain(swa_q_norm[0], SWA_HEADS), tile_gain(swa_k_norm[0], SWA_KV_HEADS), tables, bd, geom)
    att1 = _attention("window", q1, k1, v1t, n_batch, seq, lc, SWA_HEADS, SWA_KV_HEADS, sink=swa_sink[0])
    router = _router_weights(router_group_w[1], router_group_b[1], router_expert_w[1], router_expert_b[1])
    w_out1 = _bf(od_w_out[0])
    tok = lambda w_: pl.BlockSpec((tt, w_), lambda t: (t, 0))
    x2, hrow1, rt1, counts1 = _out_proj(
        functools.partial(_out_odd_kernel, geom), (att1, w_out1, x1),
        [tok(SWA_HEADS * HEAD_DIM), _full(w_out1.shape), tok(d)],
        d, mod[1], row2(norm_ffn[1]), router, n_lat, "out_odd")
    f1, pos1 = _moe(hrow1, rt1, counts1, *_expert_weights(exp_w_gate[1], exp_w_up[1], exp_w_down[1]))
    out = _final(x2, f1, pos1, mod[1], tiles_per_batch)
    return out.reshape(n_batch, seq, d)
```

```python
import functools

import numpy as np
import jax
import jax.numpy as jnp
from jax import lax
from jax.experimental import pallas as pl
from jax.experimental.pallas import tpu as pltpu

F32 = jnp.float32
BF16 = jnp.bfloat16

GRID_W = 64
HEAD_DIM = 64
AXIS_DIM = HEAD_DIM // 2
ROPE_THETA = 10000.0
EPS = 1e-6
N_MOD = 6
GLA_HEADS = 4
GLA_DK = 64
GLA_DV = 128
GLA_GATE_RANK = 16
GLA_GATE_NORM = 16.0
GLA_CHUNK = 64
ATT_HEADS = 8
ATT_KV_HEADS = 2
SWA_HEADS = 16
SWA_KV_HEADS = 2
SWA_WINDOW = 128
MOE_GROUPS = 4
MOE_EPG = 4
D_EXPERT = 256
N_PAIRS = 6
N_BUCKETS = MOE_GROUPS * N_PAIRS

LANES = 128
MXU_DIM = 256
TOK_TILE = 512
ATT_Q_TILE = 128
ATT_K_CHUNK = 512
ATT_COL = 4 * HEAD_DIM
WIN_SUB = 4
DENSE_SUB = 2
MOE_TILE = 256
META_LANES = LANES
VMEM_LIMIT = 56 * 1024 * 1024
NEG_BIG = -1e30
LOG2E = 1.4426950408889634
Q_SCALE = HEAD_DIM ** -0.5 * LOG2E


def _bf(x):
    return x.astype(BF16)


def _split2(x):
    hi = _bf(x)
    lo = _bf(x - hi.astype(F32))
    return hi, lo


def _dot(a, b):
    return jnp.dot(a, b, preferred_element_type=F32)


def _dot_nt(a, b):
    return lax.dot_general(a, b, (((1,), (1,)), ((), ())), preferred_element_type=F32)


def _dot_tn(a, b):
    return lax.dot_general(a, b, (((0,), (0,)), ((), ())), preferred_element_type=F32)


def _silu(x):
    return x / (1.0 + jnp.exp(-x))


def _rms(x):
    return x * lax.rsqrt(jnp.mean(x * x, axis=-1, keepdims=True) + EPS)


def _cparams(sem):
    return pltpu.CompilerParams(dimension_semantics=sem, vmem_limit_bytes=VMEM_LIMIT)


def _full(shape):
    n = len(shape)
    return pl.BlockSpec(shape, lambda *_: (0,) * n)


def _mod_kernel(c_ref, w_ref, b_ref, o_ref):
    c = c_ref[...]
    ch, cl = _split2(_silu(c))
    wh, wl = _split2(w_ref[0])
    o_ref[0] = _dot(ch, wh) + _dot(ch, wl) + _dot(cl, wh) + b_ref[0]


def _modulation(c_rows, mod_w, mod_b):
    depth, d, n = mod_w.shape
    tn = n // 4
    return pl.pallas_call(
        _mod_kernel,
        out_shape=jax.ShapeDtypeStruct((depth, 16, n), F32),
        grid=(depth, n // tn),
        in_specs=[pl.BlockSpec((16, d), lambda i, j: (0, 0)),
                  pl.BlockSpec((1, d, tn), lambda i, j: (i, 0, j)),
                  pl.BlockSpec((1, 1, tn), lambda i, j: (i, 0, j))],
        out_specs=pl.BlockSpec((1, 16, tn), lambda i, j: (i, 0, j)),
        compiler_params=_cparams(("arbitrary", "arbitrary")),
        name="modulation",
    )(c_rows, mod_w, mod_b.reshape(depth, 1, n))


def _mod_row(t, n_lat_tiles, tiles_per_batch, n_batch):
    return jnp.where(t < n_lat_tiles, t // tiles_per_batch, n_batch)


def _modulated(x, gain, mod_ref, row, k_shift, k_scale):
    d = x.shape[-1]
    shift = mod_ref[pl.ds(row, 1), k_shift * d:(k_shift + 1) * d]
    scale = mod_ref[pl.ds(row, 1), k_scale * d:(k_scale + 1) * d]
    return _rms(x) * gain * (1.0 + scale) + shift


def _rope_tables(seq):
    rows = seq // GRID_W
    row = jnp.repeat(jnp.arange(rows, dtype=jnp.int32), GRID_W)
    col = jnp.tile(jnp.arange(GRID_W, dtype=jnp.int32), rows)
    inv_freq = ROPE_THETA ** (-jnp.arange(0, AXIS_DIM, 2, dtype=F32) / AXIS_DIM)
    ang = jnp.stack([row[:, None] * inv_freq, col[:, None] * inv_freq], axis=1)
    cos, sin = jnp.cos(ang), jnp.sin(ang)
    zero = jnp.zeros_like(sin)
    cos64 = jnp.concatenate([cos[:, 0], cos[:, 0], cos[:, 1], cos[:, 1]], axis=-1)
    sa64 = jnp.concatenate([-sin[:, 0], zero[:, 0], -sin[:, 1], zero[:, 1]], axis=-1)
    sb64 = jnp.concatenate([zero[:, 0], sin[:, 0], zero[:, 1], sin[:, 1]], axis=-1)

    def widen(t, fill):
        t = jnp.concatenate([t, t], axis=-1)
        return jnp.concatenate([t, jnp.full((TOK_TILE, LANES), fill, F32)], axis=0)

    return widen(cos64, 1.0), widen(sa64, 0.0), widen(sb64, 0.0)


def _head_sumsq(y, bd):
    w = y.shape[-1]
    outs = []
    for s in range(0, w, MXU_DIM):
        e = min(s + MXU_DIM, w)
        hi, lo = _split2(y[:, s:e])
        b = bd[0:e - s, 0:e - s]
        outs.append(_dot(hi, b) + _dot(lo, b))
    return outs[0] if len(outs) == 1 else jnp.concatenate(outs, axis=-1)


def _qk_norm_rope(z, gain, bd, cos, sa, sb):
    w = z.shape[-1]
    rep = w // LANES
    ss = _head_sumsq(z * z, bd)
    y = z * lax.rsqrt(ss * (1.0 / HEAD_DIM) + EPS) * gain

    def wide(t):
        return t if rep == 1 else jnp.concatenate([t] * rep, axis=-1)

    return (y * wide(cos) + pltpu.roll(y, w - AXIS_DIM // 2, 1) * wide(sa)
            + pltpu.roll(y, AXIS_DIM // 2, 1) * wide(sb))


def _kv_rep(kv128):
    lane = lax.broadcasted_iota(jnp.int32, kv128.shape, 1)
    sw = pltpu.roll(kv128, HEAD_DIM, 1)
    a0 = jnp.where(lane < HEAD_DIM, kv128, sw)
    a1 = jnp.where(lane < HEAD_DIM, sw, kv128)
    return jnp.concatenate([a0, a0, a1, a1], axis=-1)


def _rope_block(t, n_lat_tiles, tiles_per_batch):
    return jnp.where(t < n_lat_tiles, t % tiles_per_batch, tiles_per_batch)


EV_GQ, EV_GK, EV_GV, EV_GR, EV_AQ, EV_AK, EV_LR, EV_END = 0, 256, 512, 1024, 1536, 2048, 2176, 2304


def _proj_even_kernel(geom, xl_ref, xc_ref, mod_ref, gain_ref, w_ref, wvt_ref, gw_ref, gb_ref, qg_ref, kg_ref,
                      cos_ref, sa_ref, sb_ref, bd_ref,
                      gq_ref, gk_ref, gv_ref, gr_ref, g_ref, aq_ref, ak_ref, avt_ref):
    t = pl.program_id(0)
    row = _mod_row(t, *geom)
    x = jnp.where(t < geom[0], xl_ref[...], xc_ref[...])
    hb = _bf(_modulated(x, gain_ref[...], mod_ref, row, 0, 1))

    def seg(a, b):
        return _dot(hb, w_ref[:, a:b])

    gq_ref[...] = seg(EV_GQ, EV_GK) * (GLA_DK ** -0.5)
    gk_ref[...] = seg(EV_GK, EV_GV)
    gv_ref[...] = _bf(seg(EV_GV, EV_GR))
    gr_ref[...] = seg(EV_GR, EV_AQ)
    zg = _dot(_bf(seg(EV_LR, EV_END)), gw_ref[...]) + gb_ref[...]
    g_ref[...] = -(jnp.maximum(-zg, 0.0) + jnp.log1p(jnp.exp(-jnp.abs(zg)))) * (1.0 / GLA_GATE_NORM)
    bd = bd_ref[...]
    cos, sa, sb = cos_ref[...], sa_ref[...], sb_ref[...]
    aq = _qk_norm_rope(seg(EV_AQ, EV_AK), qg_ref[...], bd, cos, sa, sb)
    aq_ref[...] = _bf(aq * Q_SCALE)
    ak = _qk_norm_rope(seg(EV_AK, EV_LR), kg_ref[...], bd, cos, sa, sb)
    ak_ref[...] = _bf(_kv_rep(ak))
    avt_ref[0] = _bf(_dot_nt(wvt_ref[...], hb))


def _vt_spec():
    return pl.BlockSpec((1, LANES, TOK_TILE), lambda t: (t, 0, 0))


def _lat_ctx_specs(block, n_lat_tiles, lead=()):
    z = (0,) * len(lead)
    lat = pl.BlockSpec(lead + block, lambda t: z + (jnp.minimum(t, n_lat_tiles - 1), 0))
    ctx = pl.BlockSpec(lead + block, lambda t: z + (jnp.maximum(t - n_lat_tiles, 0), 0))
    return [lat, ctx]


def _proj_even(x_lat, x_ctx, mod, gain, w, wvt, gw, gb, qg, kg, tables, bd, geom):
    d = x_lat.shape[1]
    n = x_lat.shape[0] + x_ctx.shape[0]
    n_lat_tiles, tiles_per_batch, _ = geom
    tt = TOK_TILE
    cos, sa, sb = tables
    tok = lambda w_: pl.BlockSpec((tt, w_), lambda t: (t, 0))
    rope = pl.BlockSpec((tt, LANES), lambda t: (_rope_block(t, n_lat_tiles, tiles_per_batch), 0))
    outs = [(256, F32), (256, F32), (512, BF16), (512, F32), (512, F32), (512, BF16), (512, BF16)]
    return pl.pallas_call(
        functools.partial(_proj_even_kernel, geom),
        out_shape=[jax.ShapeDtypeStruct((n, w_), dt) for w_, dt in outs]
        + [jax.ShapeDtypeStruct((n // tt, LANES, tt), BF16)],
        grid=(n // tt,),
        in_specs=_lat_ctx_specs((tt, d), n_lat_tiles)
        + [_full(mod.shape), _full(gain.shape), _full(w.shape), _full(wvt.shape), _full(gw.shape),
           _full(gb.shape), _full(qg.shape), _full(kg.shape), rope, rope, rope, _full(bd.shape)],
        out_specs=[tok(w_) for w_, _ in outs] + [_vt_spec()],
        compiler_params=_cparams(("arbitrary",)),
        name="proj_even",
    )(x_lat, x_ctx, mod, gain, w, wvt, gw, gb, qg, kg, cos, sa, sb, bd)


def _proj_odd_kernel(geom, x_ref, pos_ref, pos_next_ref, fs_hbm, mod_prev_ref, mod_ref, gain_ref, w_ref, wvt_ref,
                     qg_ref, kg_ref, cos_ref, sa_ref, sb_ref, bd_ref, x1_ref, q_ref, k_ref, vt_ref, fbuf, fsem):
    t = pl.program_id(0)
    row = _mod_row(t, *geom)
    d = x_ref.shape[-1]
    gate = mod_prev_ref[pl.ds(row, 1), 5 * d:6 * d]
    f = _gather_tile(t, pl.num_programs(0), pos_ref, pos_next_ref, fs_hbm, fbuf, fsem)
    x1 = x_ref[...] + gate * f
    x1_ref[...] = x1
    hb = _bf(_modulated(x1, gain_ref[...], mod_ref, row, 0, 1))
    bd = bd_ref[...]
    cos, sa, sb = cos_ref[...], sa_ref[...], sb_ref[...]
    q = _qk_norm_rope(_dot(hb, w_ref[:, 0:d]), qg_ref[...], bd, cos, sa, sb)
    q_ref[...] = _bf(q * Q_SCALE)
    k = _qk_norm_rope(_dot(hb, w_ref[:, d:d + LANES]), kg_ref[...], bd, cos, sa, sb)
    k_ref[...] = _bf(_kv_rep(k))
    vt_ref[0] = _bf(_dot_nt(wvt_ref[...], hb))


def _proj_odd(x_all, f_sorted, pos, mod_prev, mod, gain, w, wvt, qg, kg, tables, bd, geom):
    n, d = x_all.shape
    n_lat_tiles, tiles_per_batch, _ = geom
    tt = TOK_TILE
    cos, sa, sb = tables
    tok = lambda w_: pl.BlockSpec((tt, w_), lambda t: (t, 0))
    rope = pl.BlockSpec((tt, LANES), lambda t: (_rope_block(t, n_lat_tiles, tiles_per_batch), 0))
    outs = [(d, F32), (d, BF16), (512, BF16)]
    return pl.pallas_call(
        functools.partial(_proj_odd_kernel, geom),
        out_shape=[jax.ShapeDtypeStruct((n, w_), dt) for w_, dt in outs]
        + [jax.ShapeDtypeStruct((n // tt, LANES, tt), BF16)],
        grid=(n // tt,),
        in_specs=[tok(d)] + _gather_specs(n // tt)
        + [_full(mod_prev.shape), _full(mod.shape), _full(gain.shape), _full(w.shape),
           _full(wvt.shape), _full(qg.shape), _full(kg.shape), rope, rope, rope, _full(bd.shape)],
        out_specs=[tok(w_) for w_, _ in outs] + [_vt_spec()],
        scratch_shapes=_gather_scratch(d),
        compiler_params=_cparams(("arbitrary",)),
        name="proj_odd",
    )(x_all, pos, pos, f_sorted, mod_prev, mod, gain, w, wvt, qg, kg, cos, sa, sb, bd)


N_LEVELS = 6
GLA_MX_ROWS = (N_LEVELS + 2) * GLA_CHUNK


def _gla_constants():
    c = GLA_CHUNK
    mx = np.zeros((2, GLA_MX_ROWS, c), np.float32)
    pat = np.zeros((2, N_LEVELS + 1, c, GLA_HEADS * c), np.float32)
    r = np.arange(c)
    for lvl in range(N_LEVELS):
        h = 1 << lvl
        ref = (r // (2 * h)) * 2 * h + h - 1
        upper = (r % (2 * h)) >= h
        m = np.zeros((c, c), np.float32)
        for i in range(c):
            if upper[i]:
                m[i, ref[i] + 1:i + 1] = 1.0
            else:
                m[i, i + 1:ref[i] + 1] = 1.0
        mx[0, lvl * c:(lvl + 1) * c] = m
        same = (r[:, None] // (2 * h)) == (r[None, :] // (2 * h))
        p = same & upper[:, None] & (~upper)[None, :]
        pat[0, lvl] = np.tile(p.astype(np.float32), (1, GLA_HEADS))
    mx[0, N_LEVELS * c:(N_LEVELS + 1) * c] = (r[None, :] <= r[:, None])
    mx[0, (N_LEVELS + 1) * c:(N_LEVELS + 2) * c] = (r[None, :] > r[:, None])
    pat[0, N_LEVELS] = np.tile(np.eye(c, dtype=np.float32), (1, GLA_HEADS))
    for k in range(N_LEVELS + 2):
        mx[1, k * c:(k + 1) * c] = mx[0, k * c:(k + 1) * c][::-1, ::-1]
    for k in range(N_LEVELS + 1):
        pat[1, k] = np.tile(pat[0, k, :, 0:c][::-1, ::-1], (1, GLA_HEADS))
    return mx, pat


GLA_GROUP = 2


def _gla_chunks(chunks, mx, pat_ref, s_ref):
    c = GLA_CHUNK
    lane_head = lax.broadcasted_iota(jnp.int32, (c, GLA_HEADS * GLA_DK), 1) // GLA_DK

    def stack_heads(a):
        return jnp.concatenate([_bf(jnp.where(lane_head == h, a, 0.0)) for h in range(GLA_HEADS)], axis=0)

    xs = [_dot(mx, jnp.concatenate(_split2(g), axis=0)) for _, _, _, g in chunks]
    atts = [jnp.where(pat_ref[0, N_LEVELS] > 0.0, _dot_nt(_bf(q), stack_heads(k)), 0.0)
            for q, k, _, _ in chunks]
    for lvl in range(N_LEVELS):
        for j, (q, k, _, _) in enumerate(chunks):
            e = jnp.exp(xs[j][lvl * c:(lvl + 1) * c])
            atts[j] = atts[j] + jnp.where(pat_ref[0, lvl] > 0.0, _dot_nt(_bf(q * e), stack_heads(k * e)), 0.0)
    outs, qes, news, a_cols = [], [], [], []
    for j, (q, k, v, _) in enumerate(chunks):
        vhead = lax.broadcasted_iota(jnp.int32, v.shape, 1) // GLA_DV
        v_bd = jnp.concatenate([jnp.where(vhead == h, v, jnp.zeros_like(v)) for h in range(GLA_HEADS)], axis=0)
        outs.append(_dot(_bf(atts[j]), v_bd))
        bcum = xs[j][N_LEVELS * c:(N_LEVELS + 1) * c]
        brem = xs[j][(N_LEVELS + 1) * c:(N_LEVELS + 2) * c]
        qes.append(stack_heads(q * jnp.exp(bcum)))
        kt = jnp.transpose(k * jnp.exp(brem))
        news.append(jnp.concatenate(
            [_dot(_bf(kt[h * GLA_DK:(h + 1) * GLA_DK]), v[:, h * GLA_DV:(h + 1) * GLA_DV])
             for h in range(GLA_HEADS)], axis=0))
        tot = bcum[0:1] + brem[0:1]
        a_cols.append(jnp.transpose(jnp.exp(jnp.broadcast_to(tot, (8, tot.shape[1]))))[:, 0:1])
    s = s_ref[...]
    for j in range(len(chunks)):
        o_inter = _dot(qes[j], _bf(s))
        outs[j] = outs[j] + jnp.concatenate([o_inter[h * c:(h + 1) * c] for h in range(GLA_HEADS)], axis=-1)
        s = a_cols[j] * s + news[j]
    s_ref[...] = s
    return outs


def _gla_kernel(n_lat_chunks, n_ctx_chunks, ql_ref, kl_ref, vl_ref, gl_ref, qc_ref, kc_ref, vc_ref, gc_ref,
                mx_ref, pat_ref, ol_ref, oc_ref, s_ref):
    d = pl.program_id(1)
    c = GLA_CHUNK
    s_ref[...] = jnp.zeros_like(s_ref)
    mx = mx_ref[0]

    def run(n_chunks, q_ref, k_ref, v_ref, g_ref, o_ref):
        def body(i, carry):
            rows = []
            for j in range(GLA_GROUP):
                step = i * GLA_GROUP + j
                ci = jnp.where(d == 0, step, n_chunks - 1 - step)
                rows.append(pl.ds(pl.multiple_of(ci * c, c), c))
            outs = _gla_chunks([(q_ref[r, :], k_ref[r, :], v_ref[r, :], g_ref[r, :]) for r in rows],
                               mx, pat_ref, s_ref)
            for r, o in zip(rows, outs):
                o_ref[0, r, :] = o
            return carry
        lax.fori_loop(0, n_chunks // GLA_GROUP, body, 0)

    run(n_ctx_chunks, qc_ref, kc_ref, vc_ref, gc_ref, oc_ref)
    run(n_lat_chunks, ql_ref, kl_ref, vl_ref, gl_ref, ol_ref)


def _gla(gq, gk, gv, g, n_batch, seq, lc):
    n = gq.shape[0]
    mx_np, pat_np = _gla_constants()
    mx = jnp.asarray(np.concatenate([mx_np, mx_np], axis=2), BF16)
    pat = jnp.asarray(pat_np, F32)
    ctx0 = n_batch * seq // lc
    hk, hv = GLA_HEADS * GLA_DK, GLA_HEADS * GLA_DV
    lat = lambda w_, col: pl.BlockSpec((seq, w_), lambda b, d_: (b, col(d_)))
    ctx = lambda w_, col: pl.BlockSpec((lc, w_), lambda b, d_: (ctx0 + b, col(d_)))
    zero = lambda d_: 0
    same = lambda d_: d_
    return pl.pallas_call(
        functools.partial(_gla_kernel, seq // GLA_CHUNK, lc // GLA_CHUNK),
        out_shape=[jax.ShapeDtypeStruct((2, n_batch * seq, hv), F32),
                   jax.ShapeDtypeStruct((2, n_batch * lc, hv), F32)],
        grid=(n_batch, 2),
        in_specs=[lat(hk, zero), lat(hk, zero), lat(hv, zero), lat(hk, same),
                  ctx(hk, zero), ctx(hk, zero), ctx(hv, zero), ctx(hk, same),
                  pl.BlockSpec((1,) + mx.shape[1:], lambda b, d_: (d_, 0, 0)),
                  pl.BlockSpec((1,) + pat.shape[1:], lambda b, d_: (d_, 0, 0, 0))],
        out_specs=[pl.BlockSpec((1, seq, hv), lambda b, d_: (d_, b, 0)),
                   pl.BlockSpec((1, lc, hv), lambda b, d_: (d_, b, 0))],
        scratch_shapes=[pltpu.VMEM((GLA_HEADS * GLA_DK, GLA_DV), F32)],
        compiler_params=_cparams(("arbitrary", "arbitrary")),
        name="gla_scan",
    )(gq, gk, gv, g, gq, gk, gv, g, mx, pat)


def _stack_heads(q):
    lane_head = lax.broadcasted_iota(jnp.int32, q.shape, 1) // HEAD_DIM
    return jnp.concatenate([jnp.where(lane_head == h, q, jnp.zeros_like(q)) for h in range(4)], axis=0)


ONES_ROWS = 16


def _with_ones(vt):
    return jnp.concatenate([vt, jnp.ones((ONES_ROWS, vt.shape[1]), vt.dtype)], axis=0)


def _attn_store(acc, l, o_ref, u, tq):
    out = acc * (1.0 / l)
    out = jnp.concatenate([out[:, h * tq:(h + 1) * tq] for h in range(4)], axis=0)
    o_ref[u * tq:(u + 1) * tq, :] = _bf(jnp.transpose(out))


def _attn_dense_kernel(n_chunks, n_sub, *refs):
    if n_chunks:
        q_ref, kl_ref, vtl_ref, kc_ref, vtc_ref, o_ref = refs
    else:
        q_ref, kc_ref, vtc_ref, o_ref = refs
    tq = q_ref.shape[0] // n_sub
    cols = 4 * tq
    q4 = [_stack_heads(q_ref[u * tq:(u + 1) * tq, :]) for u in range(n_sub)]

    def scores(c, u):
        if c < n_chunks:
            return _dot_nt(kl_ref[c * ATT_K_CHUNK:(c + 1) * ATT_K_CHUNK, :], q4[u])
        return _dot_nt(kc_ref[...], q4[u])

    def update(carry, st, vt_aug):
        m, acc = carry
        m_new = jnp.maximum(m, jnp.max(st, axis=0, keepdims=True))
        acc = jnp.exp2(m - m_new) * acc + _dot(vt_aug, _bf(jnp.exp2(st - m_new)))
        return m_new, acc

    carry = [(jnp.full((1, cols), NEG_BIG, F32), jnp.zeros((HEAD_DIM + ONES_ROWS, cols), F32))
             for _ in range(n_sub)]
    st = [scores(0, u) for u in range(n_sub)]
    for c in range(n_chunks + 1):
        st_next = [scores(c + 1, u) for u in range(n_sub)] if c < n_chunks else None
        vt_aug = _with_ones(vtl_ref[c] if c < n_chunks else vtc_ref[0])
        carry = [update(carry[u], st[u], vt_aug) for u in range(n_sub)]
        st = st_next
    for u in range(n_sub):
        acc = carry[u][1]
        _attn_store(acc[0:HEAD_DIM], acc[HEAD_DIM:HEAD_DIM + 1], o_ref, u, tq)


def _attn_window_kernel(seq, q_ref, *refs):
    nk = WIN_SUB + 2
    k_refs, v_refs = refs[0:nk], refs[nk:2 * nk]
    kc_ref, vtc_ref, sink_ref, o_ref = refs[2 * nk:]
    tq = SWA_WINDOW
    i = pl.program_id(2)
    kb = jnp.concatenate([r[...] for r in k_refs], axis=0)
    vtb = _with_ones(jnp.concatenate([r[0] for r in v_refs], axis=1))
    kc, vtc, sink = kc_ref[...], _with_ones(vtc_ref[0]), sink_ref[0]
    span = 3 * tq
    sb, sc = [], []
    for u in range(WIN_SUB):
        q4 = _stack_heads(q_ref[u * tq:(u + 1) * tq, :])
        first = i * WIN_SUB + u - 1
        kpos = first * tq + lax.broadcasted_iota(jnp.int32, (span, tq), 0)
        qpos = (first + 1) * tq + lax.broadcasted_iota(jnp.int32, (span, tq), 1)
        ok = (kpos >= 0) & (kpos < seq) & (jnp.abs(kpos - qpos) <= SWA_WINDOW)
        bias = jnp.where(ok, 0.0, NEG_BIG)
        sb.append(_dot_nt(kb[u * tq:u * tq + span], q4) + jnp.concatenate([bias] * 4, axis=1))
        sc.append(_dot_nt(kc, q4))
    ms, pbs, pcs = [], [], []
    for u in range(WIN_SUB):
        m = jnp.maximum(jnp.maximum(jnp.max(sb[u], axis=0, keepdims=True), jnp.max(sc[u], axis=0, keepdims=True)),
                        sink)
        ms.append(m)
        pbs.append(_bf(jnp.exp2(sb[u] - m)))
        pcs.append(_bf(jnp.exp2(sc[u] - m)))
    for u in range(WIN_SUB):
        acc = _dot(vtb[:, u * tq:u * tq + span], pbs[u]) + _dot(vtc, pcs[u])
        l = acc[HEAD_DIM:HEAD_DIM + 1] + jnp.exp2(sink - ms[u])
        _attn_store(acc[0:HEAD_DIM], l, o_ref, u, tq)


def _attention(mode, q, k_rep, vt, n_batch, seq, lc, n_heads, n_kv, sink=None):
    ncol = n_heads * HEAD_DIM // ATT_COL
    col_per_kv = ncol // n_kv
    tq = ATT_Q_TILE
    tpb = seq // TOK_TILE
    n_lat_tiles = n_batch * tpb
    ctx_per_tile = TOK_TILE // lc
    ctx0 = n_batch * seq // lc
    assert ATT_K_CHUNK == TOK_TILE and TOK_TILE % lc == 0 and tq == SWA_WINDOW
    kv = lambda j: j // col_per_kv
    k_ctx = pl.BlockSpec((lc, ATT_COL), lambda b, j, i: (ctx0 + b, kv(j)))
    vt_ctx = pl.BlockSpec((1, HEAD_DIM, lc), lambda b, j, i: (n_lat_tiles + b // ctx_per_tile, kv(j), b % ctx_per_tile))
    if mode == "ctx":
        nq = lc // tq
        q0 = n_batch * seq // tq
        kern = functools.partial(_attn_dense_kernel, 0, 1)
        args = (q, k_rep, vt)
        in_specs = [pl.BlockSpec((tq, ATT_COL), lambda b, j, i: (q0 + b * nq + i, j)), k_ctx, vt_ctx]
    elif mode == "dense":
        tq = DENSE_SUB * ATT_Q_TILE
        nq = seq // tq
        kern = functools.partial(_attn_dense_kernel, tpb, DENSE_SUB)
        args = (q, k_rep, vt, k_rep, vt)
        in_specs = [pl.BlockSpec((tq, ATT_COL), lambda b, j, i: (b * nq + i, j)),
                    pl.BlockSpec((seq, ATT_COL), lambda b, j, i: (b, kv(j))),
                    pl.BlockSpec((tpb, HEAD_DIM, TOK_TILE), lambda b, j, i: (b, kv(j), 0)), k_ctx, vt_ctx]
    else:
        wb = SWA_WINDOW
        tq = WIN_SUB * wb
        nq = seq // tq
        nkb = seq // wb
        per_tile = TOK_TILE // wb
        kern = functools.partial(_attn_window_kernel, seq)
        nb = lambda i, o: jnp.clip(i * WIN_SUB + o, 0, nkb - 1)
        k_nb = lambda o: pl.BlockSpec((wb, ATT_COL), lambda b, j, i: (b * nkb + nb(i, o), kv(j)))
        v_nb = lambda o: pl.BlockSpec(
            (1, HEAD_DIM, wb), lambda b, j, i: (b * tpb + nb(i, o) // per_tile, kv(j), nb(i, o) % per_tile))
        offs = range(-1, WIN_SUB + 1)
        sink_row = jnp.repeat(sink.reshape(ncol, 1, 4), wb, axis=2) * LOG2E
        args = (q,) + (k_rep,) * len(offs) + (vt,) * len(offs) + (k_rep, vt, sink_row)
        in_specs = ([pl.BlockSpec((tq, ATT_COL), lambda b, j, i: (b * nq + i, j))]
                    + [k_nb(o) for o in offs] + [v_nb(o) for o in offs]
                    + [k_ctx, vt_ctx, pl.BlockSpec((1, 1, 4 * wb), lambda b, j, i: (j, 0, 0))])
    return pl.pallas_call(
        kern,
        out_shape=jax.ShapeDtypeStruct((n_batch * nq * tq, n_heads * HEAD_DIM), BF16),
        grid=(n_batch, ncol, nq),
        in_specs=in_specs,
        out_specs=pl.BlockSpec((tq, ATT_COL), lambda b, j, i: (b * nq + i, j)),
        compiler_params=_cparams(("arbitrary", "arbitrary", "arbitrary")),
        name="attention_" + mode,
    )(*args)


ROUTE_ROWS = 8


def _route(h, wrt_ref, wrt_hi_ref, brt_ref):
    hh, hl = _split2(h)
    a = _dot_nt(wrt_ref[...], hh)
    lt = a[0:LANES] + a[LANES:2 * LANES] + _dot_nt(wrt_hi_ref[...], hl) + brt_ref[...]
    col = lambda i: lt[i:i + 1, :]
    gl = [col(i) for i in range(MOE_GROUPS)]
    gmax = functools.reduce(jnp.maximum, gl)
    gi = jnp.where(gl[0] == gmax, 0, jnp.where(gl[1] == gmax, 1, jnp.where(gl[2] == gmax, 2, 3)))
    g_weight = 1.0 / functools.reduce(lambda a, b: a + b, [jnp.exp(x - gmax) for x in gl])
    el = []
    for j in range(MOE_EPG):
        cand = [col(MOE_GROUPS + g * MOE_EPG + j) for g in range(MOE_GROUPS)]
        el.append(jnp.where(gi == 0, cand[0], jnp.where(gi == 1, cand[1], jnp.where(gi == 2, cand[2], cand[3]))))
    m1 = functools.reduce(jnp.maximum, el)
    i1 = jnp.where(el[0] == m1, 0, jnp.where(el[1] == m1, 1, jnp.where(el[2] == m1, 2, 3)))
    rest = [jnp.where(i1 == j, -jnp.inf, el[j]) for j in range(MOE_EPG)]
    m2 = functools.reduce(jnp.maximum, rest)
    i2 = jnp.where(rest[0] == m2, 0, jnp.where(rest[1] == m2, 1, jnp.where(rest[2] == m2, 2, 3)))
    e2 = jnp.exp(m2 - m1)
    w1 = g_weight / (1.0 + e2)
    w2 = g_weight * e2 / (1.0 + e2)
    lo = jnp.minimum(i1, i2)
    hi = jnp.maximum(i1, i2)
    w_lo = jnp.where(i1 == lo, w1, w2)
    w_hi = jnp.where(i1 == lo, w2, w1)
    pair = jnp.where(lo == 0, hi - 1, jnp.where(lo == 1, hi + 1, N_PAIRS - 1))
    return w_lo, w_hi, gi * N_PAIRS + pair


def _out_tail(geom, m, x, mod_ref, gain_ffn_ref, wrt_ref, wrt_hi_ref, brt_ref, triu_ref,
              x_new_ref, hrow_ref, rt_ref, counts_ref, run_ref):
    t = pl.program_id(0)
    row = _mod_row(t, *geom)
    tt, d = x.shape

    @pl.when(t == 0)
    def _():
        run_ref[...] = jnp.zeros_like(run_ref)

    x_new = x + mod_ref[pl.ds(row, 1), 2 * d:3 * d] * m
    x_new_ref[...] = x_new
    h = _modulated(x_new, gain_ffn_ref[...], mod_ref, row, 3, 4)
    hrow_ref[:, 0:d] = h
    w_lo, w_hi, bucket = _route(h, wrt_ref, wrt_hi_ref, brt_ref)
    onehot = lax.broadcasted_iota(jnp.int32, (LANES, tt), 0) == bucket
    ones = jnp.where(onehot, 1.0, 0.0)
    before = _dot(_bf(ones), triu_ref[...]) + run_ref[...]
    rank = jnp.sum(jnp.where(onehot, before, 0.0), axis=0, keepdims=True)
    run = run_ref[...] + jnp.sum(ones, axis=1, keepdims=True)
    run_ref[...] = run
    counts_ref[...] = jnp.broadcast_to(run, counts_ref.shape)
    rec = jnp.concatenate([w_lo, w_hi, bucket.astype(F32), rank, jnp.zeros((ROUTE_ROWS - 4, tt), F32)], axis=0)
    rt_ref[0] = rec
    meta_t = jnp.concatenate([rec, jnp.zeros((META_LANES - ROUTE_ROWS, tt), F32)], axis=0)
    hrow_ref[:, d:d + META_LANES] = jnp.transpose(meta_t)


def _pick(t, n_lat_tiles, lat_ref, ctx_ref):
    return jnp.where(t < n_lat_tiles, lat_ref[...], ctx_ref[...])


def _out_even_kernel(geom, ol_ref, oc_ref, r_ref, attl_ref, attc_ref, gn_ref, w_ref, xl_ref, xc_ref, mod_ref,
                     gain_ffn_ref, wrt_ref, wrt_hi_ref, brt_ref, triu_ref,
                     x_new_ref, hrow_ref, rt_ref, counts_ref, run_ref):
    t = pl.program_id(0)
    o2 = _pick(t, geom[0], ol_ref, oc_ref)
    o = o2[0] + o2[1]
    r = r_ref[...]
    parts = []
    for h in range(GLA_HEADS):
        sl = slice(h * GLA_DV, (h + 1) * GLA_DV)
        parts.append(_rms(o[:, sl]) * gn_ref[...] * _silu(r[:, sl]))
    a = _bf(jnp.concatenate(parts, axis=-1))
    half = a.shape[-1]
    m = _dot(a, w_ref[0:half, :]) + _dot(_pick(t, geom[0], attl_ref, attc_ref), w_ref[half:, :])
    _out_tail(geom, m, _pick(t, geom[0], xl_ref, xc_ref), mod_ref, gain_ffn_ref, wrt_ref, wrt_hi_ref, brt_ref,
              triu_ref, x_new_ref, hrow_ref, rt_ref, counts_ref, run_ref)


def _out_odd_kernel(geom, att_ref, w_ref, x_ref, mod_ref, gain_ffn_ref, wrt_ref, wrt_hi_ref, brt_ref, triu_ref,
                    x_new_ref, hrow_ref, rt_ref, counts_ref, run_ref):
    m = _dot(att_ref[...], w_ref[...])
    _out_tail(geom, m, x_ref[...], mod_ref, gain_ffn_ref, wrt_ref, wrt_hi_ref, brt_ref, triu_ref,
              x_new_ref, hrow_ref, rt_ref, counts_ref, run_ref)


def _out_proj(kernel, lead_args, lead_specs, d, mod, gain_ffn, router, n_rows, name):
    tt = TOK_TILE
    tok = lambda w_: pl.BlockSpec((tt, w_), lambda t: (t, 0))
    r = np.arange(tt)
    triu = jnp.asarray((r[:, None] < r[None, :]).astype(np.float32), BF16)
    wrt, wrt_hi, br = router
    brt = jnp.broadcast_to(br.reshape(LANES, 1), (LANES, tt))
    return pl.pallas_call(
        kernel,
        out_shape=[jax.ShapeDtypeStruct((n_rows, d), F32), jax.ShapeDtypeStruct((n_rows, d + META_LANES), F32),
                   jax.ShapeDtypeStruct((n_rows // tt, ROUTE_ROWS, tt), F32),
                   jax.ShapeDtypeStruct((LANES, LANES), F32)],
        grid=(n_rows // tt,),
        in_specs=lead_specs + [_full(mod.shape), _full(gain_ffn.shape), _full(wrt.shape), _full(wrt_hi.shape),
                               _full(brt.shape), _full(triu.shape)],
        out_specs=[tok(d), tok(d + META_LANES), pl.BlockSpec((1, ROUTE_ROWS, tt), lambda t: (t, 0, 0)),
                   _full((LANES, LANES))],
        scratch_shapes=[pltpu.VMEM((LANES, 1), F32)],
        compiler_params=_cparams(("arbitrary",)),
        name=name,
    )(*lead_args, mod, gain_ffn, wrt, wrt_hi, brt, triu)


SUBLANES = 8


def _for_each_row(n_rows, fn):
    def body(g, c):
        base = pl.multiple_of(g * SUBLANES, SUBLANES)
        for j in range(SUBLANES):
            fn(base + j)
        return c
    lax.fori_loop(0, n_rows // SUBLANES, body, 0)


def _pos_kernel(start_ref, rt_ref, pos_ref):
    bucket = rt_ref[:, 2, :].astype(jnp.int32)
    base = jnp.zeros_like(bucket)
    for b in range(N_BUCKETS):
        base = jnp.where(bucket == b, start_ref[b], base)
    pos_ref[:, 0, :] = base + rt_ref[:, 3, :].astype(jnp.int32)


def _dispatch_kernel(last_ref, pos_ref, x_ref, xs_ref, zbuf, sem, zsem):
    tt = x_ref.shape[0]
    tm = zbuf.shape[0]

    @pl.when(pl.program_id(0) == 0)
    def _():
        zbuf[...] = jnp.zeros_like(zbuf)

        def zero_copy(b):
            return pltpu.make_async_copy(zbuf, xs_ref.at[pl.ds(jnp.maximum(last_ref[b], 0) * tm, tm)], zsem)

        for b in range(2 * N_BUCKETS):
            @pl.when(last_ref[b] >= 0)
            def _():
                zero_copy(b).start()
        for b in range(2 * N_BUCKETS):
            @pl.when(last_ref[b] >= 0)
            def _():
                zero_copy(b).wait()

    _for_each_row(tt, lambda r: pltpu.make_async_copy(
        x_ref.at[pl.ds(r, 1)], xs_ref.at[pl.ds(pos_ref[0, 0, r], 1)], sem).start())
    pltpu.make_async_copy(x_ref, xs_ref.at[pl.ds(0, tt)], sem).wait()


def _moe_mlp_kernel(tlo_ref, thi_ref, nused_ref, xs_ref, wg_lo, wu_lo, wd_lo, wg_hi, wu_hi, wd_hi, f_ref):
    del tlo_ref, thi_ref
    d = f_ref.shape[1]
    used = pl.program_id(0) < nused_ref[0]

    @pl.when(used)
    def _():
        xb = _bf(xs_ref[:, 0:d])
        w_lo = xs_ref[:, d:d + 1]
        w_hi = xs_ref[:, d + 1:d + 2]
        hid_lo = _silu(_dot(xb, wg_lo[0])) * _dot(xb, wu_lo[0]) * w_lo
        hid_hi = _silu(_dot(xb, wg_hi[0])) * _dot(xb, wu_hi[0]) * w_hi
        f_ref[...] = _dot(_bf(hid_lo), wd_lo[0]) + _dot(_bf(hid_hi), wd_hi[0])

    @pl.when(jnp.logical_not(used))
    def _():
        f_ref[...] = jnp.zeros_like(f_ref)


def _moe(hrow, rt, counts, w_gate, w_up, w_down):
    n, dw = hrow.shape
    d = dw - META_LANES
    tm, tt = MOE_TILE, TOK_TILE
    n_tiles = n // tm + N_BUCKETS
    p = n_tiles * tm
    cnt = counts[0:N_BUCKETS, 0].astype(jnp.int32)
    tiles_b = (cnt + tm - 1) // tm
    tile_end = jnp.cumsum(tiles_b)
    start_b = (tile_end - tiles_b) * tm
    n_used = tile_end[-1].reshape(1)
    spare = n_used[0] + jnp.arange(N_BUCKETS, dtype=jnp.int32)
    last_tile = jnp.concatenate([jnp.where(tiles_b > 0, tile_end - 1, -1), jnp.where(spare < n_tiles, spare, -1)])
    pos = pl.pallas_call(
        _pos_kernel,
        out_shape=jax.ShapeDtypeStruct((n // tt, 1, tt), jnp.int32),
        grid_spec=pltpu.PrefetchScalarGridSpec(
            num_scalar_prefetch=1, grid=(1,),
            in_specs=[pl.BlockSpec(rt.shape, lambda i, s: (0, 0, 0))],
            out_specs=pl.BlockSpec((n // tt, 1, tt), lambda i, s: (0, 0, 0))),
        compiler_params=_cparams(("arbitrary",)),
        name="moe_positions",
    )(start_b, rt)
    tile_ids = jnp.arange(n_tiles, dtype=jnp.int32)
    tile_bucket = jnp.sum((tile_ids[:, None] >= tile_end[None, :]).astype(jnp.int32), axis=1)
    tile_bucket = jnp.minimum(tile_bucket, jnp.sum((n_used[0] - 1 >= tile_end).astype(jnp.int32)))
    tile_bucket = jnp.minimum(tile_bucket, N_BUCKETS - 1)
    pair_lo = jnp.asarray([0, 0, 0, 1, 1, 2], jnp.int32)
    pair_hi = jnp.asarray([1, 2, 3, 2, 3, 3], jnp.int32)
    grp = tile_bucket // N_PAIRS
    t_lo = grp * MOE_EPG + pair_lo[tile_bucket % N_PAIRS]
    t_hi = grp * MOE_EPG + pair_hi[tile_bucket % N_PAIRS]

    xs = pl.pallas_call(
        _dispatch_kernel,
        out_shape=jax.ShapeDtypeStruct((p, dw), F32),
        grid_spec=pltpu.PrefetchScalarGridSpec(
            num_scalar_prefetch=1, grid=(n // tt,),
            in_specs=[pl.BlockSpec((1, 1, tt), lambda t, s: (t, 0, 0), memory_space=pltpu.SMEM),
                      pl.BlockSpec((tt, dw), lambda t, s: (t, 0))],
            out_specs=pl.BlockSpec(memory_space=pl.ANY),
            scratch_shapes=[pltpu.VMEM((tm, dw), F32), pltpu.SemaphoreType.DMA(()), pltpu.SemaphoreType.DMA(())]),
        compiler_params=_cparams(("arbitrary",)),
        name="moe_dispatch",
    )(last_tile, pos, hrow)

    f = D_EXPERT
    up_lo = pl.BlockSpec((1, d, f), lambda t, lo, hi, nu: (lo[t], 0, 0))
    up_hi = pl.BlockSpec((1, d, f), lambda t, lo, hi, nu: (hi[t], 0, 0))
    dn_lo = pl.BlockSpec((1, f, d), lambda t, lo, hi, nu: (lo[t], 0, 0))
    dn_hi = pl.BlockSpec((1, f, d), lambda t, lo, hi, nu: (hi[t], 0, 0))
    grid_spec = pltpu.PrefetchScalarGridSpec(
        num_scalar_prefetch=3,
        grid=(n_tiles,),
        in_specs=[pl.BlockSpec((tm, dw), lambda t, lo, hi, nu: (jnp.minimum(t, nu[0] - 1), 0)),
                  up_lo, up_lo, dn_lo, up_hi, up_hi, dn_hi],
        out_specs=pl.BlockSpec((tm, d), lambda t, *_: (t, 0)),
    )
    f_sorted = pl.pallas_call(
        _moe_mlp_kernel,
        out_shape=jax.ShapeDtypeStruct((p, d), F32),
        grid_spec=grid_spec,
        compiler_params=_cparams(("arbitrary",)),
        name="moe_experts",
    )(t_lo, t_hi, n_used, xs, w_gate, w_up, w_down, w_gate, w_up, w_down)
    return f_sorted, pos


def _gather_tile(t, n_t, pos_ref, pos_next_ref, src_hbm, buf, sem):
    tt = buf.shape[1]

    def start(p_ref, slot):
        _for_each_row(tt, lambda r: pltpu.make_async_copy(
            src_hbm.at[pl.ds(p_ref[0, 0, r], 1)], buf.at[slot].at[pl.ds(r, 1)], sem.at[slot]).start())

    slot = t % 2

    @pl.when(t == 0)
    def _():
        start(pos_ref, 0)

    pltpu.make_async_copy(src_hbm.at[pl.ds(0, tt)], buf.at[slot], sem.at[slot]).wait()

    @pl.when(t + 1 < n_t)
    def _():
        start(pos_next_ref, 1 - slot)

    return buf[slot]


def _gather_specs(n_t):
    tt = TOK_TILE
    return [pl.BlockSpec((1, 1, tt), lambda t: (t, 0, 0), memory_space=pltpu.SMEM),
            pl.BlockSpec((1, 1, tt), lambda t: (jnp.minimum(t + 1, n_t - 1), 0, 0), memory_space=pltpu.SMEM),
            pl.BlockSpec(memory_space=pl.ANY)]


def _gather_scratch(d):
    return [pltpu.VMEM((2, TOK_TILE, d), F32), pltpu.SemaphoreType.DMA((2,))]


def _final_kernel(tiles_per_batch, x_ref, pos_ref, pos_next_ref, fs_hbm, mod_ref, o_ref, fbuf, fsem):
    t = pl.program_id(0)
    row = t // tiles_per_batch
    d = x_ref.shape[-1]
    f = _gather_tile(t, pl.num_programs(0), pos_ref, pos_next_ref, fs_hbm, fbuf, fsem)
    o_ref[...] = x_ref[...] + mod_ref[pl.ds(row, 1), 5 * d:6 * d] * f


def _final(x_lat, f_sorted, pos, mod, tiles_per_batch):
    n, d = x_lat.shape
    tok = pl.BlockSpec((TOK_TILE, d), lambda t: (t, 0))
    return pl.pallas_call(
        functools.partial(_final_kernel, tiles_per_batch),
        out_shape=jax.ShapeDtypeStruct((n, d), F32),
        grid=(n // TOK_TILE,),
        in_specs=[tok] + _gather_specs(n // TOK_TILE) + [_full(mod.shape)],
        out_specs=tok,
        scratch_shapes=_gather_scratch(d),
        compiler_params=_cparams(("arbitrary",)),
        name="final_residual",
    )(x_lat, pos, pos, f_sorted, mod)


def _block_diag_ones():
    r = np.arange(MXU_DIM) // HEAD_DIM
    return jnp.asarray((r[:, None] == r[None, :]).astype(np.float32), BF16)


def _router_weights(wg, bg, we, be):
    d = wg.shape[0]
    n = MOE_GROUPS + MOE_GROUPS * MOE_EPG
    wt = jnp.concatenate([wg, we, jnp.zeros((d, LANES - n), F32)], axis=1).T
    b = jnp.concatenate([bg, be, jnp.zeros((LANES - n,), F32)])
    hi = _bf(wt)
    lo = _bf(wt - hi.astype(F32))
    return jnp.concatenate([hi, lo], axis=0), hi, b


def _expert_weights(w_gate, w_up, w_down):
    g, e, d, f = w_gate.shape
    return (_bf(w_gate).reshape(g * e, d, f), _bf(w_up).reshape(g * e, d, f), _bf(w_down).reshape(g * e, f, d))


def kernel(x, c, ctx, c_ctx, mod_w, mod_b, norm_mix, norm_ffn, ev_w_in, ev_w_out, gla_gate_w, gla_gate_b,
           gla_out_norm, att_q_norm, att_k_norm, od_w_in, od_w_out, swa_sink, swa_q_norm, swa_k_norm,
           router_group_w, router_group_b, router_expert_w, router_expert_b, exp_w_gate, exp_w_up, exp_w_down):
    n_batch, seq, d = x.shape
    lc = ctx.shape[1]
    depth = mod_w.shape[0]
    n_lat = n_batch * seq
    tiles_per_batch = seq // TOK_TILE
    geom = (n_lat // TOK_TILE, tiles_per_batch, n_batch)
    assert depth == 2 and seq % TOK_TILE == 0 and (n_batch * lc) % TOK_TILE == 0 and n_batch < 16

    x_lat, x_ctx = x.reshape(n_lat, d), ctx.reshape(n_batch * lc, d)
    n_all = n_lat + n_batch * lc
    c_rows = jnp.zeros((16, d), F32).at[:n_batch].set(c).at[n_batch].set(c_ctx)
    mod = _modulation(c_rows, mod_w, mod_b)
    tables = _rope_tables(seq)
    bd = _block_diag_ones()
    row2 = lambda v: v.reshape(1, -1)
    tile_gain = lambda gvec, reps: jnp.tile(gvec, reps).reshape(1, -1)

    w0 = ev_w_in[0]
    seg = np.cumsum([0, 256, 256, 512, 512, 32, 512, 128, 128])
    cols = lambda i: w0[:, seg[i]:seg[i + 1]]
    w_even = _bf(jnp.concatenate([cols(0), cols(1), cols(2), cols(3), cols(5), cols(6), cols(4),
                                  jnp.zeros((d, EV_END - EV_LR - 2 * GLA_GATE_RANK), F32)], axis=1))
    wvt_even = _bf(cols(7).T)
    hk = GLA_HEADS * GLA_DK
    gw = jnp.zeros((LANES, 2 * hk), F32)
    gw = gw.at[0:GLA_GATE_RANK, 0:hk].set(gla_gate_w[0, 0])
    gw = gw.at[GLA_GATE_RANK:2 * GLA_GATE_RANK, hk:2 * hk].set(gla_gate_w[0, 1])
    gb = gla_gate_b[0].reshape(1, 2 * hk)
    gq, gk, gv, gr, g, aq, ak, avt = _proj_even(
        x_lat, x_ctx, mod[0], row2(norm_mix[0]), w_even, wvt_even, _bf(gw), gb,
        tile_gain(att_q_norm[0], ATT_HEADS), tile_gain(att_k_norm[0], ATT_KV_HEADS), tables, bd, geom)
    o_lat, o_ctx = _gla(gq, gk, gv, g, n_batch, seq, lc)
    att_lat = _attention("dense", aq, ak, avt, n_batch, seq, lc, ATT_HEADS, ATT_KV_HEADS)
    att_ctx = _attention("ctx", aq, ak, avt, n_batch, seq, lc, ATT_HEADS, ATT_KV_HEADS)
    router = _router_weights(router_group_w[0], router_group_b[0], router_expert_w[0], router_expert_b[0])
    tt = TOK_TILE
    nlt = geom[0]
    gn = row2(gla_out_norm[0])
    w_out0 = _bf(ev_w_out[0])
    hv, ha = GLA_HEADS * GLA_DV, ATT_HEADS * HEAD_DIM
    x_mid, hrow, rt, counts = _out_proj(
        functools.partial(_out_even_kernel, geom),
        (o_lat, o_ctx, gr, att_lat, att_ctx, gn, w_out0, x_lat, x_ctx),
        _lat_ctx_specs((tt, hv), nlt, lead=(2,)) + [pl.BlockSpec((tt, hv), lambda t: (t, 0))]
        + _lat_ctx_specs((tt, ha), nlt) + [_full(gn.shape), _full(w_out0.shape)] + _lat_ctx_specs((tt, d), nlt),
        d, mod[0], row2(norm_ffn[0]), router, n_all, "out_even")
    f0, pos0 = _moe(hrow, rt, counts, *_expert_weights(exp_w_gate[0], exp_w_up[0], exp_w_down[0]))

    w_odd = od_w_in[0]
    x1, q1, k1, v1t = _proj_odd(
        x_mid, f0, pos0, mod[0], mod[1], row2(norm_mix[1]), _bf(w_odd[:, 0:d + LANES]), _bf(w_odd[:, d + LANES:].T),
        tile_gain(swa_q_norm[0], SWA_HEADS), tile_gain(swa_k_norm[0], SWA_KV_HEADS), tables, bd, geom)
    att1 = _attention("window", q1, k1, v1t, n_batch, seq, lc, SWA_HEADS, SWA_KV_HEADS, sink=swa_sink[0])
    router = _router_weights(router_group_w[1], router_group_b[1], router_expert_w[1], router_expert_b[1])
    w_out1 = _bf(od_w_out[0])
    tok = lambda w_: pl.BlockSpec((tt, w_), lambda t: (t, 0))
    x2, hrow1, rt1, counts1 = _out_proj(
        functools.partial(_out_odd_kernel, geom), (att1, w_out1, x1),
        [tok(SWA_HEADS * HEAD_DIM), _full(w_out1.shape), tok(d)],
        d, mod[1], row2(norm_ffn[1]), router, n_lat, "out_odd")
    f1, pos1 = _moe(hrow1, rt1, counts1, *_expert_weights(exp_w_gate[1], exp_w_up[1], exp_w_down[1]))
    out = _final(x2, f1, pos1, mod[1], tiles_per_batch)
    return out.reshape(n_batch, seq, d)
```

```python
import functools

import numpy as np
import jax
import jax.numpy as jnp
from jax import lax
from jax.experimental import pallas as pl
from jax.experimental.pallas import tpu as pltpu

F32 = jnp.float32
BF16 = jnp.bfloat16

GRID_W = 64
HEAD_DIM = 64
AXIS_DIM = HEAD_DIM // 2
ROPE_THETA = 10000.0
EPS = 1e-6
N_MOD = 6
GLA_HEADS = 4
GLA_DK = 64
GLA_DV = 128
GLA_GATE_RANK = 16
GLA_GATE_NORM = 16.0
GLA_CHUNK = 64
ATT_HEADS = 8
ATT_KV_HEADS = 2
SWA_HEADS = 16
SWA_KV_HEADS = 2
SWA_WINDOW = 128
MOE_GROUPS = 4
MOE_EPG = 4
D_EXPERT = 256
N_PAIRS = 6
N_BUCKETS = MOE_GROUPS * N_PAIRS

LANES = 128
MXU_DIM = 256
TOK_TILE = 512
ATT_Q_TILE = 128
ATT_K_CHUNK = 512
ATT_COL = 4 * HEAD_DIM
WIN_SUB = 4
DENSE_SUB = 2
MOE_TILE = 256
META_LANES = LANES
VMEM_LIMIT = 56 * 1024 * 1024
NEG_BIG = -1e30
LOG2E = 1.4426950408889634
Q_SCALE = HEAD_DIM ** -0.5 * LOG2E


def _bf(x):
    return x.astype(BF16)


def _split2(x):
    hi = _bf(x)
    lo = _bf(x - hi.astype(F32))
    return hi, lo


def _dot(a, b):
    return jnp.dot(a, b, preferred_element_type=F32)


def _dot_nt(a, b):
    return lax.dot_general(a, b, (((1,), (1,)), ((), ())), preferred_element_type=F32)


def _dot_tn(a, b):
    return lax.dot_general(a, b, (((0,), (0,)), ((), ())), preferred_element_type=F32)


def _silu(x):
    return x / (1.0 + jnp.exp(-x))


def _rms(x):
    return x * lax.rsqrt(jnp.mean(x * x, axis=-1, keepdims=True) + EPS)


def _cparams(sem):
    return pltpu.CompilerParams(dimension_semantics=sem, vmem_limit_bytes=VMEM_LIMIT)


def _full(shape):
    n = len(shape)
    return pl.BlockSpec(shape, lambda *_: (0,) * n)


def _mod_kernel(c_ref, w_ref, b_ref, o_ref):
    c = c_ref[...]
    ch, cl = _split2(_silu(c))
    wh, wl = _split2(w_ref[0])
    o_ref[0] = _dot(ch, wh) + _dot(ch, wl) + _dot(cl, wh) + b_ref[0]


def _modulation(c_rows, mod_w, mod_b):
    depth, d, n = mod_w.shape
    tn = n // 4
    return pl.pallas_call(
        _mod_kernel,
        out_shape=jax.ShapeDtypeStruct((depth, 16, n), F32),
        grid=(depth, n // tn),
        in_specs=[pl.BlockSpec((16, d), lambda i, j: (0, 0)),
                  pl.BlockSpec((1, d, tn), lambda i, j: (i, 0, j)),
                  pl.BlockSpec((1, 1, tn), lambda i, j: (i, 0, j))],
        out_specs=pl.BlockSpec((1, 16, tn), lambda i, j: (i, 0, j)),
        compiler_params=_cparams(("arbitrary", "arbitrary")),
        name="modulation",
    )(c_rows, mod_w, mod_b.reshape(depth, 1, n))


def _mod_row(t, n_lat_tiles, tiles_per_batch, n_batch):
    return jnp.where(t < n_lat_tiles, t // tiles_per_batch, n_batch)


def _modulated(x, gain, mod_ref, row, k_shift, k_scale):
    d = x.shape[-1]
    shift = mod_ref[pl.ds(row, 1), k_shift * d:(k_shift + 1) * d]
    scale = mod_ref[pl.ds(row, 1), k_scale * d:(k_scale + 1) * d]
    return _rms(x) * gain * (1.0 + scale) + shift


def _rope_tables(seq):
    rows = seq // GRID_W
    row = np.repeat(np.arange(rows), GRID_W)
    col = np.tile(np.arange(GRID_W), rows)
    inv_freq = ROPE_THETA ** (-np.arange(0, AXIS_DIM, 2, dtype=np.float64) / AXIS_DIM)
    ang = np.stack([row[:, None] * inv_freq, col[:, None] * inv_freq], axis=1)
    cos, sin = np.cos(ang), np.sin(ang)
    zero = np.zeros_like(sin)
    cos64 = np.concatenate([cos[:, 0], cos[:, 0], cos[:, 1], cos[:, 1]], axis=-1)
    sa64 = np.concatenate([-sin[:, 0], zero[:, 0], -sin[:, 1], zero[:, 1]], axis=-1)
    sb64 = np.concatenate([zero[:, 0], sin[:, 0], zero[:, 1], sin[:, 1]], axis=-1)

    def widen(t, fill):
        t = np.concatenate([t, t], axis=-1)
        return jnp.asarray(np.concatenate([t, np.full((TOK_TILE, LANES), fill)], axis=0), F32)

    return widen(cos64, 1.0), widen(sa64, 0.0), widen(sb64, 0.0)


def _head_sumsq(y, bd):
    w = y.shape[-1]
    outs = []
    for s in range(0, w, MXU_DIM):
        e = min(s + MXU_DIM, w)
        hi, lo = _split2(y[:, s:e])
        b = bd[0:e - s, 0:e - s]
        outs.append(_dot(hi, b) + _dot(lo, b))
    return outs[0] if len(outs) == 1 else jnp.concatenate(outs, axis=-1)


def _qk_norm_rope(z, gain, bd, cos, sa, sb):
    w = z.shape[-1]
    rep = w // LANES
    ss = _head_sumsq(z * z, bd)
    y = z * lax.rsqrt(ss * (1.0 / HEAD_DIM) + EPS) * gain

    def wide(t):
        return t if rep == 1 else jnp.concatenate([t] * rep, axis=-1)

    return (y * wide(cos) + pltpu.roll(y, w - AXIS_DIM // 2, 1) * wide(sa)
            + pltpu.roll(y, AXIS_DIM // 2, 1) * wide(sb))


def _kv_rep(kv128):
    lane = lax.broadcasted_iota(jnp.int32, kv128.shape, 1)
    sw = pltpu.roll(kv128, HEAD_DIM, 1)
    a0 = jnp.where(lane < HEAD_DIM, kv128, sw)
    a1 = jnp.where(lane < HEAD_DIM, sw, kv128)
    return jnp.concatenate([a0, a0, a1, a1], axis=-1)


def _rope_block(t, n_lat_tiles, tiles_per_batch):
    return jnp.where(t < n_lat_tiles, t % tiles_per_batch, tiles_per_batch)


EV_GQ, EV_GK, EV_GV, EV_GR, EV_AQ, EV_AK, EV_LR, EV_END = 0, 256, 512, 1024, 1536, 2048, 2176, 2304


def _proj_even_kernel(geom, xl_ref, xc_ref, mod_ref, gain_ref, w_ref, wvt_ref, gw_ref, gb_ref, qg_ref, kg_ref,
                      cos_ref, sa_ref, sb_ref, bd_ref,
                      gq_ref, gk_ref, gv_ref, gr_ref, g_ref, aq_ref, ak_ref, avt_ref):
    t = pl.program_id(0)
    row = _mod_row(t, *geom)
    x = jnp.where(t < geom[0], xl_ref[...], xc_ref[...])
    hb = _bf(_modulated(x, gain_ref[...], mod_ref, row, 0, 1))

    def seg(a, b):
        return _dot(hb, w_ref[:, a:b])

    gq_ref[...] = seg(EV_GQ, EV_GK) * (GLA_DK ** -0.5)
    gk_ref[...] = seg(EV_GK, EV_GV)
    gv_ref[...] = _bf(seg(EV_GV, EV_GR))
    gr_ref[...] = seg(EV_GR, EV_AQ)
    zg = _dot(_bf(seg(EV_LR, EV_END)), gw_ref[...]) + gb_ref[...]
    g_ref[...] = -(jnp.maximum(-zg, 0.0) + jnp.log1p(jnp.exp(-jnp.abs(zg)))) * (1.0 / GLA_GATE_NORM)
    bd = bd_ref[...]
    cos, sa, sb = cos_ref[...], sa_ref[...], sb_ref[...]
    aq = _qk_norm_rope(seg(EV_AQ, EV_AK), qg_ref[...], bd, cos, sa, sb)
    aq_ref[...] = _bf(aq * Q_SCALE)
    ak = _qk_norm_rope(seg(EV_AK, EV_LR), kg_ref[...], bd, cos, sa, sb)
    ak_ref[...] = _bf(_kv_rep(ak))
    avt_ref[0] = _bf(_dot_nt(wvt_ref[...], hb))


def _vt_spec():
    return pl.BlockSpec((1, LANES, TOK_TILE), lambda t: (t, 0, 0))


def _lat_ctx_specs(block, n_lat_tiles, lead=()):
    z = (0,) * len(lead)
    lat = pl.BlockSpec(lead + block, lambda t: z + (jnp.minimum(t, n_lat_tiles - 1), 0))
    ctx = pl.BlockSpec(lead + block, lambda t: z + (jnp.maximum(t - n_lat_tiles, 0), 0))
    return [lat, ctx]


def _proj_even(x_lat, x_ctx, mod, gain, w, wvt, gw, gb, qg, kg, tables, bd, geom):
    d = x_lat.shape[1]
    n = x_lat.shape[0] + x_ctx.shape[0]
    n_lat_tiles, tiles_per_batch, _ = geom
    tt = TOK_TILE
    cos, sa, sb = tables
    tok = lambda w_: pl.BlockSpec((tt, w_), lambda t: (t, 0))
    rope = pl.BlockSpec((tt, LANES), lambda t: (_rope_block(t, n_lat_tiles, tiles_per_batch), 0))
    outs = [(256, F32), (256, F32), (512, BF16), (512, F32), (512, F32), (512, BF16), (512, BF16)]
    return pl.pallas_call(
        functools.partial(_proj_even_kernel, geom),
        out_shape=[jax.ShapeDtypeStruct((n, w_), dt) for w_, dt in outs]
        + [jax.ShapeDtypeStruct((n // tt, LANES, tt), BF16)],
        grid=(n // tt,),
        in_specs=_lat_ctx_specs((tt, d), n_lat_tiles)
        + [_full(mod.shape), _full(gain.shape), _full(w.shape), _full(wvt.shape), _full(gw.shape),
           _full(gb.shape), _full(qg.shape), _full(kg.shape), rope, rope, rope, _full(bd.shape)],
        out_specs=[tok(w_) for w_, _ in outs] + [_vt_spec()],
        compiler_params=_cparams(("arbitrary",)),
        name="proj_even",
    )(x_lat, x_ctx, mod, gain, w, wvt, gw, gb, qg, kg, cos, sa, sb, bd)


def _proj_odd_kernel(geom, x_ref, pos_ref, pos_next_ref, fs_hbm, mod_prev_ref, mod_ref, gain_ref, w_ref, wvt_ref,
                     qg_ref, kg_ref, cos_ref, sa_ref, sb_ref, bd_ref, x1_ref, q_ref, k_ref, vt_ref, fbuf, fsem):
    t = pl.program_id(0)
    row = _mod_row(t, *geom)
    d = x_ref.shape[-1]
    gate = mod_prev_ref[pl.ds(row, 1), 5 * d:6 * d]
    f = _gather_tile(t, pl.num_programs(0), pos_ref, pos_next_ref, fs_hbm, fbuf, fsem, inline_prefetch=True)
    x1 = x_ref[...] + gate * f
    x1_ref[...] = x1
    hb = _bf(_modulated(x1, gain_ref[...], mod_ref, row, 0, 1))
    bd = bd_ref[...]
    cos, sa, sb = cos_ref[...], sa_ref[...], sb_ref[...]
    q = _qk_norm_rope(_dot(hb, w_ref[:, 0:d]), qg_ref[...], bd, cos, sa, sb)
    q_ref[...] = _bf(q * Q_SCALE)
    k = _qk_norm_rope(_dot(hb, w_ref[:, d:d + LANES]), kg_ref[...], bd, cos, sa, sb)
    k_ref[...] = _bf(_kv_rep(k))
    vt_ref[0] = _bf(_dot_nt(wvt_ref[...], hb))
    _gather_drain(t, pl.num_programs(0), fs_hbm, fbuf, fsem)


def _proj_odd(x_all, f_sorted, pos, mod_prev, mod, gain, w, wvt, qg, kg, tables, bd, geom):
    n, d = x_all.shape
    n_lat_tiles, tiles_per_batch, _ = geom
    tt = TOK_TILE
    cos, sa, sb = tables
    tok = lambda w_: pl.BlockSpec((tt, w_), lambda t: (t, 0))
    rope = pl.BlockSpec((tt, LANES), lambda t: (_rope_block(t, n_lat_tiles, tiles_per_batch), 0))
    outs = [(d, F32), (d, BF16), (512, BF16)]
    return pl.pallas_call(
        functools.partial(_proj_odd_kernel, geom),
        out_shape=[jax.ShapeDtypeStruct((n, w_), dt) for w_, dt in outs]
        + [jax.ShapeDtypeStruct((n // tt, LANES, tt), BF16)],
        grid=(n // tt,),
        in_specs=[tok(d)] + _gather_specs(n // tt)
        + [_full(mod_prev.shape), _full(mod.shape), _full(gain.shape), _full(w.shape),
           _full(wvt.shape), _full(qg.shape), _full(kg.shape), rope, rope, rope, _full(bd.shape)],
        out_specs=[tok(w_) for w_, _ in outs] + [_vt_spec()],
        scratch_shapes=_gather_scratch(d),
        compiler_params=_cparams(("arbitrary",)),
        name="proj_odd",
    )(x_all, pos, pos, f_sorted, mod_prev, mod, gain, w, wvt, qg, kg, cos, sa, sb, bd)


N_LEVELS = 6
GLA_MX_ROWS = (N_LEVELS + 2) * GLA_CHUNK


def _gla_constants():
    c = GLA_CHUNK
    mx = np.zeros((2, GLA_MX_ROWS, c), np.float32)
    pat = np.zeros((2, N_LEVELS + 1, c, GLA_HEADS * c), np.float32)
    r = np.arange(c)
    for lvl in range(N_LEVELS):
        h = 1 << lvl
        ref = (r // (2 * h)) * 2 * h + h - 1
        upper = (r % (2 * h)) >= h
        m = np.zeros((c, c), np.float32)
        for i in range(c):
            if upper[i]:
                m[i, ref[i] + 1:i + 1] = 1.0
            else:
                m[i, i + 1:ref[i] + 1] = 1.0
        mx[0, lvl * c:(lvl + 1) * c] = m
        same = (r[:, None] // (2 * h)) == (r[None, :] // (2 * h))
        p = same & upper[:, None] & (~upper)[None, :]
        pat[0, lvl] = np.tile(p.astype(np.float32), (1, GLA_HEADS))
    mx[0, N_LEVELS * c:(N_LEVELS + 1) * c] = (r[None, :] <= r[:, None])
    mx[0, (N_LEVELS + 1) * c:(N_LEVELS + 2) * c] = (r[None, :] > r[:, None])
    pat[0, N_LEVELS] = np.tile(np.eye(c, dtype=np.float32), (1, GLA_HEADS))
    for k in range(N_LEVELS + 2):
        mx[1, k * c:(k + 1) * c] = mx[0, k * c:(k + 1) * c][::-1, ::-1]
    for k in range(N_LEVELS + 1):
        pat[1, k] = np.tile(pat[0, k, :, 0:c][::-1, ::-1], (1, GLA_HEADS))
    return mx, pat


GLA_BLOCK = 1024
GLA_GROUP = 4


def _gla_chunks(chunks, mx, pat_ref, s_ref):
    c = GLA_CHUNK
    lane_head = lax.broadcasted_iota(jnp.int32, (c, GLA_HEADS * GLA_DK), 1) // GLA_DK

    def stack_heads(a):
        return jnp.concatenate([_bf(jnp.where(lane_head == h, a, 0.0)) for h in range(GLA_HEADS)], axis=0)

    xs = [_dot(mx, jnp.concatenate(_split2(g), axis=0)) for _, _, _, g in chunks]
    atts = [jnp.where(pat_ref[0, N_LEVELS] > 0.0, _dot_nt(_bf(q), stack_heads(k)), 0.0)
            for q, k, _, _ in chunks]
    for lvl in range(N_LEVELS):
        for j, (q, k, _, _) in enumerate(chunks):
            e = jnp.exp(xs[j][lvl * c:(lvl + 1) * c])
            atts[j] = atts[j] + jnp.where(pat_ref[0, lvl] > 0.0, _dot_nt(_bf(q * e), stack_heads(k * e)), 0.0)
    outs, qes, news, a_cols = [], [], [], []
    for j, (q, k, v, _) in enumerate(chunks):
        vhead = lax.broadcasted_iota(jnp.int32, v.shape, 1) // GLA_DV
        v_bd = jnp.concatenate([jnp.where(vhead == h, v, jnp.zeros_like(v)) for h in range(GLA_HEADS)], axis=0)
        outs.append(_dot(_bf(atts[j]), v_bd))
        bcum = xs[j][N_LEVELS * c:(N_LEVELS + 1) * c]
        brem = xs[j][(N_LEVELS + 1) * c:(N_LEVELS + 2) * c]
        qes.append(stack_heads(q * jnp.exp(bcum)))
        kt = jnp.transpose(k * jnp.exp(brem))
        news.append(jnp.concatenate(
            [_dot(_bf(kt[h * GLA_DK:(h + 1) * GLA_DK]), v[:, h * GLA_DV:(h + 1) * GLA_DV])
             for h in range(GLA_HEADS)], axis=0))
        tot = bcum[0:1] + brem[0:1]
        a_cols.append(jnp.transpose(jnp.exp(jnp.broadcast_to(tot, (8, tot.shape[1]))))[:, 0:1])
    s = s_ref[...]
    for j in range(len(chunks)):
        o_inter = _dot(qes[j], _bf(s))
        outs[j] = outs[j] + jnp.concatenate([o_inter[h * c:(h + 1) * c] for h in range(GLA_HEADS)], axis=-1)
        s = a_cols[j] * s + news[j]
    s_ref[...] = s
    return outs


def _gla_kernel(n_lat_chunks, n_ctx_chunks, ql_ref, kl_ref, vl_ref, gl_ref, qc_ref, kc_ref, vc_ref, gc_ref,
                mx_ref, pat_ref, ol_ref, oc_ref, s_ref):
    d = pl.program_id(1)
    c = GLA_CHUNK
    mx = mx_ref[0]

    def run(n_chunks, q_ref, k_ref, v_ref, g_ref, o_ref):
        def body(i, carry):
            rows = []
            for j in range(GLA_GROUP):
                step = i * GLA_GROUP + j
                ci = jnp.where(d == 0, step, n_chunks - 1 - step)
                rows.append(pl.ds(pl.multiple_of(ci * c, c), c))
            outs = _gla_chunks([(q_ref[r, :], k_ref[r, :], v_ref[r, :], g_ref[r, :]) for r in rows],
                               mx, pat_ref, s_ref)
            for r, o in zip(rows, outs):
                o_ref[0, r, :] = o
            return carry
        lax.fori_loop(0, n_chunks // GLA_GROUP, body, 0)

    @pl.when(pl.program_id(2) == 0)
    def _():
        s_ref[...] = jnp.zeros_like(s_ref)
        run(n_ctx_chunks, qc_ref, kc_ref, vc_ref, gc_ref, oc_ref)

    run(n_lat_chunks, ql_ref, kl_ref, vl_ref, gl_ref, ol_ref)


def _gla(gq, gk, gv, g, n_batch, seq, lc):
    mx_np, pat_np = _gla_constants()
    mx = jnp.asarray(np.concatenate([mx_np, mx_np], axis=2), BF16)
    pat = jnp.asarray(pat_np, F32)
    ctx0 = n_batch * seq // lc
    hk, hv = GLA_HEADS * GLA_DK, GLA_HEADS * GLA_DV
    blk = min(GLA_BLOCK, seq)
    nb = seq // blk
    assert seq % blk == 0 and blk % (GLA_CHUNK * GLA_GROUP) == 0 and lc % (GLA_CHUNK * GLA_GROUP) == 0
    row = lambda b, d_, i: b * nb + jnp.where(d_ == 0, i, nb - 1 - i)
    lat = lambda w_, col: pl.BlockSpec((blk, w_), lambda b, d_, i: (row(b, d_, i), col(d_)))
    ctx = lambda w_, col: pl.BlockSpec((lc, w_), lambda b, d_, i: (ctx0 + b, col(d_)))
    zero = lambda d_: 0
    same = lambda d_: d_
    return pl.pallas_call(
        functools.partial(_gla_kernel, blk // GLA_CHUNK, lc // GLA_CHUNK),
        out_shape=[jax.ShapeDtypeStruct((2, n_batch * seq, hv), F32),
                   jax.ShapeDtypeStruct((2, n_batch * lc, hv), F32)],
        grid=(n_batch, 2, nb),
        in_specs=[lat(hk, zero), lat(hk, zero), lat(hv, zero), lat(hk, same),
                  ctx(hk, zero), ctx(hk, zero), ctx(hv, zero), ctx(hk, same),
                  pl.BlockSpec((1,) + mx.shape[1:], lambda b, d_, i: (d_, 0, 0)),
                  pl.BlockSpec((1,) + pat.shape[1:], lambda b, d_, i: (d_, 0, 0, 0))],
        out_specs=[pl.BlockSpec((1, blk, hv), lambda b, d_, i: (d_, row(b, d_, i), 0)),
                   pl.BlockSpec((1, lc, hv), lambda b, d_, i: (d_, b, 0))],
        scratch_shapes=[pltpu.VMEM((GLA_HEADS * GLA_DK, GLA_DV), F32)],
        compiler_params=_cparams(("arbitrary", "arbitrary", "arbitrary")),
        name="gla_scan",
    )(gq, gk, gv, g, gq, gk, gv, g, mx, pat)


def _stack_heads(q):
    lane_head = lax.broadcasted_iota(jnp.int32, q.shape, 1) // HEAD_DIM
    return jnp.concatenate([jnp.where(lane_head == h, q, jnp.zeros_like(q)) for h in range(4)], axis=0)


ONES_ROWS = 16


def _with_ones(vt):
    return jnp.concatenate([vt, jnp.ones((ONES_ROWS, vt.shape[1]), vt.dtype)], axis=0)


def _attn_store(acc, l, o_ref, u, tq):
    out = acc * (1.0 / l)
    out = jnp.concatenate([out[:, h * tq:(h + 1) * tq] for h in range(4)], axis=0)
    o_ref[u * tq:(u + 1) * tq, :] = _bf(jnp.transpose(out))


def _attn_dense_kernel(n_chunks, n_sub, *refs):
    if n_chunks:
        q_ref, kl_ref, vtl_ref, kc_ref, vtc_ref, o_ref = refs
    else:
        q_ref, kc_ref, vtc_ref, o_ref = refs
    tq = q_ref.shape[0] // n_sub
    cols = 4 * tq
    q4 = [_stack_heads(q_ref[u * tq:(u + 1) * tq, :]) for u in range(n_sub)]

    def scores(c, u):
        if c < n_chunks:
            return _dot_nt(kl_ref[c * ATT_K_CHUNK:(c + 1) * ATT_K_CHUNK, :], q4[u])
        return _dot_nt(kc_ref[...], q4[u])

    def update(carry, st, vt_aug):
        m, acc = carry
        m_new = jnp.maximum(m, jnp.max(st, axis=0, keepdims=True))
        acc = jnp.exp2(m - m_new) * acc + _dot(vt_aug, _bf(jnp.exp2(st - m_new)))
        return m_new, acc

    carry = [(jnp.full((1, cols), NEG_BIG, F32), jnp.zeros((HEAD_DIM + ONES_ROWS, cols), F32))
             for _ in range(n_sub)]
    st = [scores(0, u) for u in range(n_sub)]
    for c in range(n_chunks + 1):
        st_next = [scores(c + 1, u) for u in range(n_sub)] if c < n_chunks else None
        vt_aug = _with_ones(vtl_ref[c] if c < n_chunks else vtc_ref[0])
        carry = [update(carry[u], st[u], vt_aug) for u in range(n_sub)]
        st = st_next
    for u in range(n_sub):
        acc = carry[u][1]
        _attn_store(acc[0:HEAD_DIM], acc[HEAD_DIM:HEAD_DIM + 1], o_ref, u, tq)


def _attn_window_kernel(seq, q_ref, *refs):
    nk = WIN_SUB + 2
    k_refs, v_refs = refs[0:nk], refs[nk:2 * nk]
    kc_ref, vtc_ref, sink_ref, o_ref = refs[2 * nk:]
    tq = SWA_WINDOW
    i = pl.program_id(2)
    kb = jnp.concatenate([r[...] for r in k_refs], axis=0)
    vtb = _with_ones(jnp.concatenate([r[0] for r in v_refs], axis=1))
    kc, vtc, sink = kc_ref[...], _with_ones(vtc_ref[0]), sink_ref[0]
    span = 3 * tq
    sb, sc = [], []
    for u in range(WIN_SUB):
        q4 = _stack_heads(q_ref[u * tq:(u + 1) * tq, :])
        first = i * WIN_SUB + u - 1
        kpos = first * tq + lax.broadcasted_iota(jnp.int32, (span, tq), 0)
        qpos = (first + 1) * tq + lax.broadcasted_iota(jnp.int32, (span, tq), 1)
        ok = (kpos >= 0) & (kpos < seq) & (jnp.abs(kpos - qpos) <= SWA_WINDOW)
        bias = jnp.where(ok, 0.0, NEG_BIG)
        sb.append(_dot_nt(kb[u * tq:u * tq + span], q4) + jnp.concatenate([bias] * 4, axis=1))
        sc.append(_dot_nt(kc, q4))
    ms, pbs, pcs = [], [], []
    for u in range(WIN_SUB):
        m = jnp.maximum(jnp.maximum(jnp.max(sb[u], axis=0, keepdims=True), jnp.max(sc[u], axis=0, keepdims=True)),
                        sink)
        ms.append(m)
        pbs.append(_bf(jnp.exp2(sb[u] - m)))
        pcs.append(_bf(jnp.exp2(sc[u] - m)))
    for u in range(WIN_SUB):
        acc = _dot(vtb[:, u * tq:u * tq + span], pbs[u]) + _dot(vtc, pcs[u])
        l = acc[HEAD_DIM:HEAD_DIM + 1] + jnp.exp2(sink - ms[u])
        _attn_store(acc[0:HEAD_DIM], l, o_ref, u, tq)


def _attention(mode, q, k_rep, vt, n_batch, seq, lc, n_heads, n_kv, sink=None):
    ncol = n_heads * HEAD_DIM // ATT_COL
    col_per_kv = ncol // n_kv
    tq = ATT_Q_TILE
    tpb = seq // TOK_TILE
    n_lat_tiles = n_batch * tpb
    ctx_per_tile = TOK_TILE // lc
    ctx0 = n_batch * seq // lc
    assert ATT_K_CHUNK == TOK_TILE and TOK_TILE % lc == 0 and tq == SWA_WINDOW
    kv = lambda j: j // col_per_kv
    k_ctx = pl.BlockSpec((lc, ATT_COL), lambda b, j, i: (ctx0 + b, kv(j)))
    vt_ctx = pl.BlockSpec((1, HEAD_DIM, lc), lambda b, j, i: (n_lat_tiles + b // ctx_per_tile, kv(j), b % ctx_per_tile))
    if mode == "ctx":
        nq = lc // tq
        q0 = n_batch * seq // tq
        kern = functools.partial(_attn_dense_kernel, 0, 1)
        args = (q, k_rep, vt)
        in_specs = [pl.BlockSpec((tq, ATT_COL), lambda b, j, i: (q0 + b * nq + i, j)), k_ctx, vt_ctx]
    elif mode == "dense":
        tq = DENSE_SUB * ATT_Q_TILE
        nq = seq // tq
        kern = functools.partial(_attn_dense_kernel, tpb, DENSE_SUB)
        args = (q, k_rep, vt, k_rep, vt)
        in_specs = [pl.BlockSpec((tq, ATT_COL), lambda b, j, i: (b * nq + i, j)),
                    pl.BlockSpec((seq, ATT_COL), lambda b, j, i: (b, kv(j))),
                    pl.BlockSpec((tpb, HEAD_DIM, TOK_TILE), lambda b, j, i: (b, kv(j), 0)), k_ctx, vt_ctx]
    else:
        wb = SWA_WINDOW
        tq = WIN_SUB * wb
        nq = seq // tq
        nkb = seq // wb
        per_tile = TOK_TILE // wb
        kern = functools.partial(_attn_window_kernel, seq)
        nb = lambda i, o: jnp.clip(i * WIN_SUB + o, 0, nkb - 1)
        k_nb = lambda o: pl.BlockSpec((wb, ATT_COL), lambda b, j, i: (b * nkb + nb(i, o), kv(j)))
        v_nb = lambda o: pl.BlockSpec(
            (1, HEAD_DIM, wb), lambda b, j, i: (b * tpb + nb(i, o) // per_tile, kv(j), nb(i, o) % per_tile))
        offs = range(-1, WIN_SUB + 1)
        sink_row = jnp.repeat(sink.reshape(ncol, 1, 4), wb, axis=2) * LOG2E
        args = (q,) + (k_rep,) * len(offs) + (vt,) * len(offs) + (k_rep, vt, sink_row)
        in_specs = ([pl.BlockSpec((tq, ATT_COL), lambda b, j, i: (b * nq + i, j))]
                    + [k_nb(o) for o in offs] + [v_nb(o) for o in offs]
                    + [k_ctx, vt_ctx, pl.BlockSpec((1, 1, 4 * wb), lambda b, j, i: (j, 0, 0))])
    return pl.pallas_call(
        kern,
        out_shape=jax.ShapeDtypeStruct((n_batch * nq * tq, n_heads * HEAD_DIM), BF16),
        grid=(n_batch, ncol, nq),
        in_specs=in_specs,
        out_specs=pl.BlockSpec((tq, ATT_COL), lambda b, j, i: (b * nq + i, j)),
        compiler_params=_cparams(("arbitrary", "arbitrary", "arbitrary")),
        name="attention_" + mode,
    )(*args)


ROUTE_ROWS = 8


def _route(h, wrt_ref, wrt_hi_ref, brt_ref):
    hh, hl = _split2(h)
    a = _dot_nt(wrt_ref[...], hh)
    lt = a[0:LANES] + a[LANES:2 * LANES] + _dot_nt(wrt_hi_ref[...], hl) + brt_ref[...]
    col = lambda i: lt[i:i + 1, :]
    gl = [col(i) for i in range(MOE_GROUPS)]
    gmax = functools.reduce(jnp.maximum, gl)
    gi = jnp.where(gl[0] == gmax, 0, jnp.where(gl[1] == gmax, 1, jnp.where(gl[2] == gmax, 2, 3)))
    g_weight = 1.0 / functools.reduce(lambda a, b: a + b, [jnp.exp(x - gmax) for x in gl])
    el = []
    for j in range(MOE_EPG):
        cand = [col(MOE_GROUPS + g * MOE_EPG + j) for g in range(MOE_GROUPS)]
        el.append(jnp.where(gi == 0, cand[0], jnp.where(gi == 1, cand[1], jnp.where(gi == 2, cand[2], cand[3]))))
    m1 = functools.reduce(jnp.maximum, el)
    i1 = jnp.where(el[0] == m1, 0, jnp.where(el[1] == m1, 1, jnp.where(el[2] == m1, 2, 3)))
    rest = [jnp.where(i1 == j, -jnp.inf, el[j]) for j in range(MOE_EPG)]
    m2 = functools.reduce(jnp.maximum, rest)
    i2 = jnp.where(rest[0] == m2, 0, jnp.where(rest[1] == m2, 1, jnp.where(rest[2] == m2, 2, 3)))
    e2 = jnp.exp(m2 - m1)
    w1 = g_weight / (1.0 + e2)
    w2 = g_weight * e2 / (1.0 + e2)
    lo = jnp.minimum(i1, i2)
    hi = jnp.maximum(i1, i2)
    w_lo = jnp.where(i1 == lo, w1, w2)
    w_hi = jnp.where(i1 == lo, w2, w1)
    pair = jnp.where(lo == 0, hi - 1, jnp.where(lo == 1, hi + 1, N_PAIRS - 1))
    return w_lo, w_hi, gi * N_PAIRS + pair


def _out_tail(geom, m, x, mod_ref, gain_ffn_ref, wrt_ref, wrt_hi_ref, brt_ref, triu_ref,
              x_new_ref, hrow_ref, rt_ref, counts_ref, run_ref):
    t = pl.program_id(0)
    row = _mod_row(t, *geom)
    tt, d = x.shape

    @pl.when(t == 0)
    def _():
        run_ref[...] = jnp.zeros_like(run_ref)

    x_new = x + mod_ref[pl.ds(row, 1), 2 * d:3 * d] * m
    x_new_ref[...] = x_new
    h = _modulated(x_new, gain_ffn_ref[...], mod_ref, row, 3, 4)
    hrow_ref[:, 0:d] = h
    w_lo, w_hi, bucket = _route(h, wrt_ref, wrt_hi_ref, brt_ref)
    onehot = lax.broadcasted_iota(jnp.int32, (LANES, tt), 0) == bucket
    ones = jnp.where(onehot, 1.0, 0.0)
    before = _dot(_bf(ones), triu_ref[...]) + run_ref[...]
    rank = jnp.sum(jnp.where(onehot, before, 0.0), axis=0, keepdims=True)
    run = run_ref[...] + jnp.sum(ones, axis=1, keepdims=True)
    run_ref[...] = run
    counts_ref[...] = jnp.broadcast_to(run, counts_ref.shape)
    rec = jnp.concatenate([w_lo, w_hi, bucket.astype(F32), rank, jnp.zeros((ROUTE_ROWS - 4, tt), F32)], axis=0)
    rt_ref[0] = rec
    meta_t = jnp.concatenate([rec, jnp.zeros((META_LANES - ROUTE_ROWS, tt), F32)], axis=0)
    hrow_ref[:, d:d + META_LANES] = jnp.transpose(meta_t)


def _pick(t, n_lat_tiles, lat_ref, ctx_ref):
    return jnp.where(t < n_lat_tiles, lat_ref[...], ctx_ref[...])


def _out_even_kernel(geom, ol_ref, oc_ref, r_ref, attl_ref, attc_ref, gn_ref, w_ref, xl_ref, xc_ref, mod_ref,
                     gain_ffn_ref, wrt_ref, wrt_hi_ref, brt_ref, triu_ref,
                     x_new_ref, hrow_ref, rt_ref, counts_ref, run_ref):
    t = pl.program_id(0)
    o2 = _pick(t, geom[0], ol_ref, oc_ref)
    o = o2[0] + o2[1]
    r = r_ref[...]
    parts = []
    for h in range(GLA_HEADS):
        sl = slice(h * GLA_DV, (h + 1) * GLA_DV)
        parts.append(_rms(o[:, sl]) * gn_ref[...] * _silu(r[:, sl]))
    a = _bf(jnp.concatenate(parts, axis=-1))
    half = a.shape[-1]
    m = _dot(a, w_ref[0:half, :]) + _dot(_pick(t, geom[0], attl_ref, attc_ref), w_ref[half:, :])
    _out_tail(geom, m, _pick(t, geom[0], xl_ref, xc_ref), mod_ref, gain_ffn_ref, wrt_ref, wrt_hi_ref, brt_ref,
              triu_ref, x_new_ref, hrow_ref, rt_ref, counts_ref, run_ref)


def _out_odd_kernel(geom, att_ref, w_ref, x_ref, mod_ref, gain_ffn_ref, wrt_ref, wrt_hi_ref, brt_ref, triu_ref,
                    x_new_ref, hrow_ref, rt_ref, counts_ref, run_ref):
    m = _dot(att_ref[...], w_ref[...])
    _out_tail(geom, m, x_ref[...], mod_ref, gain_ffn_ref, wrt_ref, wrt_hi_ref, brt_ref, triu_ref,
              x_new_ref, hrow_ref, rt_ref, counts_ref, run_ref)


def _out_proj(kernel, lead_args, lead_specs, d, mod, gain_ffn, router, n_rows, name):
    tt = TOK_TILE
    tok = lambda w_: pl.BlockSpec((tt, w_), lambda t: (t, 0))
    r = np.arange(tt)
    triu = jnp.asarray((r[:, None] < r[None, :]).astype(np.float32), BF16)
    wrt, wrt_hi, br = router
    brt = jnp.broadcast_to(br.reshape(LANES, 1), (LANES, tt))
    return pl.pallas_call(
        kernel,
        out_shape=[jax.ShapeDtypeStruct((n_rows, d), F32), jax.ShapeDtypeStruct((n_rows, d + META_LANES), F32),
                   jax.ShapeDtypeStruct((n_rows // tt, ROUTE_ROWS, tt), F32),
                   jax.ShapeDtypeStruct((LANES, LANES), F32)],
        grid=(n_rows // tt,),
        in_specs=lead_specs + [_full(mod.shape), _full(gain_ffn.shape), _full(wrt.shape), _full(wrt_hi.shape),
                               _full(brt.shape), _full(triu.shape)],
        out_specs=[tok(d), tok(d + META_LANES), pl.BlockSpec((1, ROUTE_ROWS, tt), lambda t: (t, 0, 0)),
                   _full((LANES, LANES))],
        scratch_shapes=[pltpu.VMEM((LANES, 1), F32)],
        compiler_params=_cparams(("arbitrary",)),
        name=name,
    )(*lead_args, mod, gain_ffn, wrt, wrt_hi, brt, triu)


SUBLANES = 8


def _for_each_row(n_rows, fn):
    def body(g, c):
        base = pl.multiple_of(g * SUBLANES, SUBLANES)
        for j in range(SUBLANES):
            fn(base + j)
        return c
    lax.fori_loop(0, n_rows // SUBLANES, body, 0)


def _pos_kernel(start_ref, rt_ref, pos_ref):
    bucket = rt_ref[:, 2, :].astype(jnp.int32)
    base = jnp.zeros_like(bucket)
    for b in range(N_BUCKETS):
        base = jnp.where(bucket == b, start_ref[b], base)
    pos_ref[:, 0, :] = base + rt_ref[:, 3, :].astype(jnp.int32)


def _dispatch_kernel(last_ref, pos_ref, x_ref, xs_ref, zbuf, sem, zsem):
    tt = x_ref.shape[0]
    tm = zbuf.shape[0]

    @pl.when(pl.program_id(0) == 0)
    def _():
        zbuf[...] = jnp.zeros_like(zbuf)

        def zero_copy(b):
            return pltpu.make_async_copy(zbuf, xs_ref.at[pl.ds(jnp.maximum(last_ref[b], 0) * tm, tm)], zsem)

        for b in range(2 * N_BUCKETS):
            @pl.when(last_ref[b] >= 0)
            def _():
                zero_copy(b).start()
        for b in range(2 * N_BUCKETS):
            @pl.when(last_ref[b] >= 0)
            def _():
                zero_copy(b).wait()

    _for_each_row(tt, lambda r: pltpu.make_async_copy(
        x_ref.at[pl.ds(r, 1)], xs_ref.at[pl.ds(pos_ref[0, 0, r], 1)], sem).start())
    pltpu.make_async_copy(x_ref, xs_ref.at[pl.ds(0, tt)], sem).wait()


def _moe_mlp_kernel(tlo_ref, thi_ref, nused_ref, xs_ref, wg_lo, wu_lo, wd_lo, wg_hi, wu_hi, wd_hi, f_ref):
    del tlo_ref, thi_ref
    d = f_ref.shape[1]
    used = pl.program_id(0) < nused_ref[0]

    @pl.when(used)
    def _():
        xb = _bf(xs_ref[:, 0:d])
        w_lo = xs_ref[:, d:d + 1]
        w_hi = xs_ref[:, d + 1:d + 2]
        hid_lo = _silu(_dot(xb, wg_lo[0])) * _dot(xb, wu_lo[0]) * w_lo
        hid_hi = _silu(_dot(xb, wg_hi[0])) * _dot(xb, wu_hi[0]) * w_hi
        f_ref[...] = _dot(_bf(hid_lo), wd_lo[0]) + _dot(_bf(hid_hi), wd_hi[0])

    @pl.when(jnp.logical_not(used))
    def _():
        f_ref[...] = jnp.zeros_like(f_ref)


def _moe(hrow, rt, counts, experts, layer):
    n, dw = hrow.shape
    d = dw - META_LANES
    tm, tt = MOE_TILE, TOK_TILE
    n_tiles = n // tm + N_BUCKETS
    p = n_tiles * tm
    cnt = counts[0:N_BUCKETS, 0].astype(jnp.int32)
    tiles_b = (cnt + tm - 1) // tm
    tile_end = jnp.cumsum(tiles_b)
    start_b = (tile_end - tiles_b) * tm
    n_used = tile_end[-1].reshape(1)
    spare = n_used[0] + jnp.arange(N_BUCKETS, dtype=jnp.int32)
    last_tile = jnp.concatenate([jnp.where(tiles_b > 0, tile_end - 1, -1), jnp.where(spare < n_tiles, spare, -1)])
    pos = pl.pallas_call(
        _pos_kernel,
        out_shape=jax.ShapeDtypeStruct((n // tt, 1, tt), jnp.int32),
        grid_spec=pltpu.PrefetchScalarGridSpec(
            num_scalar_prefetch=1, grid=(1,),
            in_specs=[pl.BlockSpec(rt.shape, lambda i, s: (0, 0, 0))],
            out_specs=pl.BlockSpec((n // tt, 1, tt), lambda i, s: (0, 0, 0))),
        compiler_params=_cparams(("arbitrary",)),
        name="moe_positions",
    )(start_b, rt)
    tile_ids = jnp.arange(n_tiles, dtype=jnp.int32)
    tile_bucket = jnp.sum((tile_ids[:, None] >= tile_end[None, :]).astype(jnp.int32), axis=1)
    tile_bucket = jnp.minimum(tile_bucket, jnp.sum((n_used[0] - 1 >= tile_end).astype(jnp.int32)))
    tile_bucket = jnp.minimum(tile_bucket, N_BUCKETS - 1)
    pair_lo = jnp.asarray([0, 0, 0, 1, 1, 2], jnp.int32)
    pair_hi = jnp.asarray([1, 2, 3, 2, 3, 3], jnp.int32)
    grp = tile_bucket // N_PAIRS
    w_gate, w_up, w_down = experts
    first = layer * MOE_GROUPS * MOE_EPG
    t_lo = first + grp * MOE_EPG + pair_lo[tile_bucket % N_PAIRS]
    t_hi = first + grp * MOE_EPG + pair_hi[tile_bucket % N_PAIRS]

    xs = pl.pallas_call(
        _dispatch_kernel,
        out_shape=jax.ShapeDtypeStruct((p, dw), F32),
        grid_spec=pltpu.PrefetchScalarGridSpec(
            num_scalar_prefetch=1, grid=(n // tt,),
            in_specs=[pl.BlockSpec((1, 1, tt), lambda t, s: (t, 0, 0), memory_space=pltpu.SMEM),
                      pl.BlockSpec((tt, dw), lambda t, s: (t, 0))],
            out_specs=pl.BlockSpec(memory_space=pl.ANY),
            scratch_shapes=[pltpu.VMEM((tm, dw), F32), pltpu.SemaphoreType.DMA(()), pltpu.SemaphoreType.DMA(())]),
        compiler_params=_cparams(("arbitrary",)),
        name="moe_dispatch",
    )(last_tile, pos, hrow)

    f = D_EXPERT
    up_lo = pl.BlockSpec((1, d, f), lambda t, lo, hi, nu: (lo[t], 0, 0))
    up_hi = pl.BlockSpec((1, d, f), lambda t, lo, hi, nu: (hi[t], 0, 0))
    dn_lo = pl.BlockSpec((1, f, d), lambda t, lo, hi, nu: (lo[t], 0, 0))
    dn_hi = pl.BlockSpec((1, f, d), lambda t, lo, hi, nu: (hi[t], 0, 0))
    grid_spec = pltpu.PrefetchScalarGridSpec(
        num_scalar_prefetch=3,
        grid=(n_tiles,),
        in_specs=[pl.BlockSpec((tm, dw), lambda t, lo, hi, nu: (jnp.minimum(t, nu[0] - 1), 0)),
                  up_lo, up_lo, dn_lo, up_hi, up_hi, dn_hi],
        out_specs=pl.BlockSpec((tm, d), lambda t, *_: (t, 0)),
    )
    f_sorted = pl.pallas_call(
        _moe_mlp_kernel,
        out_shape=jax.ShapeDtypeStruct((p, d), F32),
        grid_spec=grid_spec,
        compiler_params=_cparams(("arbitrary",)),
        name="moe_experts",
    )(t_lo, t_hi, n_used, xs, w_gate, w_up, w_down, w_gate, w_up, w_down)
    return f_sorted, pos


def _gather_tile(t, n_t, pos_ref, pos_next_ref, src_hbm, buf, sem, inline_prefetch=False):
    tt = buf.shape[1]

    def start(p_ref, slot):
        _for_each_row(tt, lambda r: pltpu.make_async_copy(
            src_hbm.at[pl.ds(p_ref[0, 0, r], 1)], buf.at[slot].at[pl.ds(r, 1)], sem.at[slot]).start())

    slot = t % 2

    def wait(s):
        pltpu.make_async_copy(src_hbm.at[pl.ds(0, tt)], buf.at[s], sem.at[s]).wait()

    @pl.when(t == 0)
    def _():
        start(pos_ref, 0)

    wait(slot)
    if inline_prefetch:
        for r in range(tt):
            pltpu.make_async_copy(src_hbm.at[pl.ds(pos_next_ref[0, 0, r], 1)], buf.at[1 - slot].at[pl.ds(r, 1)],
                                  sem.at[1 - slot]).start()
    else:
        @pl.when(t + 1 < n_t)
        def _():
            start(pos_next_ref, 1 - slot)

    return buf[slot]


def _gather_drain(t, n_t, src_hbm, buf, sem):
    tt = buf.shape[1]

    @pl.when(t == n_t - 1)
    def _():
        pltpu.make_async_copy(src_hbm.at[pl.ds(0, tt)], buf.at[1 - t % 2], sem.at[1 - t % 2]).wait()


def _gather_specs(n_t):
    tt = TOK_TILE
    return [pl.BlockSpec((1, 1, tt), lambda t: (t, 0, 0), memory_space=pltpu.SMEM),
            pl.BlockSpec((1, 1, tt), lambda t: (jnp.minimum(t + 1, n_t - 1), 0, 0), memory_space=pltpu.SMEM),
            pl.BlockSpec(memory_space=pl.ANY)]


def _gather_scratch(d):
    return [pltpu.VMEM((2, TOK_TILE, d), F32), pltpu.SemaphoreType.DMA((2,))]


def _final_kernel(tiles_per_batch, x_ref, pos_ref, pos_next_ref, fs_hbm, mod_ref, o_ref, fbuf, fsem):
    t = pl.program_id(0)
    row = t // tiles_per_batch
    d = x_ref.shape[-1]
    f = _gather_tile(t, pl.num_programs(0), pos_ref, pos_next_ref, fs_hbm, fbuf, fsem)
    o_ref[...] = x_ref[...] + mod_ref[pl.ds(row, 1), 5 * d:6 * d] * f


def _final(x_lat, f_sorted, pos, mod, tiles_per_batch):
    n, d = x_lat.shape
    tok = pl.BlockSpec((TOK_TILE, d), lambda t: (t, 0))
    return pl.pallas_call(
        functools.partial(_final_kernel, tiles_per_batch),
        out_shape=jax.ShapeDtypeStruct((n, d), F32),
        grid=(n // TOK_TILE,),
        in_specs=[tok] + _gather_specs(n // TOK_TILE) + [_full(mod.shape)],
        out_specs=tok,
        scratch_shapes=_gather_scratch(d),
        compiler_params=_cparams(("arbitrary",)),
        name="final_residual",
    )(x_lat, pos, pos, f_sorted, mod)


def _block_diag_ones():
    r = np.arange(MXU_DIM) // HEAD_DIM
    return jnp.asarray((r[:, None] == r[None, :]).astype(np.float32), BF16)


def _router_weights(wg, bg, we, be):
    d = wg.shape[0]
    n = MOE_GROUPS + MOE_GROUPS * MOE_EPG
    wt = jnp.concatenate([wg, we, jnp.zeros((d, LANES - n), F32)], axis=1).T
    b = jnp.concatenate([bg, be, jnp.zeros((LANES - n,), F32)])
    hi = _bf(wt)
    lo = _bf(wt - hi.astype(F32))
    return jnp.concatenate([hi, lo], axis=0), hi, b


def _expert_weights(w_gate, w_up, w_down):
    l, g, e, d, f = w_gate.shape
    n = l * g * e
    return _bf(w_gate).reshape(n, d, f), _bf(w_up).reshape(n, d, f), _bf(w_down).reshape(n, f, d)


def kernel(x, c, ctx, c_ctx, mod_w, mod_b, norm_mix, norm_ffn, ev_w_in, ev_w_out, gla_gate_w, gla_gate_b,
           gla_out_norm, att_q_norm, att_k_norm, od_w_in, od_w_out, swa_sink, swa_q_norm, swa_k_norm,
           router_group_w, router_group_b, router_expert_w, router_expert_b, exp_w_gate, exp_w_up, exp_w_down):
    n_batch, seq, d = x.shape
    lc = ctx.shape[1]
    depth = mod_w.shape[0]
    n_lat = n_batch * seq
    tiles_per_batch = seq // TOK_TILE
    geom = (n_lat // TOK_TILE, tiles_per_batch, n_batch)
    assert depth == 2 and seq % TOK_TILE == 0 and (n_batch * lc) % TOK_TILE == 0 and n_batch < 16

    x_lat, x_ctx = x.reshape(n_lat, d), ctx.reshape(n_batch * lc, d)
    n_all = n_lat + n_batch * lc
    c_rows = jnp.zeros((16, d), F32).at[:n_batch].set(c).at[n_batch].set(c_ctx)
    mod = _modulation(c_rows, mod_w, mod_b)
    tables = _rope_tables(seq)
    bd = _block_diag_ones()
    row2 = lambda v: v.reshape(1, -1)
    tile_gain = lambda gvec, reps: jnp.tile(gvec, reps).reshape(1, -1)

    w0 = ev_w_in[0]
    seg = np.cumsum([0, 256, 256, 512, 512, 32, 512, 128, 128])
    cols = lambda i: w0[:, seg[i]:seg[i + 1]]
    w_even = _bf(jnp.concatenate([cols(0), cols(1), cols(2), cols(3), cols(5), cols(6), cols(4),
                                  jnp.zeros((d, EV_END - EV_LR - 2 * GLA_GATE_RANK), F32)], axis=1))
    wvt_even = _bf(cols(7).T)
    hk = GLA_HEADS * GLA_DK
    gw = jnp.zeros((LANES, 2 * hk), F32)
    gw = gw.at[0:GLA_GATE_RANK, 0:hk].set(gla_gate_w[0, 0])
    gw = gw.at[GLA_GATE_RANK:2 * GLA_GATE_RANK, hk:2 * hk].set(gla_gate_w[0, 1])
    gb = gla_gate_b[0].reshape(1, 2 * hk)
    gq, gk, gv, gr, g, aq, ak, avt = _proj_even(
        x_lat, x_ctx, mod[0], row2(norm_mix[0]), w_even, wvt_even, _bf(gw), gb,
        tile_gain(att_q_norm[0], ATT_HEADS), tile_gain(att_k_norm[0], ATT_KV_HEADS), tables, bd, geom)
    o_lat, o_ctx = _gla(gq, gk, gv, g, n_batch, seq, lc)
    att_lat = _attention("dense", aq, ak, avt, n_batch, seq, lc, ATT_HEADS, ATT_KV_HEADS)
    att_ctx = _attention("ctx", aq, ak, avt, n_batch, seq, lc, ATT_HEADS, ATT_KV_HEADS)
    router = _router_weights(router_group_w[0], router_group_b[0], router_expert_w[0], router_expert_b[0])
    tt = TOK_TILE
    nlt = geom[0]
    gn = row2(gla_out_norm[0])
    w_out0 = _bf(ev_w_out[0])
    hv, ha = GLA_HEADS * GLA_DV, ATT_HEADS * HEAD_DIM
    x_mid, hrow, rt, counts = _out_proj(
        functools.partial(_out_even_kernel, geom),
        (o_lat, o_ctx, gr, att_lat, att_ctx, gn, w_out0, x_lat, x_ctx),
        _lat_ctx_specs((tt, hv), nlt, lead=(2,)) + [pl.BlockSpec((tt, hv), lambda t: (t, 0))]
        + _lat_ctx_specs((tt, ha), nlt) + [_full(gn.shape), _full(w_out0.shape)] + _lat_ctx_specs((tt, d), nlt),
        d, mod[0], row2(norm_ffn[0]), router, n_all, "out_even")
    experts = _expert_weights(exp_w_gate, exp_w_up, exp_w_down)
    f0, pos0 = _moe(hrow, rt, counts, experts, 0)

    w_odd = od_w_in[0]
    x1, q1, k1, v1t = _proj_odd(
        x_mid, f0, pos0, mod[0], mod[1], row2(norm_mix[1]), _bf(w_odd[:, 0:d + LANES]), _bf(w_odd[:, d + LANES:].T),
        tile_gain(swa_q_norm[0], SWA_HEADS), tile_gain(swa_k_norm[0], SWA_KV_HEADS), tables, bd, geom)
    att1 = _attention("window", q1, k1, v1t, n_batch, seq, lc, SWA_HEADS, SWA_KV_HEADS, sink=swa_sink[0])
    router = _router_weights(router_group_w[1], router_group_b[1], router_expert_w[1], router_expert_b[1])
    w_out1 = _bf(od_w_out[0])
    tok = lambda w_: pl.BlockSpec((tt, w_), lambda t: (t, 0))
    x2, hrow1, rt1, counts1 = _out_proj(
        functools.partial(_out_odd_kernel, geom), (att1, w_out1, x1),
        [tok(SWA_HEADS * HEAD_DIM), _full(w_out1.shape), tok(d)],
        d, mod[1], row2(norm_ffn[1]), router, n_lat, "out_odd")
    f1, pos1 = _moe(hrow1, rt1, counts1, experts, 1)
    out = _final(x2, f1, pos1, mod[1], tiles_per_batch)
    return out.reshape(n_batch, seq, d)
```

```python
import functools

import numpy as np
import jax
import jax.numpy as jnp
from jax import lax
from jax.experimental import pallas as pl
from jax.experimental.pallas import tpu as pltpu

F32 = jnp.float32
BF16 = jnp.bfloat16

GRID_W = 64
HEAD_DIM = 64
AXIS_DIM = HEAD_DIM // 2
ROPE_THETA = 10000.0
EPS = 1e-6
N_MOD = 6
GLA_HEADS = 4
GLA_DK = 64
GLA_DV = 128
GLA_GATE_RANK = 16
GLA_GATE_NORM = 16.0
GLA_CHUNK = 64
ATT_HEADS = 8
ATT_KV_HEADS = 2
SWA_HEADS = 16
SWA_KV_HEADS = 2
SWA_WINDOW = 128
MOE_GROUPS = 4
MOE_EPG = 4
D_EXPERT = 256
N_PAIRS = 6
N_BUCKETS = MOE_GROUPS * N_PAIRS

LANES = 128
MXU_DIM = 256
TOK_TILE = 512
ATT_Q_TILE = 128
ATT_K_CHUNK = 512
ATT_COL = 4 * HEAD_DIM
WIN_SUB = 4
DENSE_SUB = 2
MOE_TILE = 256
META_LANES = LANES
VMEM_LIMIT = 56 * 1024 * 1024
NEG_BIG = -1e30
LOG2E = 1.4426950408889634
Q_SCALE = HEAD_DIM ** -0.5 * LOG2E


def _bf(x):
    return x.astype(BF16)


def _split2(x):
    hi = _bf(x)
    lo = _bf(x - hi.astype(F32))
    return hi, lo


def _dot(a, b):
    return jnp.dot(a, b, preferred_element_type=F32)


def _dot_nt(a, b):
    return lax.dot_general(a, b, (((1,), (1,)), ((), ())), preferred_element_type=F32)


def _dot_tn(a, b):
    return lax.dot_general(a, b, (((0,), (0,)), ((), ())), preferred_element_type=F32)


def _silu(x):
    return x / (1.0 + jnp.exp(-x))


def _rms(x):
    return x * lax.rsqrt(jnp.mean(x * x, axis=-1, keepdims=True) + EPS)


def _cparams(sem):
    return pltpu.CompilerParams(dimension_semantics=sem, vmem_limit_bytes=VMEM_LIMIT)


def _full(shape):
    n = len(shape)
    return pl.BlockSpec(shape, lambda *_: (0,) * n)


def _mod_kernel(c_ref, w_ref, b_ref, o_ref):
    c = c_ref[...]
    ch, cl = _split2(_silu(c))
    wh, wl = _split2(w_ref[0])
    o_ref[0] = _dot(ch, wh) + _dot(ch, wl) + _dot(cl, wh) + b_ref[0]


def _modulation(c_rows, mod_w, mod_b):
    depth, d, n = mod_w.shape
    tn = n // 4
    return pl.pallas_call(
        _mod_kernel,
        out_shape=jax.ShapeDtypeStruct((depth, 16, n), F32),
        grid=(depth, n // tn),
        in_specs=[pl.BlockSpec((16, d), lambda i, j: (0, 0)),
                  pl.BlockSpec((1, d, tn), lambda i, j: (i, 0, j)),
                  pl.BlockSpec((1, 1, tn), lambda i, j: (i, 0, j))],
        out_specs=pl.BlockSpec((1, 16, tn), lambda i, j: (i, 0, j)),
        compiler_params=_cparams(("arbitrary", "arbitrary")),
        name="modulation",
    )(c_rows, mod_w, mod_b.reshape(depth, 1, n))


def _mod_row(t, n_lat_tiles, tiles_per_batch, n_batch):
    return jnp.where(t < n_lat_tiles, t // tiles_per_batch, n_batch)


def _modulated(x, gain, mod_ref, row, k_shift, k_scale):
    d = x.shape[-1]
    shift = mod_ref[pl.ds(row, 1), k_shift * d:(k_shift + 1) * d]
    scale = mod_ref[pl.ds(row, 1), k_scale * d:(k_scale + 1) * d]
    return _rms(x) * gain * (1.0 + scale) + shift


def _rope_tables(seq):
    rows = seq // GRID_W
    row = np.repeat(np.arange(rows), GRID_W)
    col = np.tile(np.arange(GRID_W), rows)
    inv_freq = ROPE_THETA ** (-np.arange(0, AXIS_DIM, 2, dtype=np.float64) / AXIS_DIM)
    ang = np.stack([row[:, None] * inv_freq, col[:, None] * inv_freq], axis=1)
    cos, sin = np.cos(ang), np.sin(ang)
    zero = np.zeros_like(sin)
    cos64 = np.concatenate([cos[:, 0], cos[:, 0], cos[:, 1], cos[:, 1]], axis=-1)
    sa64 = np.concatenate([-sin[:, 0], zero[:, 0], -sin[:, 1], zero[:, 1]], axis=-1)
    sb64 = np.concatenate([zero[:, 0], sin[:, 0], zero[:, 1], sin[:, 1]], axis=-1)

    def widen(t, fill):
        t = np.concatenate([t, t], axis=-1)
        return jnp.asarray(np.concatenate([t, np.full((TOK_TILE, LANES), fill)], axis=0), F32)

    return widen(cos64, 1.0), widen(sa64, 0.0), widen(sb64, 0.0)


def _head_sumsq(y, bd):
    w = y.shape[-1]
    outs = []
    for s in range(0, w, MXU_DIM):
        e = min(s + MXU_DIM, w)
        hi, lo = _split2(y[:, s:e])
        b = bd[0:e - s, 0:e - s]
        outs.append(_dot(hi, b) + _dot(lo, b))
    return outs[0] if len(outs) == 1 else jnp.concatenate(outs, axis=-1)


def _qk_norm_rope(z, gain, bd, cos, sa, sb):
    w = z.shape[-1]
    rep = w // LANES
    ss = _head_sumsq(z * z, bd)
    y = z * lax.rsqrt(ss * (1.0 / HEAD_DIM) + EPS) * gain

    def wide(t):
        return t if rep == 1 else jnp.concatenate([t] * rep, axis=-1)

    return (y * wide(cos) + pltpu.roll(y, w - AXIS_DIM // 2, 1) * wide(sa)
            + pltpu.roll(y, AXIS_DIM // 2, 1) * wide(sb))


def _kv_rep(kv128):
    lane = lax.broadcasted_iota(jnp.int32, kv128.shape, 1)
    sw = pltpu.roll(kv128, HEAD_DIM, 1)
    a0 = jnp.where(lane < HEAD_DIM, kv128, sw)
    a1 = jnp.where(lane < HEAD_DIM, sw, kv128)
    return jnp.concatenate([a0, a0, a1, a1], axis=-1)


def _rope_block(t, n_lat_tiles, tiles_per_batch):
    return jnp.where(t < n_lat_tiles, t % tiles_per_batch, tiles_per_batch)


EV_GQ, EV_GK, EV_GV, EV_GR, EV_AQ, EV_AK, EV_LR, EV_END = 0, 256, 512, 1024, 1536, 2048, 2176, 2304


def _proj_even_kernel(geom, xl_ref, xc_ref, mod_ref, gain_ref, w_ref, wvt_ref, gw_ref, gb_ref, qg_ref, kg_ref,
                      cos_ref, sa_ref, sb_ref, bd_ref,
                      gq_ref, gk_ref, gv_ref, gr_ref, g_ref, aq_ref, ak_ref, avt_ref):
    t = pl.program_id(0)
    row = _mod_row(t, *geom)
    x = jnp.where(t < geom[0], xl_ref[...], xc_ref[...])
    hb = _bf(_modulated(x, gain_ref[...], mod_ref, row, 0, 1))

    def seg(a, b):
        return _dot(hb, w_ref[:, a:b])

    gq_ref[...] = seg(EV_GQ, EV_GK) * (GLA_DK ** -0.5)
    gk_ref[...] = seg(EV_GK, EV_GV)
    gv_ref[...] = _bf(seg(EV_GV, EV_GR))
    gr_ref[...] = seg(EV_GR, EV_AQ)
    zg = _dot(_bf(seg(EV_LR, EV_END)), gw_ref[...]) + gb_ref[...]
    g_ref[...] = -(jnp.maximum(-zg, 0.0) + jnp.log1p(jnp.exp(-jnp.abs(zg)))) * (1.0 / GLA_GATE_NORM)
    bd = bd_ref[...]
    cos, sa, sb = cos_ref[...], sa_ref[...], sb_ref[...]
    aq = _qk_norm_rope(seg(EV_AQ, EV_AK), qg_ref[...], bd, cos, sa, sb)
    aq_ref[...] = _bf(aq * Q_SCALE)
    ak = _qk_norm_rope(seg(EV_AK, EV_LR), kg_ref[...], bd, cos, sa, sb)
    ak_ref[...] = _bf(_kv_rep(ak))
    avt_ref[0] = _bf(_dot_nt(wvt_ref[...], hb))


def _vt_spec():
    return pl.BlockSpec((1, LANES, TOK_TILE), lambda t: (t, 0, 0))


def _lat_ctx_specs(block, n_lat_tiles, lead=()):
    z = (0,) * len(lead)
    lat = pl.BlockSpec(lead + block, lambda t: z + (jnp.minimum(t, n_lat_tiles - 1), 0))
    ctx = pl.BlockSpec(lead + block, lambda t: z + (jnp.maximum(t - n_lat_tiles, 0), 0))
    return [lat, ctx]


def _proj_even(x_lat, x_ctx, mod, gain, w, wvt, gw, gb, qg, kg, tables, bd, geom):
    d = x_lat.shape[1]
    n = x_lat.shape[0] + x_ctx.shape[0]
    n_lat_tiles, tiles_per_batch, _ = geom
    tt = TOK_TILE
    cos, sa, sb = tables
    tok = lambda w_: pl.BlockSpec((tt, w_), lambda t: (t, 0))
    rope = pl.BlockSpec((tt, LANES), lambda t: (_rope_block(t, n_lat_tiles, tiles_per_batch), 0))
    outs = [(256, F32), (256, F32), (512, BF16), (512, F32), (512, F32), (512, BF16), (512, BF16)]
    return pl.pallas_call(
        functools.partial(_proj_even_kernel, geom),
        out_shape=[jax.ShapeDtypeStruct((n, w_), dt) for w_, dt in outs]
        + [jax.ShapeDtypeStruct((n // tt, LANES, tt), BF16)],
        grid=(n // tt,),
        in_specs=_lat_ctx_specs((tt, d), n_lat_tiles)
        + [_full(mod.shape), _full(gain.shape), _full(w.shape), _full(wvt.shape), _full(gw.shape),
           _full(gb.shape), _full(qg.shape), _full(kg.shape), rope, rope, rope, _full(bd.shape)],
        out_specs=[tok(w_) for w_, _ in outs] + [_vt_spec()],
        compiler_params=_cparams(("arbitrary",)),
        name="proj_even",
    )(x_lat, x_ctx, mod, gain, w, wvt, gw, gb, qg, kg, cos, sa, sb, bd)


def _proj_odd_kernel(geom, x_ref, pos_ref, pos_next_ref, fs_hbm, mod_prev_ref, mod_ref, gain_ref, w_ref, wvt_ref,
                     qg_ref, kg_ref, cos_ref, sa_ref, sb_ref, bd_ref, x1_ref, q_ref, k_ref, vt_ref, fbuf, fsem):
    t = pl.program_id(0)
    row = _mod_row(t, *geom)
    d = x_ref.shape[-1]
    gate = mod_prev_ref[pl.ds(row, 1), 5 * d:6 * d]
    f = _gather_tile(t, pl.num_programs(0), pos_ref, pos_next_ref, fs_hbm, fbuf, fsem, inline_prefetch=True)
    x1 = x_ref[...] + gate * f
    x1_ref[...] = x1
    hb = _bf(_modulated(x1, gain_ref[...], mod_ref, row, 0, 1))
    bd = bd_ref[...]
    cos, sa, sb = cos_ref[...], sa_ref[...], sb_ref[...]
    q = _qk_norm_rope(_dot(hb, w_ref[:, 0:d]), qg_ref[...], bd, cos, sa, sb)
    q_ref[...] = _bf(q * Q_SCALE)
    k = _qk_norm_rope(_dot(hb, w_ref[:, d:d + LANES]), kg_ref[...], bd, cos, sa, sb)
    k_ref[...] = _bf(_kv_rep(k))
    vt_ref[0] = _bf(_dot_nt(wvt_ref[...], hb))
    _gather_drain(t, pl.num_programs(0), fs_hbm, fbuf, fsem)


def _proj_odd(x_all, f_sorted, pos, mod_prev, mod, gain, w, wvt, qg, kg, tables, bd, geom):
    n, d = x_all.shape
    n_lat_tiles, tiles_per_batch, _ = geom
    tt = TOK_TILE
    cos, sa, sb = tables
    tok = lambda w_: pl.BlockSpec((tt, w_), lambda t: (t, 0))
    rope = pl.BlockSpec((tt, LANES), lambda t: (_rope_block(t, n_lat_tiles, tiles_per_batch), 0))
    outs = [(d, F32), (d, BF16), (512, BF16)]
    return pl.pallas_call(
        functools.partial(_proj_odd_kernel, geom),
        out_shape=[jax.ShapeDtypeStruct((n, w_), dt) for w_, dt in outs]
        + [jax.ShapeDtypeStruct((n // tt, LANES, tt), BF16)],
        grid=(n // tt,),
        in_specs=[tok(d)] + _gather_specs(n // tt)
        + [_full(mod_prev.shape), _full(mod.shape), _full(gain.shape), _full(w.shape),
           _full(wvt.shape), _full(qg.shape), _full(kg.shape), rope, rope, rope, _full(bd.shape)],
        out_specs=[tok(w_) for w_, _ in outs] + [_vt_spec()],
        scratch_shapes=_gather_scratch(d),
        compiler_params=_cparams(("arbitrary",)),
        name="proj_odd",
    )(x_all, pos, pos, f_sorted, mod_prev, mod, gain, w, wvt, qg, kg, cos, sa, sb, bd)


N_LEVELS = 6
GLA_MX_ROWS = (N_LEVELS + 2) * GLA_CHUNK


def _gla_constants():
    c = GLA_CHUNK
    mx = np.zeros((2, GLA_MX_ROWS, c), np.float32)
    pat = np.zeros((2, N_LEVELS + 1, c, GLA_HEADS * c), np.float32)
    r = np.arange(c)
    for lvl in range(N_LEVELS):
        h = 1 << lvl
        ref = (r // (2 * h)) * 2 * h + h - 1
        upper = (r % (2 * h)) >= h
        m = np.zeros((c, c), np.float32)
        for i in range(c):
            if upper[i]:
                m[i, ref[i] + 1:i + 1] = 1.0
            else:
                m[i, i + 1:ref[i] + 1] = 1.0
        mx[0, lvl * c:(lvl + 1) * c] = m
        same = (r[:, None] // (2 * h)) == (r[None, :] // (2 * h))
        p = same & upper[:, None] & (~upper)[None, :]
        pat[0, lvl] = np.tile(p.astype(np.float32), (1, GLA_HEADS))
    mx[0, N_LEVELS * c:(N_LEVELS + 1) * c] = (r[None, :] <= r[:, None])
    mx[0, (N_LEVELS + 1) * c:(N_LEVELS + 2) * c] = (r[None, :] > r[:, None])
    pat[0, N_LEVELS] = np.tile(np.eye(c, dtype=np.float32), (1, GLA_HEADS))
    for k in range(N_LEVELS + 2):
        mx[1, k * c:(k + 1) * c] = mx[0, k * c:(k + 1) * c][::-1, ::-1]
    for k in range(N_LEVELS + 1):
        pat[1, k] = np.tile(pat[0, k, :, 0:c][::-1, ::-1], (1, GLA_HEADS))
    return mx, pat


GLA_BLOCK = 1024
GLA_GROUP = 4


def _gla_chunks(chunks, mx, pat_ref, s_ref):
    c = GLA_CHUNK
    lane_head = lax.broadcasted_iota(jnp.int32, (c, GLA_HEADS * GLA_DK), 1) // GLA_DK

    def stack_heads(a):
        return jnp.concatenate([_bf(jnp.where(lane_head == h, a, 0.0)) for h in range(GLA_HEADS)], axis=0)

    xs = [_dot(mx, jnp.concatenate(_split2(g), axis=0)) for _, _, _, g in chunks]
    atts = [jnp.where(pat_ref[0, N_LEVELS] > 0.0, _dot_nt(_bf(q), stack_heads(k)), 0.0)
            for q, k, _, _ in chunks]
    for lvl in range(N_LEVELS):
        for j, (q, k, _, _) in enumerate(chunks):
            e = jnp.exp(xs[j][lvl * c:(lvl + 1) * c])
            atts[j] = atts[j] + jnp.where(pat_ref[0, lvl] > 0.0, _dot_nt(_bf(q * e), stack_heads(k * e)), 0.0)
    outs, qes, news, a_cols = [], [], [], []
    for j, (q, k, v, _) in enumerate(chunks):
        vhead = lax.broadcasted_iota(jnp.int32, v.shape, 1) // GLA_DV
        v_bd = jnp.concatenate([jnp.where(vhead == h, v, jnp.zeros_like(v)) for h in range(GLA_HEADS)], axis=0)
        outs.append(_dot(_bf(atts[j]), v_bd))
        bcum = xs[j][N_LEVELS * c:(N_LEVELS + 1) * c]
        brem = xs[j][(N_LEVELS + 1) * c:(N_LEVELS + 2) * c]
        qes.append(stack_heads(q * jnp.exp(bcum)))
        kt = jnp.transpose(k * jnp.exp(brem))
        news.append(jnp.concatenate(
            [_dot(_bf(kt[h * GLA_DK:(h + 1) * GLA_DK]), v[:, h * GLA_DV:(h + 1) * GLA_DV])
             for h in range(GLA_HEADS)], axis=0))
        tot = bcum[0:1] + brem[0:1]
        a_cols.append(jnp.transpose(jnp.exp(jnp.broadcast_to(tot, (8, tot.shape[1]))))[:, 0:1])
    s = s_ref[...]
    for j in range(len(chunks)):
        o_inter = _dot(qes[j], _bf(s))
        outs[j] = outs[j] + jnp.concatenate([o_inter[h * c:(h + 1) * c] for h in range(GLA_HEADS)], axis=-1)
        s = a_cols[j] * s + news[j]
    s_ref[...] = s
    return outs


def _gla_kernel(n_lat_chunks, n_ctx_chunks, ql_ref, kl_ref, vl_ref, gl_ref, qc_ref, kc_ref, vc_ref, gc_ref,
                mx_ref, pat_ref, ol_ref, oc_ref, s_ref):
    d = pl.program_id(1)
    c = GLA_CHUNK
    mx = mx_ref[0]

    def run(n_chunks, q_ref, k_ref, v_ref, g_ref, o_ref):
        def body(i, carry):
            rows = []
            for j in range(GLA_GROUP):
                step = i * GLA_GROUP + j
                ci = jnp.where(d == 0, step, n_chunks - 1 - step)
                rows.append(pl.ds(pl.multiple_of(ci * c, c), c))
            outs = _gla_chunks([(q_ref[r, :], k_ref[r, :], v_ref[r, :], g_ref[r, :]) for r in rows],
                               mx, pat_ref, s_ref)
            for r, o in zip(rows, outs):
                o_ref[0, r, :] = o
            return carry
        lax.fori_loop(0, n_chunks // GLA_GROUP, body, 0)

    @pl.when(pl.program_id(2) == 0)
    def _():
        s_ref[...] = jnp.zeros_like(s_ref)
        run(n_ctx_chunks, qc_ref, kc_ref, vc_ref, gc_ref, oc_ref)

    run(n_lat_chunks, ql_ref, kl_ref, vl_ref, gl_ref, ol_ref)


def _gla(gq, gk, gv, g, n_batch, seq, lc):
    mx_np, pat_np = _gla_constants()
    mx = jnp.asarray(np.concatenate([mx_np, mx_np], axis=2), BF16)
    pat = jnp.asarray(pat_np, F32)
    ctx0 = n_batch * seq // lc
    hk, hv = GLA_HEADS * GLA_DK, GLA_HEADS * GLA_DV
    blk = min(GLA_BLOCK, seq)
    nb = seq // blk
    assert seq % blk == 0 and blk % (GLA_CHUNK * GLA_GROUP) == 0 and lc % (GLA_CHUNK * GLA_GROUP) == 0
    row = lambda b, d_, i: b * nb + jnp.where(d_ == 0, i, nb - 1 - i)
    lat = lambda w_, col: pl.BlockSpec((blk, w_), lambda b, d_, i: (row(b, d_, i), col(d_)))
    ctx = lambda w_, col: pl.BlockSpec((lc, w_), lambda b, d_, i: (ctx0 + b, col(d_)))
    zero = lambda d_: 0
    same = lambda d_: d_
    return pl.pallas_call(
        functools.partial(_gla_kernel, blk // GLA_CHUNK, lc // GLA_CHUNK),
        out_shape=[jax.ShapeDtypeStruct((2, n_batch * seq, hv), F32),
                   jax.ShapeDtypeStruct((2, n_batch * lc, hv), F32)],
        grid=(n_batch, 2, nb),
        in_specs=[lat(hk, zero), lat(hk, zero), lat(hv, zero), lat(hk, same),
                  ctx(hk, zero), ctx(hk, zero), ctx(hv, zero), ctx(hk, same),
                  pl.BlockSpec((1,) + mx.shape[1:], lambda b, d_, i: (d_, 0, 0)),
                  pl.BlockSpec((1,) + pat.shape[1:], lambda b, d_, i: (d_, 0, 0, 0))],
        out_specs=[pl.BlockSpec((1, blk, hv), lambda b, d_, i: (d_, row(b, d_, i), 0)),
                   pl.BlockSpec((1, lc, hv), lambda b, d_, i: (d_, b, 0))],
        scratch_shapes=[pltpu.VMEM((GLA_HEADS * GLA_DK, GLA_DV), F32)],
        compiler_params=_cparams(("arbitrary", "arbitrary", "arbitrary")),
        name="gla_scan",
    )(gq, gk, gv, g, gq, gk, gv, g, mx, pat)


def _stack_heads(q):
    lane_head = lax.broadcasted_iota(jnp.int32, q.shape, 1) // HEAD_DIM
    return jnp.concatenate([jnp.where(lane_head == h, q, jnp.zeros_like(q)) for h in range(4)], axis=0)


SAFE_SCORE_BOUND = 60.0
ONES_ROWS = 16


def _with_ones(vt):
    return jnp.concatenate([vt, jnp.ones((ONES_ROWS, vt.shape[1]), vt.dtype)], axis=0)


def _attn_store(acc, l, o_ref, u, tq):
    out = acc * (1.0 / l)
    out = jnp.concatenate([out[:, h * tq:(h + 1) * tq] for h in range(4)], axis=0)
    o_ref[u * tq:(u + 1) * tq, :] = _bf(jnp.transpose(out))


def _attn_dense_kernel(n_chunks, n_sub, *refs):
    if n_chunks:
        bound_ref, q_ref, kl_ref, vtl_ref, kc_ref, vtc_ref, o_ref = refs
    else:
        bound_ref, q_ref, kc_ref, vtc_ref, o_ref = refs
    tq = q_ref.shape[0] // n_sub
    cols = 4 * tq
    q4 = [_stack_heads(q_ref[u * tq:(u + 1) * tq, :]) for u in range(n_sub)]

    def scores(c, u):
        if c < n_chunks:
            return _dot_nt(kl_ref[c * ATT_K_CHUNK:(c + 1) * ATT_K_CHUNK, :], q4[u])
        return _dot_nt(kc_ref[...], q4[u])

    def update(carry, st, vt_aug):
        m, acc = carry
        m_new = jnp.maximum(m, jnp.max(st, axis=0, keepdims=True))
        acc = jnp.exp2(m - m_new) * acc + _dot(vt_aug, _bf(jnp.exp2(st - m_new)))
        return m_new, acc

    def run(fixed_ref):
        if fixed_ref is None:
            carry = [(jnp.full((1, cols), NEG_BIG, F32), jnp.zeros((HEAD_DIM + ONES_ROWS, cols), F32))
                     for _ in range(n_sub)]
        else:
            carry = [jnp.zeros((HEAD_DIM + ONES_ROWS, cols), F32) for _ in range(n_sub)]
        st = [scores(0, u) for u in range(n_sub)]
        for c in range(n_chunks + 1):
            st_next = [scores(c + 1, u) for u in range(n_sub)] if c < n_chunks else None
            vt_aug = _with_ones(vtl_ref[c] if c < n_chunks else vtc_ref[0])
            if fixed_ref is None:
                carry = [update(carry[u], st[u], vt_aug) for u in range(n_sub)]
            else:
                carry = [carry[u] + _dot(vt_aug, _bf(jnp.exp2(st[u] - fixed_ref))) for u in range(n_sub)]
            st = st_next
        for u in range(n_sub):
            acc = carry[u][1] if fixed_ref is None else carry[u]
            _attn_store(acc[0:HEAD_DIM], acc[HEAD_DIM:HEAD_DIM + 1], o_ref, u, tq)

    bound = bound_ref[0]

    @pl.when(bound <= SAFE_SCORE_BOUND)
    def _():
        run(bound)

    @pl.when(bound > SAFE_SCORE_BOUND)
    def _():
        run(None)


def _attn_window_kernel(seq, bound_ref, q_ref, *refs):
    nk = WIN_SUB + 2
    k_refs, v_refs = refs[0:nk], refs[nk:2 * nk]
    kc_ref, vtc_ref, sink_ref, o_ref = refs[2 * nk:]
    tq = SWA_WINDOW
    i = pl.program_id(2)
    kb = jnp.concatenate([r[...] for r in k_refs], axis=0)
    vtb = _with_ones(jnp.concatenate([r[0] for r in v_refs], axis=1))
    kc, vtc, sink = kc_ref[...], _with_ones(vtc_ref[0]), sink_ref[0]
    span = 3 * tq
    sb, sc = [], []
    for u in range(WIN_SUB):
        q4 = _stack_heads(q_ref[u * tq:(u + 1) * tq, :])
        first = i * WIN_SUB + u - 1
        kpos = first * tq + lax.broadcasted_iota(jnp.int32, (span, tq), 0)
        qpos = (first + 1) * tq + lax.broadcasted_iota(jnp.int32, (span, tq), 1)
        ok = (kpos >= 0) & (kpos < seq) & (jnp.abs(kpos - qpos) <= SWA_WINDOW)
        bias = jnp.where(ok, 0.0, NEG_BIG)
        sb.append(_dot_nt(kb[u * tq:u * tq + span], q4) + jnp.concatenate([bias] * 4, axis=1))
        sc.append(_dot_nt(kc, q4))
    def finish(use_bound):
        ms, pbs, pcs = [], [], []
        for u in range(WIN_SUB):
            if use_bound:
                m = jnp.maximum(bound_ref[0], sink)
            else:
                m = jnp.maximum(jnp.maximum(jnp.max(sb[u], axis=0, keepdims=True),
                                            jnp.max(sc[u], axis=0, keepdims=True)), sink)
            ms.append(m)
            pbs.append(_bf(jnp.exp2(sb[u] - m)))
            pcs.append(_bf(jnp.exp2(sc[u] - m)))
        for u in range(WIN_SUB):
            acc = _dot(vtb[:, u * tq:u * tq + span], pbs[u]) + _dot(vtc, pcs[u])
            l = acc[HEAD_DIM:HEAD_DIM + 1] + jnp.exp2(sink - ms[u])
            _attn_store(acc[0:HEAD_DIM], l, o_ref, u, tq)

    @pl.when(bound_ref[0] <= SAFE_SCORE_BOUND)
    def _():
        finish(True)

    @pl.when(bound_ref[0] > SAFE_SCORE_BOUND)
    def _():
        finish(False)


def _score_bound(q_gain, k_gain):
    return (HEAD_DIM * Q_SCALE * 1.02 * jnp.max(jnp.abs(q_gain)) * jnp.max(jnp.abs(k_gain))).reshape(1)


def _attention(mode, q, k_rep, vt, bound, n_batch, seq, lc, n_heads, n_kv, sink=None):
    ncol = n_heads * HEAD_DIM // ATT_COL
    col_per_kv = ncol // n_kv
    tq = ATT_Q_TILE
    tpb = seq // TOK_TILE
    n_lat_tiles = n_batch * tpb
    ctx_per_tile = TOK_TILE // lc
    ctx0 = n_batch * seq // lc
    assert ATT_K_CHUNK == TOK_TILE and TOK_TILE % lc == 0 and tq == SWA_WINDOW
    kv = lambda j: j // col_per_kv
    k_ctx = pl.BlockSpec((lc, ATT_COL), lambda b, j, i: (ctx0 + b, kv(j)))
    vt_ctx = pl.BlockSpec((1, HEAD_DIM, lc), lambda b, j, i: (n_lat_tiles + b // ctx_per_tile, kv(j), b % ctx_per_tile))
    if mode == "ctx":
        nq = lc // tq
        q0 = n_batch * seq // tq
        kern = functools.partial(_attn_dense_kernel, 0, 1)
        args = (q, k_rep, vt)
        in_specs = [pl.BlockSpec((tq, ATT_COL), lambda b, j, i: (q0 + b * nq + i, j)), k_ctx, vt_ctx]
    elif mode == "dense":
        tq = DENSE_SUB * ATT_Q_TILE
        nq = seq // tq
        kern = functools.partial(_attn_dense_kernel, tpb, DENSE_SUB)
        args = (q, k_rep, vt, k_rep, vt)
        in_specs = [pl.BlockSpec((tq, ATT_COL), lambda b, j, i: (b * nq + i, j)),
                    pl.BlockSpec((seq, ATT_COL), lambda b, j, i: (b, kv(j))),
                    pl.BlockSpec((tpb, HEAD_DIM, TOK_TILE), lambda b, j, i: (b, kv(j), 0)), k_ctx, vt_ctx]
    else:
        wb = SWA_WINDOW
        tq = WIN_SUB * wb
        nq = seq // tq
        nkb = seq // wb
        per_tile = TOK_TILE // wb
        kern = functools.partial(_attn_window_kernel, seq)
        nb = lambda i, o: jnp.clip(i * WIN_SUB + o, 0, nkb - 1)
        k_nb = lambda o: pl.BlockSpec((wb, ATT_COL), lambda b, j, i: (b * nkb + nb(i, o), kv(j)))
        v_nb = lambda o: pl.BlockSpec(
            (1, HEAD_DIM, wb), lambda b, j, i: (b * tpb + nb(i, o) // per_tile, kv(j), nb(i, o) % per_tile))
        offs = range(-1, WIN_SUB + 1)
        sink_row = jnp.repeat(sink.reshape(ncol, 1, 4), wb, axis=2) * LOG2E
        args = (q,) + (k_rep,) * len(offs) + (vt,) * len(offs) + (k_rep, vt, sink_row)
        in_specs = ([pl.BlockSpec((tq, ATT_COL), lambda b, j, i: (b * nq + i, j))]
                    + [k_nb(o) for o in offs] + [v_nb(o) for o in offs]
                    + [k_ctx, vt_ctx, pl.BlockSpec((1, 1, 4 * wb), lambda b, j, i: (j, 0, 0))])
    args = (bound,) + args
    in_specs = [pl.BlockSpec(memory_space=pltpu.SMEM)] + in_specs
    return pl.pallas_call(
        kern,
        out_shape=jax.ShapeDtypeStruct((n_batch * nq * tq, n_heads * HEAD_DIM), BF16),
        grid=(n_batch, ncol, nq),
        in_specs=in_specs,
        out_specs=pl.BlockSpec((tq, ATT_COL), lambda b, j, i: (b * nq + i, j)),
        compiler_params=_cparams(("arbitrary", "arbitrary", "arbitrary")),
        name="attention_" + mode,
    )(*args)


ROUTE_ROWS = 8


def _route(h, wrt_ref, wrt_hi_ref, brt_ref):
    hh, hl = _split2(h)
    a = _dot_nt(wrt_ref[...], hh)
    lt = a[0:LANES] + a[LANES:2 * LANES] + _dot_nt(wrt_hi_ref[...], hl) + brt_ref[...]
    col = lambda i: lt[i:i + 1, :]
    gl = [col(i) for i in range(MOE_GROUPS)]
    gmax = functools.reduce(jnp.maximum, gl)
    gi = jnp.where(gl[0] == gmax, 0, jnp.where(gl[1] == gmax, 1, jnp.where(gl[2] == gmax, 2, 3)))
    g_weight = 1.0 / functools.reduce(lambda a, b: a + b, [jnp.exp(x - gmax) for x in gl])
    el = []
    for j in range(MOE_EPG):
        cand = [col(MOE_GROUPS + g * MOE_EPG + j) for g in range(MOE_GROUPS)]
        el.append(jnp.where(gi == 0, cand[0], jnp.where(gi == 1, cand[1], jnp.where(gi == 2, cand[2], cand[3]))))
    m1 = functools.reduce(jnp.maximum, el)
    i1 = jnp.where(el[0] == m1, 0, jnp.where(el[1] == m1, 1, jnp.where(el[2] == m1, 2, 3)))
    rest = [jnp.where(i1 == j, -jnp.inf, el[j]) for j in range(MOE_EPG)]
    m2 = functools.reduce(jnp.maximum, rest)
    i2 = jnp.where(rest[0] == m2, 0, jnp.where(rest[1] == m2, 1, jnp.where(rest[2] == m2, 2, 3)))
    e2 = jnp.exp(m2 - m1)
    w1 = g_weight / (1.0 + e2)
    w2 = g_weight * e2 / (1.0 + e2)
    lo = jnp.minimum(i1, i2)
    hi = jnp.maximum(i1, i2)
    w_lo = jnp.where(i1 == lo, w1, w2)
    w_hi = jnp.where(i1 == lo, w2, w1)
    pair = jnp.where(lo == 0, hi - 1, jnp.where(lo == 1, hi + 1, N_PAIRS - 1))
    return w_lo, w_hi, gi * N_PAIRS + pair


def _out_tail(geom, m, x, mod_ref, gain_ffn_ref, wrt_ref, wrt_hi_ref, brt_ref, triu_ref,
              x_new_ref, hrow_ref, rt_ref, counts_ref, run_ref):
    t = pl.program_id(0)
    row = _mod_row(t, *geom)
    tt, d = x.shape

    @pl.when(t == 0)
    def _():
        run_ref[...] = jnp.zeros_like(run_ref)

    x_new = x + mod_ref[pl.ds(row, 1), 2 * d:3 * d] * m
    x_new_ref[...] = x_new
    h = _modulated(x_new, gain_ffn_ref[...], mod_ref, row, 3, 4)
    hrow_ref[:, 0:d] = h
    w_lo, w_hi, bucket = _route(h, wrt_ref, wrt_hi_ref, brt_ref)
    onehot = lax.broadcasted_iota(jnp.int32, (LANES, tt), 0) == bucket
    ones = jnp.where(onehot, 1.0, 0.0)
    before = _dot(_bf(ones), triu_ref[...]) + run_ref[...]
    rank = jnp.sum(jnp.where(onehot, before, 0.0), axis=0, keepdims=True)
    run = run_ref[...] + jnp.sum(ones, axis=1, keepdims=True)
    run_ref[...] = run
    counts_ref[...] = jnp.broadcast_to(run, counts_ref.shape)
    rec = jnp.concatenate([w_lo, w_hi, bucket.astype(F32), rank, jnp.zeros((ROUTE_ROWS - 4, tt), F32)], axis=0)
    rt_ref[0] = rec
    meta_t = jnp.concatenate([rec, jnp.zeros((META_LANES - ROUTE_ROWS, tt), F32)], axis=0)
    hrow_ref[:, d:d + META_LANES] = jnp.transpose(meta_t)


def _pick(t, n_lat_tiles, lat_ref, ctx_ref):
    return jnp.where(t < n_lat_tiles, lat_ref[...], ctx_ref[...])


def _out_even_kernel(geom, ol_ref, oc_ref, r_ref, attl_ref, attc_ref, gn_ref, w_ref, xl_ref, xc_ref, mod_ref,
                     gain_ffn_ref, wrt_ref, wrt_hi_ref, brt_ref, triu_ref,
                     x_new_ref, hrow_ref, rt_ref, counts_ref, run_ref):
    t = pl.program_id(0)
    o2 = _pick(t, geom[0], ol_ref, oc_ref)
    o = o2[0] + o2[1]
    r = r_ref[...]
    parts = []
    for h in range(GLA_HEADS):
        sl = slice(h * GLA_DV, (h + 1) * GLA_DV)
        parts.append(_rms(o[:, sl]) * gn_ref[...] * _silu(r[:, sl]))
    a = _bf(jnp.concatenate(parts, axis=-1))
    half = a.shape[-1]
    m = _dot(a, w_ref[0:half, :]) + _dot(_pick(t, geom[0], attl_ref, attc_ref), w_ref[half:, :])
    _out_tail(geom, m, _pick(t, geom[0], xl_ref, xc_ref), mod_ref, gain_ffn_ref, wrt_ref, wrt_hi_ref, brt_ref,
              triu_ref, x_new_ref, hrow_ref, rt_ref, counts_ref, run_ref)


def _out_odd_kernel(geom, att_ref, w_ref, x_ref, mod_ref, gain_ffn_ref, wrt_ref, wrt_hi_ref, brt_ref, triu_ref,
                    x_new_ref, hrow_ref, rt_ref, counts_ref, run_ref):
    m = _dot(att_ref[...], w_ref[...])
    _out_tail(geom, m, x_ref[...], mod_ref, gain_ffn_ref, wrt_ref, wrt_hi_ref, brt_ref, triu_ref,
              x_new_ref, hrow_ref, rt_ref, counts_ref, run_ref)


def _out_proj(kernel, lead_args, lead_specs, d, mod, gain_ffn, router, n_rows, name):
    tt = TOK_TILE
    tok = lambda w_: pl.BlockSpec((tt, w_), lambda t: (t, 0))
    r = np.arange(tt)
    triu = jnp.asarray((r[:, None] < r[None, :]).astype(np.float32), BF16)
    wrt, wrt_hi, br = router
    brt = jnp.broadcast_to(br.reshape(LANES, 1), (LANES, tt))
    return pl.pallas_call(
        kernel,
        out_shape=[jax.ShapeDtypeStruct((n_rows, d), F32), jax.ShapeDtypeStruct((n_rows, d + META_LANES), F32),
                   jax.ShapeDtypeStruct((n_rows // tt, ROUTE_ROWS, tt), F32),
                   jax.ShapeDtypeStruct((LANES, LANES), F32)],
        grid=(n_rows // tt,),
        in_specs=lead_specs + [_full(mod.shape), _full(gain_ffn.shape), _full(wrt.shape), _full(wrt_hi.shape),
                               _full(brt.shape), _full(triu.shape)],
        out_specs=[tok(d), tok(d + META_LANES), pl.BlockSpec((1, ROUTE_ROWS, tt), lambda t: (t, 0, 0)),
                   _full((LANES, LANES))],
        scratch_shapes=[pltpu.VMEM((LANES, 1), F32)],
        compiler_params=_cparams(("arbitrary",)),
        name=name,
    )(*lead_args, mod, gain_ffn, wrt, wrt_hi, brt, triu)


SUBLANES = 8


def _for_each_row(n_rows, fn):
    def body(g, c):
        base = pl.multiple_of(g * SUBLANES, SUBLANES)
        for j in range(SUBLANES):
            fn(base + j)
        return c
    lax.fori_loop(0, n_rows // SUBLANES, body, 0)


def _pos_kernel(start_ref, rt_ref, pos_ref):
    bucket = rt_ref[:, 2, :].astype(jnp.int32)
    base = jnp.zeros_like(bucket)
    for b in range(N_BUCKETS):
        base = jnp.where(bucket == b, start_ref[b], base)
    pos_ref[:, 0, :] = base + rt_ref[:, 3, :].astype(jnp.int32)


def _dispatch_kernel(last_ref, pos_ref, x_ref, xs_ref, zbuf, sem, zsem):
    tt = x_ref.shape[0]
    tm = zbuf.shape[0]

    @pl.when(pl.program_id(0) == 0)
    def _():
        zbuf[...] = jnp.zeros_like(zbuf)

        def zero_copy(b):
            return pltpu.make_async_copy(zbuf, xs_ref.at[pl.ds(jnp.maximum(last_ref[b], 0) * tm, tm)], zsem)

        for b in range(2 * N_BUCKETS):
            @pl.when(last_ref[b] >= 0)
            def _():
                zero_copy(b).start()
        for b in range(2 * N_BUCKETS):
            @pl.when(last_ref[b] >= 0)
            def _():
                zero_copy(b).wait()

    _for_each_row(tt, lambda r: pltpu.make_async_copy(
        x_ref.at[pl.ds(r, 1)], xs_ref.at[pl.ds(pos_ref[0, 0, r], 1)], sem).start())
    pltpu.make_async_copy(x_ref, xs_ref.at[pl.ds(0, tt)], sem).wait()


def _moe_mlp_kernel(tlo_ref, thi_ref, nused_ref, xs_ref, wg_lo, wu_lo, wd_lo, wg_hi, wu_hi, wd_hi, f_ref):
    del tlo_ref, thi_ref
    d = f_ref.shape[1]
    used = pl.program_id(0) < nused_ref[0]

    @pl.when(used)
    def _():
        xb = _bf(xs_ref[:, 0:d])
        w_lo = xs_ref[:, d:d + 1]
        w_hi = xs_ref[:, d + 1:d + 2]
        hid_lo = _silu(_dot(xb, wg_lo[0])) * _dot(xb, wu_lo[0]) * w_lo
        hid_hi = _silu(_dot(xb, wg_hi[0])) * _dot(xb, wu_hi[0]) * w_hi
        f_ref[...] = _dot(_bf(hid_lo), wd_lo[0]) + _dot(_bf(hid_hi), wd_hi[0])

    @pl.when(jnp.logical_not(used))
    def _():
        f_ref[...] = jnp.zeros_like(f_ref)


def _moe(hrow, rt, counts, experts, layer):
    n, dw = hrow.shape
    d = dw - META_LANES
    tm, tt = MOE_TILE, TOK_TILE
    n_tiles = n // tm + N_BUCKETS
    p = n_tiles * tm
    cnt = counts[0:N_BUCKETS, 0].astype(jnp.int32)
    tiles_b = (cnt + tm - 1) // tm
    tile_end = jnp.cumsum(tiles_b)
    start_b = (tile_end - tiles_b) * tm
    n_used = tile_end[-1].reshape(1)
    spare = n_used[0] + jnp.arange(N_BUCKETS, dtype=jnp.int32)
    last_tile = jnp.concatenate([jnp.where(tiles_b > 0, tile_end - 1, -1), jnp.where(spare < n_tiles, spare, -1)])
    pos = pl.pallas_call(
        _pos_kernel,
        out_shape=jax.ShapeDtypeStruct((n // tt, 1, tt), jnp.int32),
        grid_spec=pltpu.PrefetchScalarGridSpec(
            num_scalar_prefetch=1, grid=(1,),
            in_specs=[pl.BlockSpec(rt.shape, lambda i, s: (0, 0, 0))],
            out_specs=pl.BlockSpec((n // tt, 1, tt), lambda i, s: (0, 0, 0))),
        compiler_params=_cparams(("arbitrary",)),
        name="moe_positions",
    )(start_b, rt)
    tile_ids = jnp.arange(n_tiles, dtype=jnp.int32)
    tile_bucket = jnp.sum((tile_ids[:, None] >= tile_end[None, :]).astype(jnp.int32), axis=1)
    tile_bucket = jnp.minimum(tile_bucket, jnp.sum((n_used[0] - 1 >= tile_end).astype(jnp.int32)))
    tile_bucket = jnp.minimum(tile_bucket, N_BUCKETS - 1)
    pair_lo = jnp.asarray([0, 0, 0, 1, 1, 2], jnp.int32)
    pair_hi = jnp.asarray([1, 2, 3, 2, 3, 3], jnp.int32)
    grp = tile_bucket // N_PAIRS
    w_gate, w_up, w_down = experts
    first = layer * MOE_GROUPS * MOE_EPG
    t_lo = first + grp * MOE_EPG + pair_lo[tile_bucket % N_PAIRS]
    t_hi = first + grp * MOE_EPG + pair_hi[tile_bucket % N_PAIRS]

    xs = pl.pallas_call(
        _dispatch_kernel,
        out_shape=jax.ShapeDtypeStruct((p, dw), F32),
        grid_spec=pltpu.PrefetchScalarGridSpec(
            num_scalar_prefetch=1, grid=(n // tt,),
            in_specs=[pl.BlockSpec((1, 1, tt), lambda t, s: (t, 0, 0), memory_space=pltpu.SMEM),
                      pl.BlockSpec((tt, dw), lambda t, s: (t, 0))],
            out_specs=pl.BlockSpec(memory_space=pl.ANY),
            scratch_shapes=[pltpu.VMEM((tm, dw), F32), pltpu.SemaphoreType.DMA(()), pltpu.SemaphoreType.DMA(())]),
        compiler_params=_cparams(("arbitrary",)),
        name="moe_dispatch",
    )(last_tile, pos, hrow)

    f = D_EXPERT
    up_lo = pl.BlockSpec((1, d, f), lambda t, lo, hi, nu: (lo[t], 0, 0))
    up_hi = pl.BlockSpec((1, d, f), lambda t, lo, hi, nu: (hi[t], 0, 0))
    dn_lo = pl.BlockSpec((1, f, d), lambda t, lo, hi, nu: (lo[t], 0, 0))
    dn_hi = pl.BlockSpec((1, f, d), lambda t, lo, hi, nu: (hi[t], 0, 0))
    grid_spec = pltpu.PrefetchScalarGridSpec(
        num_scalar_prefetch=3,
        grid=(n_tiles,),
        in_specs=[pl.BlockSpec((tm, dw), lambda t, lo, hi, nu: (jnp.minimum(t, nu[0] - 1), 0)),
                  up_lo, up_lo, dn_lo, up_hi, up_hi, dn_hi],
        out_specs=pl.BlockSpec((tm, d), lambda t, *_: (t, 0)),
    )
    f_sorted = pl.pallas_call(
        _moe_mlp_kernel,
        out_shape=jax.ShapeDtypeStruct((p, d), F32),
        grid_spec=grid_spec,
        compiler_params=_cparams(("arbitrary",)),
        name="moe_experts",
    )(t_lo, t_hi, n_used, xs, w_gate, w_up, w_down, w_gate, w_up, w_down)
    return f_sorted, pos


def _gather_tile(t, n_t, pos_ref, pos_next_ref, src_hbm, buf, sem, inline_prefetch=False):
    tt = buf.shape[1]

    def start(p_ref, slot):
        _for_each_row(tt, lambda r: pltpu.make_async_copy(
            src_hbm.at[pl.ds(p_ref[0, 0, r], 1)], buf.at[slot].at[pl.ds(r, 1)], sem.at[slot]).start())

    slot = t % 2

    def wait(s):
        pltpu.make_async_copy(src_hbm.at[pl.ds(0, tt)], buf.at[s], sem.at[s]).wait()

    @pl.when(t == 0)
    def _():
        start(pos_ref, 0)

    wait(slot)
    if inline_prefetch:
        for r in range(tt):
            pltpu.make_async_copy(src_hbm.at[pl.ds(pos_next_ref[0, 0, r], 1)], buf.at[1 - slot].at[pl.ds(r, 1)],
                                  sem.at[1 - slot]).start()
    else:
        @pl.when(t + 1 < n_t)
        def _():
            start(pos_next_ref, 1 - slot)

    return buf[slot]


def _gather_drain(t, n_t, src_hbm, buf, sem):
    tt = buf.shape[1]

    @pl.when(t == n_t - 1)
    def _():
        pltpu.make_async_copy(src_hbm.at[pl.ds(0, tt)], buf.at[1 - t % 2], sem.at[1 - t % 2]).wait()


def _gather_specs(n_t):
    tt = TOK_TILE
    return [pl.BlockSpec((1, 1, tt), lambda t: (t, 0, 0), memory_space=pltpu.SMEM),
            pl.BlockSpec((1, 1, tt), lambda t: (jnp.minimum(t + 1, n_t - 1), 0, 0), memory_space=pltpu.SMEM),
            pl.BlockSpec(memory_space=pl.ANY)]


def _gather_scratch(d):
    return [pltpu.VMEM((2, TOK_TILE, d), F32), pltpu.SemaphoreType.DMA((2,))]


def _final_kernel(tiles_per_batch, x_ref, pos_ref, pos_next_ref, fs_hbm, mod_ref, o_ref, fbuf, fsem):
    t = pl.program_id(0)
    row = t // tiles_per_batch
    d = x_ref.shape[-1]
    f = _gather_tile(t, pl.num_programs(0), pos_ref, pos_next_ref, fs_hbm, fbuf, fsem)
    o_ref[...] = x_ref[...] + mod_ref[pl.ds(row, 1), 5 * d:6 * d] * f


def _final(x_lat, f_sorted, pos, mod, tiles_per_batch):
    n, d = x_lat.shape
    tok = pl.BlockSpec((TOK_TILE, d), lambda t: (t, 0))
    return pl.pallas_call(
        functools.partial(_final_kernel, tiles_per_batch),
        out_shape=jax.ShapeDtypeStruct((n, d), F32),
        grid=(n // TOK_TILE,),
        in_specs=[tok] + _gather_specs(n // TOK_TILE) + [_full(mod.shape)],
        out_specs=tok,
        scratch_shapes=_gather_scratch(d),
        compiler_params=_cparams(("arbitrary",)),
        name="final_residual",
    )(x_lat, pos, pos, f_sorted, mod)


def _block_diag_ones():
    r = np.arange(MXU_DIM) // HEAD_DIM
    return jnp.asarray((r[:, None] == r[None, :]).astype(np.float32), BF16)


def _router_weights(wg, bg, we, be):
    d = wg.shape[0]
    n = MOE_GROUPS + MOE_GROUPS * MOE_EPG
    wt = jnp.concatenate([wg, we, jnp.zeros((d, LANES - n), F32)], axis=1).T
    b = jnp.concatenate([bg, be, jnp.zeros((LANES - n,), F32)])
    hi = _bf(wt)
    lo = _bf(wt - hi.astype(F32))
    return jnp.concatenate([hi, lo], axis=0), hi, b


def _expert_weights(w_gate, w_up, w_down):
    l, g, e, d, f = w_gate.shape
    n = l * g * e
    return _bf(w_gate).reshape(n, d, f), _bf(w_up).reshape(n, d, f), _bf(w_down).reshape(n, f, d)


def kernel(x, c, ctx, c_ctx, mod_w, mod_b, norm_mix, norm_ffn, ev_w_in, ev_w_out, gla_gate_w, gla_gate_b,
           gla_out_norm, att_q_norm, att_k_norm, od_w_in, od_w_out, swa_sink, swa_q_norm, swa_k_norm,
           router_group_w, router_group_b, router_expert_w, router_expert_b, exp_w_gate, exp_w_up, exp_w_down):
    n_batch, seq, d = x.shape
    lc = ctx.shape[1]
    depth = mod_w.shape[0]
    n_lat = n_batch * seq
    tiles_per_batch = seq // TOK_TILE
    geom = (n_lat // TOK_TILE, tiles_per_batch, n_batch)
    assert depth == 2 and seq % TOK_TILE == 0 and (n_batch * lc) % TOK_TILE == 0 and n_batch < 16

    x_lat, x_ctx = x.reshape(n_lat, d), ctx.reshape(n_batch * lc, d)
    n_all = n_lat + n_batch * lc
    c_rows = jnp.zeros((16, d), F32).at[:n_batch].set(c).at[n_batch].set(c_ctx)
    mod = _modulation(c_rows, mod_w, mod_b)
    tables = _rope_tables(seq)
    bd = _block_diag_ones()
    row2 = lambda v: v.reshape(1, -1)
    tile_gain = lambda gvec, reps: jnp.tile(gvec, reps).reshape(1, -1)

    w0 = ev_w_in[0]
    seg = np.cumsum([0, 256, 256, 512, 512, 32, 512, 128, 128])
    cols = lambda i: w0[:, seg[i]:seg[i + 1]]
    w_even = _bf(jnp.concatenate([cols(0), cols(1), cols(2), cols(3), cols(5), cols(6), cols(4),
                                  jnp.zeros((d, EV_END - EV_LR - 2 * GLA_GATE_RANK), F32)], axis=1))
    wvt_even = _bf(cols(7).T)
    hk = GLA_HEADS * GLA_DK
    gw = jnp.zeros((LANES, 2 * hk), F32)
    gw = gw.at[0:GLA_GATE_RANK, 0:hk].set(gla_gate_w[0, 0])
    gw = gw.at[GLA_GATE_RANK:2 * GLA_GATE_RANK, hk:2 * hk].set(gla_gate_w[0, 1])
    gb = gla_gate_b[0].reshape(1, 2 * hk)
    gq, gk, gv, gr, g, aq, ak, avt = _proj_even(
        x_lat, x_ctx, mod[0], row2(norm_mix[0]), w_even, wvt_even, _bf(gw), gb,
        tile_gain(att_q_norm[0], ATT_HEADS), tile_gain(att_k_norm[0], ATT_KV_HEADS), tables, bd, geom)
    o_lat, o_ctx = _gla(gq, gk, gv, g, n_batch, seq, lc)
    bound0 = _score_bound(att_q_norm[0], att_k_norm[0])
    att_lat = _attention("dense", aq, ak, avt, bound0, n_batch, seq, lc, ATT_HEADS, ATT_KV_HEADS)
    att_ctx = _attention("ctx", aq, ak, avt, bound0, n_batch, seq, lc, ATT_HEADS, ATT_KV_HEADS)
    router = _router_weights(router_group_w[0], router_group_b[0], router_expert_w[0], router_expert_b[0])
    tt = TOK_TILE
    nlt = geom[0]
    gn = row2(gla_out_norm[0])
    w_out0 = _bf(ev_w_out[0])
    hv, ha = GLA_HEADS * GLA_DV, ATT_HEADS * HEAD_DIM
    x_mid, hrow, rt, counts = _out_proj(
        functools.partial(_out_even_kernel, geom),
        (o_lat, o_ctx, gr, att_lat, att_ctx, gn, w_out0, x_lat, x_ctx),
        _lat_ctx_specs((tt, hv), nlt, lead=(2,)) + [pl.BlockSpec((tt, hv), lambda t: (t, 0))]
        + _lat_ctx_specs((tt, ha), nlt) + [_full(gn.shape), _full(w_out0.shape)] + _lat_ctx_specs((tt, d), nlt),
        d, mod[0], row2(norm_ffn[0]), router, n_all, "out_even")
    experts = _expert_weights(exp_w_gate, exp_w_up, exp_w_down)
    f0, pos0 = _moe(hrow, rt, counts, experts, 0)

    w_odd = od_w_in[0]
    x1, q1, k1, v1t = _proj_odd(
        x_mid, f0, pos0, mod[0], mod[1], row2(norm_mix[1]), _bf(w_odd[:, 0:d + LANES]), _bf(w_odd[:, d + LANES:].T),
        tile_gain(swa_q_norm[0], SWA_HEADS), tile_gain(swa_k_norm[0], SWA_KV_HEADS), tables, bd, geom)
    att1 = _attention("window", q1, k1, v1t, _score_bound(swa_q_norm[0], swa_k_norm[0]), n_batch, seq, lc,
                      SWA_HEADS, SWA_KV_HEADS, sink=swa_sink[0])
    router = _router_weights(router_group_w[1], router_group_b[1], router_expert_w[1], router_expert_b[1])
    w_out1 = _bf(od_w_out[0])
    tok = lambda w_: pl.BlockSpec((tt, w_), lambda t: (t, 0))
    x2, hrow1, rt1, counts1 = _out_proj(
        functools.partial(_out_odd_kernel, geom), (att1, w_out1, x1),
        [tok(SWA_HEADS * HEAD_DIM), _full(w_out1.shape), tok(d)],
        d, mod[1], row2(norm_ffn[1]), router, n_lat, "out_odd")
    f1, pos1 = _moe(hrow1, rt1, counts1, experts, 1)
    out = _final(x2, f1, pos1, mod[1], tiles_per_batch)
    return out.reshape(n_batch, seq, d)
```

```python
import functools

import numpy as np
import jax
import jax.numpy as jnp
from jax import lax
from jax.experimental import pallas as pl
from jax.experimental.pallas import tpu as pltpu

F32 = jnp.float32
BF16 = jnp.bfloat16

GRID_W = 64
HEAD_DIM = 64
AXIS_DIM = HEAD_DIM // 2
ROPE_THETA = 10000.0
EPS = 1e-6
N_MOD = 6
GLA_HEADS = 4
GLA_DK = 64
GLA_DV = 128
GLA_GATE_RANK = 16
GLA_GATE_NORM = 16.0
GLA_CHUNK = 64
ATT_HEADS = 8
ATT_KV_HEADS = 2
SWA_HEADS = 16
SWA_KV_HEADS = 2
SWA_WINDOW = 128
MOE_GROUPS = 4
MOE_EPG = 4
D_EXPERT = 256
N_PAIRS = 6
N_BUCKETS = MOE_GROUPS * N_PAIRS

LANES = 128
MXU_DIM = 256
TOK_TILE = 512
ATT_Q_TILE = 128
ATT_K_CHUNK = 512
ATT_COL = 4 * HEAD_DIM
WIN_SUB = 4
DENSE_SUB = 4
MOE_TILE = 256
DISPATCH_TILE = 1024
META_LANES = LANES
VMEM_LIMIT = 56 * 1024 * 1024
NEG_BIG = -1e30
LOG2E = 1.4426950408889634
Q_SCALE = HEAD_DIM ** -0.5 * LOG2E


def _bf(x):
    return x.astype(BF16)


def _split2(x):
    hi = _bf(x)
    lo = _bf(x - hi.astype(F32))
    return hi, lo


def _dot(a, b):
    return jnp.dot(a, b, preferred_element_type=F32)


def _dot_nt(a, b):
    return lax.dot_general(a, b, (((1,), (1,)), ((), ())), preferred_element_type=F32)


def _dot_tn(a, b):
    return lax.dot_general(a, b, (((0,), (0,)), ((), ())), preferred_element_type=F32)


def _silu(x):
    return x / (1.0 + jnp.exp(-x))


def _rms(x):
    return x * lax.rsqrt(jnp.mean(x * x, axis=-1, keepdims=True) + EPS)


def _cparams(sem):
    return pltpu.CompilerParams(dimension_semantics=sem, vmem_limit_bytes=VMEM_LIMIT)


def _full(shape):
    n = len(shape)
    return pl.BlockSpec(shape, lambda *_: (0,) * n)


def _mod_kernel(c_ref, w_ref, b_ref, o_ref):
    c = c_ref[...]
    ch, cl = _split2(_silu(c))
    wh, wl = _split2(w_ref[0])
    o_ref[0] = _dot(ch, wh) + _dot(ch, wl) + _dot(cl, wh) + b_ref[0]


def _modulation(c_rows, mod_w, mod_b):
    depth, d, n = mod_w.shape
    tn = n // 4
    return pl.pallas_call(
        _mod_kernel,
        out_shape=jax.ShapeDtypeStruct((depth, 16, n), F32),
        grid=(depth, n // tn),
        in_specs=[pl.BlockSpec((16, d), lambda i, j: (0, 0)),
                  pl.BlockSpec((1, d, tn), lambda i, j: (i, 0, j)),
                  pl.BlockSpec((1, 1, tn), lambda i, j: (i, 0, j))],
        out_specs=pl.BlockSpec((1, 16, tn), lambda i, j: (i, 0, j)),
        compiler_params=_cparams(("arbitrary", "arbitrary")),
        name="modulation",
    )(c_rows, mod_w, mod_b.reshape(depth, 1, n))


def _mod_row(t, n_lat_tiles, tiles_per_batch, n_batch):
    return jnp.where(t < n_lat_tiles, t // tiles_per_batch, n_batch)


def _modulated(x, gain, mod_ref, row, k_shift, k_scale):
    d = x.shape[-1]
    shift = mod_ref[pl.ds(row, 1), k_shift * d:(k_shift + 1) * d]
    scale = mod_ref[pl.ds(row, 1), k_scale * d:(k_scale + 1) * d]
    return _rms(x) * gain * (1.0 + scale) + shift


def _rope_tables(seq):
    rows = seq // GRID_W
    row = np.repeat(np.arange(rows), GRID_W)
    col = np.tile(np.arange(GRID_W), rows)
    inv_freq = ROPE_THETA ** (-np.arange(0, AXIS_DIM, 2, dtype=np.float64) / AXIS_DIM)
    ang = np.stack([row[:, None] * inv_freq, col[:, None] * inv_freq], axis=1)
    cos, sin = np.cos(ang), np.sin(ang)
    zero = np.zeros_like(sin)
    cos64 = np.concatenate([cos[:, 0], cos[:, 0], cos[:, 1], cos[:, 1]], axis=-1)
    sa64 = np.concatenate([-sin[:, 0], zero[:, 0], -sin[:, 1], zero[:, 1]], axis=-1)
    sb64 = np.concatenate([zero[:, 0], sin[:, 0], zero[:, 1], sin[:, 1]], axis=-1)

    def widen(t, fill):
        t = np.concatenate([t, t], axis=-1)
        return jnp.asarray(np.concatenate([t, np.full((TOK_TILE, LANES), fill)], axis=0), F32)

    return widen(cos64, 1.0), widen(sa64, 0.0), widen(sb64, 0.0)


def _head_sumsq(y, bd):
    w = y.shape[-1]
    outs = []
    for s in range(0, w, MXU_DIM):
        e = min(s + MXU_DIM, w)
        hi, lo = _split2(y[:, s:e])
        b = bd[0:e - s, 0:e - s]
        outs.append(_dot(hi, b) + _dot(lo, b))
    return outs[0] if len(outs) == 1 else jnp.concatenate(outs, axis=-1)


def _qk_norm_rope(z, gain, bd, cos, sa, sb):
    w = z.shape[-1]
    rep = w // LANES
    ss = _head_sumsq(z * z, bd)
    y = z * lax.rsqrt(ss * (1.0 / HEAD_DIM) + EPS) * gain

    def wide(t):
        return t if rep == 1 else jnp.concatenate([t] * rep, axis=-1)

    return (y * wide(cos) + pltpu.roll(y, w - AXIS_DIM // 2, 1) * wide(sa)
            + pltpu.roll(y, AXIS_DIM // 2, 1) * wide(sb))


def _kv_rep(kv128):
    lane = lax.broadcasted_iota(jnp.int32, kv128.shape, 1)
    sw = pltpu.roll(kv128, HEAD_DIM, 1)
    a0 = jnp.where(lane < HEAD_DIM, kv128, sw)
    a1 = jnp.where(lane < HEAD_DIM, sw, kv128)
    return jnp.concatenate([a0, a0, a1, a1], axis=-1)


def _rope_block(t, n_lat_tiles, tiles_per_batch):
    return jnp.where(t < n_lat_tiles, t % tiles_per_batch, tiles_per_batch)


EV_GQ, EV_GK, EV_GV, EV_GR, EV_AQ, EV_AK, EV_LR, EV_END = 0, 256, 512, 1024, 1536, 2048, 2176, 2304


def _proj_even_kernel(geom, xl_ref, xc_ref, mod_ref, gain_ref, w_ref, wvt_ref, gw_ref, gb_ref, qg_ref, kg_ref,
                      cos_ref, sa_ref, sb_ref, bd_ref,
                      gq_ref, gk_ref, gv_ref, gr_ref, g_ref, aq_ref, ak_ref, avt_ref):
    t = pl.program_id(0)
    row = _mod_row(t, *geom)
    x = jnp.where(t < geom[0], xl_ref[...], xc_ref[...])
    hb = _bf(_modulated(x, gain_ref[...], mod_ref, row, 0, 1))

    def seg(a, b):
        return _dot(hb, w_ref[:, a:b])

    gq_ref[...] = seg(EV_GQ, EV_GK) * (GLA_DK ** -0.5)
    gk_ref[...] = seg(EV_GK, EV_GV)
    gv_ref[...] = _bf(seg(EV_GV, EV_GR))
    gr_ref[...] = seg(EV_GR, EV_AQ)
    zg = _dot(_bf(seg(EV_LR, EV_END)), gw_ref[...]) + gb_ref[...]
    g_ref[...] = -(jnp.maximum(-zg, 0.0) + jnp.log1p(jnp.exp(-jnp.abs(zg)))) * (1.0 / GLA_GATE_NORM)
    bd = bd_ref[...]
    cos, sa, sb = cos_ref[...], sa_ref[...], sb_ref[...]
    aq = _qk_norm_rope(seg(EV_AQ, EV_AK), qg_ref[...], bd, cos, sa, sb)
    aq_ref[...] = _bf(aq * Q_SCALE)
    ak = _qk_norm_rope(seg(EV_AK, EV_LR), kg_ref[...], bd, cos, sa, sb)
    ak_ref[...] = _bf(_kv_rep(ak))
    avt_ref[0] = _bf(_dot_nt(wvt_ref[...], hb))


def _vt_spec():
    return pl.BlockSpec((1, LANES, TOK_TILE), lambda t: (t, 0, 0))


def _lat_ctx_specs(block, n_lat_tiles, lead=()):
    z = (0,) * len(lead)
    lat = pl.BlockSpec(lead + block, lambda t: z + (jnp.minimum(t, n_lat_tiles - 1), 0))
    ctx = pl.BlockSpec(lead + block, lambda t: z + (jnp.maximum(t - n_lat_tiles, 0), 0))
    return [lat, ctx]


def _proj_even(x_lat, x_ctx, mod, gain, w, wvt, gw, gb, qg, kg, tables, bd, geom):
    d = x_lat.shape[1]
    n = x_lat.shape[0] + x_ctx.shape[0]
    n_lat_tiles, tiles_per_batch, _ = geom
    tt = TOK_TILE
    cos, sa, sb = tables
    tok = lambda w_: pl.BlockSpec((tt, w_), lambda t: (t, 0))
    rope = pl.BlockSpec((tt, LANES), lambda t: (_rope_block(t, n_lat_tiles, tiles_per_batch), 0))
    outs = [(256, F32), (256, F32), (512, BF16), (512, F32), (512, F32), (512, BF16), (512, BF16)]
    return pl.pallas_call(
        functools.partial(_proj_even_kernel, geom),
        out_shape=[jax.ShapeDtypeStruct((n, w_), dt) for w_, dt in outs]
        + [jax.ShapeDtypeStruct((n // tt, LANES, tt), BF16)],
        grid=(n // tt,),
        in_specs=_lat_ctx_specs((tt, d), n_lat_tiles)
        + [_full(mod.shape), _full(gain.shape), _full(w.shape), _full(wvt.shape), _full(gw.shape),
           _full(gb.shape), _full(qg.shape), _full(kg.shape), rope, rope, rope, _full(bd.shape)],
        out_specs=[tok(w_) for w_, _ in outs] + [_vt_spec()],
        compiler_params=_cparams(("arbitrary",)),
        name="proj_even",
    )(x_lat, x_ctx, mod, gain, w, wvt, gw, gb, qg, kg, cos, sa, sb, bd)


def _proj_odd_kernel(geom, x_ref, pos_ref, pos_next_ref, fs_hbm, mod_prev_ref, mod_ref, gain_ref, w_ref, wvt_ref,
                     qg_ref, kg_ref, cos_ref, sa_ref, sb_ref, bd_ref, x1_ref, q_ref, k_ref, vt_ref, fbuf, fsem):
    t = pl.program_id(0)
    row = _mod_row(t, *geom)
    d = x_ref.shape[-1]
    gate = mod_prev_ref[pl.ds(row, 1), 5 * d:6 * d]
    f = _gather_tile(t, pl.num_programs(0), pos_ref, pos_next_ref, fs_hbm, fbuf, fsem, inline_prefetch=True)
    x1 = x_ref[...] + gate * f
    x1_ref[...] = x1
    hb = _bf(_modulated(x1, gain_ref[...], mod_ref, row, 0, 1))
    bd = bd_ref[...]
    cos, sa, sb = cos_ref[...], sa_ref[...], sb_ref[...]
    q = _qk_norm_rope(_dot(hb, w_ref[:, 0:d]), qg_ref[...], bd, cos, sa, sb)
    q_ref[...] = _bf(q * Q_SCALE)
    k = _qk_norm_rope(_dot(hb, w_ref[:, d:d + LANES]), kg_ref[...], bd, cos, sa, sb)
    k_ref[...] = _bf(_kv_rep(k))
    vt_ref[0] = _bf(_dot_nt(wvt_ref[...], hb))
    _gather_drain(t, pl.num_programs(0), fs_hbm, fbuf, fsem)


def _proj_odd(x_all, f_sorted, pos, mod_prev, mod, gain, w, wvt, qg, kg, tables, bd, geom):
    n, d = x_all.shape
    n_lat_tiles, tiles_per_batch, _ = geom
    tt = TOK_TILE
    cos, sa, sb = tables
    tok = lambda w_: pl.BlockSpec((tt, w_), lambda t: (t, 0))
    rope = pl.BlockSpec((tt, LANES), lambda t: (_rope_block(t, n_lat_tiles, tiles_per_batch), 0))
    outs = [(d, F32), (d, BF16), (512, BF16)]
    return pl.pallas_call(
        functools.partial(_proj_odd_kernel, geom),
        out_shape=[jax.ShapeDtypeStruct((n, w_), dt) for w_, dt in outs]
        + [jax.ShapeDtypeStruct((n // tt, LANES, tt), BF16)],
        grid=(n // tt,),
        in_specs=[tok(d)] + _gather_specs(n // tt)
        + [_full(mod_prev.shape), _full(mod.shape), _full(gain.shape), _full(w.shape),
           _full(wvt.shape), _full(qg.shape), _full(kg.shape), rope, rope, rope, _full(bd.shape)],
        out_specs=[tok(w_) for w_, _ in outs] + [_vt_spec()],
        scratch_shapes=_gather_scratch(d),
        compiler_params=_cparams(("arbitrary",)),
        name="proj_odd",
    )(x_all, pos, pos, f_sorted, mod_prev, mod, gain, w, wvt, qg, kg, cos, sa, sb, bd)


N_LEVELS = 6
GLA_MX_ROWS = (N_LEVELS + 2) * GLA_CHUNK


def _gla_constants():
    c = GLA_CHUNK
    mx = np.zeros((2, GLA_MX_ROWS, c), np.float32)
    pat = np.zeros((2, N_LEVELS + 1, c, GLA_HEADS * c), np.float32)
    r = np.arange(c)
    for lvl in range(N_LEVELS):
        h = 1 << lvl
        ref = (r // (2 * h)) * 2 * h + h - 1
        upper = (r % (2 * h)) >= h
        m = np.zeros((c, c), np.float32)
        for i in range(c):
            if upper[i]:
                m[i, ref[i] + 1:i + 1] = 1.0
            else:
                m[i, i + 1:ref[i] + 1] = 1.0
        mx[0, lvl * c:(lvl + 1) * c] = m
        same = (r[:, None] // (2 * h)) == (r[None, :] // (2 * h))
        p = same & upper[:, None] & (~upper)[None, :]
        pat[0, lvl] = np.tile(p.astype(np.float32), (1, GLA_HEADS))
    mx[0, N_LEVELS * c:(N_LEVELS + 1) * c] = (r[None, :] <= r[:, None])
    mx[0, (N_LEVELS + 1) * c:(N_LEVELS + 2) * c] = (r[None, :] > r[:, None])
    pat[0, N_LEVELS] = np.tile(np.eye(c, dtype=np.float32), (1, GLA_HEADS))
    for k in range(N_LEVELS + 2):
        mx[1, k * c:(k + 1) * c] = mx[0, k * c:(k + 1) * c][::-1, ::-1]
    for k in range(N_LEVELS + 1):
        pat[1, k] = np.tile(pat[0, k, :, 0:c][::-1, ::-1], (1, GLA_HEADS))
    return mx, pat


GLA_BLOCK = 1024
GLA_GROUP = 4


def _gla_chunks(chunks, mx, pat_ref, s_ref):
    c = GLA_CHUNK
    lane_head = lax.broadcasted_iota(jnp.int32, (c, GLA_HEADS * GLA_DK), 1) // GLA_DK

    def stack_heads(a):
        return jnp.concatenate([_bf(jnp.where(lane_head == h, a, 0.0)) for h in range(GLA_HEADS)], axis=0)

    xs = [_dot(mx, jnp.concatenate(_split2(g), axis=0)) for _, _, _, g in chunks]
    atts = [jnp.where(pat_ref[0, N_LEVELS] > 0.0, _dot_nt(_bf(q), stack_heads(k)), 0.0)
            for q, k, _, _ in chunks]
    for lvl in range(N_LEVELS):
        for j, (q, k, _, _) in enumerate(chunks):
            e = jnp.exp(xs[j][lvl * c:(lvl + 1) * c])
            atts[j] = atts[j] + jnp.where(pat_ref[0, lvl] > 0.0, _dot_nt(_bf(q * e), stack_heads(k * e)), 0.0)
    outs, qes, news, a_cols = [], [], [], []
    for j, (q, k, v, _) in enumerate(chunks):
        vhead = lax.broadcasted_iota(jnp.int32, v.shape, 1) // GLA_DV
        v_bd = jnp.concatenate([jnp.where(vhead == h, v, jnp.zeros_like(v)) for h in range(GLA_HEADS)], axis=0)
        outs.append(_dot(_bf(atts[j]), v_bd))
        bcum = xs[j][N_LEVELS * c:(N_LEVELS + 1) * c]
        brem = xs[j][(N_LEVELS + 1) * c:(N_LEVELS + 2) * c]
        qes.append(stack_heads(q * jnp.exp(bcum)))
        kt = jnp.transpose(k * jnp.exp(brem))
        news.append(jnp.concatenate(
            [_dot(_bf(kt[h * GLA_DK:(h + 1) * GLA_DK]), v[:, h * GLA_DV:(h + 1) * GLA_DV])
             for h in range(GLA_HEADS)], axis=0))
        tot = bcum[0:1] + brem[0:1]
        a_cols.append(jnp.transpose(jnp.exp(jnp.broadcast_to(tot, (8, tot.shape[1]))))[:, 0:1])
    s = s_ref[...]
    for j in range(len(chunks)):
        o_inter = _dot(qes[j], _bf(s))
        outs[j] = outs[j] + jnp.concatenate([o_inter[h * c:(h + 1) * c] for h in range(GLA_HEADS)], axis=-1)
        s = a_cols[j] * s + news[j]
    s_ref[...] = s
    return outs


def _gla_kernel(n_lat_chunks, n_ctx_chunks, ql_ref, kl_ref, vl_ref, gl_ref, qc_ref, kc_ref, vc_ref, gc_ref,
                mx_ref, pat_ref, ol_ref, oc_ref, s_ref):
    d = pl.program_id(1)
    c = GLA_CHUNK
    mx = mx_ref[0]

    def run(n_chunks, q_ref, k_ref, v_ref, g_ref, o_ref):
        def body(i, carry):
            rows = []
            for j in range(GLA_GROUP):
                step = i * GLA_GROUP + j
                ci = jnp.where(d == 0, step, n_chunks - 1 - step)
                rows.append(pl.ds(pl.multiple_of(ci * c, c), c))
            outs = _gla_chunks([(q_ref[r, :], k_ref[r, :], v_ref[r, :], g_ref[r, :]) for r in rows],
                               mx, pat_ref, s_ref)
            for r, o in zip(rows, outs):
                o_ref[0, r, :] = o
            return carry
        lax.fori_loop(0, n_chunks // GLA_GROUP, body, 0)

    @pl.when(pl.program_id(2) == 0)
    def _():
        s_ref[...] = jnp.zeros_like(s_ref)
        run(n_ctx_chunks, qc_ref, kc_ref, vc_ref, gc_ref, oc_ref)

    run(n_lat_chunks, ql_ref, kl_ref, vl_ref, gl_ref, ol_ref)


def _gla(gq, gk, gv, g, n_batch, seq, lc):
    mx_np, pat_np = _gla_constants()
    mx = jnp.asarray(np.concatenate([mx_np, mx_np], axis=2), BF16)
    pat = jnp.asarray(pat_np, F32)
    ctx0 = n_batch * seq // lc
    hk, hv = GLA_HEADS * GLA_DK, GLA_HEADS * GLA_DV
    blk = min(GLA_BLOCK, seq)
    nb = seq // blk
    assert seq % blk == 0 and blk % (GLA_CHUNK * GLA_GROUP) == 0 and lc % (GLA_CHUNK * GLA_GROUP) == 0
    row = lambda b, d_, i: b * nb + jnp.where(d_ == 0, i, nb - 1 - i)
    lat = lambda w_, col: pl.BlockSpec((blk, w_), lambda b, d_, i: (row(b, d_, i), col(d_)))
    ctx = lambda w_, col: pl.BlockSpec((lc, w_), lambda b, d_, i: (ctx0 + b, col(d_)))
    zero = lambda d_: 0
    same = lambda d_: d_
    return pl.pallas_call(
        functools.partial(_gla_kernel, blk // GLA_CHUNK, lc // GLA_CHUNK),
        out_shape=[jax.ShapeDtypeStruct((2, n_batch * seq, hv), F32),
                   jax.ShapeDtypeStruct((2, n_batch * lc, hv), F32)],
        grid=(n_batch, 2, nb),
        in_specs=[lat(hk, zero), lat(hk, zero), lat(hv, zero), lat(hk, same),
                  ctx(hk, zero), ctx(hk, zero), ctx(hv, zero), ctx(hk, same),
                  pl.BlockSpec((1,) + mx.shape[1:], lambda b, d_, i: (d_, 0, 0)),
                  pl.BlockSpec((1,) + pat.shape[1:], lambda b, d_, i: (d_, 0, 0, 0))],
        out_specs=[pl.BlockSpec((1, blk, hv), lambda b, d_, i: (d_, row(b, d_, i), 0)),
                   pl.BlockSpec((1, lc, hv), lambda b, d_, i: (d_, b, 0))],
        scratch_shapes=[pltpu.VMEM((GLA_HEADS * GLA_DK, GLA_DV), F32)],
        compiler_params=_cparams(("arbitrary", "arbitrary", "arbitrary")),
        name="gla_scan",
    )(gq, gk, gv, g, gq, gk, gv, g, mx, pat)


def _stack_heads(q):
    lane_head = lax.broadcasted_iota(jnp.int32, q.shape, 1) // HEAD_DIM
    return jnp.concatenate([jnp.where(lane_head == h, q, jnp.zeros_like(q)) for h in range(4)], axis=0)


SAFE_SCORE_BOUND = 60.0
ONES_ROWS = 16


def _with_ones(vt):
    return jnp.concatenate([vt, jnp.ones((ONES_ROWS, vt.shape[1]), vt.dtype)], axis=0)


def _attn_store(acc, l, o_ref, u, tq):
    out = acc * (1.0 / l)
    out = jnp.concatenate([out[:, h * tq:(h + 1) * tq] for h in range(4)], axis=0)
    o_ref[u * tq:(u + 1) * tq, :] = _bf(jnp.transpose(out))


def _attn_dense_kernel(n_chunks, n_sub, *refs):
    if n_chunks:
        bound_ref, q_ref, kl_ref, vtl_ref, kc_ref, vtc_ref, o_ref = refs
    else:
        bound_ref, q_ref, kc_ref, vtc_ref, o_ref = refs
    tq = q_ref.shape[0] // n_sub
    cols = 4 * tq
    q4 = [_stack_heads(q_ref[u * tq:(u + 1) * tq, :]) for u in range(n_sub)]

    def scores(c, u):
        if c < n_chunks:
            return _dot_nt(kl_ref[c * ATT_K_CHUNK:(c + 1) * ATT_K_CHUNK, :], q4[u])
        return _dot_nt(kc_ref[...], q4[u])

    def update(carry, st, vt_aug):
        m, acc = carry
        m_new = jnp.maximum(m, jnp.max(st, axis=0, keepdims=True))
        acc = jnp.exp2(m - m_new) * acc + _dot(vt_aug, _bf(jnp.exp2(st - m_new)))
        return m_new, acc

    def run(fixed_ref):
        if fixed_ref is None:
            carry = [(jnp.full((1, cols), NEG_BIG, F32), jnp.zeros((HEAD_DIM + ONES_ROWS, cols), F32))
                     for _ in range(n_sub)]
        else:
            carry = [jnp.zeros((HEAD_DIM + ONES_ROWS, cols), F32) for _ in range(n_sub)]
        st = [scores(0, u) for u in range(n_sub)]
        for c in range(n_chunks + 1):
            st_next = [scores(c + 1, u) for u in range(n_sub)] if c < n_chunks else None
            vt_aug = _with_ones(vtl_ref[c] if c < n_chunks else vtc_ref[0])
            if fixed_ref is None:
                carry = [update(carry[u], st[u], vt_aug) for u in range(n_sub)]
            else:
                carry = [carry[u] + _dot(vt_aug, _bf(jnp.exp2(st[u] - fixed_ref))) for u in range(n_sub)]
            st = st_next
        for u in range(n_sub):
            acc = carry[u][1] if fixed_ref is None else carry[u]
            _attn_store(acc[0:HEAD_DIM], acc[HEAD_DIM:HEAD_DIM + 1], o_ref, u, tq)

    bound = bound_ref[0]

    @pl.when(bound <= SAFE_SCORE_BOUND)
    def _():
        run(bound)

    @pl.when(bound > SAFE_SCORE_BOUND)
    def _():
        run(None)


def _attn_window_kernel(seq, bound_ref, q_ref, *refs):
    nk = WIN_SUB + 2
    k_refs, v_refs = refs[0:nk], refs[nk:2 * nk]
    kc_ref, vtc_ref, sink_ref, o_ref = refs[2 * nk:]
    tq = SWA_WINDOW
    i = pl.program_id(2)
    kb = jnp.concatenate([r[...] for r in k_refs], axis=0)
    vtb = _with_ones(jnp.concatenate([r[0] for r in v_refs], axis=1))
    kc, vtc, sink = kc_ref[...], _with_ones(vtc_ref[0]), sink_ref[0]
    span = 3 * tq
    sb, sc = [], []
    for u in range(WIN_SUB):
        q4 = _stack_heads(q_ref[u * tq:(u + 1) * tq, :])
        first = i * WIN_SUB + u - 1
        kpos = first * tq + lax.broadcasted_iota(jnp.int32, (span, tq), 0)
        qpos = (first + 1) * tq + lax.broadcasted_iota(jnp.int32, (span, tq), 1)
        ok = (kpos >= 0) & (kpos < seq) & (jnp.abs(kpos - qpos) <= SWA_WINDOW)
        bias = jnp.where(ok, 0.0, NEG_BIG)
        sb.append(_dot_nt(kb[u * tq:u * tq + span], q4) + jnp.concatenate([bias] * 4, axis=1))
        sc.append(_dot_nt(kc, q4))
    def finish(use_bound):
        ms, pbs, pcs = [], [], []
        for u in range(WIN_SUB):
            if use_bound:
                m = jnp.maximum(bound_ref[0], sink)
            else:
                m = jnp.maximum(jnp.maximum(jnp.max(sb[u], axis=0, keepdims=True),
                                            jnp.max(sc[u], axis=0, keepdims=True)), sink)
            ms.append(m)
            pbs.append(_bf(jnp.exp2(sb[u] - m)))
            pcs.append(_bf(jnp.exp2(sc[u] - m)))
        for u in range(WIN_SUB):
            acc = _dot(vtb[:, u * tq:u * tq + span], pbs[u]) + _dot(vtc, pcs[u])
            l = acc[HEAD_DIM:HEAD_DIM + 1] + jnp.exp2(sink - ms[u])
            _attn_store(acc[0:HEAD_DIM], l, o_ref, u, tq)

    @pl.when(bound_ref[0] <= SAFE_SCORE_BOUND)
    def _():
        finish(True)

    @pl.when(bound_ref[0] > SAFE_SCORE_BOUND)
    def _():
        finish(False)


def _score_bound(q_gain, k_gain):
    return (HEAD_DIM * Q_SCALE * 1.02 * jnp.max(jnp.abs(q_gain)) * jnp.max(jnp.abs(k_gain))).reshape(1)


def _attention(mode, q, k_rep, vt, bound, n_batch, seq, lc, n_heads, n_kv, sink=None):
    ncol = n_heads * HEAD_DIM // ATT_COL
    col_per_kv = ncol // n_kv
    tq = ATT_Q_TILE
    tpb = seq // TOK_TILE
    n_lat_tiles = n_batch * tpb
    ctx_per_tile = TOK_TILE // lc
    ctx0 = n_batch * seq // lc
    assert ATT_K_CHUNK == TOK_TILE and TOK_TILE % lc == 0 and tq == SWA_WINDOW
    kv = lambda j: j // col_per_kv
    k_ctx = pl.BlockSpec((lc, ATT_COL), lambda b, j, i: (ctx0 + b, kv(j)))
    vt_ctx = pl.BlockSpec((1, HEAD_DIM, lc), lambda b, j, i: (n_lat_tiles + b // ctx_per_tile, kv(j), b % ctx_per_tile))
    if mode == "ctx":
        nq = lc // tq
        q0 = n_batch * seq // tq
        kern = functools.partial(_attn_dense_kernel, 0, 1)
        args = (q, k_rep, vt)
        in_specs = [pl.BlockSpec((tq, ATT_COL), lambda b, j, i: (q0 + b * nq + i, j)), k_ctx, vt_ctx]
    elif mode == "dense":
        tq = DENSE_SUB * ATT_Q_TILE
        nq = seq // tq
        kern = functools.partial(_attn_dense_kernel, tpb, DENSE_SUB)
        args = (q, k_rep, vt, k_rep, vt)
        in_specs = [pl.BlockSpec((tq, ATT_COL), lambda b, j, i: (b * nq + i, j)),
                    pl.BlockSpec((seq, ATT_COL), lambda b, j, i: (b, kv(j))),
                    pl.BlockSpec((tpb, HEAD_DIM, TOK_TILE), lambda b, j, i: (b, kv(j), 0)), k_ctx, vt_ctx]
    else:
        wb = SWA_WINDOW
        tq = WIN_SUB * wb
        nq = seq // tq
        nkb = seq // wb
        per_tile = TOK_TILE // wb
        kern = functools.partial(_attn_window_kernel, seq)
        nb = lambda i, o: jnp.clip(i * WIN_SUB + o, 0, nkb - 1)
        k_nb = lambda o: pl.BlockSpec((wb, ATT_COL), lambda b, j, i: (b * nkb + nb(i, o), kv(j)))
        v_nb = lambda o: pl.BlockSpec(
            (1, HEAD_DIM, wb), lambda b, j, i: (b * tpb + nb(i, o) // per_tile, kv(j), nb(i, o) % per_tile))
        offs = range(-1, WIN_SUB + 1)
        sink_row = jnp.repeat(sink.reshape(ncol, 1, 4), wb, axis=2) * LOG2E
        args = (q,) + (k_rep,) * len(offs) + (vt,) * len(offs) + (k_rep, vt, sink_row)
        in_specs = ([pl.BlockSpec((tq, ATT_COL), lambda b, j, i: (b * nq + i, j))]
                    + [k_nb(o) for o in offs] + [v_nb(o) for o in offs]
                    + [k_ctx, vt_ctx, pl.BlockSpec((1, 1, 4 * wb), lambda b, j, i: (j, 0, 0))])
    args = (bound,) + args
    in_specs = [pl.BlockSpec(memory_space=pltpu.SMEM)] + in_specs
    return pl.pallas_call(
        kern,
        out_shape=jax.ShapeDtypeStruct((n_batch * nq * tq, n_heads * HEAD_DIM), BF16),
        grid=(n_batch, ncol, nq),
        in_specs=in_specs,
        out_specs=pl.BlockSpec((tq, ATT_COL), lambda b, j, i: (b * nq + i, j)),
        compiler_params=_cparams(("arbitrary", "arbitrary", "arbitrary")),
        name="attention_" + mode,
    )(*args)


ROUTE_ROWS = 8


def _route(h, wrt_ref, wrt_hi_ref, brt_ref):
    hh, hl = _split2(h)
    a = _dot_nt(wrt_ref[...], hh)
    lt = a[0:LANES] + a[LANES:2 * LANES] + _dot_nt(wrt_hi_ref[...], hl) + brt_ref[...]
    col = lambda i: lt[i:i + 1, :]
    gl = [col(i) for i in range(MOE_GROUPS)]
    gmax = functools.reduce(jnp.maximum, gl)
    gi = jnp.where(gl[0] == gmax, 0, jnp.where(gl[1] == gmax, 1, jnp.where(gl[2] == gmax, 2, 3)))
    g_weight = 1.0 / functools.reduce(lambda a, b: a + b, [jnp.exp(x - gmax) for x in gl])
    el = []
    for j in range(MOE_EPG):
        cand = [col(MOE_GROUPS + g * MOE_EPG + j) for g in range(MOE_GROUPS)]
        el.append(jnp.where(gi == 0, cand[0], jnp.where(gi == 1, cand[1], jnp.where(gi == 2, cand[2], cand[3]))))
    m1 = functools.reduce(jnp.maximum, el)
    i1 = jnp.where(el[0] == m1, 0, jnp.where(el[1] == m1, 1, jnp.where(el[2] == m1, 2, 3)))
    rest = [jnp.where(i1 == j, -jnp.inf, el[j]) for j in range(MOE_EPG)]
    m2 = functools.reduce(jnp.maximum, rest)
    i2 = jnp.where(rest[0] == m2, 0, jnp.where(rest[1] == m2, 1, jnp.where(rest[2] == m2, 2, 3)))
    e2 = jnp.exp(m2 - m1)
    w1 = g_weight / (1.0 + e2)
    w2 = g_weight * e2 / (1.0 + e2)
    lo = jnp.minimum(i1, i2)
    hi = jnp.maximum(i1, i2)
    w_lo = jnp.where(i1 == lo, w1, w2)
    w_hi = jnp.where(i1 == lo, w2, w1)
    pair = jnp.where(lo == 0, hi - 1, jnp.where(lo == 1, hi + 1, N_PAIRS - 1))
    return w_lo, w_hi, gi * N_PAIRS + pair


def _out_tail(geom, m, x, mod_ref, gain_ffn_ref, wrt_ref, wrt_hi_ref, brt_ref, triu_ref,
              x_new_ref, hrow_ref, rt_ref, counts_ref, run_ref):
    t = pl.program_id(0)
    row = _mod_row(t, *geom)
    tt, d = x.shape

    @pl.when(t == 0)
    def _():
        run_ref[...] = jnp.zeros_like(run_ref)

    x_new = x + mod_ref[pl.ds(row, 1), 2 * d:3 * d] * m
    x_new_ref[...] = x_new
    h = _modulated(x_new, gain_ffn_ref[...], mod_ref, row, 3, 4)
    hrow_ref[:, 0:d] = h
    w_lo, w_hi, bucket = _route(h, wrt_ref, wrt_hi_ref, brt_ref)
    onehot = lax.broadcasted_iota(jnp.int32, (LANES, tt), 0) == bucket
    ones = jnp.where(onehot, 1.0, 0.0)
    before = _dot(_bf(ones), triu_ref[...]) + run_ref[...]
    rank = jnp.sum(jnp.where(onehot, before, 0.0), axis=0, keepdims=True)
    run = run_ref[...] + jnp.sum(ones, axis=1, keepdims=True)
    run_ref[...] = run
    counts_ref[...] = jnp.broadcast_to(run, counts_ref.shape)
    rec = jnp.concatenate([w_lo, w_hi, bucket.astype(F32), rank, jnp.zeros((ROUTE_ROWS - 4, tt), F32)], axis=0)
    rt_ref[0] = rec
    meta_t = jnp.concatenate([rec, jnp.zeros((META_LANES - ROUTE_ROWS, tt), F32)], axis=0)
    hrow_ref[:, d:d + META_LANES] = jnp.transpose(meta_t)


def _pick(t, n_lat_tiles, lat_ref, ctx_ref):
    return jnp.where(t < n_lat_tiles, lat_ref[...], ctx_ref[...])


def _out_even_kernel(geom, ol_ref, oc_ref, r_ref, attl_ref, attc_ref, gn_ref, w_ref, xl_ref, xc_ref, mod_ref,
                     gain_ffn_ref, wrt_ref, wrt_hi_ref, brt_ref, triu_ref,
                     x_new_ref, hrow_ref, rt_ref, counts_ref, run_ref):
    t = pl.program_id(0)
    o2 = _pick(t, geom[0], ol_ref, oc_ref)
    o = o2[0] + o2[1]
    r = r_ref[...]
    parts = []
    for h in range(GLA_HEADS):
        sl = slice(h * GLA_DV, (h + 1) * GLA_DV)
        parts.append(_rms(o[:, sl]) * gn_ref[...] * _silu(r[:, sl]))
    a = _bf(jnp.concatenate(parts, axis=-1))
    half = a.shape[-1]
    m = _dot(a, w_ref[0:half, :]) + _dot(_pick(t, geom[0], attl_ref, attc_ref), w_ref[half:, :])
    _out_tail(geom, m, _pick(t, geom[0], xl_ref, xc_ref), mod_ref, gain_ffn_ref, wrt_ref, wrt_hi_ref, brt_ref,
              triu_ref, x_new_ref, hrow_ref, rt_ref, counts_ref, run_ref)


def _out_odd_kernel(geom, att_ref, w_ref, x_ref, mod_ref, gain_ffn_ref, wrt_ref, wrt_hi_ref, brt_ref, triu_ref,
                    x_new_ref, hrow_ref, rt_ref, counts_ref, run_ref):
    m = _dot(att_ref[...], w_ref[...])
    _out_tail(geom, m, x_ref[...], mod_ref, gain_ffn_ref, wrt_ref, wrt_hi_ref, brt_ref, triu_ref,
              x_new_ref, hrow_ref, rt_ref, counts_ref, run_ref)


def _out_proj(kernel, lead_args, lead_specs, d, mod, gain_ffn, router, n_rows, name):
    tt = TOK_TILE
    tok = lambda w_: pl.BlockSpec((tt, w_), lambda t: (t, 0))
    r = np.arange(tt)
    triu = jnp.asarray((r[:, None] < r[None, :]).astype(np.float32), BF16)
    wrt, wrt_hi, br = router
    brt = jnp.broadcast_to(br.reshape(LANES, 1), (LANES, tt))
    return pl.pallas_call(
        kernel,
        out_shape=[jax.ShapeDtypeStruct((n_rows, d), F32), jax.ShapeDtypeStruct((n_rows, d + META_LANES), F32),
                   jax.ShapeDtypeStruct((n_rows // tt, ROUTE_ROWS, tt), F32),
                   jax.ShapeDtypeStruct((LANES, LANES), F32)],
        grid=(n_rows // tt,),
        in_specs=lead_specs + [_full(mod.shape), _full(gain_ffn.shape), _full(wrt.shape), _full(wrt_hi.shape),
                               _full(brt.shape), _full(triu.shape)],
        out_specs=[tok(d), tok(d + META_LANES), pl.BlockSpec((1, ROUTE_ROWS, tt), lambda t: (t, 0, 0)),
                   _full((LANES, LANES))],
        scratch_shapes=[pltpu.VMEM((LANES, 1), F32)],
        compiler_params=_cparams(("arbitrary",)),
        name=name,
    )(*lead_args, mod, gain_ffn, wrt, wrt_hi, brt, triu)


SUBLANES = 8


def _for_each_row(n_rows, fn):
    def body(g, c):
        base = pl.multiple_of(g * SUBLANES, SUBLANES)
        for j in range(SUBLANES):
            fn(base + j)
        return c
    lax.fori_loop(0, n_rows // SUBLANES, body, 0)


def _pos_kernel(start_ref, rt_ref, pos_ref):
    bucket = rt_ref[:, 2, :].astype(jnp.int32)
    base = jnp.zeros_like(bucket)
    for b in range(N_BUCKETS):
        base = jnp.where(bucket == b, start_ref[b], base)
    pos_ref[:, 0, :] = base + rt_ref[:, 3, :].astype(jnp.int32)


def _dispatch_kernel(last_ref, pos_ref, x_ref, xs_ref, zbuf, sem, zsem):
    tt = x_ref.shape[0]
    tm = zbuf.shape[0]

    @pl.when(pl.program_id(0) == 0)
    def _():
        zbuf[...] = jnp.zeros_like(zbuf)

        def zero_copy(b):
            return pltpu.make_async_copy(zbuf, xs_ref.at[pl.ds(jnp.maximum(last_ref[b], 0) * tm, tm)], zsem)

        for b in range(2 * N_BUCKETS):
            @pl.when(last_ref[b] >= 0)
            def _():
                zero_copy(b).start()
        for b in range(2 * N_BUCKETS):
            @pl.when(last_ref[b] >= 0)
            def _():
                zero_copy(b).wait()

    _for_each_row(tt, lambda r: pltpu.make_async_copy(
        x_ref.at[pl.ds(r, 1)], xs_ref.at[pl.ds(pos_ref[0, 0, r], 1)], sem).start())
    pltpu.make_async_copy(x_ref, xs_ref.at[pl.ds(0, tt)], sem).wait()


def _moe_mlp_kernel(tlo_ref, thi_ref, nused_ref, xs_ref, wg_lo, wu_lo, wd_lo, wg_hi, wu_hi, wd_hi, f_ref):
    del tlo_ref, thi_ref
    d = f_ref.shape[1]
    used = pl.program_id(0) < nused_ref[0]

    @pl.when(used)
    def _():
        xb = _bf(xs_ref[:, 0:d])
        w_lo = xs_ref[:, d:d + 1]
        w_hi = xs_ref[:, d + 1:d + 2]
        hid_lo = _silu(_dot(xb, wg_lo[0])) * _dot(xb, wu_lo[0]) * w_lo
        hid_hi = _silu(_dot(xb, wg_hi[0])) * _dot(xb, wu_hi[0]) * w_hi
        f_ref[...] = _dot(_bf(hid_lo), wd_lo[0]) + _dot(_bf(hid_hi), wd_hi[0])

    @pl.when(jnp.logical_not(used))
    def _():
        f_ref[...] = jnp.zeros_like(f_ref)


def _moe(hrow, rt, counts, experts, layer):
    n, dw = hrow.shape
    d = dw - META_LANES
    tm, tt = MOE_TILE, TOK_TILE
    n_tiles = n // tm + N_BUCKETS
    p = n_tiles * tm
    cnt = counts[0:N_BUCKETS, 0].astype(jnp.int32)
    tiles_b = (cnt + tm - 1) // tm
    tile_end = jnp.cumsum(tiles_b)
    start_b = (tile_end - tiles_b) * tm
    n_used = tile_end[-1].reshape(1)
    spare = n_used[0] + jnp.arange(N_BUCKETS, dtype=jnp.int32)
    last_tile = jnp.concatenate([jnp.where(tiles_b > 0, tile_end - 1, -1), jnp.where(spare < n_tiles, spare, -1)])
    pos = pl.pallas_call(
        _pos_kernel,
        out_shape=jax.ShapeDtypeStruct((n // tt, 1, tt), jnp.int32),
        grid_spec=pltpu.PrefetchScalarGridSpec(
            num_scalar_prefetch=1, grid=(1,),
            in_specs=[pl.BlockSpec(rt.shape, lambda i, s: (0, 0, 0))],
            out_specs=pl.BlockSpec((n // tt, 1, tt), lambda i, s: (0, 0, 0))),
        compiler_params=_cparams(("arbitrary",)),
        name="moe_positions",
    )(start_b, rt)
    tile_ids = jnp.arange(n_tiles, dtype=jnp.int32)
    tile_bucket = jnp.sum((tile_ids[:, None] >= tile_end[None, :]).astype(jnp.int32), axis=1)
    tile_bucket = jnp.minimum(tile_bucket, jnp.sum((n_used[0] - 1 >= tile_end).astype(jnp.int32)))
    tile_bucket = jnp.minimum(tile_bucket, N_BUCKETS - 1)
    pair_lo = jnp.asarray([0, 0, 0, 1, 1, 2], jnp.int32)
    pair_hi = jnp.asarray([1, 2, 3, 2, 3, 3], jnp.int32)
    grp = tile_bucket // N_PAIRS
    w_gate, w_up, w_down = experts
    first = layer * MOE_GROUPS * MOE_EPG
    t_lo = first + grp * MOE_EPG + pair_lo[tile_bucket % N_PAIRS]
    t_hi = first + grp * MOE_EPG + pair_hi[tile_bucket % N_PAIRS]

    dt = DISPATCH_TILE if n % DISPATCH_TILE == 0 else tt
    xs = pl.pallas_call(
        _dispatch_kernel,
        out_shape=jax.ShapeDtypeStruct((p, dw), F32),
        grid_spec=pltpu.PrefetchScalarGridSpec(
            num_scalar_prefetch=1, grid=(n // dt,),
            in_specs=[pl.BlockSpec((1, 1, dt), lambda t, s: (t, 0, 0), memory_space=pltpu.SMEM),
                      pl.BlockSpec((dt, dw), lambda t, s: (t, 0))],
            out_specs=pl.BlockSpec(memory_space=pl.ANY),
            scratch_shapes=[pltpu.VMEM((tm, dw), F32), pltpu.SemaphoreType.DMA(()), pltpu.SemaphoreType.DMA(())]),
        compiler_params=_cparams(("arbitrary",)),
        name="moe_dispatch",
    )(last_tile, pos.reshape(n // dt, 1, dt), hrow)

    f = D_EXPERT
    up_lo = pl.BlockSpec((1, d, f), lambda t, lo, hi, nu: (lo[t], 0, 0))
    up_hi = pl.BlockSpec((1, d, f), lambda t, lo, hi, nu: (hi[t], 0, 0))
    dn_lo = pl.BlockSpec((1, f, d), lambda t, lo, hi, nu: (lo[t], 0, 0))
    dn_hi = pl.BlockSpec((1, f, d), lambda t, lo, hi, nu: (hi[t], 0, 0))
    grid_spec = pltpu.PrefetchScalarGridSpec(
        num_scalar_prefetch=3,
        grid=(n_tiles,),
        in_specs=[pl.BlockSpec((tm, dw), lambda t, lo, hi, nu: (jnp.minimum(t, nu[0] - 1), 0)),
                  up_lo, up_lo, dn_lo, up_hi, up_hi, dn_hi],
        out_specs=pl.BlockSpec((tm, d), lambda t, *_: (t, 0)),
    )
    f_sorted = pl.pallas_call(
        _moe_mlp_kernel,
        out_shape=jax.ShapeDtypeStruct((p, d), F32),
        grid_spec=grid_spec,
        compiler_params=_cparams(("arbitrary",)),
        name="moe_experts",
    )(t_lo, t_hi, n_used, xs, w_gate, w_up, w_down, w_gate, w_up, w_down)
    return f_sorted, pos


def _gather_tile(t, n_t, pos_ref, pos_next_ref, src_hbm, buf, sem, inline_prefetch=False):
    tt = buf.shape[1]

    def start(p_ref, slot):
        _for_each_row(tt, lambda r: pltpu.make_async_copy(
            src_hbm.at[pl.ds(p_ref[0, 0, r], 1)], buf.at[slot].at[pl.ds(r, 1)], sem.at[slot]).start())

    slot = t % 2

    def wait(s):
        pltpu.make_async_copy(src_hbm.at[pl.ds(0, tt)], buf.at[s], sem.at[s]).wait()

    @pl.when(t == 0)
    def _():
        start(pos_ref, 0)

    wait(slot)
    if inline_prefetch:
        for r in range(tt):
            pltpu.make_async_copy(src_hbm.at[pl.ds(pos_next_ref[0, 0, r], 1)], buf.at[1 - slot].at[pl.ds(r, 1)],
                                  sem.at[1 - slot]).start()
    else:
        @pl.when(t + 1 < n_t)
        def _():
            start(pos_next_ref, 1 - slot)

    return buf[slot]


def _gather_drain(t, n_t, src_hbm, buf, sem):
    tt = buf.shape[1]

    @pl.when(t == n_t - 1)
    def _():
        pltpu.make_async_copy(src_hbm.at[pl.ds(0, tt)], buf.at[1 - t % 2], sem.at[1 - t % 2]).wait()


def _gather_specs(n_t):
    tt = TOK_TILE
    return [pl.BlockSpec((1, 1, tt), lambda t: (t, 0, 0), memory_space=pltpu.SMEM),
            pl.BlockSpec((1, 1, tt), lambda t: (jnp.minimum(t + 1, n_t - 1), 0, 0), memory_space=pltpu.SMEM),
            pl.BlockSpec(memory_space=pl.ANY)]


def _gather_scratch(d):
    return [pltpu.VMEM((2, TOK_TILE, d), F32), pltpu.SemaphoreType.DMA((2,))]


def _final_kernel(tiles_per_batch, x_ref, pos_ref, pos_next_ref, fs_hbm, mod_ref, o_ref, fbuf, fsem):
    t = pl.program_id(0)
    row = t // tiles_per_batch
    d = x_ref.shape[-1]
    f = _gather_tile(t, pl.num_programs(0), pos_ref, pos_next_ref, fs_hbm, fbuf, fsem)
    o_ref[...] = x_ref[...] + mod_ref[pl.ds(row, 1), 5 * d:6 * d] * f


def _final(x_lat, f_sorted, pos, mod, tiles_per_batch):
    n, d = x_lat.shape
    tok = pl.BlockSpec((TOK_TILE, d), lambda t: (t, 0))
    return pl.pallas_call(
        functools.partial(_final_kernel, tiles_per_batch),
        out_shape=jax.ShapeDtypeStruct((n, d), F32),
        grid=(n // TOK_TILE,),
        in_specs=[tok] + _gather_specs(n // TOK_TILE) + [_full(mod.shape)],
        out_specs=tok,
        scratch_shapes=_gather_scratch(d),
        compiler_params=_cparams(("arbitrary",)),
        name="final_residual",
    )(x_lat, pos, pos, f_sorted, mod)


def _block_diag_ones():
    r = np.arange(MXU_DIM) // HEAD_DIM
    return jnp.asarray((r[:, None] == r[None, :]).astype(np.float32), BF16)


def _router_weights(wg, bg, we, be):
    d = wg.shape[0]
    n = MOE_GROUPS + MOE_GROUPS * MOE_EPG
    wt = jnp.concatenate([wg, we, jnp.zeros((d, LANES - n), F32)], axis=1).T
    b = jnp.concatenate([bg, be, jnp.zeros((LANES - n,), F32)])
    hi = _bf(wt)
    lo = _bf(wt - hi.astype(F32))
    return jnp.concatenate([hi, lo], axis=0), hi, b


def _expert_weights(w_gate, w_up, w_down):
    l, g, e, d, f = w_gate.shape
    n = l * g * e
    return _bf(w_gate).reshape(n, d, f), _bf(w_up).reshape(n, d, f), _bf(w_down).reshape(n, f, d)


def kernel(x, c, ctx, c_ctx, mod_w, mod_b, norm_mix, norm_ffn, ev_w_in, ev_w_out, gla_gate_w, gla_gate_b,
           gla_out_norm, att_q_norm, att_k_norm, od_w_in, od_w_out, swa_sink, swa_q_norm, swa_k_norm,
           router_group_w, router_group_b, router_expert_w, router_expert_b, exp_w_gate, exp_w_up, exp_w_down):
    n_batch, seq, d = x.shape
    lc = ctx.shape[1]
    depth = mod_w.shape[0]
    n_lat = n_batch * seq
    tiles_per_batch = seq // TOK_TILE
    geom = (n_lat // TOK_TILE, tiles_per_batch, n_batch)
    assert depth == 2 and seq % TOK_TILE == 0 and (n_batch * lc) % TOK_TILE == 0 and n_batch < 16

    x_lat, x_ctx = x.reshape(n_lat, d), ctx.reshape(n_batch * lc, d)
    n_all = n_lat + n_batch * lc
    c_rows = jnp.zeros((16, d), F32).at[:n_batch].set(c).at[n_batch].set(c_ctx)
    mod = _modulation(c_rows, mod_w, mod_b)
    tables = _rope_tables(seq)
    bd = _block_diag_ones()
    row2 = lambda v: v.reshape(1, -1)
    tile_gain = lambda gvec, reps: jnp.tile(gvec, reps).reshape(1, -1)

    w0 = ev_w_in[0]
    seg = np.cumsum([0, 256, 256, 512, 512, 32, 512, 128, 128])
    cols = lambda i: w0[:, seg[i]:seg[i + 1]]
    w_even = _bf(jnp.concatenate([cols(0), cols(1), cols(2), cols(3), cols(5), cols(6), cols(4),
                                  jnp.zeros((d, EV_END - EV_LR - 2 * GLA_GATE_RANK), F32)], axis=1))
    wvt_even = _bf(cols(7).T)
    hk = GLA_HEADS * GLA_DK
    gw = jnp.zeros((LANES, 2 * hk), F32)
    gw = gw.at[0:GLA_GATE_RANK, 0:hk].set(gla_gate_w[0, 0])
    gw = gw.at[GLA_GATE_RANK:2 * GLA_GATE_RANK, hk:2 * hk].set(gla_gate_w[0, 1])
    gb = gla_gate_b[0].reshape(1, 2 * hk)
    gq, gk, gv, gr, g, aq, ak, avt = _proj_even(
        x_lat, x_ctx, mod[0], row2(norm_mix[0]), w_even, wvt_even, _bf(gw), gb,
        tile_gain(att_q_norm[0], ATT_HEADS), tile_gain(att_k_norm[0], ATT_KV_HEADS), tables, bd, geom)
    o_lat, o_ctx = _gla(gq, gk, gv, g, n_batch, seq, lc)
    bound0 = _score_bound(att_q_norm[0], att_k_norm[0])
    att_lat = _attention("dense", aq, ak, avt, bound0, n_batch, seq, lc, ATT_HEADS, ATT_KV_HEADS)
    att_ctx = _attention("ctx", aq, ak, avt, bound0, n_batch, seq, lc, ATT_HEADS, ATT_KV_HEADS)
    router = _router_weights(router_group_w[0], router_group_b[0], router_expert_w[0], router_expert_b[0])
    tt = TOK_TILE
    nlt = geom[0]
    gn = row2(gla_out_norm[0])
    w_out0 = _bf(ev_w_out[0])
    hv, ha = GLA_HEADS * GLA_DV, ATT_HEADS * HEAD_DIM
    x_mid, hrow, rt, counts = _out_proj(
        functools.partial(_out_even_kernel, geom),
        (o_lat, o_ctx, gr, att_lat, att_ctx, gn, w_out0, x_lat, x_ctx),
        _lat_ctx_specs((tt, hv), nlt, lead=(2,)) + [pl.BlockSpec((tt, hv), lambda t: (t, 0))]
        + _lat_ctx_specs((tt, ha), nlt) + [_full(gn.shape), _full(w_out0.shape)] + _lat_ctx_specs((tt, d), nlt),
        d, mod[0], row2(norm_ffn[0]), router, n_all, "out_even")
    experts = _expert_weights(exp_w_gate, exp_w_up, exp_w_down)
    f0, pos0 = _moe(hrow, rt, counts, experts, 0)

    w_odd = od_w_in[0]
    x1, q1, k1, v1t = _proj_odd(
        x_mid, f0, pos0, mod[0], mod[1], row2(norm_mix[1]), _bf(w_odd[:, 0:d + LANES]), _bf(w_odd[:, d + LANES:].T),
        tile_gain(swa_q_norm[0], SWA_HEADS), tile_gain(swa_k_norm[0], SWA_KV_HEADS), tables, bd, geom)
    att1 = _attention("window", q1, k1, v1t, _score_bound(swa_q_norm[0], swa_k_norm[0]), n_batch, seq, lc,
                      SWA_HEADS, SWA_KV_HEADS, sink=swa_sink[0])
    router = _router_weights(router_group_w[1], router_group_b[1], router_expert_w[1], router_expert_b[1])
    w_out1 = _bf(od_w_out[0])
    tok = lambda w_: pl.BlockSpec((tt, w_), lambda t: (t, 0))
    x2, hrow1, rt1, counts1 = _out_proj(
        functools.partial(_out_odd_kernel, geom), (att1, w_out1, x1),
        [tok(SWA_HEADS * HEAD_DIM), _full(w_out1.shape), tok(d)],
        d, mod[1], row2(norm_ffn[1]), router, n_lat, "out_odd")
    f1, pos1 = _moe(hrow1, rt1, counts1, experts, 1)
    out = _final(x2, f1, pos1, mod[1], tiles_per_batch)
    return out.reshape(n_batch, seq, d)
```

```python
import functools

import numpy as np
import jax
import jax.numpy as jnp
from jax import lax
from jax.experimental import pallas as pl
from jax.experimental.pallas import tpu as pltpu

F32 = jnp.float32
BF16 = jnp.bfloat16

GRID_W = 64
HEAD_DIM = 64
AXIS_DIM = HEAD_DIM // 2
ROPE_THETA = 10000.0
EPS = 1e-6
N_MOD = 6
GLA_HEADS = 4
GLA_DK = 64
GLA_DV = 128
GLA_GATE_RANK = 16
GLA_GATE_NORM = 16.0
GLA_CHUNK = 64
ATT_HEADS = 8
ATT_KV_HEADS = 2
SWA_HEADS = 16
SWA_KV_HEADS = 2
SWA_WINDOW = 128
MOE_GROUPS = 4
MOE_EPG = 4
D_EXPERT = 256
N_PAIRS = 6
N_BUCKETS = MOE_GROUPS * N_PAIRS

LANES = 128
MXU_DIM = 256
TOK_TILE = 512
ATT_Q_TILE = 128
ATT_K_CHUNK = 512
ATT_COL = 4 * HEAD_DIM
WIN_SUB = 4
DENSE_SUB = 4
MOE_TILE = 256
DISPATCH_TILE = 1024
META_LANES = LANES
VMEM_LIMIT = 56 * 1024 * 1024
NEG_BIG = -1e30
LOG2E = 1.4426950408889634
Q_SCALE = HEAD_DIM ** -0.5 * LOG2E


def _bf(x):
    return x.astype(BF16)


def _split2(x):
    hi = _bf(x)
    lo = _bf(x - hi.astype(F32))
    return hi, lo


def _dot(a, b):
    return jnp.dot(a, b, preferred_element_type=F32)


def _dot_nt(a, b):
    return lax.dot_general(a, b, (((1,), (1,)), ((), ())), preferred_element_type=F32)


def _dot_tn(a, b):
    return lax.dot_general(a, b, (((0,), (0,)), ((), ())), preferred_element_type=F32)


def _silu(x):
    return x / (1.0 + jnp.exp(-x))


def _rms(x):
    return x * lax.rsqrt(jnp.mean(x * x, axis=-1, keepdims=True) + EPS)


def _cparams(sem):
    return pltpu.CompilerParams(dimension_semantics=sem, vmem_limit_bytes=VMEM_LIMIT)


def _full(shape):
    n = len(shape)
    return pl.BlockSpec(shape, lambda *_: (0,) * n)


def _mod_kernel(c_ref, w_ref, b_ref, o_ref):
    c = c_ref[...]
    ch, cl = _split2(_silu(c))
    wh, wl = _split2(w_ref[0])
    o_ref[0] = _dot(ch, wh) + _dot(ch, wl) + _dot(cl, wh) + b_ref[0]


def _modulation(c_rows, mod_w, mod_b):
    depth, d, n = mod_w.shape
    tn = n // 4
    return pl.pallas_call(
        _mod_kernel,
        out_shape=jax.ShapeDtypeStruct((depth, 16, n), F32),
        grid=(depth, n // tn),
        in_specs=[pl.BlockSpec((16, d), lambda i, j: (0, 0)),
                  pl.BlockSpec((1, d, tn), lambda i, j: (i, 0, j)),
                  pl.BlockSpec((1, 1, tn), lambda i, j: (i, 0, j))],
        out_specs=pl.BlockSpec((1, 16, tn), lambda i, j: (i, 0, j)),
        compiler_params=_cparams(("arbitrary", "arbitrary")),
        name="modulation",
    )(c_rows, mod_w, mod_b.reshape(depth, 1, n))


def _mod_row(t, n_lat_tiles, tiles_per_batch, n_batch):
    return jnp.where(t < n_lat_tiles, t // tiles_per_batch, n_batch)


def _modulated(x, gain, mod_ref, row, k_shift, k_scale):
    d = x.shape[-1]
    shift = mod_ref[pl.ds(row, 1), k_shift * d:(k_shift + 1) * d]
    scale = mod_ref[pl.ds(row, 1), k_scale * d:(k_scale + 1) * d]
    return _rms(x) * gain * (1.0 + scale) + shift


def _rope_tables(seq):
    rows = seq // GRID_W
    row = np.repeat(np.arange(rows), GRID_W)
    col = np.tile(np.arange(GRID_W), rows)
    inv_freq = ROPE_THETA ** (-np.arange(0, AXIS_DIM, 2, dtype=np.float64) / AXIS_DIM)
    ang = np.stack([row[:, None] * inv_freq, col[:, None] * inv_freq], axis=1)
    cos, sin = np.cos(ang), np.sin(ang)
    zero = np.zeros_like(sin)
    cos64 = np.concatenate([cos[:, 0], cos[:, 0], cos[:, 1], cos[:, 1]], axis=-1)
    sa64 = np.concatenate([-sin[:, 0], zero[:, 0], -sin[:, 1], zero[:, 1]], axis=-1)
    sb64 = np.concatenate([zero[:, 0], sin[:, 0], zero[:, 1], sin[:, 1]], axis=-1)

    def widen(t, fill):
        t = np.concatenate([t, t], axis=-1)
        return jnp.asarray(np.concatenate([t, np.full((TOK_TILE, LANES), fill)], axis=0), F32)

    return widen(cos64, 1.0), widen(sa64, 0.0), widen(sb64, 0.0)


def _head_sumsq(y, bd):
    w = y.shape[-1]
    outs = []
    for s in range(0, w, MXU_DIM):
        e = min(s + MXU_DIM, w)
        hi, lo = _split2(y[:, s:e])
        b = bd[0:e - s, 0:e - s]
        outs.append(_dot(hi, b) + _dot(lo, b))
    return outs[0] if len(outs) == 1 else jnp.concatenate(outs, axis=-1)


def _qk_norm_rope(z, gain, bd, cos, sa, sb):
    w = z.shape[-1]
    rep = w // LANES
    ss = _head_sumsq(z * z, bd)
    y = z * lax.rsqrt(ss * (1.0 / HEAD_DIM) + EPS) * gain

    def wide(t):
        return t if rep == 1 else jnp.concatenate([t] * rep, axis=-1)

    return (y * wide(cos) + pltpu.roll(y, w - AXIS_DIM // 2, 1) * wide(sa)
            + pltpu.roll(y, AXIS_DIM // 2, 1) * wide(sb))


def _kv_rep(kv128):
    lane = lax.broadcasted_iota(jnp.int32, kv128.shape, 1)
    sw = pltpu.roll(kv128, HEAD_DIM, 1)
    a0 = jnp.where(lane < HEAD_DIM, kv128, sw)
    a1 = jnp.where(lane < HEAD_DIM, sw, kv128)
    return jnp.concatenate([a0, a0, a1, a1], axis=-1)


def _rope_block(t, n_lat_tiles, tiles_per_batch):
    return jnp.where(t < n_lat_tiles, t % tiles_per_batch, tiles_per_batch)


EV_GQ, EV_GK, EV_GV, EV_GR, EV_AQ, EV_AK, EV_LR, EV_END = 0, 256, 512, 1024, 1536, 2048, 2176, 2304


def _proj_even_kernel(geom, xl_ref, xc_ref, mod_ref, gain_ref, w_ref, wvt_ref, gw_ref, gb_ref, qg_ref, kg_ref,
                      cos_ref, sa_ref, sb_ref, bd_ref,
                      gq_ref, gk_ref, gv_ref, gr_ref, g_ref, aq_ref, ak_ref, avt_ref):
    t = pl.program_id(0)
    row = _mod_row(t, *geom)
    x = jnp.where(t < geom[0], xl_ref[...], xc_ref[...])
    hb = _bf(_modulated(x, gain_ref[...], mod_ref, row, 0, 1))

    def seg(a, b):
        return _dot(hb, w_ref[:, a:b])

    gq_ref[...] = seg(EV_GQ, EV_GK) * (GLA_DK ** -0.5)
    gk_ref[...] = seg(EV_GK, EV_GV)
    gv_ref[...] = _bf(seg(EV_GV, EV_GR))
    gr_ref[...] = seg(EV_GR, EV_AQ)
    zg = _dot(_bf(seg(EV_LR, EV_END)), gw_ref[...]) + gb_ref[...]
    g_ref[...] = -(jnp.maximum(-zg, 0.0) + jnp.log1p(jnp.exp(-jnp.abs(zg)))) * (1.0 / GLA_GATE_NORM)
    bd = bd_ref[...]
    cos, sa, sb = cos_ref[...], sa_ref[...], sb_ref[...]
    aq = _qk_norm_rope(seg(EV_AQ, EV_AK), qg_ref[...], bd, cos, sa, sb)
    aq_ref[...] = _bf(aq * Q_SCALE)
    ak = _qk_norm_rope(seg(EV_AK, EV_LR), kg_ref[...], bd, cos, sa, sb)
    ak_ref[...] = _bf(_kv_rep(ak))
    avt_ref[0] = _bf(_dot_nt(wvt_ref[...], hb))


def _vt_spec():
    return pl.BlockSpec((1, LANES, TOK_TILE), lambda t: (t, 0, 0))


def _lat_ctx_specs(block, n_lat_tiles, lead=()):
    z = (0,) * len(lead)
    lat = pl.BlockSpec(lead + block, lambda t: z + (jnp.minimum(t, n_lat_tiles - 1), 0))
    ctx = pl.BlockSpec(lead + block, lambda t: z + (jnp.maximum(t - n_lat_tiles, 0), 0))
    return [lat, ctx]


def _proj_even(x_lat, x_ctx, mod, gain, w, wvt, gw, gb, qg, kg, tables, bd, geom):
    d = x_lat.shape[1]
    n = x_lat.shape[0] + x_ctx.shape[0]
    n_lat_tiles, tiles_per_batch, _ = geom
    tt = TOK_TILE
    cos, sa, sb = tables
    tok = lambda w_: pl.BlockSpec((tt, w_), lambda t: (t, 0))
    rope = pl.BlockSpec((tt, LANES), lambda t: (_rope_block(t, n_lat_tiles, tiles_per_batch), 0))
    outs = [(256, F32), (256, F32), (512, BF16), (512, F32), (512, F32), (512, BF16), (512, BF16)]
    return pl.pallas_call(
        functools.partial(_proj_even_kernel, geom),
        out_shape=[jax.ShapeDtypeStruct((n, w_), dt) for w_, dt in outs]
        + [jax.ShapeDtypeStruct((n // tt, LANES, tt), BF16)],
        grid=(n // tt,),
        in_specs=_lat_ctx_specs((tt, d), n_lat_tiles)
        + [_full(mod.shape), _full(gain.shape), _full(w.shape), _full(wvt.shape), _full(gw.shape),
           _full(gb.shape), _full(qg.shape), _full(kg.shape), rope, rope, rope, _full(bd.shape)],
        out_specs=[tok(w_) for w_, _ in outs] + [_vt_spec()],
        compiler_params=_cparams(("arbitrary",)),
        name="proj_even",
    )(x_lat, x_ctx, mod, gain, w, wvt, gw, gb, qg, kg, cos, sa, sb, bd)


def _proj_odd_kernel(geom, x_ref, pos_ref, pos_next_ref, fs_hbm, mod_prev_ref, mod_ref, gain_ref, w_ref, wvt_ref,
                     qg_ref, kg_ref, cos_ref, sa_ref, sb_ref, bd_ref, x1_ref, q_ref, k_ref, vt_ref, fbuf, fsem):
    t = pl.program_id(0)
    row = _mod_row(t, *geom)
    d = x_ref.shape[-1]
    gate = mod_prev_ref[pl.ds(row, 1), 5 * d:6 * d]
    f = _gather_tile(t, pl.num_programs(0), pos_ref, pos_next_ref, fs_hbm, fbuf, fsem, inline_prefetch=True)
    x1 = x_ref[...] + gate * f
    x1_ref[...] = x1
    hb = _bf(_modulated(x1, gain_ref[...], mod_ref, row, 0, 1))
    bd = bd_ref[...]
    cos, sa, sb = cos_ref[...], sa_ref[...], sb_ref[...]
    q = _qk_norm_rope(_dot(hb, w_ref[:, 0:d]), qg_ref[...], bd, cos, sa, sb)
    q_ref[...] = _bf(q * Q_SCALE)
    k = _qk_norm_rope(_dot(hb, w_ref[:, d:d + LANES]), kg_ref[...], bd, cos, sa, sb)
    k_ref[...] = _bf(_kv_rep(k))
    vt_ref[0] = _bf(_dot_nt(wvt_ref[...], hb))
    _gather_drain(t, pl.num_programs(0), fs_hbm, fbuf, fsem)


def _proj_odd(x_all, f_sorted, pos, mod_prev, mod, gain, w, wvt, qg, kg, tables, bd, geom):
    n, d = x_all.shape
    n_lat_tiles, tiles_per_batch, _ = geom
    tt = TOK_TILE
    cos, sa, sb = tables
    tok = lambda w_: pl.BlockSpec((tt, w_), lambda t: (t, 0))
    rope = pl.BlockSpec((tt, LANES), lambda t: (_rope_block(t, n_lat_tiles, tiles_per_batch), 0))
    outs = [(d, F32), (d, BF16), (512, BF16)]
    return pl.pallas_call(
        functools.partial(_proj_odd_kernel, geom),
        out_shape=[jax.ShapeDtypeStruct((n, w_), dt) for w_, dt in outs]
        + [jax.ShapeDtypeStruct((n // tt, LANES, tt), BF16)],
        grid=(n // tt,),
        in_specs=[tok(d)] + _gather_specs(n // tt)
        + [_full(mod_prev.shape), _full(mod.shape), _full(gain.shape), _full(w.shape),
           _full(wvt.shape), _full(qg.shape), _full(kg.shape), rope, rope, rope, _full(bd.shape)],
        out_specs=[tok(w_) for w_, _ in outs] + [_vt_spec()],
        scratch_shapes=_gather_scratch(d),
        compiler_params=_cparams(("arbitrary",)),
        name="proj_odd",
    )(x_all, pos, pos, f_sorted, mod_prev, mod, gain, w, wvt, qg, kg, cos, sa, sb, bd)


N_LEVELS = 6
GLA_MX_ROWS = (N_LEVELS + 2) * GLA_CHUNK


def _gla_constants():
    c = GLA_CHUNK
    mx = np.zeros((2, GLA_MX_ROWS, c), np.float32)
    pat = np.zeros((2, N_LEVELS + 1, c, GLA_HEADS * c), np.float32)
    r = np.arange(c)
    for lvl in range(N_LEVELS):
        h = 1 << lvl
        ref = (r // (2 * h)) * 2 * h + h - 1
        upper = (r % (2 * h)) >= h
        m = np.zeros((c, c), np.float32)
        for i in range(c):
            if upper[i]:
                m[i, ref[i] + 1:i + 1] = 1.0
            else:
                m[i, i + 1:ref[i] + 1] = 1.0
        mx[0, lvl * c:(lvl + 1) * c] = m
        same = (r[:, None] // (2 * h)) == (r[None, :] // (2 * h))
        p = same & upper[:, None] & (~upper)[None, :]
        pat[0, lvl] = np.tile(p.astype(np.float32), (1, GLA_HEADS))
    mx[0, N_LEVELS * c:(N_LEVELS + 1) * c] = (r[None, :] <= r[:, None])
    mx[0, (N_LEVELS + 1) * c:(N_LEVELS + 2) * c] = (r[None, :] > r[:, None])
    pat[0, N_LEVELS] = np.tile(np.eye(c, dtype=np.float32), (1, GLA_HEADS))
    for k in range(N_LEVELS + 2):
        mx[1, k * c:(k + 1) * c] = mx[0, k * c:(k + 1) * c][::-1, ::-1]
    for k in range(N_LEVELS + 1):
        pat[1, k] = np.tile(pat[0, k, :, 0:c][::-1, ::-1], (1, GLA_HEADS))
    return mx, pat


GLA_BLOCK = 1024
GLA_GROUP = 4


def _gla_chunks(chunks, mx, pat_ref, s_ref):
    c = GLA_CHUNK
    lane_head = lax.broadcasted_iota(jnp.int32, (c, GLA_HEADS * GLA_DK), 1) // GLA_DK

    def stack_heads(a):
        return jnp.concatenate([_bf(jnp.where(lane_head == h, a, 0.0)) for h in range(GLA_HEADS)], axis=0)

    xs = [_dot(mx, jnp.concatenate(_split2(g), axis=0)) for _, _, _, g in chunks]
    atts = [jnp.where(pat_ref[0, N_LEVELS] > 0.0, _dot_nt(_bf(q), stack_heads(k)), 0.0)
            for q, k, _, _ in chunks]
    for lvl in range(N_LEVELS):
        for j, (q, k, _, _) in enumerate(chunks):
            e = jnp.exp(xs[j][lvl * c:(lvl + 1) * c])
            atts[j] = atts[j] + jnp.where(pat_ref[0, lvl] > 0.0, _dot_nt(_bf(q * e), stack_heads(k * e)), 0.0)
    outs, qes, news, a_cols = [], [], [], []
    for j, (q, k, v, _) in enumerate(chunks):
        vhead = lax.broadcasted_iota(jnp.int32, v.shape, 1) // GLA_DV
        v_bd = jnp.concatenate([jnp.where(vhead == h, v, jnp.zeros_like(v)) for h in range(GLA_HEADS)], axis=0)
        outs.append(_dot(_bf(atts[j]), v_bd))
        bcum = xs[j][N_LEVELS * c:(N_LEVELS + 1) * c]
        brem = xs[j][(N_LEVELS + 1) * c:(N_LEVELS + 2) * c]
        qes.append(stack_heads(q * jnp.exp(bcum)))
        kt = jnp.transpose(k * jnp.exp(brem))
        news.append(jnp.concatenate(
            [_dot(_bf(kt[h * GLA_DK:(h + 1) * GLA_DK]), v[:, h * GLA_DV:(h + 1) * GLA_DV])
             for h in range(GLA_HEADS)], axis=0))
        tot = bcum[0:1] + brem[0:1]
        a_cols.append(jnp.transpose(jnp.exp(jnp.broadcast_to(tot, (8, tot.shape[1]))))[:, 0:1])
    s = s_ref[...]
    for j in range(len(chunks)):
        o_inter = _dot(qes[j], _bf(s))
        outs[j] = outs[j] + jnp.concatenate([o_inter[h * c:(h + 1) * c] for h in range(GLA_HEADS)], axis=-1)
        s = a_cols[j] * s + news[j]
    s_ref[...] = s
    return outs


def _gla_kernel(n_lat_chunks, n_ctx_chunks, ql_ref, kl_ref, vl_ref, gl_ref, qc_ref, kc_ref, vc_ref, gc_ref,
                mx_ref, pat_ref, ol_ref, oc_ref, s_ref):
    d = pl.program_id(1)
    c = GLA_CHUNK
    mx = mx_ref[0]

    def run(n_chunks, q_ref, k_ref, v_ref, g_ref, o_ref):
        group = min(GLA_GROUP, n_chunks)

        def body(i, carry):
            rows = []
            for j in range(group):
                step = i * group + j
                ci = jnp.where(d == 0, step, n_chunks - 1 - step)
                rows.append(pl.ds(pl.multiple_of(ci * c, c), c))
            outs = _gla_chunks([(q_ref[r, :], k_ref[r, :], v_ref[r, :], g_ref[r, :]) for r in rows],
                               mx, pat_ref, s_ref)
            for r, o in zip(rows, outs):
                o_ref[0, r, :] = o
            return carry
        lax.fori_loop(0, n_chunks // group, body, 0)

    @pl.when(pl.program_id(2) == 0)
    def _():
        s_ref[...] = jnp.zeros_like(s_ref)
        run(n_ctx_chunks, qc_ref, kc_ref, vc_ref, gc_ref, oc_ref)

    run(n_lat_chunks, ql_ref, kl_ref, vl_ref, gl_ref, ol_ref)


def _gla(gq, gk, gv, g, n_batch, seq, lc):
    mx_np, pat_np = _gla_constants()
    mx = jnp.asarray(np.concatenate([mx_np, mx_np], axis=2), BF16)
    pat = jnp.asarray(pat_np, F32)
    ctx0 = n_batch * seq // lc
    hk, hv = GLA_HEADS * GLA_DK, GLA_HEADS * GLA_DV
    blk = min(GLA_BLOCK, seq)
    nb = seq // blk
    for rows_ in (blk, lc):
        assert rows_ % (GLA_CHUNK * min(GLA_GROUP, rows_ // GLA_CHUNK)) == 0
    assert seq % blk == 0
    row = lambda b, d_, i: b * nb + jnp.where(d_ == 0, i, nb - 1 - i)
    lat = lambda w_, col: pl.BlockSpec((blk, w_), lambda b, d_, i: (row(b, d_, i), col(d_)))
    ctx = lambda w_, col: pl.BlockSpec((lc, w_), lambda b, d_, i: (ctx0 + b, col(d_)))
    zero = lambda d_: 0
    same = lambda d_: d_
    return pl.pallas_call(
        functools.partial(_gla_kernel, blk // GLA_CHUNK, lc // GLA_CHUNK),
        out_shape=[jax.ShapeDtypeStruct((2, n_batch * seq, hv), F32),
                   jax.ShapeDtypeStruct((2, n_batch * lc, hv), F32)],
        grid=(n_batch, 2, nb),
        in_specs=[lat(hk, zero), lat(hk, zero), lat(hv, zero), lat(hk, same),
                  ctx(hk, zero), ctx(hk, zero), ctx(hv, zero), ctx(hk, same),
                  pl.BlockSpec((1,) + mx.shape[1:], lambda b, d_, i: (d_, 0, 0)),
                  pl.BlockSpec((1,) + pat.shape[1:], lambda b, d_, i: (d_, 0, 0, 0))],
        out_specs=[pl.BlockSpec((1, blk, hv), lambda b, d_, i: (d_, row(b, d_, i), 0)),
                   pl.BlockSpec((1, lc, hv), lambda b, d_, i: (d_, b, 0))],
        scratch_shapes=[pltpu.VMEM((GLA_HEADS * GLA_DK, GLA_DV), F32)],
        compiler_params=_cparams(("arbitrary", "arbitrary", "arbitrary")),
        name="gla_scan",
    )(gq, gk, gv, g, gq, gk, gv, g, mx, pat)


def _stack_heads(q):
    lane_head = lax.broadcasted_iota(jnp.int32, q.shape, 1) // HEAD_DIM
    return jnp.concatenate([jnp.where(lane_head == h, q, jnp.zeros_like(q)) for h in range(4)], axis=0)


SAFE_SCORE_BOUND = 60.0
ONES_ROWS = 16


def _with_ones(vt):
    return jnp.concatenate([vt, jnp.ones((ONES_ROWS, vt.shape[1]), vt.dtype)], axis=0)


def _attn_store(acc, l, o_ref, u, tq):
    out = acc * (1.0 / l)
    out = jnp.concatenate([out[:, h * tq:(h + 1) * tq] for h in range(4)], axis=0)
    o_ref[u * tq:(u + 1) * tq, :] = _bf(jnp.transpose(out))


def _attn_dense_kernel(n_chunks, n_sub, *refs):
    if n_chunks:
        bound_ref, q_ref, kl_ref, vtl_ref, kc_ref, vtc_ref, o_ref = refs
    else:
        bound_ref, q_ref, kc_ref, vtc_ref, o_ref = refs
    tq = q_ref.shape[0] // n_sub
    cols = 4 * tq
    q4 = [_stack_heads(q_ref[u * tq:(u + 1) * tq, :]) for u in range(n_sub)]

    def scores(c, u):
        if c < n_chunks:
            return _dot_nt(kl_ref[c * ATT_K_CHUNK:(c + 1) * ATT_K_CHUNK, :], q4[u])
        return _dot_nt(kc_ref[...], q4[u])

    def update(carry, st, vt_aug):
        m, acc = carry
        m_new = jnp.maximum(m, jnp.max(st, axis=0, keepdims=True))
        acc = jnp.exp2(m - m_new) * acc + _dot(vt_aug, _bf(jnp.exp2(st - m_new)))
        return m_new, acc

    def run(fixed_ref):
        if fixed_ref is None:
            carry = [(jnp.full((1, cols), NEG_BIG, F32), jnp.zeros((HEAD_DIM + ONES_ROWS, cols), F32))
                     for _ in range(n_sub)]
        else:
            carry = [jnp.zeros((HEAD_DIM + ONES_ROWS, cols), F32) for _ in range(n_sub)]
        st = [scores(0, u) for u in range(n_sub)]
        for c in range(n_chunks + 1):
            st_next = [scores(c + 1, u) for u in range(n_sub)] if c < n_chunks else None
            vt_aug = _with_ones(vtl_ref[c] if c < n_chunks else vtc_ref[0])
            if fixed_ref is None:
                carry = [update(carry[u], st[u], vt_aug) for u in range(n_sub)]
            else:
                carry = [carry[u] + _dot(vt_aug, _bf(jnp.exp2(st[u] - fixed_ref))) for u in range(n_sub)]
            st = st_next
        for u in range(n_sub):
            acc = carry[u][1] if fixed_ref is None else carry[u]
            _attn_store(acc[0:HEAD_DIM], acc[HEAD_DIM:HEAD_DIM + 1], o_ref, u, tq)

    bound = bound_ref[0]

    @pl.when(bound <= SAFE_SCORE_BOUND)
    def _():
        run(bound)

    @pl.when(bound > SAFE_SCORE_BOUND)
    def _():
        run(None)


def _attn_window_kernel(seq, bound_ref, q_ref, *refs):
    nk = WIN_SUB + 2
    k_refs, v_refs = refs[0:nk], refs[nk:2 * nk]
    kc_ref, vtc_ref, sink_ref, o_ref = refs[2 * nk:]
    tq = SWA_WINDOW
    i = pl.program_id(2)
    kb = jnp.concatenate([r[...] for r in k_refs], axis=0)
    vtb = _with_ones(jnp.concatenate([r[0] for r in v_refs], axis=1))
    kc, vtc, sink = kc_ref[...], _with_ones(vtc_ref[0]), sink_ref[0]
    span = 3 * tq
    sb, sc = [], []
    for u in range(WIN_SUB):
        q4 = _stack_heads(q_ref[u * tq:(u + 1) * tq, :])
        first = i * WIN_SUB + u - 1
        kpos = first * tq + lax.broadcasted_iota(jnp.int32, (span, tq), 0)
        qpos = (first + 1) * tq + lax.broadcasted_iota(jnp.int32, (span, tq), 1)
        ok = (kpos >= 0) & (kpos < seq) & (jnp.abs(kpos - qpos) <= SWA_WINDOW)
        bias = jnp.where(ok, 0.0, NEG_BIG)
        sb.append(_dot_nt(kb[u * tq:u * tq + span], q4) + jnp.concatenate([bias] * 4, axis=1))
        sc.append(_dot_nt(kc, q4))
    def finish(use_bound):
        ms, pbs, pcs = [], [], []
        for u in range(WIN_SUB):
            if use_bound:
                m = jnp.maximum(bound_ref[0], sink)
            else:
                m = jnp.maximum(jnp.maximum(jnp.max(sb[u], axis=0, keepdims=True),
                                            jnp.max(sc[u], axis=0, keepdims=True)), sink)
            ms.append(m)
            pbs.append(_bf(jnp.exp2(sb[u] - m)))
            pcs.append(_bf(jnp.exp2(sc[u] - m)))
        for u in range(WIN_SUB):
            acc = _dot(vtb[:, u * tq:u * tq + span], pbs[u]) + _dot(vtc, pcs[u])
            l = acc[HEAD_DIM:HEAD_DIM + 1] + jnp.exp2(sink - ms[u])
            _attn_store(acc[0:HEAD_DIM], l, o_ref, u, tq)

    @pl.when(bound_ref[0] <= SAFE_SCORE_BOUND)
    def _():
        finish(True)

    @pl.when(bound_ref[0] > SAFE_SCORE_BOUND)
    def _():
        finish(False)


def _score_bound(q_gain, k_gain):
    return (HEAD_DIM * Q_SCALE * 1.02 * jnp.max(jnp.abs(q_gain)) * jnp.max(jnp.abs(k_gain))).reshape(1)


def _attention(mode, q, k_rep, vt, bound, n_batch, seq, lc, n_heads, n_kv, sink=None):
    ncol = n_heads * HEAD_DIM // ATT_COL
    col_per_kv = ncol // n_kv
    tq = ATT_Q_TILE
    tpb = seq // TOK_TILE
    n_lat_tiles = n_batch * tpb
    ctx_per_tile = TOK_TILE // lc
    ctx0 = n_batch * seq // lc
    assert ATT_K_CHUNK == TOK_TILE and TOK_TILE % lc == 0 and tq == SWA_WINDOW
    kv = lambda j: j // col_per_kv
    k_ctx = pl.BlockSpec((lc, ATT_COL), lambda b, j, i: (ctx0 + b, kv(j)))
    vt_ctx = pl.BlockSpec((1, HEAD_DIM, lc), lambda b, j, i: (n_lat_tiles + b // ctx_per_tile, kv(j), b % ctx_per_tile))
    if mode == "ctx":
        nq = lc // tq
        q0 = n_batch * seq // tq
        kern = functools.partial(_attn_dense_kernel, 0, 1)
        args = (q, k_rep, vt)
        in_specs = [pl.BlockSpec((tq, ATT_COL), lambda b, j, i: (q0 + b * nq + i, j)), k_ctx, vt_ctx]
    elif mode == "dense":
        tq = DENSE_SUB * ATT_Q_TILE
        nq = seq // tq
        kern = functools.partial(_attn_dense_kernel, tpb, DENSE_SUB)
        args = (q, k_rep, vt, k_rep, vt)
        in_specs = [pl.BlockSpec((tq, ATT_COL), lambda b, j, i: (b * nq + i, j)),
                    pl.BlockSpec((seq, ATT_COL), lambda b, j, i: (b, kv(j))),
                    pl.BlockSpec((tpb, HEAD_DIM, TOK_TILE), lambda b, j, i: (b, kv(j), 0)), k_ctx, vt_ctx]
    else:
        wb = SWA_WINDOW
        tq = WIN_SUB * wb
        nq = seq // tq
        nkb = seq // wb
        per_tile = TOK_TILE // wb
        kern = functools.partial(_attn_window_kernel, seq)
        nb = lambda i, o: jnp.clip(i * WIN_SUB + o, 0, nkb - 1)
        k_nb = lambda o: pl.BlockSpec((wb, ATT_COL), lambda b, j, i: (b * nkb + nb(i, o), kv(j)))
        v_nb = lambda o: pl.BlockSpec(
            (1, HEAD_DIM, wb), lambda b, j, i: (b * tpb + nb(i, o) // per_tile, kv(j), nb(i, o) % per_tile))
        offs = range(-1, WIN_SUB + 1)
        sink_row = jnp.repeat(sink.reshape(ncol, 1, 4), wb, axis=2) * LOG2E
        args = (q,) + (k_rep,) * len(offs) + (vt,) * len(offs) + (k_rep, vt, sink_row)
        in_specs = ([pl.BlockSpec((tq, ATT_COL), lambda b, j, i: (b * nq + i, j))]
                    + [k_nb(o) for o in offs] + [v_nb(o) for o in offs]
                    + [k_ctx, vt_ctx, pl.BlockSpec((1, 1, 4 * wb), lambda b, j, i: (j, 0, 0))])
    args = (bound,) + args
    in_specs = [pl.BlockSpec(memory_space=pltpu.SMEM)] + in_specs
    return pl.pallas_call(
        kern,
        out_shape=jax.ShapeDtypeStruct((n_batch * nq * tq, n_heads * HEAD_DIM), BF16),
        grid=(n_batch, ncol, nq),
        in_specs=in_specs,
        out_specs=pl.BlockSpec((tq, ATT_COL), lambda b, j, i: (b * nq + i, j)),
        compiler_params=_cparams(("arbitrary", "arbitrary", "arbitrary")),
        name="attention_" + mode,
    )(*args)


ROUTE_ROWS = 8
TOKEN_TILE = (8, LANES)
IN_TOKEN_TILE = (16, LANES)


def _pack_token_tiles(h, meta):
    t, d = h.shape
    width = IN_TOKEN_TILE[0] * IN_TOKEN_TILE[1]
    row = jnp.concatenate([h, meta, jnp.zeros((t, width - d - META_LANES), F32)], axis=1)
    return row.reshape((t,) + IN_TOKEN_TILE)


def _unpack_token_tiles(tiles):
    t = tiles.shape[0]
    d = TOKEN_TILE[0] * TOKEN_TILE[1]
    row = tiles.reshape(t, IN_TOKEN_TILE[0] * IN_TOKEN_TILE[1])
    return row[:, 0:d], row[:, d:d + META_LANES]


def _route(h, wrt_ref, wrt_hi_ref, brt_ref):
    hh, hl = _split2(h)
    a = _dot_nt(wrt_ref[...], hh)
    lt = a[0:LANES] + a[LANES:2 * LANES] + _dot_nt(wrt_hi_ref[...], hl) + brt_ref[...]
    col = lambda i: lt[i:i + 1, :]
    gl = [col(i) for i in range(MOE_GROUPS)]
    gmax = functools.reduce(jnp.maximum, gl)
    gi = jnp.where(gl[0] == gmax, 0, jnp.where(gl[1] == gmax, 1, jnp.where(gl[2] == gmax, 2, 3)))
    g_weight = 1.0 / functools.reduce(lambda a, b: a + b, [jnp.exp(x - gmax) for x in gl])
    el = []
    for j in range(MOE_EPG):
        cand = [col(MOE_GROUPS + g * MOE_EPG + j) for g in range(MOE_GROUPS)]
        el.append(jnp.where(gi == 0, cand[0], jnp.where(gi == 1, cand[1], jnp.where(gi == 2, cand[2], cand[3]))))
    m1 = functools.reduce(jnp.maximum, el)
    i1 = jnp.where(el[0] == m1, 0, jnp.where(el[1] == m1, 1, jnp.where(el[2] == m1, 2, 3)))
    rest = [jnp.where(i1 == j, -jnp.inf, el[j]) for j in range(MOE_EPG)]
    m2 = functools.reduce(jnp.maximum, rest)
    i2 = jnp.where(rest[0] == m2, 0, jnp.where(rest[1] == m2, 1, jnp.where(rest[2] == m2, 2, 3)))
    e2 = jnp.exp(m2 - m1)
    w1 = g_weight / (1.0 + e2)
    w2 = g_weight * e2 / (1.0 + e2)
    lo = jnp.minimum(i1, i2)
    hi = jnp.maximum(i1, i2)
    w_lo = jnp.where(i1 == lo, w1, w2)
    w_hi = jnp.where(i1 == lo, w2, w1)
    pair = jnp.where(lo == 0, hi - 1, jnp.where(lo == 1, hi + 1, N_PAIRS - 1))
    return w_lo, w_hi, gi * N_PAIRS + pair


def _out_tail(geom, m, x, mod_ref, gain_ffn_ref, wrt_ref, wrt_hi_ref, brt_ref, triu_ref,
              x_new_ref, hrow_ref, rt_ref, counts_ref, run_ref):
    t = pl.program_id(0)
    row = _mod_row(t, *geom)
    tt, d = x.shape

    @pl.when(t == 0)
    def _():
        run_ref[...] = jnp.zeros_like(run_ref)

    x_new = x + mod_ref[pl.ds(row, 1), 2 * d:3 * d] * m
    x_new_ref[...] = x_new
    h = _modulated(x_new, gain_ffn_ref[...], mod_ref, row, 3, 4)
    w_lo, w_hi, bucket = _route(h, wrt_ref, wrt_hi_ref, brt_ref)
    onehot = lax.broadcasted_iota(jnp.int32, (LANES, tt), 0) == bucket
    ones = jnp.where(onehot, 1.0, 0.0)
    before = _dot(_bf(ones), triu_ref[...]) + run_ref[...]
    rank = jnp.sum(jnp.where(onehot, before, 0.0), axis=0, keepdims=True)
    run = run_ref[...] + jnp.sum(ones, axis=1, keepdims=True)
    run_ref[...] = run
    counts_ref[...] = jnp.broadcast_to(run, counts_ref.shape)
    rec = jnp.concatenate([w_lo, w_hi, bucket.astype(F32), rank, jnp.zeros((ROUTE_ROWS - 4, tt), F32)], axis=0)
    rt_ref[0] = rec
    meta_t = jnp.concatenate([rec, jnp.zeros((META_LANES - ROUTE_ROWS, tt), F32)], axis=0)
    hrow_ref[...] = _pack_token_tiles(h, jnp.transpose(meta_t))


def _pick(t, n_lat_tiles, lat_ref, ctx_ref):
    return jnp.where(t < n_lat_tiles, lat_ref[...], ctx_ref[...])


def _out_even_kernel(geom, ol_ref, oc_ref, r_ref, attl_ref, attc_ref, gn_ref, w_ref, xl_ref, xc_ref, mod_ref,
                     gain_ffn_ref, wrt_ref, wrt_hi_ref, brt_ref, triu_ref,
                     x_new_ref, hrow_ref, rt_ref, counts_ref, run_ref):
    t = pl.program_id(0)
    o2 = _pick(t, geom[0], ol_ref, oc_ref)
    o = o2[0] + o2[1]
    r = r_ref[...]
    parts = []
    for h in range(GLA_HEADS):
        sl = slice(h * GLA_DV, (h + 1) * GLA_DV)
        parts.append(_rms(o[:, sl]) * gn_ref[...] * _silu(r[:, sl]))
    a = _bf(jnp.concatenate(parts, axis=-1))
    half = a.shape[-1]
    m = _dot(a, w_ref[0:half, :]) + _dot(_pick(t, geom[0], attl_ref, attc_ref), w_ref[half:, :])
    _out_tail(geom, m, _pick(t, geom[0], xl_ref, xc_ref), mod_ref, gain_ffn_ref, wrt_ref, wrt_hi_ref, brt_ref,
              triu_ref, x_new_ref, hrow_ref, rt_ref, counts_ref, run_ref)


def _out_odd_kernel(geom, att_ref, w_ref, x_ref, mod_ref, gain_ffn_ref, wrt_ref, wrt_hi_ref, brt_ref, triu_ref,
                    x_new_ref, hrow_ref, rt_ref, counts_ref, run_ref):
    m = _dot(att_ref[...], w_ref[...])
    _out_tail(geom, m, x_ref[...], mod_ref, gain_ffn_ref, wrt_ref, wrt_hi_ref, brt_ref, triu_ref,
              x_new_ref, hrow_ref, rt_ref, counts_ref, run_ref)


def _out_proj(kernel, lead_args, lead_specs, d, mod, gain_ffn, router, n_rows, name):
    tt = TOK_TILE
    tok = lambda w_: pl.BlockSpec((tt, w_), lambda t: (t, 0))
    r = np.arange(tt)
    triu = jnp.asarray((r[:, None] < r[None, :]).astype(np.float32), BF16)
    wrt, wrt_hi, br = router
    brt = jnp.broadcast_to(br.reshape(LANES, 1), (LANES, tt))
    return pl.pallas_call(
        kernel,
        out_shape=[jax.ShapeDtypeStruct((n_rows, d), F32),
                   jax.ShapeDtypeStruct((n_rows,) + IN_TOKEN_TILE, F32),
                   jax.ShapeDtypeStruct((n_rows // tt, ROUTE_ROWS, tt), F32),
                   jax.ShapeDtypeStruct((LANES, LANES), F32)],
        grid=(n_rows // tt,),
        in_specs=lead_specs + [_full(mod.shape), _full(gain_ffn.shape), _full(wrt.shape), _full(wrt_hi.shape),
                               _full(brt.shape), _full(triu.shape)],
        out_specs=[tok(d), pl.BlockSpec((tt,) + IN_TOKEN_TILE, lambda t: (t, 0, 0)),
                   pl.BlockSpec((1, ROUTE_ROWS, tt), lambda t: (t, 0, 0)), _full((LANES, LANES))],
        scratch_shapes=[pltpu.VMEM((LANES, 1), F32)],
        compiler_params=_cparams(("arbitrary",)),
        name=name,
    )(*lead_args, mod, gain_ffn, wrt, wrt_hi, brt, triu)


SUBLANES = 8


def _for_each_row(n_rows, start_row_copy):
    def body(g, c):
        base = g * SUBLANES
        for j in range(SUBLANES):
            start_row_copy(base + j, j % 2)
        return c
    lax.fori_loop(0, n_rows // SUBLANES, body, 0)


def _pos_kernel(start_ref, rt_ref, pos_ref):
    bucket = rt_ref[:, 2, :].astype(jnp.int32)
    base = jnp.zeros_like(bucket)
    for b in range(N_BUCKETS):
        base = jnp.where(bucket == b, start_ref[b], base)
    pos_ref[:, 0, :] = base + rt_ref[:, 3, :].astype(jnp.int32)


def _dispatch_kernel(last_ref, pos_ref, x_ref, xs_ref, zbuf, sem, zsem):
    tt = x_ref.shape[0]
    tm = zbuf.shape[0]

    @pl.when(pl.program_id(0) == 0)
    def _():
        zbuf[...] = jnp.zeros_like(zbuf)

        def zero_copy(b):
            return pltpu.make_async_copy(zbuf, xs_ref.at[pl.ds(jnp.maximum(last_ref[b], 0) * tm, tm)], zsem)

        for b in range(2 * N_BUCKETS):
            @pl.when(last_ref[b] >= 0)
            def _():
                zero_copy(b).start()
        for b in range(2 * N_BUCKETS):
            @pl.when(last_ref[b] >= 0)
            def _():
                zero_copy(b).wait()

    _for_each_row(tt, lambda r, prio: pltpu.make_async_copy(
        x_ref.at[r], xs_ref.at[pos_ref[0, 0, r]], sem).start(priority=prio))
    pltpu.make_async_copy(x_ref, xs_ref.at[pl.ds(0, tt)], sem).wait()


def _moe_mlp_kernel(tlo_ref, thi_ref, nused_ref, xs_ref, wg_lo, wu_lo, wd_lo, wg_hi, wu_hi, wd_hi, f_ref):
    del tlo_ref, thi_ref
    used = pl.program_id(0) < nused_ref[0]

    @pl.when(used)
    def _():
        x, meta = _unpack_token_tiles(xs_ref[...])
        xb = _bf(x)
        w_lo = meta[:, 0:1]
        w_hi = meta[:, 1:2]
        hid_lo = _silu(_dot(xb, wg_lo[0])) * _dot(xb, wu_lo[0]) * w_lo
        hid_hi = _silu(_dot(xb, wg_hi[0])) * _dot(xb, wu_hi[0]) * w_hi
        f = _dot(_bf(hid_lo), wd_lo[0]) + _dot(_bf(hid_hi), wd_hi[0])
        f_ref[...] = f.reshape(f_ref.shape)

    @pl.when(jnp.logical_not(used))
    def _():
        f_ref[...] = jnp.zeros_like(f_ref)


def _moe(hrow, rt, counts, experts, layer):
    n = hrow.shape[0]
    d = TOKEN_TILE[0] * TOKEN_TILE[1]
    tm, tt = MOE_TILE, TOK_TILE
    n_tiles = n // tm + N_BUCKETS
    p = n_tiles * tm
    cnt = counts[0:N_BUCKETS, 0].astype(jnp.int32)
    tiles_b = (cnt + tm - 1) // tm
    tile_end = jnp.cumsum(tiles_b)
    start_b = (tile_end - tiles_b) * tm
    n_used = tile_end[-1].reshape(1)
    spare = n_used[0] + jnp.arange(N_BUCKETS, dtype=jnp.int32)
    last_tile = jnp.concatenate([jnp.where(tiles_b > 0, tile_end - 1, -1), jnp.where(spare < n_tiles, spare, -1)])
    pos = pl.pallas_call(
        _pos_kernel,
        out_shape=jax.ShapeDtypeStruct((n // tt, 1, tt), jnp.int32),
        grid_spec=pltpu.PrefetchScalarGridSpec(
            num_scalar_prefetch=1, grid=(1,),
            in_specs=[pl.BlockSpec(rt.shape, lambda i, s: (0, 0, 0))],
            out_specs=pl.BlockSpec((n // tt, 1, tt), lambda i, s: (0, 0, 0))),
        compiler_params=_cparams(("arbitrary",)),
        name="moe_positions",
    )(start_b, rt)
    tile_ids = jnp.arange(n_tiles, dtype=jnp.int32)
    tile_bucket = jnp.sum((tile_ids[:, None] >= tile_end[None, :]).astype(jnp.int32), axis=1)
    tile_bucket = jnp.minimum(tile_bucket, jnp.sum((n_used[0] - 1 >= tile_end).astype(jnp.int32)))
    tile_bucket = jnp.minimum(tile_bucket, N_BUCKETS - 1)
    pair_lo = jnp.asarray([0, 0, 0, 1, 1, 2], jnp.int32)
    pair_hi = jnp.asarray([1, 2, 3, 2, 3, 3], jnp.int32)
    grp = tile_bucket // N_PAIRS
    w_gate, w_up, w_down = experts
    first = layer * MOE_GROUPS * MOE_EPG
    t_lo = first + grp * MOE_EPG + pair_lo[tile_bucket % N_PAIRS]
    t_hi = first + grp * MOE_EPG + pair_hi[tile_bucket % N_PAIRS]

    dt = DISPATCH_TILE if n % DISPATCH_TILE == 0 else tt
    xs = pl.pallas_call(
        _dispatch_kernel,
        out_shape=jax.ShapeDtypeStruct((p,) + IN_TOKEN_TILE, F32),
        grid_spec=pltpu.PrefetchScalarGridSpec(
            num_scalar_prefetch=1, grid=(n // dt,),
            in_specs=[pl.BlockSpec((1, 1, dt), lambda t, s: (t, 0, 0), memory_space=pltpu.SMEM),
                      pl.BlockSpec((dt,) + IN_TOKEN_TILE, lambda t, s: (t, 0, 0))],
            out_specs=pl.BlockSpec(memory_space=pl.ANY),
            scratch_shapes=[pltpu.VMEM((tm,) + IN_TOKEN_TILE, F32), pltpu.SemaphoreType.DMA(()),
                            pltpu.SemaphoreType.DMA(())]),
        compiler_params=_cparams(("arbitrary",)),
        name="moe_dispatch",
    )(last_tile, pos.reshape(n // dt, 1, dt), hrow)

    f = D_EXPERT
    up_lo = pl.BlockSpec((1, d, f), lambda t, lo, hi, nu: (lo[t], 0, 0))
    up_hi = pl.BlockSpec((1, d, f), lambda t, lo, hi, nu: (hi[t], 0, 0))
    dn_lo = pl.BlockSpec((1, f, d), lambda t, lo, hi, nu: (lo[t], 0, 0))
    dn_hi = pl.BlockSpec((1, f, d), lambda t, lo, hi, nu: (hi[t], 0, 0))
    grid_spec = pltpu.PrefetchScalarGridSpec(
        num_scalar_prefetch=3,
        grid=(n_tiles,),
        in_specs=[pl.BlockSpec((tm,) + IN_TOKEN_TILE, lambda t, lo, hi, nu: (jnp.minimum(t, nu[0] - 1), 0, 0)),
                  up_lo, up_lo, dn_lo, up_hi, up_hi, dn_hi],
        out_specs=pl.BlockSpec((tm,) + TOKEN_TILE, lambda t, *_: (t, 0, 0)),
    )
    f_sorted = pl.pallas_call(
        _moe_mlp_kernel,
        out_shape=jax.ShapeDtypeStruct((p,) + TOKEN_TILE, F32),
        grid_spec=grid_spec,
        compiler_params=_cparams(("arbitrary",)),
        name="moe_experts",
    )(t_lo, t_hi, n_used, xs, w_gate, w_up, w_down, w_gate, w_up, w_down)
    return f_sorted, pos


def _gather_tile(t, n_t, pos_ref, pos_next_ref, src_hbm, buf, sem, inline_prefetch=False):
    tt = buf.shape[1]

    def start(p_ref, slot):
        _for_each_row(tt, lambda r, prio: pltpu.make_async_copy(
            src_hbm.at[p_ref[0, 0, r]], buf.at[slot].at[r], sem.at[slot]).start(priority=prio))

    slot = t % 2

    def wait(s):
        pltpu.make_async_copy(src_hbm.at[pl.ds(0, tt)], buf.at[s], sem.at[s]).wait()

    @pl.when(t == 0)
    def _():
        start(pos_ref, 0)

    wait(slot)
    if inline_prefetch:
        for r in range(tt):
            pltpu.make_async_copy(src_hbm.at[pos_next_ref[0, 0, r]], buf.at[1 - slot].at[r],
                                  sem.at[1 - slot]).start(priority=r % 2)
    else:
        @pl.when(t + 1 < n_t)
        def _():
            start(pos_next_ref, 1 - slot)

    return buf[slot].reshape(tt, TOKEN_TILE[0] * TOKEN_TILE[1])


def _gather_drain(t, n_t, src_hbm, buf, sem):
    tt = buf.shape[1]

    @pl.when(t == n_t - 1)
    def _():
        pltpu.make_async_copy(src_hbm.at[pl.ds(0, tt)], buf.at[1 - t % 2], sem.at[1 - t % 2]).wait()


def _gather_specs(n_t):
    tt = TOK_TILE
    return [pl.BlockSpec((1, 1, tt), lambda t: (t, 0, 0), memory_space=pltpu.SMEM),
            pl.BlockSpec((1, 1, tt), lambda t: (jnp.minimum(t + 1, n_t - 1), 0, 0), memory_space=pltpu.SMEM),
            pl.BlockSpec(memory_space=pl.ANY)]


def _gather_scratch(d):
    assert d == TOKEN_TILE[0] * TOKEN_TILE[1]
    return [pltpu.VMEM((2, TOK_TILE) + TOKEN_TILE, F32), pltpu.SemaphoreType.DMA((2,))]


def _final_kernel(tiles_per_batch, x_ref, pos_ref, pos_next_ref, fs_hbm, mod_ref, o_ref, fbuf, fsem):
    t = pl.program_id(0)
    row = t // tiles_per_batch
    d = x_ref.shape[-1]
    f = _gather_tile(t, pl.num_programs(0), pos_ref, pos_next_ref, fs_hbm, fbuf, fsem)
    o_ref[...] = x_ref[...] + mod_ref[pl.ds(row, 1), 5 * d:6 * d] * f


def _final(x_lat, f_sorted, pos, mod, tiles_per_batch):
    n, d = x_lat.shape
    tok = pl.BlockSpec((TOK_TILE, d), lambda t: (t, 0))
    return pl.pallas_call(
        functools.partial(_final_kernel, tiles_per_batch),
        out_shape=jax.ShapeDtypeStruct((n, d), F32),
        grid=(n // TOK_TILE,),
        in_specs=[tok] + _gather_specs(n // TOK_TILE) + [_full(mod.shape)],
        out_specs=tok,
        scratch_shapes=_gather_scratch(d),
        compiler_params=_cparams(("arbitrary",)),
        name="final_residual",
    )(x_lat, pos, pos, f_sorted, mod)


def _block_diag_ones():
    r = np.arange(MXU_DIM) // HEAD_DIM
    return jnp.asarray((r[:, None] == r[None, :]).astype(np.float32), BF16)


def _router_weights(wg, bg, we, be):
    d = wg.shape[0]
    n = MOE_GROUPS + MOE_GROUPS * MOE_EPG
    wt = jnp.concatenate([wg, we, jnp.zeros((d, LANES - n), F32)], axis=1).T
    b = jnp.concatenate([bg, be, jnp.zeros((LANES - n,), F32)])
    hi = _bf(wt)
    lo = _bf(wt - hi.astype(F32))
    return jnp.concatenate([hi, lo], axis=0), hi, b


def _expert_weights(w_gate, w_up, w_down):
    l, g, e, d, f = w_gate.shape
    n = l * g * e
    return _bf(w_gate).reshape(n, d, f), _bf(w_up).reshape(n, d, f), _bf(w_down).reshape(n, f, d)


def kernel(x, c, ctx, c_ctx, mod_w, mod_b, norm_mix, norm_ffn, ev_w_in, ev_w_out, gla_gate_w, gla_gate_b,
           gla_out_norm, att_q_norm, att_k_norm, od_w_in, od_w_out, swa_sink, swa_q_norm, swa_k_norm,
           router_group_w, router_group_b, router_expert_w, router_expert_b, exp_w_gate, exp_w_up, exp_w_down):
    n_batch, seq, d = x.shape
    lc = ctx.shape[1]
    depth = mod_w.shape[0]
    n_lat = n_batch * seq
    tiles_per_batch = seq // TOK_TILE
    geom = (n_lat // TOK_TILE, tiles_per_batch, n_batch)
    assert depth == 2 and seq % TOK_TILE == 0 and (n_batch * lc) % TOK_TILE == 0 and n_batch < 16

    x_lat, x_ctx = x.reshape(n_lat, d), ctx.reshape(n_batch * lc, d)
    n_all = n_lat + n_batch * lc
    c_rows = jnp.zeros((16, d), F32).at[:n_batch].set(c).at[n_batch].set(c_ctx)
    mod = _modulation(c_rows, mod_w, mod_b)
    tables = _rope_tables(seq)
    bd = _block_diag_ones()
    row2 = lambda v: v.reshape(1, -1)
    tile_gain = lambda gvec, reps: jnp.tile(gvec, reps).reshape(1, -1)

    w0 = ev_w_in[0]
    seg = np.cumsum([0, 256, 256, 512, 512, 32, 512, 128, 128])
    cols = lambda i: w0[:, seg[i]:seg[i + 1]]
    w_even = _bf(jnp.concatenate([cols(0), cols(1), cols(2), cols(3), cols(5), cols(6), cols(4),
                                  jnp.zeros((d, EV_END - EV_LR - 2 * GLA_GATE_RANK), F32)], axis=1))
    wvt_even = _bf(cols(7).T)
    hk = GLA_HEADS * GLA_DK
    gw = jnp.zeros((LANES, 2 * hk), F32)
    gw = gw.at[0:GLA_GATE_RANK, 0:hk].set(gla_gate_w[0, 0])
    gw = gw.at[GLA_GATE_RANK:2 * GLA_GATE_RANK, hk:2 * hk].set(gla_gate_w[0, 1])
    gb = gla_gate_b[0].reshape(1, 2 * hk)
    gq, gk, gv, gr, g, aq, ak, avt = _proj_even(
        x_lat, x_ctx, mod[0], row2(norm_mix[0]), w_even, wvt_even, _bf(gw), gb,
        tile_gain(att_q_norm[0], ATT_HEADS), tile_gain(att_k_norm[0], ATT_KV_HEADS), tables, bd, geom)
    o_lat, o_ctx = _gla(gq, gk, gv, g, n_batch, seq, lc)
    bound0 = _score_bound(att_q_norm[0], att_k_norm[0])
    att_lat = _attention("dense", aq, ak, avt, bound0, n_batch, seq, lc, ATT_HEADS, ATT_KV_HEADS)
    att_ctx = _attention("ctx", aq, ak, avt, bound0, n_batch, seq, lc, ATT_HEADS, ATT_KV_HEADS)
    router = _router_weights(router_group_w[0], router_group_b[0], router_expert_w[0], router_expert_b[0])
    tt = TOK_TILE
    nlt = geom[0]
    gn = row2(gla_out_norm[0])
    w_out0 = _bf(ev_w_out[0])
    hv, ha = GLA_HEADS * GLA_DV, ATT_HEADS * HEAD_DIM
    x_mid, hrow, rt, counts = _out_proj(
        functools.partial(_out_even_kernel, geom),
        (o_lat, o_ctx, gr, att_lat, att_ctx, gn, w_out0, x_lat, x_ctx),
        _lat_ctx_specs((tt, hv), nlt, lead=(2,)) + [pl.BlockSpec((tt, hv), lambda t: (t, 0))]
        + _lat_ctx_specs((tt, ha), nlt) + [_full(gn.shape), _full(w_out0.shape)] + _lat_ctx_specs((tt, d), nlt),
        d, mod[0], row2(norm_ffn[0]), router, n_all, "out_even")
    experts = _expert_weights(exp_w_gate, exp_w_up, exp_w_down)
    f0, pos0 = _moe(hrow, rt, counts, experts, 0)

    w_odd = od_w_in[0]
    x1, q1, k1, v1t = _proj_odd(
        x_mid, f0, pos0, mod[0], mod[1], row2(norm_mix[1]), _bf(w_odd[:, 0:d + LANES]), _bf(w_odd[:, d + LANES:].T),
        tile_gain(swa_q_norm[0], SWA_HEADS), tile_gain(swa_k_norm[0], SWA_KV_HEADS), tables, bd, geom)
    att1 = _attention("window", q1, k1, v1t, _score_bound(swa_q_norm[0], swa_k_norm[0]), n_batch, seq, lc,
                      SWA_HEADS, SWA_KV_HEADS, sink=swa_sink[0])
    router = _router_weights(router_group_w[1], router_group_b[1], router_expert_w[1], router_expert_b[1])
    w_out1 = _bf(od_w_out[0])
    tok = lambda w_: pl.BlockSpec((tt, w_), lambda t: (t, 0))
    x2, hrow1, rt1, counts1 = _out_proj(
        functools.partial(_out_odd_kernel, geom), (att1, w_out1, x1),
        [tok(SWA_HEADS * HEAD_DIM), _full(w_out1.shape), tok(d)],
        d, mod[1], row2(norm_ffn[1]), router, n_lat, "out_odd")
    f1, pos1 = _moe(hrow1, rt1, counts1, experts, 1)
    out = _final(x2, f1, pos1, mod[1], tiles_per_batch)
    return out.reshape(n_batch, seq, d)
```

```python
import functools

import numpy as np
import jax
import jax.numpy as jnp
from jax import lax
from jax.experimental import pallas as pl
from jax.experimental.pallas import tpu as pltpu

F32 = jnp.float32
BF16 = jnp.bfloat16

GRID_W = 64
HEAD_DIM = 64
AXIS_DIM = HEAD_DIM // 2
ROPE_THETA = 10000.0
EPS = 1e-6
N_MOD = 6
GLA_HEADS = 4
GLA_DK = 64
GLA_DV = 128
GLA_GATE_RANK = 16
GLA_GATE_NORM = 16.0
GLA_CHUNK = 64
ATT_HEADS = 8
ATT_KV_HEADS = 2
SWA_HEADS = 16
SWA_KV_HEADS = 2
SWA_WINDOW = 128
MOE_GROUPS = 4
MOE_EPG = 4
D_EXPERT = 256
N_PAIRS = 6
N_BUCKETS = MOE_GROUPS * N_PAIRS

LANES = 128
MXU_DIM = 256
TOK_TILE = 512
ATT_Q_TILE = 128
ATT_K_CHUNK = 512
ATT_COL = 4 * HEAD_DIM
WIN_SUB = 4
DENSE_SUB = 4
MOE_TILE = 256
DISPATCH_TILE = 1024
META_LANES = LANES
VMEM_LIMIT = 56 * 1024 * 1024
NEG_BIG = -1e30
LOG2E = 1.4426950408889634
Q_SCALE = HEAD_DIM ** -0.5 * LOG2E


def _bf(x):
    return x.astype(BF16)


def _split2(x):
    hi = _bf(x)
    lo = _bf(x - hi.astype(F32))
    return hi, lo


def _dot(a, b):
    return jnp.dot(a, b, preferred_element_type=F32)


def _dot_nt(a, b):
    return lax.dot_general(a, b, (((1,), (1,)), ((), ())), preferred_element_type=F32)


def _dot_tn(a, b):
    return lax.dot_general(a, b, (((0,), (0,)), ((), ())), preferred_element_type=F32)


def _silu(x):
    return x / (1.0 + jnp.exp(-x))


def _rms(x):
    return x * lax.rsqrt(jnp.mean(x * x, axis=-1, keepdims=True) + EPS)


def _cparams(sem):
    return pltpu.CompilerParams(dimension_semantics=sem, vmem_limit_bytes=VMEM_LIMIT)


def _full(shape):
    n = len(shape)
    return pl.BlockSpec(shape, lambda *_: (0,) * n)


def _mod_kernel(c_ref, w_ref, b_ref, o_ref):
    c = c_ref[...]
    ch, cl = _split2(_silu(c))
    wh, wl = _split2(w_ref[0])
    o_ref[0] = _dot(ch, wh) + _dot(ch, wl) + _dot(cl, wh) + b_ref[0]


def _modulation(c_rows, mod_w, mod_b):
    depth, d, n = mod_w.shape
    tn = n // 4
    return pl.pallas_call(
        _mod_kernel,
        out_shape=jax.ShapeDtypeStruct((depth, 16, n), F32),
        grid=(depth, n // tn),
        in_specs=[pl.BlockSpec((16, d), lambda i, j: (0, 0)),
                  pl.BlockSpec((1, d, tn), lambda i, j: (i, 0, j)),
                  pl.BlockSpec((1, 1, tn), lambda i, j: (i, 0, j))],
        out_specs=pl.BlockSpec((1, 16, tn), lambda i, j: (i, 0, j)),
        compiler_params=_cparams(("arbitrary", "arbitrary")),
        name="modulation",
    )(c_rows, mod_w, mod_b.reshape(depth, 1, n))


def _mod_row(t, n_lat_tiles, tiles_per_batch, n_batch):
    return jnp.where(t < n_lat_tiles, t // tiles_per_batch, n_batch)


def _modulated(x, gain, mod_ref, row, k_shift, k_scale):
    d = x.shape[-1]
    shift = mod_ref[pl.ds(row, 1), k_shift * d:(k_shift + 1) * d]
    scale = mod_ref[pl.ds(row, 1), k_scale * d:(k_scale + 1) * d]
    return _rms(x) * gain * (1.0 + scale) + shift


def _rope_tables(seq):
    rows = seq // GRID_W
    row = np.repeat(np.arange(rows), GRID_W)
    col = np.tile(np.arange(GRID_W), rows)
    inv_freq = ROPE_THETA ** (-np.arange(0, AXIS_DIM, 2, dtype=np.float64) / AXIS_DIM)
    ang = np.stack([row[:, None] * inv_freq, col[:, None] * inv_freq], axis=1)
    cos, sin = np.cos(ang), np.sin(ang)
    zero = np.zeros_like(sin)
    cos64 = np.concatenate([cos[:, 0], cos[:, 0], cos[:, 1], cos[:, 1]], axis=-1)
    sa64 = np.concatenate([-sin[:, 0], zero[:, 0], -sin[:, 1], zero[:, 1]], axis=-1)
    sb64 = np.concatenate([zero[:, 0], sin[:, 0], zero[:, 1], sin[:, 1]], axis=-1)

    def widen(t, fill):
        t = np.concatenate([t, t], axis=-1)
        return jnp.asarray(np.concatenate([t, np.full((TOK_TILE, LANES), fill)], axis=0), F32)

    return widen(cos64, 1.0), widen(sa64, 0.0), widen(sb64, 0.0)


def _head_sumsq(y, bd):
    w = y.shape[-1]
    outs = []
    for s in range(0, w, MXU_DIM):
        e = min(s + MXU_DIM, w)
        hi, lo = _split2(y[:, s:e])
        b = bd[0:e - s, 0:e - s]
        outs.append(_dot(hi, b) + _dot(lo, b))
    return outs[0] if len(outs) == 1 else jnp.concatenate(outs, axis=-1)


def _qk_norm_rope(z, gain, bd, cos, sa, sb):
    w = z.shape[-1]
    rep = w // LANES
    ss = _head_sumsq(z * z, bd)
    y = z * lax.rsqrt(ss * (1.0 / HEAD_DIM) + EPS) * gain

    def wide(t):
        return t if rep == 1 else jnp.concatenate([t] * rep, axis=-1)

    return (y * wide(cos) + pltpu.roll(y, w - AXIS_DIM // 2, 1) * wide(sa)
            + pltpu.roll(y, AXIS_DIM // 2, 1) * wide(sb))


def _kv_rep(kv128):
    lane = lax.broadcasted_iota(jnp.int32, kv128.shape, 1)
    sw = pltpu.roll(kv128, HEAD_DIM, 1)
    a0 = jnp.where(lane < HEAD_DIM, kv128, sw)
    a1 = jnp.where(lane < HEAD_DIM, sw, kv128)
    return jnp.concatenate([a0, a0, a1, a1], axis=-1)


def _rope_block(t, n_lat_tiles, tiles_per_batch):
    return jnp.where(t < n_lat_tiles, t % tiles_per_batch, tiles_per_batch)


EV_GQ, EV_GK, EV_GV, EV_GR, EV_AQ, EV_AK, EV_LR, EV_END = 0, 256, 512, 1024, 1536, 2048, 2176, 2304


def _proj_even_kernel(geom, xl_ref, xc_ref, mod_ref, gain_ref, w_ref, wvt_ref, gw_ref, gb_ref, qg_ref, kg_ref,
                      cos_ref, sa_ref, sb_ref, bd_ref,
                      gq_ref, gk_ref, gv_ref, gr_ref, g_ref, aq_ref, ak_ref, avt_ref):
    t = pl.program_id(0)
    row = _mod_row(t, *geom)
    x = jnp.where(t < geom[0], xl_ref[...], xc_ref[...])
    hb = _bf(_modulated(x, gain_ref[...], mod_ref, row, 0, 1))

    def seg(a, b):
        return _dot(hb, w_ref[:, a:b])

    gq_ref[...] = seg(EV_GQ, EV_GK) * (GLA_DK ** -0.5)
    gk_ref[...] = seg(EV_GK, EV_GV)
    gv_ref[...] = _bf(seg(EV_GV, EV_GR))
    gr_ref[...] = seg(EV_GR, EV_AQ)
    zg = _dot(_bf(seg(EV_LR, EV_END)), gw_ref[...]) + gb_ref[...]
    g_ref[...] = -(jnp.maximum(-zg, 0.0) + jnp.log1p(jnp.exp(-jnp.abs(zg)))) * (1.0 / GLA_GATE_NORM)
    bd = bd_ref[...]
    cos, sa, sb = cos_ref[...], sa_ref[...], sb_ref[...]
    aq = _qk_norm_rope(seg(EV_AQ, EV_AK), qg_ref[...], bd, cos, sa, sb)
    aq_ref[...] = _bf(aq * Q_SCALE)
    ak = _qk_norm_rope(seg(EV_AK, EV_LR), kg_ref[...], bd, cos, sa, sb)
    ak_ref[...] = _bf(_kv_rep(ak))
    avt_ref[0] = _bf(_dot_nt(wvt_ref[...], hb))


def _vt_spec():
    return pl.BlockSpec((1, LANES, TOK_TILE), lambda t: (t, 0, 0))


def _lat_ctx_specs(block, n_lat_tiles, lead=()):
    z = (0,) * len(lead)
    lat = pl.BlockSpec(lead + block, lambda t: z + (jnp.minimum(t, n_lat_tiles - 1), 0))
    ctx = pl.BlockSpec(lead + block, lambda t: z + (jnp.maximum(t - n_lat_tiles, 0), 0))
    return [lat, ctx]


def _proj_even(x_lat, x_ctx, mod, gain, w, wvt, gw, gb, qg, kg, tables, bd, geom):
    d = x_lat.shape[1]
    n = x_lat.shape[0] + x_ctx.shape[0]
    n_lat_tiles, tiles_per_batch, _ = geom
    tt = TOK_TILE
    cos, sa, sb = tables
    tok = lambda w_: pl.BlockSpec((tt, w_), lambda t: (t, 0))
    rope = pl.BlockSpec((tt, LANES), lambda t: (_rope_block(t, n_lat_tiles, tiles_per_batch), 0))
    outs = [(256, F32), (256, F32), (512, BF16), (512, F32), (512, F32), (512, BF16), (512, BF16)]
    return pl.pallas_call(
        functools.partial(_proj_even_kernel, geom),
        out_shape=[jax.ShapeDtypeStruct((n, w_), dt) for w_, dt in outs]
        + [jax.ShapeDtypeStruct((n // tt, LANES, tt), BF16)],
        grid=(n // tt,),
        in_specs=_lat_ctx_specs((tt, d), n_lat_tiles)
        + [_full(mod.shape), _full(gain.shape), _full(w.shape), _full(wvt.shape), _full(gw.shape),
           _full(gb.shape), _full(qg.shape), _full(kg.shape), rope, rope, rope, _full(bd.shape)],
        out_specs=[tok(w_) for w_, _ in outs] + [_vt_spec()],
        compiler_params=_cparams(("arbitrary",)),
        name="proj_even",
    )(x_lat, x_ctx, mod, gain, w, wvt, gw, gb, qg, kg, cos, sa, sb, bd)


def _proj_odd_kernel(geom, x_ref, pos_ref, pos_next_ref, rt_ref, fs_hbm, mod_prev_ref, mod_ref, gain_ref, w_ref,
                     wvt_ref, qg_ref, kg_ref, cos_ref, sa_ref, sb_ref, bd_ref, x1_ref, q_ref, k_ref, vt_ref,
                     fbuf, fsem):
    t = pl.program_id(0)
    row = _mod_row(t, *geom)
    d = x_ref.shape[-1]
    gate = mod_prev_ref[pl.ds(row, 1), 5 * d:6 * d]
    f = _gather_tile(t, pl.num_programs(0), pos_ref, pos_next_ref, rt_ref, fs_hbm, fbuf, fsem,
                     inline_prefetch=True)
    x1 = x_ref[...] + gate * f
    x1_ref[...] = x1
    hb = _bf(_modulated(x1, gain_ref[...], mod_ref, row, 0, 1))
    bd = bd_ref[...]
    cos, sa, sb = cos_ref[...], sa_ref[...], sb_ref[...]
    q = _qk_norm_rope(_dot(hb, w_ref[:, 0:d]), qg_ref[...], bd, cos, sa, sb)
    q_ref[...] = _bf(q * Q_SCALE)
    k = _qk_norm_rope(_dot(hb, w_ref[:, d:d + LANES]), kg_ref[...], bd, cos, sa, sb)
    k_ref[...] = _bf(_kv_rep(k))
    vt_ref[0] = _bf(_dot_nt(wvt_ref[...], hb))
    _gather_drain(t, pl.num_programs(0), fs_hbm, fbuf, fsem)


def _proj_odd(x_all, f_sorted, pos, rt, mod_prev, mod, gain, w, wvt, qg, kg, tables, bd, geom):
    n, d = x_all.shape
    n_lat_tiles, tiles_per_batch, _ = geom
    tt = TOK_TILE
    cos, sa, sb = tables
    tok = lambda w_: pl.BlockSpec((tt, w_), lambda t: (t, 0))
    rope = pl.BlockSpec((tt, LANES), lambda t: (_rope_block(t, n_lat_tiles, tiles_per_batch), 0))
    outs = [(d, F32), (d, BF16), (512, BF16)]
    return pl.pallas_call(
        functools.partial(_proj_odd_kernel, geom),
        out_shape=[jax.ShapeDtypeStruct((n, w_), dt) for w_, dt in outs]
        + [jax.ShapeDtypeStruct((n // tt, LANES, tt), BF16)],
        grid=(n // tt,),
        in_specs=[tok(d)] + _gather_specs(n // tt)
        + [_full(mod_prev.shape), _full(mod.shape), _full(gain.shape), _full(w.shape),
           _full(wvt.shape), _full(qg.shape), _full(kg.shape), rope, rope, rope, _full(bd.shape)],
        out_specs=[tok(w_) for w_, _ in outs] + [_vt_spec()],
        scratch_shapes=_gather_scratch(d),
        compiler_params=_cparams(("arbitrary",)),
        name="proj_odd",
    )(x_all, pos, pos, rt, f_sorted, mod_prev, mod, gain, w, wvt, qg, kg, cos, sa, sb, bd)


N_LEVELS = 6
GLA_MX_ROWS = (N_LEVELS + 2) * GLA_CHUNK


def _gla_constants():
    c = GLA_CHUNK
    mx = np.zeros((2, GLA_MX_ROWS, c), np.float32)
    pat = np.zeros((2, N_LEVELS + 1, c, GLA_HEADS * c), np.float32)
    r = np.arange(c)
    for lvl in range(N_LEVELS):
        h = 1 << lvl
        ref = (r // (2 * h)) * 2 * h + h - 1
        upper = (r % (2 * h)) >= h
        m = np.zeros((c, c), np.float32)
        for i in range(c):
            if upper[i]:
                m[i, ref[i] + 1:i + 1] = 1.0
            else:
                m[i, i + 1:ref[i] + 1] = 1.0
        mx[0, lvl * c:(lvl + 1) * c] = m
        same = (r[:, None] // (2 * h)) == (r[None, :] // (2 * h))
        p = same & upper[:, None] & (~upper)[None, :]
        pat[0, lvl] = np.tile(p.astype(np.float32), (1, GLA_HEADS))
    mx[0, N_LEVELS * c:(N_LEVELS + 1) * c] = (r[None, :] <= r[:, None])
    mx[0, (N_LEVELS + 1) * c:(N_LEVELS + 2) * c] = (r[None, :] > r[:, None])
    pat[0, N_LEVELS] = np.tile(np.eye(c, dtype=np.float32), (1, GLA_HEADS))
    for k in range(N_LEVELS + 2):
        mx[1, k * c:(k + 1) * c] = mx[0, k * c:(k + 1) * c][::-1, ::-1]
    for k in range(N_LEVELS + 1):
        pat[1, k] = np.tile(pat[0, k, :, 0:c][::-1, ::-1], (1, GLA_HEADS))
    return mx, pat


GLA_BLOCK = 1024
GLA_GROUP = 4


def _gla_chunks(chunks, mx, pat_ref, s_ref):
    c = GLA_CHUNK
    lane_head = lax.broadcasted_iota(jnp.int32, (c, GLA_HEADS * GLA_DK), 1) // GLA_DK

    def stack_heads(a):
        return jnp.concatenate([_bf(jnp.where(lane_head == h, a, 0.0)) for h in range(GLA_HEADS)], axis=0)

    xs = [_dot(mx, jnp.concatenate(_split2(g), axis=0)) for _, _, _, g in chunks]
    atts = [jnp.where(pat_ref[0, N_LEVELS] > 0.0, _dot_nt(_bf(q), stack_heads(k)), 0.0)
            for q, k, _, _ in chunks]
    for lvl in range(N_LEVELS):
        for j, (q, k, _, _) in enumerate(chunks):
            e = jnp.exp(xs[j][lvl * c:(lvl + 1) * c])
            atts[j] = atts[j] + jnp.where(pat_ref[0, lvl] > 0.0, _dot_nt(_bf(q * e), stack_heads(k * e)), 0.0)
    outs, qes, news, a_cols = [], [], [], []
    for j, (q, k, v, _) in enumerate(chunks):
        vhead = lax.broadcasted_iota(jnp.int32, v.shape, 1) // GLA_DV
        v_bd = jnp.concatenate([jnp.where(vhead == h, v, jnp.zeros_like(v)) for h in range(GLA_HEADS)], axis=0)
        outs.append(_dot(_bf(atts[j]), v_bd))
        bcum = xs[j][N_LEVELS * c:(N_LEVELS + 1) * c]
        brem = xs[j][(N_LEVELS + 1) * c:(N_LEVELS + 2) * c]
        qes.append(stack_heads(q * jnp.exp(bcum)))
        kt = jnp.transpose(k * jnp.exp(brem))
        news.append(jnp.concatenate(
            [_dot(_bf(kt[h * GLA_DK:(h + 1) * GLA_DK]), v[:, h * GLA_DV:(h + 1) * GLA_DV])
             for h in range(GLA_HEADS)], axis=0))
        tot = bcum[0:1] + brem[0:1]
        a_cols.append(jnp.transpose(jnp.exp(jnp.broadcast_to(tot, (8, tot.shape[1]))))[:, 0:1])
    s = s_ref[...]
    for j in range(len(chunks)):
        o_inter = _dot(qes[j], _bf(s))
        outs[j] = outs[j] + jnp.concatenate([o_inter[h * c:(h + 1) * c] for h in range(GLA_HEADS)], axis=-1)
        s = a_cols[j] * s + news[j]
    s_ref[...] = s
    return outs


def _gla_kernel(n_lat_chunks, n_ctx_chunks, ql_ref, kl_ref, vl_ref, gl_ref, qc_ref, kc_ref, vc_ref, gc_ref,
                mx_ref, pat_ref, ol_ref, oc_ref, s_ref):
    d = pl.program_id(1)
    c = GLA_CHUNK
    mx = mx_ref[0]

    def run(n_chunks, q_ref, k_ref, v_ref, g_ref, o_ref):
        group = min(GLA_GROUP, n_chunks)

        def body(i, carry):
            rows = []
            for j in range(group):
                step = i * group + j
                ci = jnp.where(d == 0, step, n_chunks - 1 - step)
                rows.append(pl.ds(pl.multiple_of(ci * c, c), c))
            outs = _gla_chunks([(q_ref[r, :], k_ref[r, :], v_ref[r, :], g_ref[r, :]) for r in rows],
                               mx, pat_ref, s_ref)
            for r, o in zip(rows, outs):
                o_ref[0, r, :] = o
            return carry
        lax.fori_loop(0, n_chunks // group, body, 0)

    @pl.when(pl.program_id(2) == 0)
    def _():
        s_ref[...] = jnp.zeros_like(s_ref)
        run(n_ctx_chunks, qc_ref, kc_ref, vc_ref, gc_ref, oc_ref)

    run(n_lat_chunks, ql_ref, kl_ref, vl_ref, gl_ref, ol_ref)


def _gla(gq, gk, gv, g, n_batch, seq, lc):
    mx_np, pat_np = _gla_constants()
    mx = jnp.asarray(np.concatenate([mx_np, mx_np], axis=2), BF16)
    pat = jnp.asarray(pat_np, F32)
    ctx0 = n_batch * seq // lc
    hk, hv = GLA_HEADS * GLA_DK, GLA_HEADS * GLA_DV
    blk = min(GLA_BLOCK, seq)
    nb = seq // blk
    for rows_ in (blk, lc):
        assert rows_ % (GLA_CHUNK * min(GLA_GROUP, rows_ // GLA_CHUNK)) == 0
    assert seq % blk == 0
    row = lambda b, d_, i: b * nb + jnp.where(d_ == 0, i, nb - 1 - i)
    lat = lambda w_, col: pl.BlockSpec((blk, w_), lambda b, d_, i: (row(b, d_, i), col(d_)))
    ctx = lambda w_, col: pl.BlockSpec((lc, w_), lambda b, d_, i: (ctx0 + b, col(d_)))
    zero = lambda d_: 0
    same = lambda d_: d_
    return pl.pallas_call(
        functools.partial(_gla_kernel, blk // GLA_CHUNK, lc // GLA_CHUNK),
        out_shape=[jax.ShapeDtypeStruct((2, n_batch * seq, hv), F32),
                   jax.ShapeDtypeStruct((2, n_batch * lc, hv), F32)],
        grid=(n_batch, 2, nb),
        in_specs=[lat(hk, zero), lat(hk, zero), lat(hv, zero), lat(hk, same),
                  ctx(hk, zero), ctx(hk, zero), ctx(hv, zero), ctx(hk, same),
                  pl.BlockSpec((1,) + mx.shape[1:], lambda b, d_, i: (d_, 0, 0)),
                  pl.BlockSpec((1,) + pat.shape[1:], lambda b, d_, i: (d_, 0, 0, 0))],
        out_specs=[pl.BlockSpec((1, blk, hv), lambda b, d_, i: (d_, row(b, d_, i), 0)),
                   pl.BlockSpec((1, lc, hv), lambda b, d_, i: (d_, b, 0))],
        scratch_shapes=[pltpu.VMEM((GLA_HEADS * GLA_DK, GLA_DV), F32)],
        compiler_params=_cparams(("arbitrary", "arbitrary", "arbitrary")),
        name="gla_scan",
    )(gq, gk, gv, g, gq, gk, gv, g, mx, pat)


def _stack_heads(q):
    lane_head = lax.broadcasted_iota(jnp.int32, q.shape, 1) // HEAD_DIM
    return jnp.concatenate([jnp.where(lane_head == h, q, jnp.zeros_like(q)) for h in range(4)], axis=0)


SAFE_SCORE_BOUND = 60.0
ONES_ROWS = 16


def _with_ones(vt):
    return jnp.concatenate([vt, jnp.ones((ONES_ROWS, vt.shape[1]), vt.dtype)], axis=0)


def _attn_store(acc, l, o_ref, u, tq):
    out = acc * (1.0 / l)
    out = jnp.concatenate([out[:, h * tq:(h + 1) * tq] for h in range(4)], axis=0)
    o_ref[u * tq:(u + 1) * tq, :] = _bf(jnp.transpose(out))


def _attn_dense_kernel(n_chunks, n_sub, *refs):
    if n_chunks:
        bound_ref, q_ref, kl_ref, vtl_ref, kc_ref, vtc_ref, o_ref = refs
    else:
        bound_ref, q_ref, kc_ref, vtc_ref, o_ref = refs
    tq = q_ref.shape[0] // n_sub
    cols = 4 * tq
    q4 = [_stack_heads(q_ref[u * tq:(u + 1) * tq, :]) for u in range(n_sub)]

    def scores(c, u):
        if c < n_chunks:
            return _dot_nt(kl_ref[c * ATT_K_CHUNK:(c + 1) * ATT_K_CHUNK, :], q4[u])
        return _dot_nt(kc_ref[...], q4[u])

    def update(carry, st, vt_aug):
        m, acc = carry
        m_new = jnp.maximum(m, jnp.max(st, axis=0, keepdims=True))
        acc = jnp.exp2(m - m_new) * acc + _dot(vt_aug, _bf(jnp.exp2(st - m_new)))
        return m_new, acc

    def run(fixed_ref):
        if fixed_ref is None:
            carry = [(jnp.full((1, cols), NEG_BIG, F32), jnp.zeros((HEAD_DIM + ONES_ROWS, cols), F32))
                     for _ in range(n_sub)]
        else:
            carry = [jnp.zeros((HEAD_DIM + ONES_ROWS, cols), F32) for _ in range(n_sub)]
        st = [scores(0, u) for u in range(n_sub)]
        for c in range(n_chunks + 1):
            st_next = [scores(c + 1, u) for u in range(n_sub)] if c < n_chunks else None
            vt_aug = _with_ones(vtl_ref[c] if c < n_chunks else vtc_ref[0])
            if fixed_ref is None:
                carry = [update(carry[u], st[u], vt_aug) for u in range(n_sub)]
            else:
                carry = [carry[u] + _dot(vt_aug, _bf(jnp.exp2(st[u] - fixed_ref))) for u in range(n_sub)]
            st = st_next
        for u in range(n_sub):
            acc = carry[u][1] if fixed_ref is None else carry[u]
            _attn_store(acc[0:HEAD_DIM], acc[HEAD_DIM:HEAD_DIM + 1], o_ref, u, tq)

    bound = bound_ref[0]

    @pl.when(bound <= SAFE_SCORE_BOUND)
    def _():
        run(bound)

    @pl.when(bound > SAFE_SCORE_BOUND)
    def _():
        run(None)


def _attn_window_kernel(seq, bound_ref, q_ref, *refs):
    nk = WIN_SUB + 2
    k_refs, v_refs = refs[0:nk], refs[nk:2 * nk]
    kc_ref, vtc_ref, sink_ref, o_ref = refs[2 * nk:]
    tq = SWA_WINDOW
    i = pl.program_id(2)
    kb = jnp.concatenate([r[...] for r in k_refs], axis=0)
    vtb = _with_ones(jnp.concatenate([r[0] for r in v_refs], axis=1))
    kc, vtc, sink = kc_ref[...], _with_ones(vtc_ref[0]), sink_ref[0]
    span = 3 * tq
    sb, sc = [], []
    for u in range(WIN_SUB):
        q4 = _stack_heads(q_ref[u * tq:(u + 1) * tq, :])
        first = i * WIN_SUB + u - 1
        kpos = first * tq + lax.broadcasted_iota(jnp.int32, (span, tq), 0)
        qpos = (first + 1) * tq + lax.broadcasted_iota(jnp.int32, (span, tq), 1)
        ok = (kpos >= 0) & (kpos < seq) & (jnp.abs(kpos - qpos) <= SWA_WINDOW)
        bias = jnp.where(ok, 0.0, NEG_BIG)
        sb.append(_dot_nt(kb[u * tq:u * tq + span], q4) + jnp.concatenate([bias] * 4, axis=1))
        sc.append(_dot_nt(kc, q4))
    def finish(use_bound):
        ms, pbs, pcs = [], [], []
        for u in range(WIN_SUB):
            if use_bound:
                m = jnp.maximum(bound_ref[0], sink)
            else:
                m = jnp.maximum(jnp.maximum(jnp.max(sb[u], axis=0, keepdims=True),
                                            jnp.max(sc[u], axis=0, keepdims=True)), sink)
            ms.append(m)
            pbs.append(_bf(jnp.exp2(sb[u] - m)))
            pcs.append(_bf(jnp.exp2(sc[u] - m)))
        for u in range(WIN_SUB):
            acc = _dot(vtb[:, u * tq:u * tq + span], pbs[u]) + _dot(vtc, pcs[u])
            l = acc[HEAD_DIM:HEAD_DIM + 1] + jnp.exp2(sink - ms[u])
            _attn_store(acc[0:HEAD_DIM], l, o_ref, u, tq)

    @pl.when(bound_ref[0] <= SAFE_SCORE_BOUND)
    def _():
        finish(True)

    @pl.when(bound_ref[0] > SAFE_SCORE_BOUND)
    def _():
        finish(False)


def _score_bound(q_gain, k_gain):
    return (HEAD_DIM * Q_SCALE * 1.02 * jnp.max(jnp.abs(q_gain)) * jnp.max(jnp.abs(k_gain))).reshape(1)


def _attention(mode, q, k_rep, vt, bound, n_batch, seq, lc, n_heads, n_kv, sink=None):
    ncol = n_heads * HEAD_DIM // ATT_COL
    col_per_kv = ncol // n_kv
    tq = ATT_Q_TILE
    tpb = seq // TOK_TILE
    n_lat_tiles = n_batch * tpb
    ctx_per_tile = TOK_TILE // lc
    ctx0 = n_batch * seq // lc
    assert ATT_K_CHUNK == TOK_TILE and TOK_TILE % lc == 0 and tq == SWA_WINDOW
    kv = lambda j: j // col_per_kv
    k_ctx = pl.BlockSpec((lc, ATT_COL), lambda b, j, i: (ctx0 + b, kv(j)))
    vt_ctx = pl.BlockSpec((1, HEAD_DIM, lc), lambda b, j, i: (n_lat_tiles + b // ctx_per_tile, kv(j), b % ctx_per_tile))
    if mode == "ctx":
        nq = lc // tq
        q0 = n_batch * seq // tq
        kern = functools.partial(_attn_dense_kernel, 0, 1)
        args = (q, k_rep, vt)
        in_specs = [pl.BlockSpec((tq, ATT_COL), lambda b, j, i: (q0 + b * nq + i, j)), k_ctx, vt_ctx]
    elif mode == "dense":
        tq = DENSE_SUB * ATT_Q_TILE
        nq = seq // tq
        kern = functools.partial(_attn_dense_kernel, tpb, DENSE_SUB)
        args = (q, k_rep, vt, k_rep, vt)
        in_specs = [pl.BlockSpec((tq, ATT_COL), lambda b, j, i: (b * nq + i, j)),
                    pl.BlockSpec((seq, ATT_COL), lambda b, j, i: (b, kv(j))),
                    pl.BlockSpec((tpb, HEAD_DIM, TOK_TILE), lambda b, j, i: (b, kv(j), 0)), k_ctx, vt_ctx]
    else:
        wb = SWA_WINDOW
        tq = WIN_SUB * wb
        nq = seq // tq
        nkb = seq // wb
        per_tile = TOK_TILE // wb
        kern = functools.partial(_attn_window_kernel, seq)
        nb = lambda i, o: jnp.clip(i * WIN_SUB + o, 0, nkb - 1)
        k_nb = lambda o: pl.BlockSpec((wb, ATT_COL), lambda b, j, i: (b * nkb + nb(i, o), kv(j)))
        v_nb = lambda o: pl.BlockSpec(
            (1, HEAD_DIM, wb), lambda b, j, i: (b * tpb + nb(i, o) // per_tile, kv(j), nb(i, o) % per_tile))
        offs = range(-1, WIN_SUB + 1)
        sink_row = jnp.repeat(sink.reshape(ncol, 1, 4), wb, axis=2) * LOG2E
        args = (q,) + (k_rep,) * len(offs) + (vt,) * len(offs) + (k_rep, vt, sink_row)
        in_specs = ([pl.BlockSpec((tq, ATT_COL), lambda b, j, i: (b * nq + i, j))]
                    + [k_nb(o) for o in offs] + [v_nb(o) for o in offs]
                    + [k_ctx, vt_ctx, pl.BlockSpec((1, 1, 4 * wb), lambda b, j, i: (j, 0, 0))])
    args = (bound,) + args
    in_specs = [pl.BlockSpec(memory_space=pltpu.SMEM)] + in_specs
    return pl.pallas_call(
        kern,
        out_shape=jax.ShapeDtypeStruct((n_batch * nq * tq, n_heads * HEAD_DIM), BF16),
        grid=(n_batch, ncol, nq),
        in_specs=in_specs,
        out_specs=pl.BlockSpec((tq, ATT_COL), lambda b, j, i: (b * nq + i, j)),
        compiler_params=_cparams(("arbitrary", "arbitrary", "arbitrary")),
        name="attention_" + mode,
    )(*args)


ROUTE_ROWS = 8
TOKEN_TILE = (8, LANES)
OUT_TOKEN_TILE = (16, LANES)


def _to_token_tiles(x, tile):
    return x.reshape((x.shape[0],) + tile)


def _from_token_tiles(tiles):
    return tiles.reshape(tiles.shape[0], tiles.shape[1] * tiles.shape[2])


def _route(h, wrt_ref, wrt_hi_ref, brt_ref):
    hh, hl = _split2(h)
    a = _dot_nt(wrt_ref[...], hh)
    lt = a[0:LANES] + a[LANES:2 * LANES] + _dot_nt(wrt_hi_ref[...], hl) + brt_ref[...]
    col = lambda i: lt[i:i + 1, :]
    gl = [col(i) for i in range(MOE_GROUPS)]
    gmax = functools.reduce(jnp.maximum, gl)
    gi = jnp.where(gl[0] == gmax, 0, jnp.where(gl[1] == gmax, 1, jnp.where(gl[2] == gmax, 2, 3)))
    g_weight = 1.0 / functools.reduce(lambda a, b: a + b, [jnp.exp(x - gmax) for x in gl])
    el = []
    for j in range(MOE_EPG):
        cand = [col(MOE_GROUPS + g * MOE_EPG + j) for g in range(MOE_GROUPS)]
        el.append(jnp.where(gi == 0, cand[0], jnp.where(gi == 1, cand[1], jnp.where(gi == 2, cand[2], cand[3]))))
    m1 = functools.reduce(jnp.maximum, el)
    i1 = jnp.where(el[0] == m1, 0, jnp.where(el[1] == m1, 1, jnp.where(el[2] == m1, 2, 3)))
    rest = [jnp.where(i1 == j, -jnp.inf, el[j]) for j in range(MOE_EPG)]
    m2 = functools.reduce(jnp.maximum, rest)
    i2 = jnp.where(rest[0] == m2, 0, jnp.where(rest[1] == m2, 1, jnp.where(rest[2] == m2, 2, 3)))
    e2 = jnp.exp(m2 - m1)
    w1 = g_weight / (1.0 + e2)
    w2 = g_weight * e2 / (1.0 + e2)
    lo = jnp.minimum(i1, i2)
    hi = jnp.maximum(i1, i2)
    w_lo = jnp.where(i1 == lo, w1, w2)
    w_hi = jnp.where(i1 == lo, w2, w1)
    pair = jnp.where(lo == 0, hi - 1, jnp.where(lo == 1, hi + 1, N_PAIRS - 1))
    return w_lo, w_hi, gi * N_PAIRS + pair


def _out_tail(geom, m, x, mod_ref, gain_ffn_ref, wrt_ref, wrt_hi_ref, brt_ref, triu_ref,
              x_new_ref, hrow_ref, rt_ref, counts_ref, run_ref):
    t = pl.program_id(0)
    row = _mod_row(t, *geom)
    tt, d = x.shape

    @pl.when(t == 0)
    def _():
        run_ref[...] = jnp.zeros_like(run_ref)

    x_new = x + mod_ref[pl.ds(row, 1), 2 * d:3 * d] * m
    x_new_ref[...] = x_new
    h = _modulated(x_new, gain_ffn_ref[...], mod_ref, row, 3, 4)
    w_lo, w_hi, bucket = _route(h, wrt_ref, wrt_hi_ref, brt_ref)
    onehot = lax.broadcasted_iota(jnp.int32, (LANES, tt), 0) == bucket
    ones = jnp.where(onehot, 1.0, 0.0)
    before = _dot(_bf(ones), triu_ref[...]) + run_ref[...]
    rank = jnp.sum(jnp.where(onehot, before, 0.0), axis=0, keepdims=True)
    run = run_ref[...] + jnp.sum(ones, axis=1, keepdims=True)
    run_ref[...] = run
    counts_ref[...] = jnp.broadcast_to(run, counts_ref.shape)
    rec = jnp.concatenate([w_lo, w_hi, bucket.astype(F32), rank, jnp.zeros((ROUTE_ROWS - 4, tt), F32)], axis=0)
    rt_ref[0] = rec
    hrow_ref[...] = _to_token_tiles(h, TOKEN_TILE)


def _pick(t, n_lat_tiles, lat_ref, ctx_ref):
    return jnp.where(t < n_lat_tiles, lat_ref[...], ctx_ref[...])


def _out_even_kernel(geom, ol_ref, oc_ref, r_ref, attl_ref, attc_ref, gn_ref, w_ref, xl_ref, xc_ref, mod_ref,
                     gain_ffn_ref, wrt_ref, wrt_hi_ref, brt_ref, triu_ref,
                     x_new_ref, hrow_ref, rt_ref, counts_ref, run_ref):
    t = pl.program_id(0)
    o2 = _pick(t, geom[0], ol_ref, oc_ref)
    o = o2[0] + o2[1]
    r = r_ref[...]
    parts = []
    for h in range(GLA_HEADS):
        sl = slice(h * GLA_DV, (h + 1) * GLA_DV)
        parts.append(_rms(o[:, sl]) * gn_ref[...] * _silu(r[:, sl]))
    a = _bf(jnp.concatenate(parts, axis=-1))
    half = a.shape[-1]
    m = _dot(a, w_ref[0:half, :]) + _dot(_pick(t, geom[0], attl_ref, attc_ref), w_ref[half:, :])
    _out_tail(geom, m, _pick(t, geom[0], xl_ref, xc_ref), mod_ref, gain_ffn_ref, wrt_ref, wrt_hi_ref, brt_ref,
              triu_ref, x_new_ref, hrow_ref, rt_ref, counts_ref, run_ref)


def _out_odd_kernel(geom, att_ref, w_ref, x_ref, mod_ref, gain_ffn_ref, wrt_ref, wrt_hi_ref, brt_ref, triu_ref,
                    x_new_ref, hrow_ref, rt_ref, counts_ref, run_ref):
    m = _dot(att_ref[...], w_ref[...])
    _out_tail(geom, m, x_ref[...], mod_ref, gain_ffn_ref, wrt_ref, wrt_hi_ref, brt_ref, triu_ref,
              x_new_ref, hrow_ref, rt_ref, counts_ref, run_ref)


def _out_proj(kernel, lead_args, lead_specs, d, mod, gain_ffn, router, n_rows, name):
    tt = TOK_TILE
    tok = lambda w_: pl.BlockSpec((tt, w_), lambda t: (t, 0))
    r = np.arange(tt)
    triu = jnp.asarray((r[:, None] < r[None, :]).astype(np.float32), BF16)
    wrt, wrt_hi, br = router
    brt = jnp.broadcast_to(br.reshape(LANES, 1), (LANES, tt))
    return pl.pallas_call(
        kernel,
        out_shape=[jax.ShapeDtypeStruct((n_rows, d), F32),
                   jax.ShapeDtypeStruct((n_rows,) + TOKEN_TILE, F32),
                   jax.ShapeDtypeStruct((n_rows // tt, ROUTE_ROWS, tt), F32),
                   jax.ShapeDtypeStruct((LANES, LANES), F32)],
        grid=(n_rows // tt,),
        in_specs=lead_specs + [_full(mod.shape), _full(gain_ffn.shape), _full(wrt.shape), _full(wrt_hi.shape),
                               _full(brt.shape), _full(triu.shape)],
        out_specs=[tok(d), pl.BlockSpec((tt,) + TOKEN_TILE, lambda t: (t, 0, 0)),
                   pl.BlockSpec((1, ROUTE_ROWS, tt), lambda t: (t, 0, 0)), _full((LANES, LANES))],
        scratch_shapes=[pltpu.VMEM((LANES, 1), F32)],
        compiler_params=_cparams(("arbitrary",)),
        name=name,
    )(*lead_args, mod, gain_ffn, wrt, wrt_hi, brt, triu)


SUBLANES = 8


def _for_each_row(n_rows, start_row_copy):
    def body(g, c):
        base = g * SUBLANES
        for j in range(SUBLANES):
            start_row_copy(base + j, j % 2)
        return c
    lax.fori_loop(0, n_rows // SUBLANES, body, 0)


def _pos_kernel(start_ref, rt_ref, pos_ref):
    bucket = rt_ref[:, 2, :].astype(jnp.int32)
    base = jnp.zeros_like(bucket)
    for b in range(N_BUCKETS):
        base = jnp.where(bucket == b, start_ref[b], base)
    pos_ref[:, 0, :] = base + rt_ref[:, 3, :].astype(jnp.int32)


def _dispatch_kernel(last_ref, pos_ref, x_ref, xs_ref, zbuf, sem, zsem):
    tt = x_ref.shape[0]
    tm = zbuf.shape[0]

    @pl.when(pl.program_id(0) == 0)
    def _():
        zbuf[...] = jnp.zeros_like(zbuf)

        def zero_copy(b):
            return pltpu.make_async_copy(zbuf, xs_ref.at[pl.ds(jnp.maximum(last_ref[b], 0) * tm, tm)], zsem)

        for b in range(2 * N_BUCKETS):
            @pl.when(last_ref[b] >= 0)
            def _():
                zero_copy(b).start()
        for b in range(2 * N_BUCKETS):
            @pl.when(last_ref[b] >= 0)
            def _():
                zero_copy(b).wait()

    _for_each_row(tt, lambda r, prio: pltpu.make_async_copy(
        x_ref.at[r], xs_ref.at[pos_ref[0, 0, r]], sem).start(priority=prio))
    pltpu.make_async_copy(x_ref, xs_ref.at[pl.ds(0, tt)], sem).wait()


def _moe_mlp_kernel(tlo_ref, thi_ref, nused_ref, xs_ref, wg_lo, wu_lo, wd_lo, wg_hi, wu_hi, wd_hi, f_ref):
    del tlo_ref, thi_ref
    used = pl.program_id(0) < nused_ref[0]

    @pl.when(used)
    def _():
        xb = _bf(_from_token_tiles(xs_ref[...]))
        hid_lo = _silu(_dot(xb, wg_lo[0])) * _dot(xb, wu_lo[0])
        hid_hi = _silu(_dot(xb, wg_hi[0])) * _dot(xb, wu_hi[0])
        f2 = jnp.concatenate([_dot(_bf(hid_lo), wd_lo[0]), _dot(_bf(hid_hi), wd_hi[0])], axis=1)
        f_ref[...] = _to_token_tiles(f2, OUT_TOKEN_TILE)

    @pl.when(jnp.logical_not(used))
    def _():
        f_ref[...] = jnp.zeros_like(f_ref)


def _moe(hrow, rt, counts, experts, layer):
    n = hrow.shape[0]
    d = TOKEN_TILE[0] * TOKEN_TILE[1]
    tm, tt = MOE_TILE, TOK_TILE
    n_tiles = n // tm + N_BUCKETS
    p = n_tiles * tm
    cnt = counts[0:N_BUCKETS, 0].astype(jnp.int32)
    tiles_b = (cnt + tm - 1) // tm
    tile_end = jnp.cumsum(tiles_b)
    start_b = (tile_end - tiles_b) * tm
    n_used = tile_end[-1].reshape(1)
    spare = n_used[0] + jnp.arange(N_BUCKETS, dtype=jnp.int32)
    last_tile = jnp.concatenate([jnp.where(tiles_b > 0, tile_end - 1, -1), jnp.where(spare < n_tiles, spare, -1)])
    pos = pl.pallas_call(
        _pos_kernel,
        out_shape=jax.ShapeDtypeStruct((n // tt, 1, tt), jnp.int32),
        grid_spec=pltpu.PrefetchScalarGridSpec(
            num_scalar_prefetch=1, grid=(1,),
            in_specs=[pl.BlockSpec(rt.shape, lambda i, s: (0, 0, 0))],
            out_specs=pl.BlockSpec((n // tt, 1, tt), lambda i, s: (0, 0, 0))),
        compiler_params=_cparams(("arbitrary",)),
        name="moe_positions",
    )(start_b, rt)
    tile_ids = jnp.arange(n_tiles, dtype=jnp.int32)
    tile_bucket = jnp.sum((tile_ids[:, None] >= tile_end[None, :]).astype(jnp.int32), axis=1)
    tile_bucket = jnp.minimum(tile_bucket, jnp.sum((n_used[0] - 1 >= tile_end).astype(jnp.int32)))
    tile_bucket = jnp.minimum(tile_bucket, N_BUCKETS - 1)
    pair_lo = jnp.asarray([0, 0, 0, 1, 1, 2], jnp.int32)
    pair_hi = jnp.asarray([1, 2, 3, 2, 3, 3], jnp.int32)
    grp = tile_bucket // N_PAIRS
    w_gate, w_up, w_down = experts
    first = layer * MOE_GROUPS * MOE_EPG
    t_lo = first + grp * MOE_EPG + pair_lo[tile_bucket % N_PAIRS]
    t_hi = first + grp * MOE_EPG + pair_hi[tile_bucket % N_PAIRS]

    dt = DISPATCH_TILE if n % DISPATCH_TILE == 0 else tt
    xs = pl.pallas_call(
        _dispatch_kernel,
        out_shape=jax.ShapeDtypeStruct((p,) + TOKEN_TILE, F32),
        grid_spec=pltpu.PrefetchScalarGridSpec(
            num_scalar_prefetch=1, grid=(n // dt,),
            in_specs=[pl.BlockSpec((1, 1, dt), lambda t, s: (t, 0, 0), memory_space=pltpu.SMEM),
                      pl.BlockSpec((dt,) + TOKEN_TILE, lambda t, s: (t, 0, 0))],
            out_specs=pl.BlockSpec(memory_space=pl.ANY),
            scratch_shapes=[pltpu.VMEM((tm,) + TOKEN_TILE, F32), pltpu.SemaphoreType.DMA(()),
                            pltpu.SemaphoreType.DMA(())]),
        compiler_params=_cparams(("arbitrary",)),
        name="moe_dispatch",
    )(last_tile, pos.reshape(n // dt, 1, dt), hrow)

    f = D_EXPERT
    up_lo = pl.BlockSpec((1, d, f), lambda t, lo, hi, nu: (lo[t], 0, 0))
    up_hi = pl.BlockSpec((1, d, f), lambda t, lo, hi, nu: (hi[t], 0, 0))
    dn_lo = pl.BlockSpec((1, f, d), lambda t, lo, hi, nu: (lo[t], 0, 0))
    dn_hi = pl.BlockSpec((1, f, d), lambda t, lo, hi, nu: (hi[t], 0, 0))
    grid_spec = pltpu.PrefetchScalarGridSpec(
        num_scalar_prefetch=3,
        grid=(n_tiles,),
        in_specs=[pl.BlockSpec((tm,) + TOKEN_TILE, lambda t, lo, hi, nu: (jnp.minimum(t, nu[0] - 1), 0, 0)),
                  up_lo, up_lo, dn_lo, up_hi, up_hi, dn_hi],
        out_specs=pl.BlockSpec((tm,) + OUT_TOKEN_TILE, lambda t, *_: (t, 0, 0)),
    )
    f_sorted = pl.pallas_call(
        _moe_mlp_kernel,
        out_shape=jax.ShapeDtypeStruct((p,) + OUT_TOKEN_TILE, F32),
        grid_spec=grid_spec,
        compiler_params=_cparams(("arbitrary",)),
        name="moe_experts",
    )(t_lo, t_hi, n_used, xs, w_gate, w_up, w_down, w_gate, w_up, w_down)
    return f_sorted, pos


def _gather_tile(t, n_t, pos_ref, pos_next_ref, rt_ref, src_hbm, buf, sem, inline_prefetch=False):
    tt = buf.shape[1]

    def start(p_ref, slot):
        _for_each_row(tt, lambda r, prio: pltpu.make_async_copy(
            src_hbm.at[p_ref[0, 0, r]], buf.at[slot].at[r], sem.at[slot]).start(priority=prio))

    slot = t % 2

    def wait(s):
        pltpu.make_async_copy(src_hbm.at[pl.ds(0, tt)], buf.at[s], sem.at[s]).wait()

    @pl.when(t == 0)
    def _():
        start(pos_ref, 0)

    wait(slot)
    if inline_prefetch:
        for r in range(tt):
            pltpu.make_async_copy(src_hbm.at[pos_next_ref[0, 0, r]], buf.at[1 - slot].at[r],
                                  sem.at[1 - slot]).start(priority=r % 2)
    else:
        @pl.when(t + 1 < n_t)
        def _():
            start(pos_next_ref, 1 - slot)

    f2 = _from_token_tiles(buf[slot])
    d = f2.shape[1] // 2
    w = jnp.transpose(rt_ref[0])
    return w[:, 0:1] * f2[:, 0:d] + w[:, 1:2] * f2[:, d:2 * d]


def _gather_drain(t, n_t, src_hbm, buf, sem):
    tt = buf.shape[1]

    @pl.when(t == n_t - 1)
    def _():
        pltpu.make_async_copy(src_hbm.at[pl.ds(0, tt)], buf.at[1 - t % 2], sem.at[1 - t % 2]).wait()


def _gather_specs(n_t):
    tt = TOK_TILE
    return [pl.BlockSpec((1, 1, tt), lambda t: (t, 0, 0), memory_space=pltpu.SMEM),
            pl.BlockSpec((1, 1, tt), lambda t: (jnp.minimum(t + 1, n_t - 1), 0, 0), memory_space=pltpu.SMEM),
            pl.BlockSpec((1, ROUTE_ROWS, tt), lambda t: (t, 0, 0)),
            pl.BlockSpec(memory_space=pl.ANY)]


def _gather_scratch(d):
    assert 2 * d == OUT_TOKEN_TILE[0] * OUT_TOKEN_TILE[1]
    return [pltpu.VMEM((2, TOK_TILE) + OUT_TOKEN_TILE, F32), pltpu.SemaphoreType.DMA((2,))]


def _final_kernel(tiles_per_batch, x_ref, pos_ref, pos_next_ref, rt_ref, fs_hbm, mod_ref, o_ref, fbuf, fsem):
    t = pl.program_id(0)
    row = t // tiles_per_batch
    d = x_ref.shape[-1]
    f = _gather_tile(t, pl.num_programs(0), pos_ref, pos_next_ref, rt_ref, fs_hbm, fbuf, fsem)
    o_ref[...] = x_ref[...] + mod_ref[pl.ds(row, 1), 5 * d:6 * d] * f


def _final(x_lat, f_sorted, pos, rt, mod, tiles_per_batch):
    n, d = x_lat.shape
    tok = pl.BlockSpec((TOK_TILE, d), lambda t: (t, 0))
    return pl.pallas_call(
        functools.partial(_final_kernel, tiles_per_batch),
        out_shape=jax.ShapeDtypeStruct((n, d), F32),
        grid=(n // TOK_TILE,),
        in_specs=[tok] + _gather_specs(n // TOK_TILE) + [_full(mod.shape)],
        out_specs=tok,
        scratch_shapes=_gather_scratch(d),
        compiler_params=_cparams(("arbitrary",)),
        name="final_residual",
    )(x_lat, pos, pos, rt, f_sorted, mod)


def _block_diag_ones():
    r = np.arange(MXU_DIM) // HEAD_DIM
    return jnp.asarray((r[:, None] == r[None, :]).astype(np.float32), BF16)


def _router_weights(wg, bg, we, be):
    d = wg.shape[0]
    n = MOE_GROUPS + MOE_GROUPS * MOE_EPG
    wt = jnp.concatenate([wg, we, jnp.zeros((d, LANES - n), F32)], axis=1).T
    b = jnp.concatenate([bg, be, jnp.zeros((LANES - n,), F32)])
    hi = _bf(wt)
    lo = _bf(wt - hi.astype(F32))
    return jnp.concatenate([hi, lo], axis=0), hi, b


def _expert_weights(w_gate, w_up, w_down):
    l, g, e, d, f = w_gate.shape
    n = l * g * e
    return _bf(w_gate).reshape(n, d, f), _bf(w_up).reshape(n, d, f), _bf(w_down).reshape(n, f, d)


def kernel(x, c, ctx, c_ctx, mod_w, mod_b, norm_mix, norm_ffn, ev_w_in, ev_w_out, gla_gate_w, gla_gate_b,
           gla_out_norm, att_q_norm, att_k_norm, od_w_in, od_w_out, swa_sink, swa_q_norm, swa_k_norm,
           router_group_w, router_group_b, router_expert_w, router_expert_b, exp_w_gate, exp_w_up, exp_w_down):
    n_batch, seq, d = x.shape
    lc = ctx.shape[1]
    depth = mod_w.shape[0]
    n_lat = n_batch * seq
    tiles_per_batch = seq // TOK_TILE
    geom = (n_lat // TOK_TILE, tiles_per_batch, n_batch)
    assert depth == 2 and seq % TOK_TILE == 0 and (n_batch * lc) % TOK_TILE == 0 and n_batch < 16

    x_lat, x_ctx = x.reshape(n_lat, d), ctx.reshape(n_batch * lc, d)
    n_all = n_lat + n_batch * lc
    c_rows = jnp.zeros((16, d), F32).at[:n_batch].set(c).at[n_batch].set(c_ctx)
    mod = _modulation(c_rows, mod_w, mod_b)
    tables = _rope_tables(seq)
    bd = _block_diag_ones()
    row2 = lambda v: v.reshape(1, -1)
    tile_gain = lambda gvec, reps: jnp.tile(gvec, reps).reshape(1, -1)

    w0 = ev_w_in[0]
    seg = np.cumsum([0, 256, 256, 512, 512, 32, 512, 128, 128])
    cols = lambda i: w0[:, seg[i]:seg[i + 1]]
    w_even = _bf(jnp.concatenate([cols(0), cols(1), cols(2), cols(3), cols(5), cols(6), cols(4),
                                  jnp.zeros((d, EV_END - EV_LR - 2 * GLA_GATE_RANK), F32)], axis=1))
    wvt_even = _bf(cols(7).T)
    hk = GLA_HEADS * GLA_DK
    gw = jnp.zeros((LANES, 2 * hk), F32)
    gw = gw.at[0:GLA_GATE_RANK, 0:hk].set(gla_gate_w[0, 0])
    gw = gw.at[GLA_GATE_RANK:2 * GLA_GATE_RANK, hk:2 * hk].set(gla_gate_w[0, 1])
    gb = gla_gate_b[0].reshape(1, 2 * hk)
    gq, gk, gv, gr, g, aq, ak, avt = _proj_even(
        x_lat, x_ctx, mod[0], row2(norm_mix[0]), w_even, wvt_even, _bf(gw), gb,
        tile_gain(att_q_norm[0], ATT_HEADS), tile_gain(att_k_norm[0], ATT_KV_HEADS), tables, bd, geom)
    o_lat, o_ctx = _gla(gq, gk, gv, g, n_batch, seq, lc)
    bound0 = _score_bound(att_q_norm[0], att_k_norm[0])
    att_lat = _attention("dense", aq, ak, avt, bound0, n_batch, seq, lc, ATT_HEADS, ATT_KV_HEADS)
    att_ctx = _attention("ctx", aq, ak, avt, bound0, n_batch, seq, lc, ATT_HEADS, ATT_KV_HEADS)
    router = _router_weights(router_group_w[0], router_group_b[0], router_expert_w[0], router_expert_b[0])
    tt = TOK_TILE
    nlt = geom[0]
    gn = row2(gla_out_norm[0])
    w_out0 = _bf(ev_w_out[0])
    hv, ha = GLA_HEADS * GLA_DV, ATT_HEADS * HEAD_DIM
    x_mid, hrow, rt, counts = _out_proj(
        functools.partial(_out_even_kernel, geom),
        (o_lat, o_ctx, gr, att_lat, att_ctx, gn, w_out0, x_lat, x_ctx),
        _lat_ctx_specs((tt, hv), nlt, lead=(2,)) + [pl.BlockSpec((tt, hv), lambda t: (t, 0))]
        + _lat_ctx_specs((tt, ha), nlt) + [_full(gn.shape), _full(w_out0.shape)] + _lat_ctx_specs((tt, d), nlt),
        d, mod[0], row2(norm_ffn[0]), router, n_all, "out_even")
    experts = _expert_weights(exp_w_gate, exp_w_up, exp_w_down)
    f0, pos0 = _moe(hrow, rt, counts, experts, 0)

    w_odd = od_w_in[0]
    x1, q1, k1, v1t = _proj_odd(
        x_mid, f0, pos0, rt, mod[0], mod[1], row2(norm_mix[1]), _bf(w_odd[:, 0:d + LANES]), _bf(w_odd[:, d + LANES:].T),
        tile_gain(swa_q_norm[0], SWA_HEADS), tile_gain(swa_k_norm[0], SWA_KV_HEADS), tables, bd, geom)
    att1 = _attention("window", q1, k1, v1t, _score_bound(swa_q_norm[0], swa_k_norm[0]), n_batch, seq, lc,
                      SWA_HEADS, SWA_KV_HEADS, sink=swa_sink[0])
    router = _router_weights(router_group_w[1], router_group_b[1], router_expert_w[1], router_expert_b[1])
    w_out1 = _bf(od_w_out[0])
    tok = lambda w_: pl.BlockSpec((tt, w_), lambda t: (t, 0))
    x2, hrow1, rt1, counts1 = _out_proj(
        functools.partial(_out_odd_kernel, geom), (att1, w_out1, x1),
        [tok(SWA_HEADS * HEAD_DIM), _full(w_out1.shape), tok(d)],
        d, mod[1], row2(norm_ffn[1]), router, n_lat, "out_odd")
    f1, pos1 = _moe(hrow1, rt1, counts1, experts, 1)
    out = _final(x2, f1, pos1, rt1, mod[1], tiles_per_batch)
    return out.reshape(n_batch, seq, d)
```

```python
import functools

import numpy as np
import jax
import jax.numpy as jnp
from jax import lax
from jax.experimental import pallas as pl
from jax.experimental.pallas import tpu as pltpu

F32 = jnp.float32
BF16 = jnp.bfloat16

GRID_W = 64
HEAD_DIM = 64
AXIS_DIM = HEAD_DIM // 2
ROPE_THETA = 10000.0
EPS = 1e-6
GLA_HEADS = 4
GLA_DK = 64
GLA_DV = 128
GLA_GATE_RANK = 16
GLA_GATE_NORM = 16.0
GLA_CHUNK = 64
ATT_HEADS = 8
ATT_KV_HEADS = 2
SWA_HEADS = 16
SWA_KV_HEADS = 2
SWA_WINDOW = 128
MOE_GROUPS = 4
MOE_EPG = 4
D_EXPERT = 256
N_PAIRS = 6
N_BUCKETS = MOE_GROUPS * N_PAIRS

LANES = 128
MXU_DIM = 256
TOK_TILE = 512
ATT_Q_TILE = 128
ATT_K_CHUNK = 512
ATT_COL = 4 * HEAD_DIM
WIN_SUB = 4
DENSE_SUB = 4
MOE_TILE = 256
DISPATCH_TILE = 1024
VMEM_LIMIT = 56 * 1024 * 1024
NEG_BIG = -1e30
LOG2E = 1.4426950408889634
Q_SCALE = HEAD_DIM ** -0.5 * LOG2E


def _bf(x):
    return x.astype(BF16)


def _split2(x):
    hi = _bf(x)
    lo = _bf(x - hi.astype(F32))
    return hi, lo


def _dot(a, b):
    return jnp.dot(a, b, preferred_element_type=F32)


def _dot_nt(a, b):
    return lax.dot_general(a, b, (((1,), (1,)), ((), ())), preferred_element_type=F32)


def _silu(x):
    return x / (1.0 + jnp.exp(-x))


def _rms(x):
    return x * lax.rsqrt(jnp.mean(x * x, axis=-1, keepdims=True) + EPS)


def _cparams(sem):
    return pltpu.CompilerParams(dimension_semantics=sem, vmem_limit_bytes=VMEM_LIMIT)


def _full(shape):
    n = len(shape)
    return pl.BlockSpec(shape, lambda *_: (0,) * n)


def _mod_kernel(c_ref, w_ref, b_ref, o_ref):
    c = c_ref[...]
    ch, cl = _split2(_silu(c))
    wh, wl = _split2(w_ref[0])
    o_ref[0] = _dot(ch, wh) + _dot(ch, wl) + _dot(cl, wh) + b_ref[0]


def _modulation(c_rows, mod_w, mod_b):
    depth, d, n = mod_w.shape
    tn = n // 4
    return pl.pallas_call(
        _mod_kernel,
        out_shape=jax.ShapeDtypeStruct((depth, 16, n), F32),
        grid=(depth, n // tn),
        in_specs=[pl.BlockSpec((16, d), lambda i, j: (0, 0)),
                  pl.BlockSpec((1, d, tn), lambda i, j: (i, 0, j)),
                  pl.BlockSpec((1, 1, tn), lambda i, j: (i, 0, j))],
        out_specs=pl.BlockSpec((1, 16, tn), lambda i, j: (i, 0, j)),
        compiler_params=_cparams(("arbitrary", "arbitrary")),
        name="modulation",
    )(c_rows, mod_w, mod_b.reshape(depth, 1, n))


def _mod_row(t, n_lat_tiles, tiles_per_batch, n_batch):
    return jnp.where(t < n_lat_tiles, t // tiles_per_batch, n_batch)


def _modulated(x, gain, mod_ref, row, k_shift, k_scale):
    d = x.shape[-1]
    shift = mod_ref[pl.ds(row, 1), k_shift * d:(k_shift + 1) * d]
    scale = mod_ref[pl.ds(row, 1), k_scale * d:(k_scale + 1) * d]
    return _rms(x) * gain * (1.0 + scale) + shift


def _rope_tables(seq):
    rows = seq // GRID_W
    row = np.repeat(np.arange(rows), GRID_W)
    col = np.tile(np.arange(GRID_W), rows)
    inv_freq = ROPE_THETA ** (-np.arange(0, AXIS_DIM, 2, dtype=np.float64) / AXIS_DIM)
    ang = np.stack([row[:, None] * inv_freq, col[:, None] * inv_freq], axis=1)
    cos, sin = np.cos(ang), np.sin(ang)
    zero = np.zeros_like(sin)
    cos64 = np.concatenate([cos[:, 0], cos[:, 0], cos[:, 1], cos[:, 1]], axis=-1)
    sa64 = np.concatenate([-sin[:, 0], zero[:, 0], -sin[:, 1], zero[:, 1]], axis=-1)
    sb64 = np.concatenate([zero[:, 0], sin[:, 0], zero[:, 1], sin[:, 1]], axis=-1)

    def widen(t, fill):
        t = np.concatenate([t, t], axis=-1)
        return jnp.asarray(np.concatenate([t, np.full((TOK_TILE, LANES), fill)], axis=0), F32)

    return widen(cos64, 1.0), widen(sa64, 0.0), widen(sb64, 0.0)


def _head_sumsq(y, bd):
    w = y.shape[-1]
    outs = []
    for s in range(0, w, MXU_DIM):
        e = min(s + MXU_DIM, w)
        hi, lo = _split2(y[:, s:e])
        b = bd[0:e - s, 0:e - s]
        outs.append(_dot(hi, b) + _dot(lo, b))
    return outs[0] if len(outs) == 1 else jnp.concatenate(outs, axis=-1)


def _qk_norm_rope(z, gain, bd, cos, sa, sb):
    w = z.shape[-1]
    rep = w // LANES
    ss = _head_sumsq(z * z, bd)
    y = z * lax.rsqrt(ss * (1.0 / HEAD_DIM) + EPS) * gain

    def wide(t):
        return t if rep == 1 else jnp.concatenate([t] * rep, axis=-1)

    return (y * wide(cos) + pltpu.roll(y, w - AXIS_DIM // 2, 1) * wide(sa)
            + pltpu.roll(y, AXIS_DIM // 2, 1) * wide(sb))


def _kv_rep(kv128):
    lane = lax.broadcasted_iota(jnp.int32, kv128.shape, 1)
    sw = pltpu.roll(kv128, HEAD_DIM, 1)
    a0 = jnp.where(lane < HEAD_DIM, kv128, sw)
    a1 = jnp.where(lane < HEAD_DIM, sw, kv128)
    return jnp.concatenate([a0, a0, a1, a1], axis=-1)


def _rope_block(t, n_lat_tiles, tiles_per_batch):
    return jnp.where(t < n_lat_tiles, t % tiles_per_batch, tiles_per_batch)


EV_GQ, EV_GK, EV_GV, EV_GR, EV_AQ, EV_AK, EV_LR, EV_END = 0, 256, 512, 1024, 1536, 2048, 2176, 2304


def _proj_even_kernel(geom, xl_ref, xc_ref, mod_ref, gain_ref, w_ref, wvt_ref, gw_ref, gb_ref, qg_ref, kg_ref,
                      cos_ref, sa_ref, sb_ref, bd_ref,
                      gq_ref, gk_ref, gv_ref, gr_ref, g_ref, aq_ref, ak_ref, avt_ref):
    t = pl.program_id(0)
    row = _mod_row(t, *geom)
    x = jnp.where(t < geom[0], xl_ref[...], xc_ref[...])
    hb = _bf(_modulated(x, gain_ref[...], mod_ref, row, 0, 1))

    def seg(a, b):
        return _dot(hb, w_ref[:, a:b])

    gq_ref[...] = seg(EV_GQ, EV_GK) * (GLA_DK ** -0.5)
    gk_ref[...] = seg(EV_GK, EV_GV)
    gv_ref[...] = _bf(seg(EV_GV, EV_GR))
    gr_ref[...] = seg(EV_GR, EV_AQ)
    zg = _dot(_bf(seg(EV_LR, EV_END)), gw_ref[...]) + gb_ref[...]
    g_ref[...] = -(jnp.maximum(-zg, 0.0) + jnp.log1p(jnp.exp(-jnp.abs(zg)))) * (1.0 / GLA_GATE_NORM)
    bd = bd_ref[...]
    cos, sa, sb = cos_ref[...], sa_ref[...], sb_ref[...]
    aq = _qk_norm_rope(seg(EV_AQ, EV_AK), qg_ref[...], bd, cos, sa, sb)
    aq_ref[...] = _bf(aq * Q_SCALE)
    ak = _qk_norm_rope(seg(EV_AK, EV_LR), kg_ref[...], bd, cos, sa, sb)
    ak_ref[...] = _bf(_kv_rep(ak))
    avt_ref[0] = _bf(_dot_nt(wvt_ref[...], hb))


def _vt_spec():
    return pl.BlockSpec((1, LANES, TOK_TILE), lambda t: (t, 0, 0))


def _lat_ctx_specs(block, n_lat_tiles, lead=()):
    z = (0,) * len(lead)
    lat = pl.BlockSpec(lead + block, lambda t: z + (jnp.minimum(t, n_lat_tiles - 1), 0))
    ctx = pl.BlockSpec(lead + block, lambda t: z + (jnp.maximum(t - n_lat_tiles, 0), 0))
    return [lat, ctx]


def _proj_even(x_lat, x_ctx, mod, gain, w, wvt, gw, gb, qg, kg, tables, bd, geom):
    d = x_lat.shape[1]
    n = x_lat.shape[0] + x_ctx.shape[0]
    n_lat_tiles, tiles_per_batch, _ = geom
    tt = TOK_TILE
    cos, sa, sb = tables
    tok = lambda w_: pl.BlockSpec((tt, w_), lambda t: (t, 0))
    rope = pl.BlockSpec((tt, LANES), lambda t: (_rope_block(t, n_lat_tiles, tiles_per_batch), 0))
    outs = [(256, F32), (256, F32), (512, BF16), (512, F32), (512, F32), (512, BF16), (512, BF16)]
    return pl.pallas_call(
        functools.partial(_proj_even_kernel, geom),
        out_shape=[jax.ShapeDtypeStruct((n, w_), dt) for w_, dt in outs]
        + [jax.ShapeDtypeStruct((n // tt, LANES, tt), BF16)],
        grid=(n // tt,),
        in_specs=_lat_ctx_specs((tt, d), n_lat_tiles)
        + [_full(mod.shape), _full(gain.shape), _full(w.shape), _full(wvt.shape), _full(gw.shape),
           _full(gb.shape), _full(qg.shape), _full(kg.shape), rope, rope, rope, _full(bd.shape)],
        out_specs=[tok(w_) for w_, _ in outs] + [_vt_spec()],
        compiler_params=_cparams(("arbitrary",)),
        name="proj_even",
    )(x_lat, x_ctx, mod, gain, w, wvt, gw, gb, qg, kg, cos, sa, sb, bd)


def _proj_odd_kernel(geom, x_ref, pos_ref, pos_next_ref, rt_ref, fs_hbm, mod_prev_ref, mod_ref, gain_ref, w_ref,
                     wvt_ref, qg_ref, kg_ref, cos_ref, sa_ref, sb_ref, bd_ref, x1_ref, q_ref, k_ref, vt_ref,
                     fbuf, fsem):
    t = pl.program_id(0)
    row = _mod_row(t, *geom)
    d = x_ref.shape[-1]
    gate = mod_prev_ref[pl.ds(row, 1), 5 * d:6 * d]
    f = _gather_tile(t, pl.num_programs(0), pos_ref, pos_next_ref, rt_ref, fs_hbm, fbuf, fsem,
                     inline_prefetch=True)
    x1 = x_ref[...] + gate * f
    x1_ref[...] = x1
    hb = _bf(_modulated(x1, gain_ref[...], mod_ref, row, 0, 1))
    bd = bd_ref[...]
    cos, sa, sb = cos_ref[...], sa_ref[...], sb_ref[...]
    q = _qk_norm_rope(_dot(hb, w_ref[:, 0:d]), qg_ref[...], bd, cos, sa, sb)
    q_ref[...] = _bf(q * Q_SCALE)
    k = _qk_norm_rope(_dot(hb, w_ref[:, d:d + LANES]), kg_ref[...], bd, cos, sa, sb)
    k_ref[...] = _bf(_kv_rep(k))
    vt_ref[0] = _bf(_dot_nt(wvt_ref[...], hb))
    _gather_drain(t, pl.num_programs(0), fs_hbm, fbuf, fsem)


def _proj_odd(x_all, f_sorted, pos, rt, mod_prev, mod, gain, w, wvt, qg, kg, tables, bd, geom):
    n, d = x_all.shape
    n_lat_tiles, tiles_per_batch, _ = geom
    tt = TOK_TILE
    cos, sa, sb = tables
    tok = lambda w_: pl.BlockSpec((tt, w_), lambda t: (t, 0))
    rope = pl.BlockSpec((tt, LANES), lambda t: (_rope_block(t, n_lat_tiles, tiles_per_batch), 0))
    outs = [(d, F32), (d, BF16), (512, BF16)]
    return pl.pallas_call(
        functools.partial(_proj_odd_kernel, geom),
        out_shape=[jax.ShapeDtypeStruct((n, w_), dt) for w_, dt in outs]
        + [jax.ShapeDtypeStruct((n // tt, LANES, tt), BF16)],
        grid=(n // tt,),
        in_specs=[tok(d)] + _gather_specs(n // tt)
        + [_full(mod_prev.shape), _full(mod.shape), _full(gain.shape), _full(w.shape),
           _full(wvt.shape), _full(qg.shape), _full(kg.shape), rope, rope, rope, _full(bd.shape)],
        out_specs=[tok(w_) for w_, _ in outs] + [_vt_spec()],
        scratch_shapes=_gather_scratch(d),
        compiler_params=_cparams(("arbitrary",)),
        name="proj_odd",
    )(x_all, pos, pos, rt, f_sorted, mod_prev, mod, gain, w, wvt, qg, kg, cos, sa, sb, bd)


N_LEVELS = 6
GLA_MX_ROWS = (N_LEVELS + 2) * GLA_CHUNK


def _gla_constants():
    c = GLA_CHUNK
    mx = np.zeros((2, GLA_MX_ROWS, c), np.float32)
    pat = np.zeros((2, N_LEVELS + 1, c, GLA_HEADS * c), np.float32)
    r = np.arange(c)
    for lvl in range(N_LEVELS):
        h = 1 << lvl
        ref = (r // (2 * h)) * 2 * h + h - 1
        upper = (r % (2 * h)) >= h
        m = np.zeros((c, c), np.float32)
        for i in range(c):
            if upper[i]:
                m[i, ref[i] + 1:i + 1] = 1.0
            else:
                m[i, i + 1:ref[i] + 1] = 1.0
        mx[0, lvl * c:(lvl + 1) * c] = m
        same = (r[:, None] // (2 * h)) == (r[None, :] // (2 * h))
        p = same & upper[:, None] & (~upper)[None, :]
        pat[0, lvl] = np.tile(p.astype(np.float32), (1, GLA_HEADS))
    mx[0, N_LEVELS * c:(N_LEVELS + 1) * c] = (r[None, :] <= r[:, None])
    mx[0, (N_LEVELS + 1) * c:(N_LEVELS + 2) * c] = (r[None, :] > r[:, None])
    pat[0, N_LEVELS] = np.tile(np.eye(c, dtype=np.float32), (1, GLA_HEADS))
    for k in range(N_LEVELS + 2):
        mx[1, k * c:(k + 1) * c] = mx[0, k * c:(k + 1) * c][::-1, ::-1]
    for k in range(N_LEVELS + 1):
        pat[1, k] = np.tile(pat[0, k, :, 0:c][::-1, ::-1], (1, GLA_HEADS))
    return mx, pat


GLA_BLOCK = 1024
GLA_GROUP = 4


def _gla_chunks(chunks, mx, pat_ref, s_ref):
    c = GLA_CHUNK
    lane_head = lax.broadcasted_iota(jnp.int32, (c, GLA_HEADS * GLA_DK), 1) // GLA_DK

    def stack_heads(a):
        return jnp.concatenate([_bf(jnp.where(lane_head == h, a, 0.0)) for h in range(GLA_HEADS)], axis=0)

    xs = [_dot(mx, jnp.concatenate(_split2(g), axis=0)) for _, _, _, g in chunks]
    atts = [jnp.where(pat_ref[0, N_LEVELS] > 0.0, _dot_nt(_bf(q), stack_heads(k)), 0.0)
            for q, k, _, _ in chunks]
    for lvl in range(N_LEVELS):
        for j, (q, k, _, _) in enumerate(chunks):
            e = jnp.exp(xs[j][lvl * c:(lvl + 1) * c])
            atts[j] = atts[j] + jnp.where(pat_ref[0, lvl] > 0.0, _dot_nt(_bf(q * e), stack_heads(k * e)), 0.0)
    outs, qes, news, a_cols = [], [], [], []
    for j, (q, k, v, _) in enumerate(chunks):
        vhead = lax.broadcasted_iota(jnp.int32, v.shape, 1) // GLA_DV
        v_bd = jnp.concatenate([jnp.where(vhead == h, v, jnp.zeros_like(v)) for h in range(GLA_HEADS)], axis=0)
        outs.append(_dot(_bf(atts[j]), v_bd))
        bcum = xs[j][N_LEVELS * c:(N_LEVELS + 1) * c]
        brem = xs[j][(N_LEVELS + 1) * c:(N_LEVELS + 2) * c]
        qes.append(stack_heads(q * jnp.exp(bcum)))
        kt = jnp.transpose(k * jnp.exp(brem))
        news.append(jnp.concatenate(
            [_dot(_bf(kt[h * GLA_DK:(h + 1) * GLA_DK]), v[:, h * GLA_DV:(h + 1) * GLA_DV])
             for h in range(GLA_HEADS)], axis=0))
        tot = bcum[0:1] + brem[0:1]
        a_cols.append(jnp.transpose(jnp.exp(jnp.broadcast_to(tot, (8, tot.shape[1]))))[:, 0:1])
    s = s_ref[...]
    for j in range(len(chunks)):
        o_inter = _dot(qes[j], _bf(s))
        outs[j] = outs[j] + jnp.concatenate([o_inter[h * c:(h + 1) * c] for h in range(GLA_HEADS)], axis=-1)
        s = a_cols[j] * s + news[j]
    s_ref[...] = s
    return outs


def _gla_kernel(n_lat_chunks, n_ctx_chunks, ql_ref, kl_ref, vl_ref, gl_ref, qc_ref, kc_ref, vc_ref, gc_ref,
                mx_ref, pat_ref, ol_ref, oc_ref, s_ref):
    d = pl.program_id(1)
    c = GLA_CHUNK
    mx = mx_ref[0]

    def run(n_chunks, q_ref, k_ref, v_ref, g_ref, o_ref):
        group = min(GLA_GROUP, n_chunks)

        def body(i, carry):
            rows = []
            for j in range(group):
                step = i * group + j
                ci = jnp.where(d == 0, step, n_chunks - 1 - step)
                rows.append(pl.ds(pl.multiple_of(ci * c, c), c))
            outs = _gla_chunks([(q_ref[r, :], k_ref[r, :], v_ref[r, :], g_ref[r, :]) for r in rows],
                               mx, pat_ref, s_ref)
            for r, o in zip(rows, outs):
                o_ref[0, r, :] = o
            return carry
        lax.fori_loop(0, n_chunks // group, body, 0)

    @pl.when(pl.program_id(2) == 0)
    def _():
        s_ref[...] = jnp.zeros_like(s_ref)
        run(n_ctx_chunks, qc_ref, kc_ref, vc_ref, gc_ref, oc_ref)

    run(n_lat_chunks, ql_ref, kl_ref, vl_ref, gl_ref, ol_ref)


def _gla(gq, gk, gv, g, n_batch, seq, lc):
    mx_np, pat_np = _gla_constants()
    mx = jnp.asarray(np.concatenate([mx_np, mx_np], axis=2), BF16)
    pat = jnp.asarray(pat_np, F32)
    ctx0 = n_batch * seq // lc
    hk, hv = GLA_HEADS * GLA_DK, GLA_HEADS * GLA_DV
    blk = min(GLA_BLOCK, seq)
    nb = seq // blk
    for rows_ in (blk, lc):
        assert rows_ % (GLA_CHUNK * min(GLA_GROUP, rows_ // GLA_CHUNK)) == 0
    assert seq % blk == 0
    row = lambda b, d_, i: b * nb + jnp.where(d_ == 0, i, nb - 1 - i)
    lat = lambda w_, col: pl.BlockSpec((blk, w_), lambda b, d_, i: (row(b, d_, i), col(d_)))
    ctx = lambda w_, col: pl.BlockSpec((lc, w_), lambda b, d_, i: (ctx0 + b, col(d_)))
    zero = lambda d_: 0
    same = lambda d_: d_
    return pl.pallas_call(
        functools.partial(_gla_kernel, blk // GLA_CHUNK, lc // GLA_CHUNK),
        out_shape=[jax.ShapeDtypeStruct((2, n_batch * seq, hv), F32),
                   jax.ShapeDtypeStruct((2, n_batch * lc, hv), F32)],
        grid=(n_batch, 2, nb),
        in_specs=[lat(hk, zero), lat(hk, zero), lat(hv, zero), lat(hk, same),
                  ctx(hk, zero), ctx(hk, zero), ctx(hv, zero), ctx(hk, same),
                  pl.BlockSpec((1,) + mx.shape[1:], lambda b, d_, i: (d_, 0, 0)),
                  pl.BlockSpec((1,) + pat.shape[1:], lambda b, d_, i: (d_, 0, 0, 0))],
        out_specs=[pl.BlockSpec((1, blk, hv), lambda b, d_, i: (d_, row(b, d_, i), 0)),
                   pl.BlockSpec((1, lc, hv), lambda b, d_, i: (d_, b, 0))],
        scratch_shapes=[pltpu.VMEM((GLA_HEADS * GLA_DK, GLA_DV), F32)],
        compiler_params=_cparams(("arbitrary", "arbitrary", "arbitrary")),
        name="gla_scan",
    )(gq, gk, gv, g, gq, gk, gv, g, mx, pat)


def _stack_heads(q):
    lane_head = lax.broadcasted_iota(jnp.int32, q.shape, 1) // HEAD_DIM
    return jnp.concatenate([jnp.where(lane_head == h, q, jnp.zeros_like(q)) for h in range(4)], axis=0)


SAFE_SCORE_BOUND = 60.0
ONES_ROWS = 16


def _with_ones(vt):
    return jnp.concatenate([vt, jnp.ones((ONES_ROWS, vt.shape[1]), vt.dtype)], axis=0)


def _attn_store(acc, l, o_ref, u, tq):
    out = acc * (1.0 / l)
    out = jnp.concatenate([out[:, h * tq:(h + 1) * tq] for h in range(4)], axis=0)
    o_ref[u * tq:(u + 1) * tq, :] = _bf(jnp.transpose(out))


def _attn_dense_kernel(n_chunks, n_sub, *refs):
    if n_chunks:
        bound_ref, q_ref, kl_ref, vtl_ref, kc_ref, vtc_ref, o_ref = refs
    else:
        bound_ref, q_ref, kc_ref, vtc_ref, o_ref = refs
    tq = q_ref.shape[0] // n_sub
    cols = 4 * tq
    q4 = [_stack_heads(q_ref[u * tq:(u + 1) * tq, :]) for u in range(n_sub)]

    def scores(c, u):
        if c < n_chunks:
            return _dot_nt(kl_ref[c * ATT_K_CHUNK:(c + 1) * ATT_K_CHUNK, :], q4[u])
        return _dot_nt(kc_ref[...], q4[u])

    def update(carry, st, vt_aug):
        m, acc = carry
        m_new = jnp.maximum(m, jnp.max(st, axis=0, keepdims=True))
        acc = jnp.exp2(m - m_new) * acc + _dot(vt_aug, _bf(jnp.exp2(st - m_new)))
        return m_new, acc

    def run(fixed_ref):
        if fixed_ref is None:
            carry = [(jnp.full((1, cols), NEG_BIG, F32), jnp.zeros((HEAD_DIM + ONES_ROWS, cols), F32))
                     for _ in range(n_sub)]
        else:
            carry = [jnp.zeros((HEAD_DIM + ONES_ROWS, cols), F32) for _ in range(n_sub)]
        st = [scores(0, u) for u in range(n_sub)]
        for c in range(n_chunks + 1):
            st_next = [scores(c + 1, u) for u in range(n_sub)] if c < n_chunks else None
            vt_aug = _with_ones(vtl_ref[c] if c < n_chunks else vtc_ref[0])
            if fixed_ref is None:
                carry = [update(carry[u], st[u], vt_aug) for u in range(n_sub)]
            else:
                carry = [carry[u] + _dot(vt_aug, _bf(jnp.exp2(st[u] - fixed_ref))) for u in range(n_sub)]
            st = st_next
        for u in range(n_sub):
            acc = carry[u][1] if fixed_ref is None else carry[u]
            _attn_store(acc[0:HEAD_DIM], acc[HEAD_DIM:HEAD_DIM + 1], o_ref, u, tq)

    bound = bound_ref[0]

    @pl.when(bound <= SAFE_SCORE_BOUND)
    def _():
        run(bound)

    @pl.when(bound > SAFE_SCORE_BOUND)
    def _():
        run(None)


def _attn_window_kernel(seq, bound_ref, q_ref, *refs):
    nk = WIN_SUB + 2
    k_refs, v_refs = refs[0:nk], refs[nk:2 * nk]
    kc_ref, vtc_ref, sink_ref, o_ref = refs[2 * nk:]
    tq = SWA_WINDOW
    i = pl.program_id(2)
    kb = jnp.concatenate([r[...] for r in k_refs], axis=0)
    vtb = _with_ones(jnp.concatenate([r[0] for r in v_refs], axis=1))
    kc, vtc, sink = kc_ref[...], _with_ones(vtc_ref[0]), sink_ref[0]
    span = 3 * tq
    r_i = lax.broadcasted_iota(jnp.int32, (tq, tq), 0)
    c_i = lax.broadcasted_iota(jnp.int32, (tq, tq), 1)
    band_lo = jnp.where(r_i >= c_i, 0.0, NEG_BIG)
    band_hi = jnp.where(r_i <= c_i, 0.0, NEG_BIG)
    n_blocks = seq // tq

    def run(use_bound):
        sb, sc = [], []
        for u in range(WIN_SUB):
            q4 = _stack_heads(q_ref[u * tq:(u + 1) * tq, :])
            first = i * WIN_SUB + u - 1
            bias_lo = jnp.where(first < 0, NEG_BIG, band_lo)
            bias_hi = jnp.where(first + 2 >= n_blocks, NEG_BIG, band_hi)
            s = _dot_nt(kb[u * tq:u * tq + span], q4)
            sb.append(jnp.concatenate([s[0:tq] + jnp.concatenate([bias_lo] * 4, axis=1), s[tq:2 * tq],
                                       s[2 * tq:span] + jnp.concatenate([bias_hi] * 4, axis=1)], axis=0))
            sc.append(_dot_nt(kc, q4))
        ms, pbs, pcs = [], [], []
        for u in range(WIN_SUB):
            if use_bound:
                m = jnp.maximum(bound_ref[0], sink)
            else:
                m = jnp.maximum(jnp.maximum(jnp.max(sb[u], axis=0, keepdims=True),
                                            jnp.max(sc[u], axis=0, keepdims=True)), sink)
            ms.append(m)
            pbs.append(_bf(jnp.exp2(sb[u] - m)))
            pcs.append(_bf(jnp.exp2(sc[u] - m)))
        for u in range(WIN_SUB):
            acc = _dot(vtb[:, u * tq:u * tq + span], pbs[u]) + _dot(vtc, pcs[u])
            l = acc[HEAD_DIM:HEAD_DIM + 1] + jnp.exp2(sink - ms[u])
            _attn_store(acc[0:HEAD_DIM], l, o_ref, u, tq)

    @pl.when(bound_ref[0] <= SAFE_SCORE_BOUND)
    def _():
        run(True)

    @pl.when(bound_ref[0] > SAFE_SCORE_BOUND)
    def _():
        run(False)


def _score_bound(q_gain, k_gain):
    return (HEAD_DIM * Q_SCALE * 1.02 * jnp.max(jnp.abs(q_gain)) * jnp.max(jnp.abs(k_gain))).reshape(1)


def _attention(mode, q, k_rep, vt, bound, n_batch, seq, lc, n_heads, n_kv, sink=None):
    ncol = n_heads * HEAD_DIM // ATT_COL
    col_per_kv = ncol // n_kv
    tq = ATT_Q_TILE
    tpb = seq // TOK_TILE
    n_lat_tiles = n_batch * tpb
    ctx_per_tile = TOK_TILE // lc
    ctx0 = n_batch * seq // lc
    assert ATT_K_CHUNK == TOK_TILE and TOK_TILE % lc == 0 and tq == SWA_WINDOW
    kv = lambda j: j // col_per_kv
    k_ctx = pl.BlockSpec((lc, ATT_COL), lambda b, j, i: (ctx0 + b, kv(j)))
    vt_ctx = pl.BlockSpec((1, HEAD_DIM, lc), lambda b, j, i: (n_lat_tiles + b // ctx_per_tile, kv(j), b % ctx_per_tile))
    if mode == "ctx":
        nq = lc // tq
        q0 = n_batch * seq // tq
        kern = functools.partial(_attn_dense_kernel, 0, 1)
        args = (q, k_rep, vt)
        in_specs = [pl.BlockSpec((tq, ATT_COL), lambda b, j, i: (q0 + b * nq + i, j)), k_ctx, vt_ctx]
    elif mode == "dense":
        tq = DENSE_SUB * ATT_Q_TILE
        nq = seq // tq
        kern = functools.partial(_attn_dense_kernel, tpb, DENSE_SUB)
        args = (q, k_rep, vt, k_rep, vt)
        in_specs = [pl.BlockSpec((tq, ATT_COL), lambda b, j, i: (b * nq + i, j)),
                    pl.BlockSpec((seq, ATT_COL), lambda b, j, i: (b, kv(j))),
                    pl.BlockSpec((tpb, HEAD_DIM, TOK_TILE), lambda b, j, i: (b, kv(j), 0)), k_ctx, vt_ctx]
    else:
        wb = SWA_WINDOW
        tq = WIN_SUB * wb
        nq = seq // tq
        nkb = seq // wb
        per_tile = TOK_TILE // wb
        kern = functools.partial(_attn_window_kernel, seq)
        nb = lambda i, o: jnp.clip(i * WIN_SUB + o, 0, nkb - 1)
        k_nb = lambda o: pl.BlockSpec((wb, ATT_COL), lambda b, j, i: (b * nkb + nb(i, o), kv(j)))
        v_nb = lambda o: pl.BlockSpec(
            (1, HEAD_DIM, wb), lambda b, j, i: (b * tpb + nb(i, o) // per_tile, kv(j), nb(i, o) % per_tile))
        offs = range(-1, WIN_SUB + 1)
        sink_row = jnp.repeat(sink.reshape(ncol, 1, 4), wb, axis=2) * LOG2E
        args = (q,) + (k_rep,) * len(offs) + (vt,) * len(offs) + (k_rep, vt, sink_row)
        in_specs = ([pl.BlockSpec((tq, ATT_COL), lambda b, j, i: (b * nq + i, j))]
                    + [k_nb(o) for o in offs] + [v_nb(o) for o in offs]
                    + [k_ctx, vt_ctx, pl.BlockSpec((1, 1, 4 * wb), lambda b, j, i: (j, 0, 0))])
    args = (bound,) + args
    in_specs = [pl.BlockSpec(memory_space=pltpu.SMEM)] + in_specs
    return pl.pallas_call(
        kern,
        out_shape=jax.ShapeDtypeStruct((n_batch * nq * tq, n_heads * HEAD_DIM), BF16),
        grid=(n_batch, ncol, nq),
        in_specs=in_specs,
        out_specs=pl.BlockSpec((tq, ATT_COL), lambda b, j, i: (b * nq + i, j)),
        compiler_params=_cparams(("arbitrary", "arbitrary", "arbitrary")),
        name="attention_" + mode,
    )(*args)


ROUTE_ROWS = 8
TOKEN_TILE = (8, LANES)
OUT_TOKEN_TILE = (16, LANES)


def _to_token_tiles(x, tile):
    return x.reshape((x.shape[0],) + tile)


def _from_token_tiles(tiles):
    return tiles.reshape(tiles.shape[0], tiles.shape[1] * tiles.shape[2])


def _route(h, wrt_ref, wrt_hi_ref, brt_ref):
    hh, hl = _split2(h)
    a = _dot_nt(wrt_ref[...], hh)
    lt = a[0:LANES] + a[LANES:2 * LANES] + _dot_nt(wrt_hi_ref[...], hl) + brt_ref[...]
    col = lambda i: lt[i:i + 1, :]
    gl = [col(i) for i in range(MOE_GROUPS)]
    gmax = functools.reduce(jnp.maximum, gl)
    gi = jnp.where(gl[0] == gmax, 0, jnp.where(gl[1] == gmax, 1, jnp.where(gl[2] == gmax, 2, 3)))
    g_weight = 1.0 / functools.reduce(lambda a, b: a + b, [jnp.exp(x - gmax) for x in gl])
    el = []
    for j in range(MOE_EPG):
        cand = [col(MOE_GROUPS + g * MOE_EPG + j) for g in range(MOE_GROUPS)]
        el.append(jnp.where(gi == 0, cand[0], jnp.where(gi == 1, cand[1], jnp.where(gi == 2, cand[2], cand[3]))))
    m1 = functools.reduce(jnp.maximum, el)
    i1 = jnp.where(el[0] == m1, 0, jnp.where(el[1] == m1, 1, jnp.where(el[2] == m1, 2, 3)))
    rest = [jnp.where(i1 == j, -jnp.inf, el[j]) for j in range(MOE_EPG)]
    m2 = functools.reduce(jnp.maximum, rest)
    i2 = jnp.where(rest[0] == m2, 0, jnp.where(rest[1] == m2, 1, jnp.where(rest[2] == m2, 2, 3)))
    e2 = jnp.exp(m2 - m1)
    w1 = g_weight / (1.0 + e2)
    w2 = g_weight * e2 / (1.0 + e2)
    lo = jnp.minimum(i1, i2)
    hi = jnp.maximum(i1, i2)
    w_lo = jnp.where(i1 == lo, w1, w2)
    w_hi = jnp.where(i1 == lo, w2, w1)
    pair = jnp.where(lo == 0, hi - 1, jnp.where(lo == 1, hi + 1, N_PAIRS - 1))
    return w_lo, w_hi, gi * N_PAIRS + pair


def _out_tail(geom, m, x, mod_ref, gain_ffn_ref, wrt_ref, wrt_hi_ref, brt_ref, triu_ref,
              x_new_ref, hrow_ref, rt_ref, counts_ref, run_ref):
    t = pl.program_id(0)
    row = _mod_row(t, *geom)
    tt, d = x.shape

    @pl.when(t == 0)
    def _():
        run_ref[...] = jnp.zeros_like(run_ref)

    x_new = x + mod_ref[pl.ds(row, 1), 2 * d:3 * d] * m
    x_new_ref[...] = x_new
    h = _modulated(x_new, gain_ffn_ref[...], mod_ref, row, 3, 4)
    w_lo, w_hi, bucket = _route(h, wrt_ref, wrt_hi_ref, brt_ref)
    onehot = lax.broadcasted_iota(jnp.int32, (LANES, tt), 0) == bucket
    ones = jnp.where(onehot, 1.0, 0.0)
    before = _dot(_bf(ones), triu_ref[...]) + run_ref[...]
    rank = jnp.sum(jnp.where(onehot, before, 0.0), axis=0, keepdims=True)
    run = run_ref[...] + jnp.sum(ones, axis=1, keepdims=True)
    run_ref[...] = run
    counts_ref[...] = jnp.broadcast_to(run, counts_ref.shape)
    rec = jnp.concatenate([w_lo, w_hi, bucket.astype(F32), rank, jnp.zeros((ROUTE_ROWS - 4, tt), F32)], axis=0)
    rt_ref[0] = rec
    hrow_ref[...] = _to_token_tiles(h, TOKEN_TILE)


def _pick(t, n_lat_tiles, lat_ref, ctx_ref):
    return jnp.where(t < n_lat_tiles, lat_ref[...], ctx_ref[...])


def _out_even_kernel(geom, ol_ref, oc_ref, r_ref, attl_ref, attc_ref, gn_ref, w_ref, xl_ref, xc_ref, mod_ref,
                     gain_ffn_ref, wrt_ref, wrt_hi_ref, brt_ref, triu_ref,
                     x_new_ref, hrow_ref, rt_ref, counts_ref, run_ref):
    t = pl.program_id(0)
    o2 = _pick(t, geom[0], ol_ref, oc_ref)
    o = o2[0] + o2[1]
    r = r_ref[...]
    parts = []
    for h in range(GLA_HEADS):
        sl = slice(h * GLA_DV, (h + 1) * GLA_DV)
        parts.append(_rms(o[:, sl]) * gn_ref[...] * _silu(r[:, sl]))
    a = _bf(jnp.concatenate(parts, axis=-1))
    half = a.shape[-1]
    m = _dot(a, w_ref[0:half, :]) + _dot(_pick(t, geom[0], attl_ref, attc_ref), w_ref[half:, :])
    _out_tail(geom, m, _pick(t, geom[0], xl_ref, xc_ref), mod_ref, gain_ffn_ref, wrt_ref, wrt_hi_ref, brt_ref,
              triu_ref, x_new_ref, hrow_ref, rt_ref, counts_ref, run_ref)


def _out_odd_kernel(geom, att_ref, w_ref, x_ref, mod_ref, gain_ffn_ref, wrt_ref, wrt_hi_ref, brt_ref, triu_ref,
                    x_new_ref, hrow_ref, rt_ref, counts_ref, run_ref):
    m = _dot(att_ref[...], w_ref[...])
    _out_tail(geom, m, x_ref[...], mod_ref, gain_ffn_ref, wrt_ref, wrt_hi_ref, brt_ref, triu_ref,
              x_new_ref, hrow_ref, rt_ref, counts_ref, run_ref)


def _out_proj(kernel, lead_args, lead_specs, d, mod, gain_ffn, router, n_rows, name):
    tt = TOK_TILE
    tok = lambda w_: pl.BlockSpec((tt, w_), lambda t: (t, 0))
    r = np.arange(tt)
    triu = jnp.asarray((r[:, None] < r[None, :]).astype(np.float32), BF16)
    wrt, wrt_hi, br = router
    brt = jnp.broadcast_to(br.reshape(LANES, 1), (LANES, tt))
    return pl.pallas_call(
        kernel,
        out_shape=[jax.ShapeDtypeStruct((n_rows, d), F32),
                   jax.ShapeDtypeStruct((n_rows,) + TOKEN_TILE, F32),
                   jax.ShapeDtypeStruct((n_rows // tt, ROUTE_ROWS, tt), F32),
                   jax.ShapeDtypeStruct((LANES, LANES), F32)],
        grid=(n_rows // tt,),
        in_specs=lead_specs + [_full(mod.shape), _full(gain_ffn.shape), _full(wrt.shape), _full(wrt_hi.shape),
                               _full(brt.shape), _full(triu.shape)],
        out_specs=[tok(d), pl.BlockSpec((tt,) + TOKEN_TILE, lambda t: (t, 0, 0)),
                   pl.BlockSpec((1, ROUTE_ROWS, tt), lambda t: (t, 0, 0)), _full((LANES, LANES))],
        scratch_shapes=[pltpu.VMEM((LANES, 1), F32)],
        compiler_params=_cparams(("arbitrary",)),
        name=name,
    )(*lead_args, mod, gain_ffn, wrt, wrt_hi, brt, triu)


SUBLANES = 8


def _for_each_row(n_rows, start_row_copy):
    def body(g, c):
        base = g * SUBLANES
        for j in range(SUBLANES):
            start_row_copy(base + j, j % 2)
        return c
    lax.fori_loop(0, n_rows // SUBLANES, body, 0)


def _pos_kernel(start_ref, rt_ref, pos_ref):
    bucket = rt_ref[:, 2, :].astype(jnp.int32)
    base = jnp.zeros_like(bucket)
    for b in range(N_BUCKETS):
        base = jnp.where(bucket == b, start_ref[b], base)
    pos_ref[:, 0, :] = base + rt_ref[:, 3, :].astype(jnp.int32)


def _dispatch_kernel(last_ref, pos_ref, x_ref, xs_ref, zbuf, sem, zsem):
    tt = x_ref.shape[0]
    tm = zbuf.shape[0]

    @pl.when(pl.program_id(0) == 0)
    def _():
        zbuf[...] = jnp.zeros_like(zbuf)

        def zero_copy(b):
            return pltpu.make_async_copy(zbuf, xs_ref.at[pl.ds(jnp.maximum(last_ref[b], 0) * tm, tm)], zsem)

        for b in range(2 * N_BUCKETS):
            @pl.when(last_ref[b] >= 0)
            def _():
                zero_copy(b).start()
        for b in range(2 * N_BUCKETS):
            @pl.when(last_ref[b] >= 0)
            def _():
                zero_copy(b).wait()

    _for_each_row(tt, lambda r, prio: pltpu.make_async_copy(
        x_ref.at[r], xs_ref.at[pos_ref[0, 0, r]], sem).start(priority=prio))
    pltpu.make_async_copy(x_ref, xs_ref.at[pl.ds(0, tt)], sem).wait()


def _moe_mlp_kernel(tlo_ref, thi_ref, nused_ref, xs_ref, wg_lo, wu_lo, wd_lo, wg_hi, wu_hi, wd_hi, f_ref):
    del tlo_ref, thi_ref
    used = pl.program_id(0) < nused_ref[0]

    @pl.when(used)
    def _():
        xb = _bf(_from_token_tiles(xs_ref[...]))
        hid_lo = _silu(_dot(xb, wg_lo[0])) * _dot(xb, wu_lo[0])
        hid_hi = _silu(_dot(xb, wg_hi[0])) * _dot(xb, wu_hi[0])
        f2 = jnp.concatenate([_dot(_bf(hid_lo), wd_lo[0]), _dot(_bf(hid_hi), wd_hi[0])], axis=1)
        f_ref[...] = _to_token_tiles(f2, OUT_TOKEN_TILE)

    @pl.when(jnp.logical_not(used))
    def _():
        f_ref[...] = jnp.zeros_like(f_ref)


def _moe(hrow, rt, counts, experts, layer):
    n = hrow.shape[0]
    d = TOKEN_TILE[0] * TOKEN_TILE[1]
    tm, tt = MOE_TILE, TOK_TILE
    n_tiles = n // tm + N_BUCKETS
    p = n_tiles * tm
    cnt = counts[0:N_BUCKETS, 0].astype(jnp.int32)
    tiles_b = (cnt + tm - 1) // tm
    tile_end = jnp.cumsum(tiles_b)
    start_b = (tile_end - tiles_b) * tm
    n_used = tile_end[-1].reshape(1)
    spare = n_used[0] + jnp.arange(N_BUCKETS, dtype=jnp.int32)
    last_tile = jnp.concatenate([jnp.where(tiles_b > 0, tile_end - 1, -1), jnp.where(spare < n_tiles, spare, -1)])
    pos = pl.pallas_call(
        _pos_kernel,
        out_shape=jax.ShapeDtypeStruct((n // tt, 1, tt), jnp.int32),
        grid_spec=pltpu.PrefetchScalarGridSpec(
            num_scalar_prefetch=1, grid=(1,),
            in_specs=[pl.BlockSpec(rt.shape, lambda i, s: (0, 0, 0))],
            out_specs=pl.BlockSpec((n // tt, 1, tt), lambda i, s: (0, 0, 0))),
        compiler_params=_cparams(("arbitrary",)),
        name="moe_positions",
    )(start_b, rt)
    tile_ids = jnp.arange(n_tiles, dtype=jnp.int32)
    tile_bucket = jnp.sum((tile_ids[:, None] >= tile_end[None, :]).astype(jnp.int32), axis=1)
    tile_bucket = jnp.minimum(tile_bucket, jnp.sum((n_used[0] - 1 >= tile_end).astype(jnp.int32)))
    tile_bucket = jnp.minimum(tile_bucket, N_BUCKETS - 1)
    pair_lo = jnp.asarray([0, 0, 0, 1, 1, 2], jnp.int32)
    pair_hi = jnp.asarray([1, 2, 3, 2, 3, 3], jnp.int32)
    grp = tile_bucket // N_PAIRS
    w_gate, w_up, w_down = experts
    first = layer * MOE_GROUPS * MOE_EPG
    t_lo = first + grp * MOE_EPG + pair_lo[tile_bucket % N_PAIRS]
    t_hi = first + grp * MOE_EPG + pair_hi[tile_bucket % N_PAIRS]

    dt = DISPATCH_TILE if n % DISPATCH_TILE == 0 else tt
    xs = pl.pallas_call(
        _dispatch_kernel,
        out_shape=jax.ShapeDtypeStruct((p,) + TOKEN_TILE, F32),
        grid_spec=pltpu.PrefetchScalarGridSpec(
            num_scalar_prefetch=1, grid=(n // dt,),
            in_specs=[pl.BlockSpec((1, 1, dt), lambda t, s: (t, 0, 0), memory_space=pltpu.SMEM),
                      pl.BlockSpec((dt,) + TOKEN_TILE, lambda t, s: (t, 0, 0))],
            out_specs=pl.BlockSpec(memory_space=pl.ANY),
            scratch_shapes=[pltpu.VMEM((tm,) + TOKEN_TILE, F32), pltpu.SemaphoreType.DMA(()),
                            pltpu.SemaphoreType.DMA(())]),
        compiler_params=_cparams(("arbitrary",)),
        name="moe_dispatch",
    )(last_tile, pos.reshape(n // dt, 1, dt), hrow)

    f = D_EXPERT
    up_lo = pl.BlockSpec((1, d, f), lambda t, lo, hi, nu: (lo[t], 0, 0))
    up_hi = pl.BlockSpec((1, d, f), lambda t, lo, hi, nu: (hi[t], 0, 0))
    dn_lo = pl.BlockSpec((1, f, d), lambda t, lo, hi, nu: (lo[t], 0, 0))
    dn_hi = pl.BlockSpec((1, f, d), lambda t, lo, hi, nu: (hi[t], 0, 0))
    grid_spec = pltpu.PrefetchScalarGridSpec(
        num_scalar_prefetch=3,
        grid=(n_tiles,),
        in_specs=[pl.BlockSpec((tm,) + TOKEN_TILE, lambda t, lo, hi, nu: (jnp.minimum(t, nu[0] - 1), 0, 0)),
                  up_lo, up_lo, dn_lo, up_hi, up_hi, dn_hi],
        out_specs=pl.BlockSpec((tm,) + OUT_TOKEN_TILE, lambda t, *_: (t, 0, 0)),
    )
    f_sorted = pl.pallas_call(
        _moe_mlp_kernel,
        out_shape=jax.ShapeDtypeStruct((p,) + OUT_TOKEN_TILE, F32),
        grid_spec=grid_spec,
        compiler_params=_cparams(("arbitrary",)),
        name="moe_experts",
    )(t_lo, t_hi, n_used, xs, w_gate, w_up, w_down, w_gate, w_up, w_down)
    return f_sorted, pos


def _gather_tile(t, n_t, pos_ref, pos_next_ref, rt_ref, src_hbm, buf, sem, inline_prefetch=False):
    tt = buf.shape[1]

    def start(p_ref, slot):
        _for_each_row(tt, lambda r, prio: pltpu.make_async_copy(
            src_hbm.at[p_ref[0, 0, r]], buf.at[slot].at[r], sem.at[slot]).start(priority=prio))

    slot = t % 2

    def wait(s):
        pltpu.make_async_copy(src_hbm.at[pl.ds(0, tt)], buf.at[s], sem.at[s]).wait()

    @pl.when(t == 0)
    def _():
        start(pos_ref, 0)

    wait(slot)
    if inline_prefetch:
        for r in range(tt):
            pltpu.make_async_copy(src_hbm.at[pos_next_ref[0, 0, r]], buf.at[1 - slot].at[r],
                                  sem.at[1 - slot]).start(priority=r % 2)
    else:
        @pl.when(t + 1 < n_t)
        def _():
            start(pos_next_ref, 1 - slot)

    f2 = _from_token_tiles(buf[slot])
    d = f2.shape[1] // 2
    w = jnp.transpose(rt_ref[0])
    return w[:, 0:1] * f2[:, 0:d] + w[:, 1:2] * f2[:, d:2 * d]


def _gather_drain(t, n_t, src_hbm, buf, sem):
    tt = buf.shape[1]

    @pl.when(t == n_t - 1)
    def _():
        pltpu.make_async_copy(src_hbm.at[pl.ds(0, tt)], buf.at[1 - t % 2], sem.at[1 - t % 2]).wait()


def _gather_specs(n_t):
    tt = TOK_TILE
    return [pl.BlockSpec((1, 1, tt), lambda t: (t, 0, 0), memory_space=pltpu.SMEM),
            pl.BlockSpec((1, 1, tt), lambda t: (jnp.minimum(t + 1, n_t - 1), 0, 0), memory_space=pltpu.SMEM),
            pl.BlockSpec((1, ROUTE_ROWS, tt), lambda t: (t, 0, 0)),
            pl.BlockSpec(memory_space=pl.ANY)]


def _gather_scratch(d):
    assert 2 * d == OUT_TOKEN_TILE[0] * OUT_TOKEN_TILE[1]
    return [pltpu.VMEM((2, TOK_TILE) + OUT_TOKEN_TILE, F32), pltpu.SemaphoreType.DMA((2,))]


def _final_kernel(tiles_per_batch, x_ref, pos_ref, pos_next_ref, rt_ref, fs_hbm, mod_ref, o_ref, fbuf, fsem):
    t = pl.program_id(0)
    row = t // tiles_per_batch
    d = x_ref.shape[-1]
    f = _gather_tile(t, pl.num_programs(0), pos_ref, pos_next_ref, rt_ref, fs_hbm, fbuf, fsem)
    o_ref[...] = x_ref[...] + mod_ref[pl.ds(row, 1), 5 * d:6 * d] * f


def _final(x_lat, f_sorted, pos, rt, mod, tiles_per_batch):
    n, d = x_lat.shape
    tok = pl.BlockSpec((TOK_TILE, d), lambda t: (t, 0))
    return pl.pallas_call(
        functools.partial(_final_kernel, tiles_per_batch),
        out_shape=jax.ShapeDtypeStruct((n, d), F32),
        grid=(n // TOK_TILE,),
        in_specs=[tok] + _gather_specs(n // TOK_TILE) + [_full(mod.shape)],
        out_specs=tok,
        scratch_shapes=_gather_scratch(d),
        compiler_params=_cparams(("arbitrary",)),
        name="final_residual",
    )(x_lat, pos, pos, rt, f_sorted, mod)


def _block_diag_ones():
    r = np.arange(MXU_DIM) // HEAD_DIM
    return jnp.asarray((r[:, None] == r[None, :]).astype(np.float32), BF16)


def _router_weights(wg, bg, we, be):
    d = wg.shape[0]
    n = MOE_GROUPS + MOE_GROUPS * MOE_EPG
    wt = jnp.concatenate([wg, we, jnp.zeros((d, LANES - n), F32)], axis=1).T
    b = jnp.concatenate([bg, be, jnp.zeros((LANES - n,), F32)])
    hi = _bf(wt)
    lo = _bf(wt - hi.astype(F32))
    return jnp.concatenate([hi, lo], axis=0), hi, b


def _expert_weights(w_gate, w_up, w_down):
    l, g, e, d, f = w_gate.shape
    n = l * g * e
    return _bf(w_gate).reshape(n, d, f), _bf(w_up).reshape(n, d, f), _bf(w_down).reshape(n, f, d)


def kernel(x, c, ctx, c_ctx, mod_w, mod_b, norm_mix, norm_ffn, ev_w_in, ev_w_out, gla_gate_w, gla_gate_b,
           gla_out_norm, att_q_norm, att_k_norm, od_w_in, od_w_out, swa_sink, swa_q_norm, swa_k_norm,
           router_group_w, router_group_b, router_expert_w, router_expert_b, exp_w_gate, exp_w_up, exp_w_down):
    n_batch, seq, d = x.shape
    lc = ctx.shape[1]
    depth = mod_w.shape[0]
    n_lat = n_batch * seq
    tiles_per_batch = seq // TOK_TILE
    geom = (n_lat // TOK_TILE, tiles_per_batch, n_batch)
    assert depth == 2 and seq % TOK_TILE == 0 and (n_batch * lc) % TOK_TILE == 0 and n_batch < 16

    x_lat, x_ctx = x.reshape(n_lat, d), ctx.reshape(n_batch * lc, d)
    n_all = n_lat + n_batch * lc
    c_rows = jnp.zeros((16, d), F32).at[:n_batch].set(c).at[n_batch].set(c_ctx)
    mod = _modulation(c_rows, mod_w, mod_b)
    tables = _rope_tables(seq)
    bd = _block_diag_ones()
    row2 = lambda v: v.reshape(1, -1)
    tile_gain = lambda gvec, reps: jnp.tile(gvec, reps).reshape(1, -1)

    w0 = ev_w_in[0]
    seg = np.cumsum([0, 256, 256, 512, 512, 32, 512, 128, 128])
    cols = lambda i: w0[:, seg[i]:seg[i + 1]]
    w_even = _bf(jnp.concatenate([cols(0), cols(1), cols(2), cols(3), cols(5), cols(6), cols(4),
                                  jnp.zeros((d, EV_END - EV_LR - 2 * GLA_GATE_RANK), F32)], axis=1))
    wvt_even = _bf(cols(7).T)
    hk = GLA_HEADS * GLA_DK
    gw = jnp.zeros((LANES, 2 * hk), F32)
    gw = gw.at[0:GLA_GATE_RANK, 0:hk].set(gla_gate_w[0, 0])
    gw = gw.at[GLA_GATE_RANK:2 * GLA_GATE_RANK, hk:2 * hk].set(gla_gate_w[0, 1])
    gb = gla_gate_b[0].reshape(1, 2 * hk)
    gq, gk, gv, gr, g, aq, ak, avt = _proj_even(
        x_lat, x_ctx, mod[0], row2(norm_mix[0]), w_even, wvt_even, _bf(gw), gb,
        tile_gain(att_q_norm[0], ATT_HEADS), tile_gain(att_k_norm[0], ATT_KV_HEADS), tables, bd, geom)
    o_lat, o_ctx = _gla(gq, gk, gv, g, n_batch, seq, lc)
    bound0 = _score_bound(att_q_norm[0], att_k_norm[0])
    att_lat = _attention("dense", aq, ak, avt, bound0, n_batch, seq, lc, ATT_HEADS, ATT_KV_HEADS)
    att_ctx = _attention("ctx", aq, ak, avt, bound0, n_batch, seq, lc, ATT_HEADS, ATT_KV_HEADS)
    router = _router_weights(router_group_w[0], router_group_b[0], router_expert_w[0], router_expert_b[0])
    tt = TOK_TILE
    nlt = geom[0]
    gn = row2(gla_out_norm[0])
    w_out0 = _bf(ev_w_out[0])
    hv, ha = GLA_HEADS * GLA_DV, ATT_HEADS * HEAD_DIM
    x_mid, hrow, rt, counts = _out_proj(
        functools.partial(_out_even_kernel, geom),
        (o_lat, o_ctx, gr, att_lat, att_ctx, gn, w_out0, x_lat, x_ctx),
        _lat_ctx_specs((tt, hv), nlt, lead=(2,)) + [pl.BlockSpec((tt, hv), lambda t: (t, 0))]
        + _lat_ctx_specs((tt, ha), nlt) + [_full(gn.shape), _full(w_out0.shape)] + _lat_ctx_specs((tt, d), nlt),
        d, mod[0], row2(norm_ffn[0]), router, n_all, "out_even")
    experts = _expert_weights(exp_w_gate, exp_w_up, exp_w_down)
    f0, pos0 = _moe(hrow, rt, counts, experts, 0)

    w_odd = od_w_in[0]
    x1, q1, k1, v1t = _proj_odd(
        x_mid, f0, pos0, rt, mod[0], mod[1], row2(norm_mix[1]), _bf(w_odd[:, 0:d + LANES]), _bf(w_odd[:, d + LANES:].T),
        tile_gain(swa_q_norm[0], SWA_HEADS), tile_gain(swa_k_norm[0], SWA_KV_HEADS), tables, bd, geom)
    att1 = _attention("window", q1, k1, v1t, _score_bound(swa_q_norm[0], swa_k_norm[0]), n_batch, seq, lc,
                      SWA_HEADS, SWA_KV_HEADS, sink=swa_sink[0])
    router = _router_weights(router_group_w[1], router_group_b[1], router_expert_w[1], router_expert_b[1])
    w_out1 = _bf(od_w_out[0])
    tok = lambda w_: pl.BlockSpec((tt, w_), lambda t: (t, 0))
    x2, hrow1, rt1, counts1 = _out_proj(
        functools.partial(_out_odd_kernel, geom), (att1, w_out1, x1),
        [tok(SWA_HEADS * HEAD_DIM), _full(w_out1.shape), tok(d)],
        d, mod[1], row2(norm_ffn[1]), router, n_lat, "out_odd")
    f1, pos1 = _moe(hrow1, rt1, counts1, experts, 1)
    out = _final(x2, f1, pos1, rt1, mod[1], tiles_per_batch)
    return out.reshape(n_batch, seq, d)
```

```python
import functools

import numpy as np
import jax
import jax.numpy as jnp
from jax import lax
from jax.experimental import pallas as pl
from jax.experimental.pallas import tpu as pltpu

F32 = jnp.float32
BF16 = jnp.bfloat16

GRID_W = 64
HEAD_DIM = 64
AXIS_DIM = HEAD_DIM // 2
ROPE_THETA = 10000.0
EPS = 1e-6
GLA_HEADS = 4
GLA_DK = 64
GLA_DV = 128
GLA_GATE_RANK = 16
GLA_GATE_NORM = 16.0
GLA_CHUNK = 64
ATT_HEADS = 8
ATT_KV_HEADS = 2
SWA_HEADS = 16
SWA_KV_HEADS = 2
SWA_WINDOW = 128
MOE_GROUPS = 4
MOE_EPG = 4
D_EXPERT = 256
N_PAIRS = 6
N_BUCKETS = MOE_GROUPS * N_PAIRS

LANES = 128
MXU_DIM = 256
TOK_TILE = 512
ATT_Q_TILE = 128
ATT_K_CHUNK = 512
ATT_COL = 4 * HEAD_DIM
WIN_SUB = 8
DENSE_SUB = 4
MOE_TILE = 256
DISPATCH_TILE = 1024
VMEM_LIMIT = 56 * 1024 * 1024
NEG_BIG = -1e30
LOG2E = 1.4426950408889634
Q_SCALE = HEAD_DIM ** -0.5 * LOG2E


def _bf(x):
    return x.astype(BF16)


def _split2(x):
    hi = _bf(x)
    lo = _bf(x - hi.astype(F32))
    return hi, lo


def _dot(a, b):
    return jnp.dot(a, b, preferred_element_type=F32)


def _dot_nt(a, b):
    return lax.dot_general(a, b, (((1,), (1,)), ((), ())), preferred_element_type=F32)


def _silu(x):
    return x / (1.0 + jnp.exp(-x))


def _rms(x):
    return x * lax.rsqrt(jnp.mean(x * x, axis=-1, keepdims=True) + EPS)


def _cparams(sem):
    return pltpu.CompilerParams(dimension_semantics=sem, vmem_limit_bytes=VMEM_LIMIT)


def _full(shape):
    n = len(shape)
    return pl.BlockSpec(shape, lambda *_: (0,) * n)


def _mod_kernel(c_ref, w_ref, b_ref, o_ref):
    c = c_ref[...]
    ch, cl = _split2(_silu(c))
    wh, wl = _split2(w_ref[0])
    o_ref[0] = _dot(ch, wh) + _dot(ch, wl) + _dot(cl, wh) + b_ref[0]


def _modulation(c_rows, mod_w, mod_b):
    depth, d, n = mod_w.shape
    tn = n // 4
    return pl.pallas_call(
        _mod_kernel,
        out_shape=jax.ShapeDtypeStruct((depth, 16, n), F32),
        grid=(depth, n // tn),
        in_specs=[pl.BlockSpec((16, d), lambda i, j: (0, 0)),
                  pl.BlockSpec((1, d, tn), lambda i, j: (i, 0, j)),
                  pl.BlockSpec((1, 1, tn), lambda i, j: (i, 0, j))],
        out_specs=pl.BlockSpec((1, 16, tn), lambda i, j: (i, 0, j)),
        compiler_params=_cparams(("arbitrary", "arbitrary")),
        name="modulation",
    )(c_rows, mod_w, mod_b.reshape(depth, 1, n))


def _mod_row(t, n_lat_tiles, tiles_per_batch, n_batch):
    return jnp.where(t < n_lat_tiles, t // tiles_per_batch, n_batch)


def _modulated(x, gain, mod_ref, row, k_shift, k_scale):
    d = x.shape[-1]
    shift = mod_ref[pl.ds(row, 1), k_shift * d:(k_shift + 1) * d]
    scale = mod_ref[pl.ds(row, 1), k_scale * d:(k_scale + 1) * d]
    return _rms(x) * gain * (1.0 + scale) + shift


def _rope_tables(seq):
    rows = seq // GRID_W
    row = np.repeat(np.arange(rows), GRID_W)
    col = np.tile(np.arange(GRID_W), rows)
    inv_freq = ROPE_THETA ** (-np.arange(0, AXIS_DIM, 2, dtype=np.float64) / AXIS_DIM)
    ang = np.stack([row[:, None] * inv_freq, col[:, None] * inv_freq], axis=1)
    cos, sin = np.cos(ang), np.sin(ang)
    zero = np.zeros_like(sin)
    cos64 = np.concatenate([cos[:, 0], cos[:, 0], cos[:, 1], cos[:, 1]], axis=-1)
    sa64 = np.concatenate([-sin[:, 0], zero[:, 0], -sin[:, 1], zero[:, 1]], axis=-1)
    sb64 = np.concatenate([zero[:, 0], sin[:, 0], zero[:, 1], sin[:, 1]], axis=-1)

    def widen(t, fill):
        t = np.concatenate([t, t], axis=-1)
        return jnp.asarray(np.concatenate([t, np.full((TOK_TILE, LANES), fill)], axis=0), F32)

    return widen(cos64, 1.0), widen(sa64, 0.0), widen(sb64, 0.0)


def _head_sumsq(y, bd):
    w = y.shape[-1]
    outs = []
    for s in range(0, w, MXU_DIM):
        e = min(s + MXU_DIM, w)
        hi, lo = _split2(y[:, s:e])
        b = bd[0:e - s, 0:e - s]
        outs.append(_dot(hi, b) + _dot(lo, b))
    return outs[0] if len(outs) == 1 else jnp.concatenate(outs, axis=-1)


def _qk_norm_rope(z, gain, bd, cos, sa, sb):
    w = z.shape[-1]
    rep = w // LANES
    ss = _head_sumsq(z * z, bd)
    y = z * lax.rsqrt(ss * (1.0 / HEAD_DIM) + EPS) * gain

    def wide(t):
        return t if rep == 1 else jnp.concatenate([t] * rep, axis=-1)

    return (y * wide(cos) + pltpu.roll(y, w - AXIS_DIM // 2, 1) * wide(sa)
            + pltpu.roll(y, AXIS_DIM // 2, 1) * wide(sb))


def _kv_rep(kv128):
    lane = lax.broadcasted_iota(jnp.int32, kv128.shape, 1)
    sw = pltpu.roll(kv128, HEAD_DIM, 1)
    a0 = jnp.where(lane < HEAD_DIM, kv128, sw)
    a1 = jnp.where(lane < HEAD_DIM, sw, kv128)
    return jnp.concatenate([a0, a0, a1, a1], axis=-1)


def _rope_block(t, n_lat_tiles, tiles_per_batch):
    return jnp.where(t < n_lat_tiles, t % tiles_per_batch, tiles_per_batch)


EV_GQ, EV_GK, EV_GV, EV_GR, EV_AQ, EV_AK, EV_LR, EV_END = 0, 256, 512, 1024, 1536, 2048, 2176, 2304


def _proj_even_kernel(geom, xl_ref, xc_ref, mod_ref, gain_ref, w_ref, wvt_ref, gw_ref, gb_ref, qg_ref, kg_ref,
                      cos_ref, sa_ref, sb_ref, bd_ref,
                      gq_ref, gk_ref, gv_ref, gr_ref, g_ref, aq_ref, ak_ref, avt_ref):
    t = pl.program_id(0)
    row = _mod_row(t, *geom)
    x = jnp.where(t < geom[0], xl_ref[...], xc_ref[...])
    hb = _bf(_modulated(x, gain_ref[...], mod_ref, row, 0, 1))

    def seg(a, b):
        return _dot(hb, w_ref[:, a:b])

    gq_ref[...] = seg(EV_GQ, EV_GK) * (GLA_DK ** -0.5)
    gk_ref[...] = seg(EV_GK, EV_GV)
    gv_ref[...] = _bf(seg(EV_GV, EV_GR))
    gr_ref[...] = seg(EV_GR, EV_AQ)
    zg = _dot(_bf(seg(EV_LR, EV_END)), gw_ref[...]) + gb_ref[...]
    g_ref[...] = -(jnp.maximum(-zg, 0.0) + jnp.log1p(jnp.exp(-jnp.abs(zg)))) * (1.0 / GLA_GATE_NORM)
    bd = bd_ref[...]
    cos, sa, sb = cos_ref[...], sa_ref[...], sb_ref[...]
    aq = _qk_norm_rope(seg(EV_AQ, EV_AK), qg_ref[...], bd, cos, sa, sb)
    aq_ref[...] = _bf(aq * Q_SCALE)
    ak = _qk_norm_rope(seg(EV_AK, EV_LR), kg_ref[...], bd, cos, sa, sb)
    ak_ref[...] = _bf(_kv_rep(ak))
    avt_ref[0] = _bf(_dot_nt(wvt_ref[...], hb))


def _vt_spec():
    return pl.BlockSpec((1, LANES, TOK_TILE), lambda t: (t, 0, 0))


def _lat_ctx_specs(block, n_lat_tiles, lead=()):
    z = (0,) * len(lead)
    lat = pl.BlockSpec(lead + block, lambda t: z + (jnp.minimum(t, n_lat_tiles - 1), 0))
    ctx = pl.BlockSpec(lead + block, lambda t: z + (jnp.maximum(t - n_lat_tiles, 0), 0))
    return [lat, ctx]


def _proj_even(x_lat, x_ctx, mod, gain, w, wvt, gw, gb, qg, kg, tables, bd, geom):
    d = x_lat.shape[1]
    n = x_lat.shape[0] + x_ctx.shape[0]
    n_lat_tiles, tiles_per_batch, _ = geom
    tt = TOK_TILE
    cos, sa, sb = tables
    tok = lambda w_: pl.BlockSpec((tt, w_), lambda t: (t, 0))
    rope = pl.BlockSpec((tt, LANES), lambda t: (_rope_block(t, n_lat_tiles, tiles_per_batch), 0))
    outs = [(256, F32), (256, F32), (512, BF16), (512, F32), (512, F32), (512, BF16), (512, BF16)]
    return pl.pallas_call(
        functools.partial(_proj_even_kernel, geom),
        out_shape=[jax.ShapeDtypeStruct((n, w_), dt) for w_, dt in outs]
        + [jax.ShapeDtypeStruct((n // tt, LANES, tt), BF16)],
        grid=(n // tt,),
        in_specs=_lat_ctx_specs((tt, d), n_lat_tiles)
        + [_full(mod.shape), _full(gain.shape), _full(w.shape), _full(wvt.shape), _full(gw.shape),
           _full(gb.shape), _full(qg.shape), _full(kg.shape), rope, rope, rope, _full(bd.shape)],
        out_specs=[tok(w_) for w_, _ in outs] + [_vt_spec()],
        compiler_params=_cparams(("arbitrary",)),
        name="proj_even",
    )(x_lat, x_ctx, mod, gain, w, wvt, gw, gb, qg, kg, cos, sa, sb, bd)


def _proj_odd_kernel(geom, x_ref, pos_ref, pos_next_ref, rt_ref, fs_hbm, mod_prev_ref, mod_ref, gain_ref, w_ref,
                     wvt_ref, qg_ref, kg_ref, cos_ref, sa_ref, sb_ref, bd_ref, x1_ref, q_ref, k_ref, vt_ref,
                     fbuf, fsem):
    t = pl.program_id(0)
    row = _mod_row(t, *geom)
    d = x_ref.shape[-1]
    gate = mod_prev_ref[pl.ds(row, 1), 5 * d:6 * d]
    f = _gather_tile(t, pl.num_programs(0), pos_ref, pos_next_ref, rt_ref, fs_hbm, fbuf, fsem,
                     inline_prefetch=True)
    x1 = x_ref[...] + gate * f
    x1_ref[...] = x1
    hb = _bf(_modulated(x1, gain_ref[...], mod_ref, row, 0, 1))
    bd = bd_ref[...]
    cos, sa, sb = cos_ref[...], sa_ref[...], sb_ref[...]
    q = _qk_norm_rope(_dot(hb, w_ref[:, 0:d]), qg_ref[...], bd, cos, sa, sb)
    q_ref[...] = _bf(q * Q_SCALE)
    k = _qk_norm_rope(_dot(hb, w_ref[:, d:d + LANES]), kg_ref[...], bd, cos, sa, sb)
    k_ref[...] = _bf(_kv_rep(k))
    vt_ref[0] = _bf(_dot_nt(wvt_ref[...], hb))
    _gather_drain(t, pl.num_programs(0), fs_hbm, fbuf, fsem)


def _proj_odd(x_all, f_sorted, pos, rt, mod_prev, mod, gain, w, wvt, qg, kg, tables, bd, geom):
    n, d = x_all.shape
    n_lat_tiles, tiles_per_batch, _ = geom
    tt = TOK_TILE
    cos, sa, sb = tables
    tok = lambda w_: pl.BlockSpec((tt, w_), lambda t: (t, 0))
    rope = pl.BlockSpec((tt, LANES), lambda t: (_rope_block(t, n_lat_tiles, tiles_per_batch), 0))
    outs = [(d, F32), (d, BF16), (512, BF16)]
    return pl.pallas_call(
        functools.partial(_proj_odd_kernel, geom),
        out_shape=[jax.ShapeDtypeStruct((n, w_), dt) for w_, dt in outs]
        + [jax.ShapeDtypeStruct((n // tt, LANES, tt), BF16)],
        grid=(n // tt,),
        in_specs=[tok(d)] + _gather_specs(n // tt)
        + [_full(mod_prev.shape), _full(mod.shape), _full(gain.shape), _full(w.shape),
           _full(wvt.shape), _full(qg.shape), _full(kg.shape), rope, rope, rope, _full(bd.shape)],
        out_specs=[tok(w_) for w_, _ in outs] + [_vt_spec()],
        scratch_shapes=_gather_scratch(d),
        compiler_params=_cparams(("arbitrary",)),
        name="proj_odd",
    )(x_all, pos, pos, rt, f_sorted, mod_prev, mod, gain, w, wvt, qg, kg, cos, sa, sb, bd)


N_LEVELS = 6
GLA_MX_ROWS = (N_LEVELS + 2) * GLA_CHUNK


def _gla_constants():
    c = GLA_CHUNK
    mx = np.zeros((2, GLA_MX_ROWS, c), np.float32)
    pat = np.zeros((2, N_LEVELS + 1, c, GLA_HEADS * c), np.float32)
    r = np.arange(c)
    for lvl in range(N_LEVELS):
        h = 1 << lvl
        ref = (r // (2 * h)) * 2 * h + h - 1
        upper = (r % (2 * h)) >= h
        m = np.zeros((c, c), np.float32)
        for i in range(c):
            if upper[i]:
                m[i, ref[i] + 1:i + 1] = 1.0
            else:
                m[i, i + 1:ref[i] + 1] = 1.0
        mx[0, lvl * c:(lvl + 1) * c] = m
        same = (r[:, None] // (2 * h)) == (r[None, :] // (2 * h))
        p = same & upper[:, None] & (~upper)[None, :]
        pat[0, lvl] = np.tile(p.astype(np.float32), (1, GLA_HEADS))
    mx[0, N_LEVELS * c:(N_LEVELS + 1) * c] = (r[None, :] <= r[:, None])
    mx[0, (N_LEVELS + 1) * c:(N_LEVELS + 2) * c] = (r[None, :] > r[:, None])
    pat[0, N_LEVELS] = np.tile(np.eye(c, dtype=np.float32), (1, GLA_HEADS))
    for k in range(N_LEVELS + 2):
        mx[1, k * c:(k + 1) * c] = mx[0, k * c:(k + 1) * c][::-1, ::-1]
    for k in range(N_LEVELS + 1):
        pat[1, k] = np.tile(pat[0, k, :, 0:c][::-1, ::-1], (1, GLA_HEADS))
    return mx, pat


GLA_BLOCK = 1024
GLA_GROUP = 4


def _gla_chunks(chunks, mx, pat_ref, s_ref):
    c = GLA_CHUNK
    lane_head = lax.broadcasted_iota(jnp.int32, (c, GLA_HEADS * GLA_DK), 1) // GLA_DK

    def stack_heads(a):
        return jnp.concatenate([_bf(jnp.where(lane_head == h, a, 0.0)) for h in range(GLA_HEADS)], axis=0)

    xs = [_dot(mx, jnp.concatenate(_split2(g), axis=0)) for _, _, _, g in chunks]
    atts = [jnp.where(pat_ref[0, N_LEVELS] > 0.0, _dot_nt(_bf(q), stack_heads(k)), 0.0)
            for q, k, _, _ in chunks]
    for lvl in range(N_LEVELS):
        for j, (q, k, _, _) in enumerate(chunks):
            e = jnp.exp(xs[j][lvl * c:(lvl + 1) * c])
            atts[j] = atts[j] + jnp.where(pat_ref[0, lvl] > 0.0, _dot_nt(_bf(q * e), stack_heads(k * e)), 0.0)
    outs, qes, news, a_cols = [], [], [], []
    for j, (q, k, v, _) in enumerate(chunks):
        vhead = lax.broadcasted_iota(jnp.int32, v.shape, 1) // GLA_DV
        v_bd = jnp.concatenate([jnp.where(vhead == h, v, jnp.zeros_like(v)) for h in range(GLA_HEADS)], axis=0)
        outs.append(_dot(_bf(atts[j]), v_bd))
        bcum = xs[j][N_LEVELS * c:(N_LEVELS + 1) * c]
        brem = xs[j][(N_LEVELS + 1) * c:(N_LEVELS + 2) * c]
        qes.append(stack_heads(q * jnp.exp(bcum)))
        kt = jnp.transpose(k * jnp.exp(brem))
        news.append(jnp.concatenate(
            [_dot(_bf(kt[h * GLA_DK:(h + 1) * GLA_DK]), v[:, h * GLA_DV:(h + 1) * GLA_DV])
             for h in range(GLA_HEADS)], axis=0))
        tot = bcum[0:1] + brem[0:1]
        a_cols.append(jnp.transpose(jnp.exp(jnp.broadcast_to(tot, (8, tot.shape[1]))))[:, 0:1])
    s = s_ref[...]
    for j in range(len(chunks)):
        o_inter = _dot(qes[j], _bf(s))
        outs[j] = outs[j] + jnp.concatenate([o_inter[h * c:(h + 1) * c] for h in range(GLA_HEADS)], axis=-1)
        s = a_cols[j] * s + news[j]
    s_ref[...] = s
    return outs


def _gla_kernel(n_lat_chunks, n_ctx_chunks, ql_ref, kl_ref, vl_ref, gl_ref, qc_ref, kc_ref, vc_ref, gc_ref,
                mx_ref, pat_ref, ol_ref, oc_ref, s_ref):
    d = pl.program_id(1)
    c = GLA_CHUNK
    mx = mx_ref[0]

    def run(n_chunks, q_ref, k_ref, v_ref, g_ref, o_ref):
        group = min(GLA_GROUP, n_chunks)

        def body(i, carry):
            rows = []
            for j in range(group):
                step = i * group + j
                ci = jnp.where(d == 0, step, n_chunks - 1 - step)
                rows.append(pl.ds(pl.multiple_of(ci * c, c), c))
            outs = _gla_chunks([(q_ref[r, :], k_ref[r, :], v_ref[r, :], g_ref[r, :]) for r in rows],
                               mx, pat_ref, s_ref)
            for r, o in zip(rows, outs):
                o_ref[0, r, :] = o
            return carry
        lax.fori_loop(0, n_chunks // group, body, 0)

    @pl.when(pl.program_id(2) == 0)
    def _():
        s_ref[...] = jnp.zeros_like(s_ref)
        run(n_ctx_chunks, qc_ref, kc_ref, vc_ref, gc_ref, oc_ref)

    run(n_lat_chunks, ql_ref, kl_ref, vl_ref, gl_ref, ol_ref)


def _gla(gq, gk, gv, g, n_batch, seq, lc):
    mx_np, pat_np = _gla_constants()
    mx = jnp.asarray(np.concatenate([mx_np, mx_np], axis=2), BF16)
    pat = jnp.asarray(pat_np, F32)
    ctx0 = n_batch * seq // lc
    hk, hv = GLA_HEADS * GLA_DK, GLA_HEADS * GLA_DV
    blk = min(GLA_BLOCK, seq)
    nb = seq // blk
    for rows_ in (blk, lc):
        assert rows_ % (GLA_CHUNK * min(GLA_GROUP, rows_ // GLA_CHUNK)) == 0
    assert seq % blk == 0
    row = lambda b, d_, i: b * nb + jnp.where(d_ == 0, i, nb - 1 - i)
    lat = lambda w_, col: pl.BlockSpec((blk, w_), lambda b, d_, i: (row(b, d_, i), col(d_)))
    ctx = lambda w_, col: pl.BlockSpec((lc, w_), lambda b, d_, i: (ctx0 + b, col(d_)))
    zero = lambda d_: 0
    same = lambda d_: d_
    return pl.pallas_call(
        functools.partial(_gla_kernel, blk // GLA_CHUNK, lc // GLA_CHUNK),
        out_shape=[jax.ShapeDtypeStruct((2, n_batch * seq, hv), F32),
                   jax.ShapeDtypeStruct((2, n_batch * lc, hv), F32)],
        grid=(n_batch, 2, nb),
        in_specs=[lat(hk, zero), lat(hk, zero), lat(hv, zero), lat(hk, same),
                  ctx(hk, zero), ctx(hk, zero), ctx(hv, zero), ctx(hk, same),
                  pl.BlockSpec((1,) + mx.shape[1:], lambda b, d_, i: (d_, 0, 0)),
                  pl.BlockSpec((1,) + pat.shape[1:], lambda b, d_, i: (d_, 0, 0, 0))],
        out_specs=[pl.BlockSpec((1, blk, hv), lambda b, d_, i: (d_, row(b, d_, i), 0)),
                   pl.BlockSpec((1, lc, hv), lambda b, d_, i: (d_, b, 0))],
        scratch_shapes=[pltpu.VMEM((GLA_HEADS * GLA_DK, GLA_DV), F32)],
        compiler_params=_cparams(("arbitrary", "arbitrary", "arbitrary")),
        name="gla_scan",
    )(gq, gk, gv, g, gq, gk, gv, g, mx, pat)


def _stack_heads(q):
    lane_head = lax.broadcasted_iota(jnp.int32, q.shape, 1) // HEAD_DIM
    return jnp.concatenate([jnp.where(lane_head == h, q, jnp.zeros_like(q)) for h in range(4)], axis=0)


SAFE_SCORE_BOUND = 60.0
ONES_ROWS = 16


def _with_ones(vt):
    return jnp.concatenate([vt, jnp.ones((ONES_ROWS, vt.shape[1]), vt.dtype)], axis=0)


def _attn_store(acc, l, o_ref, u, tq):
    out = acc * (1.0 / l)
    out = jnp.concatenate([out[:, h * tq:(h + 1) * tq] for h in range(4)], axis=0)
    o_ref[u * tq:(u + 1) * tq, :] = _bf(jnp.transpose(out))


def _attn_dense_kernel(n_chunks, n_sub, *refs):
    if n_chunks:
        bound_ref, q_ref, kl_ref, vtl_ref, kc_ref, vtc_ref, o_ref = refs
    else:
        bound_ref, q_ref, kc_ref, vtc_ref, o_ref = refs
    tq = q_ref.shape[0] // n_sub
    cols = 4 * tq
    q4 = [_stack_heads(q_ref[u * tq:(u + 1) * tq, :]) for u in range(n_sub)]

    def scores(c, u):
        if c < n_chunks:
            return _dot_nt(kl_ref[c * ATT_K_CHUNK:(c + 1) * ATT_K_CHUNK, :], q4[u])
        return _dot_nt(kc_ref[...], q4[u])

    def update(carry, st, vt_aug):
        m, acc = carry
        m_new = jnp.maximum(m, jnp.max(st, axis=0, keepdims=True))
        acc = jnp.exp2(m - m_new) * acc + _dot(vt_aug, _bf(jnp.exp2(st - m_new)))
        return m_new, acc

    def run(fixed_ref):
        if fixed_ref is None:
            carry = [(jnp.full((1, cols), NEG_BIG, F32), jnp.zeros((HEAD_DIM + ONES_ROWS, cols), F32))
                     for _ in range(n_sub)]
        else:
            carry = [jnp.zeros((HEAD_DIM + ONES_ROWS, cols), F32) for _ in range(n_sub)]
        lookahead = fixed_ref is None
        st = [scores(0, u) for u in range(n_sub)]
        for c in range(n_chunks + 1):
            st_next = [scores(c + 1, u) for u in range(n_sub)] if (lookahead and c < n_chunks) else None
            vt_aug = _with_ones(vtl_ref[c] if c < n_chunks else vtc_ref[0])
            if fixed_ref is None:
                carry = [update(carry[u], st[u], vt_aug) for u in range(n_sub)]
            else:
                carry = [carry[u] + _dot(vt_aug, _bf(jnp.exp2(st[u] - fixed_ref))) for u in range(n_sub)]
            if not lookahead and c < n_chunks:
                st_next = [scores(c + 1, u) for u in range(n_sub)]
            st = st_next
        for u in range(n_sub):
            acc = carry[u][1] if fixed_ref is None else carry[u]
            _attn_store(acc[0:HEAD_DIM], acc[HEAD_DIM:HEAD_DIM + 1], o_ref, u, tq)

    bound = bound_ref[0]

    @pl.when(bound <= SAFE_SCORE_BOUND)
    def _():
        run(bound)

    @pl.when(bound > SAFE_SCORE_BOUND)
    def _():
        run(None)


def _attn_window_kernel(seq, bound_ref, q_ref, *refs):
    nk = WIN_SUB + 2
    k_refs, v_refs = refs[0:nk], refs[nk:2 * nk]
    kc_ref, vtc_ref, sink_ref, o_ref = refs[2 * nk:]
    tq = SWA_WINDOW
    i = pl.program_id(2)
    kb = jnp.concatenate([r[...] for r in k_refs], axis=0)
    vtb = _with_ones(jnp.concatenate([r[0] for r in v_refs], axis=1))
    kc, vtc, sink = kc_ref[...], _with_ones(vtc_ref[0]), sink_ref[0]
    span = 3 * tq
    r_i = lax.broadcasted_iota(jnp.int32, (tq, tq), 0)
    c_i = lax.broadcasted_iota(jnp.int32, (tq, tq), 1)
    band_lo = jnp.where(r_i >= c_i, 0.0, NEG_BIG)
    band_hi = jnp.where(r_i <= c_i, 0.0, NEG_BIG)
    n_blocks = seq // tq

    def run(use_bound):
        sb, sc = [], []
        for u in range(WIN_SUB):
            q4 = _stack_heads(q_ref[u * tq:(u + 1) * tq, :])
            first = i * WIN_SUB + u - 1
            bias_lo = jnp.where(first < 0, NEG_BIG, band_lo)
            bias_hi = jnp.where(first + 2 >= n_blocks, NEG_BIG, band_hi)
            s = _dot_nt(kb[u * tq:u * tq + span], q4)
            sb.append(jnp.concatenate([s[0:tq] + jnp.concatenate([bias_lo] * 4, axis=1), s[tq:2 * tq],
                                       s[2 * tq:span] + jnp.concatenate([bias_hi] * 4, axis=1)], axis=0))
            sc.append(_dot_nt(kc, q4))
        ms, pbs, pcs = [], [], []
        for u in range(WIN_SUB):
            if use_bound:
                m = jnp.maximum(bound_ref[0], sink)
            else:
                m = jnp.maximum(jnp.maximum(jnp.max(sb[u], axis=0, keepdims=True),
                                            jnp.max(sc[u], axis=0, keepdims=True)), sink)
            ms.append(m)
            pbs.append(_bf(jnp.exp2(sb[u] - m)))
            pcs.append(_bf(jnp.exp2(sc[u] - m)))
        for u in range(WIN_SUB):
            acc = _dot(vtb[:, u * tq:u * tq + span], pbs[u]) + _dot(vtc, pcs[u])
            l = acc[HEAD_DIM:HEAD_DIM + 1] + jnp.exp2(sink - ms[u])
            _attn_store(acc[0:HEAD_DIM], l, o_ref, u, tq)

    @pl.when(bound_ref[0] <= SAFE_SCORE_BOUND)
    def _():
        run(True)

    @pl.when(bound_ref[0] > SAFE_SCORE_BOUND)
    def _():
        run(False)


def _score_bound(q_gain, k_gain):
    return (HEAD_DIM * Q_SCALE * 1.02 * jnp.max(jnp.abs(q_gain)) * jnp.max(jnp.abs(k_gain))).reshape(1)


def _attention(mode, q, k_rep, vt, bound, n_batch, seq, lc, n_heads, n_kv, sink=None):
    ncol = n_heads * HEAD_DIM // ATT_COL
    col_per_kv = ncol // n_kv
    tq = ATT_Q_TILE
    tpb = seq // TOK_TILE
    n_lat_tiles = n_batch * tpb
    ctx_per_tile = TOK_TILE // lc
    ctx0 = n_batch * seq // lc
    assert ATT_K_CHUNK == TOK_TILE and TOK_TILE % lc == 0 and tq == SWA_WINDOW
    kv = lambda j: j // col_per_kv
    k_ctx = pl.BlockSpec((lc, ATT_COL), lambda b, j, i: (ctx0 + b, kv(j)))
    vt_ctx = pl.BlockSpec((1, HEAD_DIM, lc), lambda b, j, i: (n_lat_tiles + b // ctx_per_tile, kv(j), b % ctx_per_tile))
    if mode == "ctx":
        nq = lc // tq
        q0 = n_batch * seq // tq
        kern = functools.partial(_attn_dense_kernel, 0, 1)
        args = (q, k_rep, vt)
        in_specs = [pl.BlockSpec((tq, ATT_COL), lambda b, j, i: (q0 + b * nq + i, j)), k_ctx, vt_ctx]
    elif mode == "dense":
        tq = DENSE_SUB * ATT_Q_TILE
        nq = seq // tq
        kern = functools.partial(_attn_dense_kernel, tpb, DENSE_SUB)
        args = (q, k_rep, vt, k_rep, vt)
        in_specs = [pl.BlockSpec((tq, ATT_COL), lambda b, j, i: (b * nq + i, j)),
                    pl.BlockSpec((seq, ATT_COL), lambda b, j, i: (b, kv(j))),
                    pl.BlockSpec((tpb, HEAD_DIM, TOK_TILE), lambda b, j, i: (b, kv(j), 0)), k_ctx, vt_ctx]
    else:
        wb = SWA_WINDOW
        tq = WIN_SUB * wb
        nq = seq // tq
        nkb = seq // wb
        per_tile = TOK_TILE // wb
        kern = functools.partial(_attn_window_kernel, seq)
        nb = lambda i, o: jnp.clip(i * WIN_SUB + o, 0, nkb - 1)
        k_nb = lambda o: pl.BlockSpec((wb, ATT_COL), lambda b, j, i: (b * nkb + nb(i, o), kv(j)))
        v_nb = lambda o: pl.BlockSpec(
            (1, HEAD_DIM, wb), lambda b, j, i: (b * tpb + nb(i, o) // per_tile, kv(j), nb(i, o) % per_tile))
        offs = range(-1, WIN_SUB + 1)
        sink_row = jnp.repeat(sink.reshape(ncol, 1, 4), wb, axis=2) * LOG2E
        args = (q,) + (k_rep,) * len(offs) + (vt,) * len(offs) + (k_rep, vt, sink_row)
        in_specs = ([pl.BlockSpec((tq, ATT_COL), lambda b, j, i: (b * nq + i, j))]
                    + [k_nb(o) for o in offs] + [v_nb(o) for o in offs]
                    + [k_ctx, vt_ctx, pl.BlockSpec((1, 1, 4 * wb), lambda b, j, i: (j, 0, 0))])
    args = (bound,) + args
    in_specs = [pl.BlockSpec(memory_space=pltpu.SMEM)] + in_specs
    return pl.pallas_call(
        kern,
        out_shape=jax.ShapeDtypeStruct((n_batch * nq * tq, n_heads * HEAD_DIM), BF16),
        grid=(n_batch, ncol, nq),
        in_specs=in_specs,
        out_specs=pl.BlockSpec((tq, ATT_COL), lambda b, j, i: (b * nq + i, j)),
        compiler_params=_cparams(("arbitrary", "arbitrary", "arbitrary")),
        name="attention_" + mode,
    )(*args)


ROUTE_ROWS = 8
TOKEN_TILE = (8, LANES)
OUT_TOKEN_TILE = (16, LANES)


def _to_token_tiles(x, tile):
    return x.reshape((x.shape[0],) + tile)


def _from_token_tiles(tiles):
    return tiles.reshape(tiles.shape[0], tiles.shape[1] * tiles.shape[2])


def _route(h, wrt_ref, wrt_hi_ref, brt_ref):
    hh, hl = _split2(h)
    a = _dot_nt(wrt_ref[...], hh)
    lt = a[0:LANES] + a[LANES:2 * LANES] + _dot_nt(wrt_hi_ref[...], hl) + brt_ref[...]
    col = lambda i: lt[i:i + 1, :]
    gl = [col(i) for i in range(MOE_GROUPS)]
    gmax = functools.reduce(jnp.maximum, gl)
    gi = jnp.where(gl[0] == gmax, 0, jnp.where(gl[1] == gmax, 1, jnp.where(gl[2] == gmax, 2, 3)))
    g_weight = 1.0 / functools.reduce(lambda a, b: a + b, [jnp.exp(x - gmax) for x in gl])
    el = []
    for j in range(MOE_EPG):
        cand = [col(MOE_GROUPS + g * MOE_EPG + j) for g in range(MOE_GROUPS)]
        el.append(jnp.where(gi == 0, cand[0], jnp.where(gi == 1, cand[1], jnp.where(gi == 2, cand[2], cand[3]))))
    m1 = functools.reduce(jnp.maximum, el)
    i1 = jnp.where(el[0] == m1, 0, jnp.where(el[1] == m1, 1, jnp.where(el[2] == m1, 2, 3)))
    rest = [jnp.where(i1 == j, -jnp.inf, el[j]) for j in range(MOE_EPG)]
    m2 = functools.reduce(jnp.maximum, rest)
    i2 = jnp.where(rest[0] == m2, 0, jnp.where(rest[1] == m2, 1, jnp.where(rest[2] == m2, 2, 3)))
    e2 = jnp.exp(m2 - m1)
    w1 = g_weight / (1.0 + e2)
    w2 = g_weight * e2 / (1.0 + e2)
    lo = jnp.minimum(i1, i2)
    hi = jnp.maximum(i1, i2)
    w_lo = jnp.where(i1 == lo, w1, w2)
    w_hi = jnp.where(i1 == lo, w2, w1)
    pair = jnp.where(lo == 0, hi - 1, jnp.where(lo == 1, hi + 1, N_PAIRS - 1))
    return w_lo, w_hi, gi * N_PAIRS + pair


def _out_tail(geom, m, x, mod_ref, gain_ffn_ref, wrt_ref, wrt_hi_ref, brt_ref, triu_ref,
              x_new_ref, hrow_ref, rt_ref, counts_ref, run_ref):
    t = pl.program_id(0)
    row = _mod_row(t, *geom)
    tt, d = x.shape

    @pl.when(t == 0)
    def _():
        run_ref[...] = jnp.zeros_like(run_ref)

    x_new = x + mod_ref[pl.ds(row, 1), 2 * d:3 * d] * m
    x_new_ref[...] = x_new
    h = _modulated(x_new, gain_ffn_ref[...], mod_ref, row, 3, 4)
    w_lo, w_hi, bucket = _route(h, wrt_ref, wrt_hi_ref, brt_ref)
    onehot = lax.broadcasted_iota(jnp.int32, (LANES, tt), 0) == bucket
    ones = jnp.where(onehot, 1.0, 0.0)
    before = _dot(_bf(ones), triu_ref[...]) + run_ref[...]
    rank = jnp.sum(jnp.where(onehot, before, 0.0), axis=0, keepdims=True)
    run = run_ref[...] + jnp.sum(ones, axis=1, keepdims=True)
    run_ref[...] = run
    counts_ref[...] = jnp.broadcast_to(run, counts_ref.shape)
    rec = jnp.concatenate([w_lo, w_hi, bucket.astype(F32), rank, jnp.zeros((ROUTE_ROWS - 4, tt), F32)], axis=0)
    rt_ref[0] = rec
    hrow_ref[...] = _to_token_tiles(h, TOKEN_TILE)


def _pick(t, n_lat_tiles, lat_ref, ctx_ref):
    return jnp.where(t < n_lat_tiles, lat_ref[...], ctx_ref[...])


def _out_even_kernel(geom, ol_ref, oc_ref, r_ref, attl_ref, attc_ref, gn_ref, w_ref, xl_ref, xc_ref, mod_ref,
                     gain_ffn_ref, wrt_ref, wrt_hi_ref, brt_ref, triu_ref,
                     x_new_ref, hrow_ref, rt_ref, counts_ref, run_ref):
    t = pl.program_id(0)
    o2 = _pick(t, geom[0], ol_ref, oc_ref)
    o = o2[0] + o2[1]
    r = r_ref[...]
    parts = []
    for h in range(GLA_HEADS):
        sl = slice(h * GLA_DV, (h + 1) * GLA_DV)
        parts.append(_rms(o[:, sl]) * gn_ref[...] * _silu(r[:, sl]))
    a = _bf(jnp.concatenate(parts, axis=-1))
    half = a.shape[-1]
    m = _dot(a, w_ref[0:half, :]) + _dot(_pick(t, geom[0], attl_ref, attc_ref), w_ref[half:, :])
    _out_tail(geom, m, _pick(t, geom[0], xl_ref, xc_ref), mod_ref, gain_ffn_ref, wrt_ref, wrt_hi_ref, brt_ref,
              triu_ref, x_new_ref, hrow_ref, rt_ref, counts_ref, run_ref)


def _out_odd_kernel(geom, att_ref, w_ref, x_ref, mod_ref, gain_ffn_ref, wrt_ref, wrt_hi_ref, brt_ref, triu_ref,
                    x_new_ref, hrow_ref, rt_ref, counts_ref, run_ref):
    m = _dot(att_ref[...], w_ref[...])
    _out_tail(geom, m, x_ref[...], mod_ref, gain_ffn_ref, wrt_ref, wrt_hi_ref, brt_ref, triu_ref,
              x_new_ref, hrow_ref, rt_ref, counts_ref, run_ref)


def _out_proj(kernel, lead_args, lead_specs, d, mod, gain_ffn, router, n_rows, name):
    tt = TOK_TILE
    tok = lambda w_: pl.BlockSpec((tt, w_), lambda t: (t, 0))
    r = np.arange(tt)
    triu = jnp.asarray((r[:, None] < r[None, :]).astype(np.float32), BF16)
    wrt, wrt_hi, br = router
    brt = jnp.broadcast_to(br.reshape(LANES, 1), (LANES, tt))
    return pl.pallas_call(
        kernel,
        out_shape=[jax.ShapeDtypeStruct((n_rows, d), F32),
                   jax.ShapeDtypeStruct((n_rows,) + TOKEN_TILE, F32),
                   jax.ShapeDtypeStruct((n_rows // tt, ROUTE_ROWS, tt), F32),
                   jax.ShapeDtypeStruct((LANES, LANES), F32)],
        grid=(n_rows // tt,),
        in_specs=lead_specs + [_full(mod.shape), _full(gain_ffn.shape), _full(wrt.shape), _full(wrt_hi.shape),
                               _full(brt.shape), _full(triu.shape)],
        out_specs=[tok(d), pl.BlockSpec((tt,) + TOKEN_TILE, lambda t: (t, 0, 0)),
                   pl.BlockSpec((1, ROUTE_ROWS, tt), lambda t: (t, 0, 0)), _full((LANES, LANES))],
        scratch_shapes=[pltpu.VMEM((LANES, 1), F32)],
        compiler_params=_cparams(("arbitrary",)),
        name=name,
    )(*lead_args, mod, gain_ffn, wrt, wrt_hi, brt, triu)


SUBLANES = 8


def _for_each_row(n_rows, start_row_copy):
    def body(g, c):
        base = g * SUBLANES
        for j in range(SUBLANES):
            start_row_copy(base + j, j % 2)
        return c
    lax.fori_loop(0, n_rows // SUBLANES, body, 0)


def _pos_kernel(start_ref, rt_ref, pos_ref):
    bucket = rt_ref[:, 2, :].astype(jnp.int32)
    base = jnp.zeros_like(bucket)
    for b in range(N_BUCKETS):
        base = jnp.where(bucket == b, start_ref[b], base)
    pos_ref[:, 0, :] = base + rt_ref[:, 3, :].astype(jnp.int32)


def _dispatch_kernel(last_ref, pos_ref, x_ref, xs_ref, zbuf, sem, zsem):
    tt = x_ref.shape[0]
    tm = zbuf.shape[0]

    @pl.when(pl.program_id(0) == 0)
    def _():
        zbuf[...] = jnp.zeros_like(zbuf)

        def zero_copy(b):
            return pltpu.make_async_copy(zbuf, xs_ref.at[pl.ds(jnp.maximum(last_ref[b], 0) * tm, tm)], zsem)

        for b in range(2 * N_BUCKETS):
            @pl.when(last_ref[b] >= 0)
            def _():
                zero_copy(b).start()
        for b in range(2 * N_BUCKETS):
            @pl.when(last_ref[b] >= 0)
            def _():
                zero_copy(b).wait()

    _for_each_row(tt, lambda r, prio: pltpu.make_async_copy(
        x_ref.at[r], xs_ref.at[pos_ref[0, 0, r]], sem).start(priority=prio))
    pltpu.make_async_copy(x_ref, xs_ref.at[pl.ds(0, tt)], sem).wait()


def _moe_mlp_kernel(tlo_ref, thi_ref, nused_ref, xs_ref, wg_lo, wu_lo, wd_lo, wg_hi, wu_hi, wd_hi, f_ref):
    del tlo_ref, thi_ref
    used = pl.program_id(0) < nused_ref[0]

    @pl.when(used)
    def _():
        xb = _bf(_from_token_tiles(xs_ref[...]))
        hid_lo = _silu(_dot(xb, wg_lo[0])) * _dot(xb, wu_lo[0])
        hid_hi = _silu(_dot(xb, wg_hi[0])) * _dot(xb, wu_hi[0])
        f2 = jnp.concatenate([_dot(_bf(hid_lo), wd_lo[0]), _dot(_bf(hid_hi), wd_hi[0])], axis=1)
        f_ref[...] = _to_token_tiles(f2, OUT_TOKEN_TILE)

    @pl.when(jnp.logical_not(used))
    def _():
        f_ref[...] = jnp.zeros_like(f_ref)


def _moe(hrow, rt, counts, experts, layer):
    n = hrow.shape[0]
    d = TOKEN_TILE[0] * TOKEN_TILE[1]
    tm, tt = MOE_TILE, TOK_TILE
    n_tiles = n // tm + N_BUCKETS
    p = n_tiles * tm
    cnt = counts[0:N_BUCKETS, 0].astype(jnp.int32)
    tiles_b = (cnt + tm - 1) // tm
    tile_end = jnp.cumsum(tiles_b)
    start_b = (tile_end - tiles_b) * tm
    n_used = tile_end[-1].reshape(1)
    spare = n_used[0] + jnp.arange(N_BUCKETS, dtype=jnp.int32)
    last_tile = jnp.concatenate([jnp.where(tiles_b > 0, tile_end - 1, -1), jnp.where(spare < n_tiles, spare, -1)])
    pos = pl.pallas_call(
        _pos_kernel,
        out_shape=jax.ShapeDtypeStruct((n // tt, 1, tt), jnp.int32),
        grid_spec=pltpu.PrefetchScalarGridSpec(
            num_scalar_prefetch=1, grid=(1,),
            in_specs=[pl.BlockSpec(rt.shape, lambda i, s: (0, 0, 0))],
            out_specs=pl.BlockSpec((n // tt, 1, tt), lambda i, s: (0, 0, 0))),
        compiler_params=_cparams(("arbitrary",)),
        name="moe_positions",
    )(start_b, rt)
    tile_ids = jnp.arange(n_tiles, dtype=jnp.int32)
    tile_bucket = jnp.sum((tile_ids[:, None] >= tile_end[None, :]).astype(jnp.int32), axis=1)
    tile_bucket = jnp.minimum(tile_bucket, jnp.sum((n_used[0] - 1 >= tile_end).astype(jnp.int32)))
    tile_bucket = jnp.minimum(tile_bucket, N_BUCKETS - 1)
    pair_lo = jnp.asarray([0, 0, 0, 1, 1, 2], jnp.int32)
    pair_hi = jnp.asarray([1, 2, 3, 2, 3, 3], jnp.int32)
    grp = tile_bucket // N_PAIRS
    w_gate, w_up, w_down = experts
    first = layer * MOE_GROUPS * MOE_EPG
    t_lo = first + grp * MOE_EPG + pair_lo[tile_bucket % N_PAIRS]
    t_hi = first + grp * MOE_EPG + pair_hi[tile_bucket % N_PAIRS]

    dt = DISPATCH_TILE if n % DISPATCH_TILE == 0 else tt
    xs = pl.pallas_call(
        _dispatch_kernel,
        out_shape=jax.ShapeDtypeStruct((p,) + TOKEN_TILE, F32),
        grid_spec=pltpu.PrefetchScalarGridSpec(
            num_scalar_prefetch=1, grid=(n // dt,),
            in_specs=[pl.BlockSpec((1, 1, dt), lambda t, s: (t, 0, 0), memory_space=pltpu.SMEM),
                      pl.BlockSpec((dt,) + TOKEN_TILE, lambda t, s: (t, 0, 0))],
            out_specs=pl.BlockSpec(memory_space=pl.ANY),
            scratch_shapes=[pltpu.VMEM((tm,) + TOKEN_TILE, F32), pltpu.SemaphoreType.DMA(()),
                            pltpu.SemaphoreType.DMA(())]),
        compiler_params=_cparams(("arbitrary",)),
        name="moe_dispatch",
    )(last_tile, pos.reshape(n // dt, 1, dt), hrow)

    f = D_EXPERT
    up_lo = pl.BlockSpec((1, d, f), lambda t, lo, hi, nu: (lo[t], 0, 0))
    up_hi = pl.BlockSpec((1, d, f), lambda t, lo, hi, nu: (hi[t], 0, 0))
    dn_lo = pl.BlockSpec((1, f, d), lambda t, lo, hi, nu: (lo[t], 0, 0))
    dn_hi = pl.BlockSpec((1, f, d), lambda t, lo, hi, nu: (hi[t], 0, 0))
    grid_spec = pltpu.PrefetchScalarGridSpec(
        num_scalar_prefetch=3,
        grid=(n_tiles,),
        in_specs=[pl.BlockSpec((tm,) + TOKEN_TILE, lambda t, lo, hi, nu: (jnp.minimum(t, nu[0] - 1), 0, 0)),
                  up_lo, up_lo, dn_lo, up_hi, up_hi, dn_hi],
        out_specs=pl.BlockSpec((tm,) + OUT_TOKEN_TILE, lambda t, *_: (t, 0, 0)),
    )
    f_sorted = pl.pallas_call(
        _moe_mlp_kernel,
        out_shape=jax.ShapeDtypeStruct((p,) + OUT_TOKEN_TILE, F32),
        grid_spec=grid_spec,
        compiler_params=_cparams(("arbitrary",)),
        name="moe_experts",
    )(t_lo, t_hi, n_used, xs, w_gate, w_up, w_down, w_gate, w_up, w_down)
    return f_sorted, pos


def _gather_tile(t, n_t, pos_ref, pos_next_ref, rt_ref, src_hbm, buf, sem, inline_prefetch=False):
    tt = buf.shape[1]

    def start(p_ref, slot):
        _for_each_row(tt, lambda r, prio: pltpu.make_async_copy(
            src_hbm.at[p_ref[0, 0, r]], buf.at[slot].at[r], sem.at[slot]).start(priority=prio))

    slot = t % 2

    def wait(s):
        pltpu.make_async_copy(src_hbm.at[pl.ds(0, tt)], buf.at[s], sem.at[s]).wait()

    @pl.when(t == 0)
    def _():
        start(pos_ref, 0)

    wait(slot)
    if inline_prefetch:
        for r in range(tt):
            pltpu.make_async_copy(src_hbm.at[pos_next_ref[0, 0, r]], buf.at[1 - slot].at[r],
                                  sem.at[1 - slot]).start(priority=r % 2)
    else:
        @pl.when(t + 1 < n_t)
        def _():
            start(pos_next_ref, 1 - slot)

    f2 = _from_token_tiles(buf[slot])
    d = f2.shape[1] // 2
    w = jnp.transpose(rt_ref[0])
    return w[:, 0:1] * f2[:, 0:d] + w[:, 1:2] * f2[:, d:2 * d]


def _gather_drain(t, n_t, src_hbm, buf, sem):
    tt = buf.shape[1]

    @pl.when(t == n_t - 1)
    def _():
        pltpu.make_async_copy(src_hbm.at[pl.ds(0, tt)], buf.at[1 - t % 2], sem.at[1 - t % 2]).wait()


def _gather_specs(n_t):
    tt = TOK_TILE
    return [pl.BlockSpec((1, 1, tt), lambda t: (t, 0, 0), memory_space=pltpu.SMEM),
            pl.BlockSpec((1, 1, tt), lambda t: (jnp.minimum(t + 1, n_t - 1), 0, 0), memory_space=pltpu.SMEM),
            pl.BlockSpec((1, ROUTE_ROWS, tt), lambda t: (t, 0, 0)),
            pl.BlockSpec(memory_space=pl.ANY)]


def _gather_scratch(d):
    assert 2 * d == OUT_TOKEN_TILE[0] * OUT_TOKEN_TILE[1]
    return [pltpu.VMEM((2, TOK_TILE) + OUT_TOKEN_TILE, F32), pltpu.SemaphoreType.DMA((2,))]


def _final_kernel(tiles_per_batch, x_ref, pos_ref, pos_next_ref, rt_ref, fs_hbm, mod_ref, o_ref, fbuf, fsem):
    t = pl.program_id(0)
    row = t // tiles_per_batch
    d = x_ref.shape[-1]
    f = _gather_tile(t, pl.num_programs(0), pos_ref, pos_next_ref, rt_ref, fs_hbm, fbuf, fsem)
    o_ref[...] = x_ref[...] + mod_ref[pl.ds(row, 1), 5 * d:6 * d] * f


def _final(x_lat, f_sorted, pos, rt, mod, tiles_per_batch):
    n, d = x_lat.shape
    tok = pl.BlockSpec((TOK_TILE, d), lambda t: (t, 0))
    return pl.pallas_call(
        functools.partial(_final_kernel, tiles_per_batch),
        out_shape=jax.ShapeDtypeStruct((n, d), F32),
        grid=(n // TOK_TILE,),
        in_specs=[tok] + _gather_specs(n // TOK_TILE) + [_full(mod.shape)],
        out_specs=tok,
        scratch_shapes=_gather_scratch(d),
        compiler_params=_cparams(("arbitrary",)),
        name="final_residual",
    )(x_lat, pos, pos, rt, f_sorted, mod)


def _block_diag_ones():
    r = np.arange(MXU_DIM) // HEAD_DIM
    return jnp.asarray((r[:, None] == r[None, :]).astype(np.float32), BF16)


def _router_weights(wg, bg, we, be):
    d = wg.shape[0]
    n = MOE_GROUPS + MOE_GROUPS * MOE_EPG
    wt = jnp.concatenate([wg, we, jnp.zeros((d, LANES - n), F32)], axis=1).T
    b = jnp.concatenate([bg, be, jnp.zeros((LANES - n,), F32)])
    hi = _bf(wt)
    lo = _bf(wt - hi.astype(F32))
    return jnp.concatenate([hi, lo], axis=0), hi, b


def _expert_weights(w_gate, w_up, w_down):
    l, g, e, d, f = w_gate.shape
    n = l * g * e
    return _bf(w_gate).reshape(n, d, f), _bf(w_up).reshape(n, d, f), _bf(w_down).reshape(n, f, d)


def kernel(x, c, ctx, c_ctx, mod_w, mod_b, norm_mix, norm_ffn, ev_w_in, ev_w_out, gla_gate_w, gla_gate_b,
           gla_out_norm, att_q_norm, att_k_norm, od_w_in, od_w_out, swa_sink, swa_q_norm, swa_k_norm,
           router_group_w, router_group_b, router_expert_w, router_expert_b, exp_w_gate, exp_w_up, exp_w_down):
    n_batch, seq, d = x.shape
    lc = ctx.shape[1]
    depth = mod_w.shape[0]
    n_lat = n_batch * seq
    tiles_per_batch = seq // TOK_TILE
    geom = (n_lat // TOK_TILE, tiles_per_batch, n_batch)
    assert depth == 2 and seq % TOK_TILE == 0 and (n_batch * lc) % TOK_TILE == 0 and n_batch < 16

    x_lat, x_ctx = x.reshape(n_lat, d), ctx.reshape(n_batch * lc, d)
    n_all = n_lat + n_batch * lc
    c_rows = jnp.zeros((16, d), F32).at[:n_batch].set(c).at[n_batch].set(c_ctx)
    mod = _modulation(c_rows, mod_w, mod_b)
    tables = _rope_tables(seq)
    bd = _block_diag_ones()
    row2 = lambda v: v.reshape(1, -1)
    tile_gain = lambda gvec, reps: jnp.tile(gvec, reps).reshape(1, -1)

    w0 = ev_w_in[0]
    seg = np.cumsum([0, 256, 256, 512, 512, 32, 512, 128, 128])
    cols = lambda i: w0[:, seg[i]:seg[i + 1]]
    w_even = _bf(jnp.concatenate([cols(0), cols(1), cols(2), cols(3), cols(5), cols(6), cols(4),
                                  jnp.zeros((d, EV_END - EV_LR - 2 * GLA_GATE_RANK), F32)], axis=1))
    wvt_even = _bf(cols(7).T)
    hk = GLA_HEADS * GLA_DK
    gw = jnp.zeros((LANES, 2 * hk), F32)
    gw = gw.at[0:GLA_GATE_RANK, 0:hk].set(gla_gate_w[0, 0])
    gw = gw.at[GLA_GATE_RANK:2 * GLA_GATE_RANK, hk:2 * hk].set(gla_gate_w[0, 1])
    gb = gla_gate_b[0].reshape(1, 2 * hk)
    gq, gk, gv, gr, g, aq, ak, avt = _proj_even(
        x_lat, x_ctx, mod[0], row2(norm_mix[0]), w_even, wvt_even, _bf(gw), gb,
        tile_gain(att_q_norm[0], ATT_HEADS), tile_gain(att_k_norm[0], ATT_KV_HEADS), tables, bd, geom)
    o_lat, o_ctx = _gla(gq, gk, gv, g, n_batch, seq, lc)
    bound0 = _score_bound(att_q_norm[0], att_k_norm[0])
    att_lat = _attention("dense", aq, ak, avt, bound0, n_batch, seq, lc, ATT_HEADS, ATT_KV_HEADS)
    att_ctx = _attention("ctx", aq, ak, avt, bound0, n_batch, seq, lc, ATT_HEADS, ATT_KV_HEADS)
    router = _router_weights(router_group_w[0], router_group_b[0], router_expert_w[0], router_expert_b[0])
    tt = TOK_TILE
    nlt = geom[0]
    gn = row2(gla_out_norm[0])
    w_out0 = _bf(ev_w_out[0])
    hv, ha = GLA_HEADS * GLA_DV, ATT_HEADS * HEAD_DIM
    x_mid, hrow, rt, counts = _out_proj(
        functools.partial(_out_even_kernel, geom),
        (o_lat, o_ctx, gr, att_lat, att_ctx, gn, w_out0, x_lat, x_ctx),
        _lat_ctx_specs((tt, hv), nlt, lead=(2,)) + [pl.BlockSpec((tt, hv), lambda t: (t, 0))]
        + _lat_ctx_specs((tt, ha), nlt) + [_full(gn.shape), _full(w_out0.shape)] + _lat_ctx_specs((tt, d), nlt),
        d, mod[0], row2(norm_ffn[0]), router, n_all, "out_even")
    experts = _expert_weights(exp_w_gate, exp_w_up, exp_w_down)
    f0, pos0 = _moe(hrow, rt, counts, experts, 0)

    w_odd = od_w_in[0]
    x1, q1, k1, v1t = _proj_odd(
        x_mid, f0, pos0, rt, mod[0], mod[1], row2(norm_mix[1]), _bf(w_odd[:, 0:d + LANES]), _bf(w_odd[:, d + LANES:].T),
        tile_gain(swa_q_norm[0], SWA_HEADS), tile_gain(swa_k_norm[0], SWA_KV_HEADS), tables, bd, geom)
    att1 = _attention("window", q1, k1, v1t, _score_bound(swa_q_norm[0], swa_k_norm[0]), n_batch, seq, lc,
                      SWA_HEADS, SWA_KV_HEADS, sink=swa_sink[0])
    router = _router_weights(router_group_w[1], router_group_b[1], router_expert_w[1], router_expert_b[1])
    w_out1 = _bf(od_w_out[0])
    tok = lambda w_: pl.BlockSpec((tt, w_), lambda t: (t, 0))
    x2, hrow1, rt1, counts1 = _out_proj(
        functools.partial(_out_odd_kernel, geom), (att1, w_out1, x1),
        [tok(SWA_HEADS * HEAD_DIM), _full(w_out1.shape), tok(d)],
        d, mod[1], row2(norm_ffn[1]), router, n_lat, "out_odd")
    f1, pos1 = _moe(hrow1, rt1, counts1, experts, 1)
    out = _final(x2, f1, pos1, rt1, mod[1], tiles_per_batch)
    return out.reshape(n_batch, seq, d)
```

```python
import functools

import numpy as np
import jax
import jax.numpy as jnp
from jax import lax
from jax.experimental import pallas as pl
from jax.experimental.pallas import tpu as pltpu

F32 = jnp.float32
BF16 = jnp.bfloat16

GRID_W = 64
HEAD_DIM = 64
AXIS_DIM = HEAD_DIM // 2
ROPE_THETA = 10000.0
EPS = 1e-6
GLA_HEADS = 4
GLA_DK = 64
GLA_DV = 128
GLA_GATE_RANK = 16
GLA_GATE_NORM = 16.0
GLA_CHUNK = 64
ATT_HEADS = 8
ATT_KV_HEADS = 2
SWA_HEADS = 16
SWA_KV_HEADS = 2
SWA_WINDOW = 128
MOE_GROUPS = 4
MOE_EPG = 4
D_EXPERT = 256
N_PAIRS = 6
N_BUCKETS = MOE_GROUPS * N_PAIRS

LANES = 128
MXU_DIM = 256
TOK_TILE = 512
ATT_Q_TILE = 128
ATT_K_CHUNK = 512
ATT_COL = 4 * HEAD_DIM
WIN_SUB = 8
DENSE_SUB = 4
MOE_TILE = 256
MOE_PAIR = 2
DISPATCH_TILE = 1024
VMEM_LIMIT = 56 * 1024 * 1024
NEG_BIG = -1e30
LOG2E = 1.4426950408889634
Q_SCALE = HEAD_DIM ** -0.5 * LOG2E


def _bf(x):
    return x.astype(BF16)


def _split2(x):
    hi = _bf(x)
    lo = _bf(x - hi.astype(F32))
    return hi, lo


def _dot(a, b):
    return jnp.dot(a, b, preferred_element_type=F32)


def _dot_nt(a, b):
    return lax.dot_general(a, b, (((1,), (1,)), ((), ())), preferred_element_type=F32)


def _silu(x):
    return x / (1.0 + jnp.exp(-x))


def _rms(x):
    return x * lax.rsqrt(jnp.mean(x * x, axis=-1, keepdims=True) + EPS)


def _cparams(sem):
    return pltpu.CompilerParams(dimension_semantics=sem, vmem_limit_bytes=VMEM_LIMIT)


def _full(shape):
    n = len(shape)
    return pl.BlockSpec(shape, lambda *_: (0,) * n)


def _mod_kernel(c_ref, w_ref, b_ref, o_ref):
    c = c_ref[...]
    ch, cl = _split2(_silu(c))
    wh, wl = _split2(w_ref[0])
    o_ref[0] = _dot(ch, wh) + _dot(ch, wl) + _dot(cl, wh) + b_ref[0]


def _modulation(c_rows, mod_w, mod_b):
    depth, d, n = mod_w.shape
    tn = n // 4
    return pl.pallas_call(
        _mod_kernel,
        out_shape=jax.ShapeDtypeStruct((depth, 16, n), F32),
        grid=(depth, n // tn),
        in_specs=[pl.BlockSpec((16, d), lambda i, j: (0, 0)),
                  pl.BlockSpec((1, d, tn), lambda i, j: (i, 0, j)),
                  pl.BlockSpec((1, 1, tn), lambda i, j: (i, 0, j))],
        out_specs=pl.BlockSpec((1, 16, tn), lambda i, j: (i, 0, j)),
        compiler_params=_cparams(("arbitrary", "arbitrary")),
        name="modulation",
    )(c_rows, mod_w, mod_b.reshape(depth, 1, n))


def _mod_row(t, n_lat_tiles, tiles_per_batch, n_batch):
    return jnp.where(t < n_lat_tiles, t // tiles_per_batch, n_batch)


def _modulated(x, gain, mod_ref, row, k_shift, k_scale):
    d = x.shape[-1]
    shift = mod_ref[pl.ds(row, 1), k_shift * d:(k_shift + 1) * d]
    scale = mod_ref[pl.ds(row, 1), k_scale * d:(k_scale + 1) * d]
    return _rms(x) * gain * (1.0 + scale) + shift


def _rope_tables(seq):
    rows = seq // GRID_W
    row = np.repeat(np.arange(rows), GRID_W)
    col = np.tile(np.arange(GRID_W), rows)
    inv_freq = ROPE_THETA ** (-np.arange(0, AXIS_DIM, 2, dtype=np.float64) / AXIS_DIM)
    ang = np.stack([row[:, None] * inv_freq, col[:, None] * inv_freq], axis=1)
    cos, sin = np.cos(ang), np.sin(ang)
    zero = np.zeros_like(sin)
    cos64 = np.concatenate([cos[:, 0], cos[:, 0], cos[:, 1], cos[:, 1]], axis=-1)
    sa64 = np.concatenate([-sin[:, 0], zero[:, 0], -sin[:, 1], zero[:, 1]], axis=-1)
    sb64 = np.concatenate([zero[:, 0], sin[:, 0], zero[:, 1], sin[:, 1]], axis=-1)

    def widen(t, fill):
        t = np.concatenate([t, t], axis=-1)
        return jnp.asarray(np.concatenate([t, np.full((TOK_TILE, LANES), fill)], axis=0), F32)

    return widen(cos64, 1.0), widen(sa64, 0.0), widen(sb64, 0.0)


def _head_sumsq(y, bd):
    w = y.shape[-1]
    outs = []
    for s in range(0, w, MXU_DIM):
        e = min(s + MXU_DIM, w)
        hi, lo = _split2(y[:, s:e])
        b = bd[0:e - s, 0:e - s]
        outs.append(_dot(hi, b) + _dot(lo, b))
    return outs[0] if len(outs) == 1 else jnp.concatenate(outs, axis=-1)


def _qk_norm_rope(z, gain, bd, cos, sa, sb):
    w = z.shape[-1]
    rep = w // LANES
    ss = _head_sumsq(z * z, bd)
    y = z * lax.rsqrt(ss * (1.0 / HEAD_DIM) + EPS) * gain

    def wide(t):
        return t if rep == 1 else jnp.concatenate([t] * rep, axis=-1)

    return (y * wide(cos) + pltpu.roll(y, w - AXIS_DIM // 2, 1) * wide(sa)
            + pltpu.roll(y, AXIS_DIM // 2, 1) * wide(sb))


def _kv_rep(kv128):
    lane = lax.broadcasted_iota(jnp.int32, kv128.shape, 1)
    sw = pltpu.roll(kv128, HEAD_DIM, 1)
    a0 = jnp.where(lane < HEAD_DIM, kv128, sw)
    a1 = jnp.where(lane < HEAD_DIM, sw, kv128)
    return jnp.concatenate([a0, a0, a1, a1], axis=-1)


def _rope_block(t, n_lat_tiles, tiles_per_batch):
    return jnp.where(t < n_lat_tiles, t % tiles_per_batch, tiles_per_batch)


EV_GQ, EV_GK, EV_GV, EV_GR, EV_AQ, EV_AK, EV_LR, EV_END = 0, 256, 512, 1024, 1536, 2048, 2176, 2304


def _proj_even_kernel(geom, xl_ref, xc_ref, mod_ref, gain_ref, w_ref, wvt_ref, gw_ref, gb_ref, qg_ref, kg_ref,
                      cos_ref, sa_ref, sb_ref, bd_ref,
                      gq_ref, gk_ref, gv_ref, gr_ref, g_ref, aq_ref, ak_ref, avt_ref):
    t = pl.program_id(0)
    row = _mod_row(t, *geom)
    x = jnp.where(t < geom[0], xl_ref[...], xc_ref[...])
    hb = _bf(_modulated(x, gain_ref[...], mod_ref, row, 0, 1))

    def seg(a, b):
        return _dot(hb, w_ref[:, a:b])

    gq_ref[...] = seg(EV_GQ, EV_GK) * (GLA_DK ** -0.5)
    gk_ref[...] = seg(EV_GK, EV_GV)
    gv_ref[...] = _bf(seg(EV_GV, EV_GR))
    gr_ref[...] = seg(EV_GR, EV_AQ)
    zg = _dot(_bf(seg(EV_LR, EV_END)), gw_ref[...]) + gb_ref[...]
    g_ref[...] = -(jnp.maximum(-zg, 0.0) + jnp.log1p(jnp.exp(-jnp.abs(zg)))) * (1.0 / GLA_GATE_NORM)
    bd = bd_ref[...]
    cos, sa, sb = cos_ref[...], sa_ref[...], sb_ref[...]
    aq = _qk_norm_rope(seg(EV_AQ, EV_AK), qg_ref[...], bd, cos, sa, sb)
    aq_ref[...] = _bf(aq * Q_SCALE)
    ak = _qk_norm_rope(seg(EV_AK, EV_LR), kg_ref[...], bd, cos, sa, sb)
    ak_ref[...] = _bf(_kv_rep(ak))
    avt_ref[0] = _bf(_dot_nt(wvt_ref[...], hb))


def _vt_spec():
    return pl.BlockSpec((1, LANES, TOK_TILE), lambda t: (t, 0, 0))


def _lat_ctx_specs(block, n_lat_tiles, lead=()):
    z = (0,) * len(lead)
    lat = pl.BlockSpec(lead + block, lambda t: z + (jnp.minimum(t, n_lat_tiles - 1), 0))
    ctx = pl.BlockSpec(lead + block, lambda t: z + (jnp.maximum(t - n_lat_tiles, 0), 0))
    return [lat, ctx]


def _proj_even(x_lat, x_ctx, mod, gain, w, wvt, gw, gb, qg, kg, tables, bd, geom):
    d = x_lat.shape[1]
    n = x_lat.shape[0] + x_ctx.shape[0]
    n_lat_tiles, tiles_per_batch, _ = geom
    tt = TOK_TILE
    cos, sa, sb = tables
    tok = lambda w_: pl.BlockSpec((tt, w_), lambda t: (t, 0))
    rope = pl.BlockSpec((tt, LANES), lambda t: (_rope_block(t, n_lat_tiles, tiles_per_batch), 0))
    outs = [(256, F32), (256, F32), (512, BF16), (512, F32), (512, F32), (512, BF16), (512, BF16)]
    return pl.pallas_call(
        functools.partial(_proj_even_kernel, geom),
        out_shape=[jax.ShapeDtypeStruct((n, w_), dt) for w_, dt in outs]
        + [jax.ShapeDtypeStruct((n // tt, LANES, tt), BF16)],
        grid=(n // tt,),
        in_specs=_lat_ctx_specs((tt, d), n_lat_tiles)
        + [_full(mod.shape), _full(gain.shape), _full(w.shape), _full(wvt.shape), _full(gw.shape),
           _full(gb.shape), _full(qg.shape), _full(kg.shape), rope, rope, rope, _full(bd.shape)],
        out_specs=[tok(w_) for w_, _ in outs] + [_vt_spec()],
        compiler_params=_cparams(("arbitrary",)),
        name="proj_even",
    )(x_lat, x_ctx, mod, gain, w, wvt, gw, gb, qg, kg, cos, sa, sb, bd)


def _proj_odd_kernel(geom, x_ref, pos_ref, pos_next_ref, rt_ref, fs_hbm, mod_prev_ref, mod_ref, gain_ref, w_ref,
                     wvt_ref, qg_ref, kg_ref, cos_ref, sa_ref, sb_ref, bd_ref, x1_ref, q_ref, k_ref, vt_ref,
                     fbuf, fsem):
    t = pl.program_id(0)
    row = _mod_row(t, *geom)
    d = x_ref.shape[-1]
    gate = mod_prev_ref[pl.ds(row, 1), 5 * d:6 * d]
    f = _gather_tile(t, pl.num_programs(0), pos_ref, pos_next_ref, rt_ref, fs_hbm, fbuf, fsem,
                     inline_prefetch=True)
    x1 = x_ref[...] + gate * f
    x1_ref[...] = x1
    hb = _bf(_modulated(x1, gain_ref[...], mod_ref, row, 0, 1))
    bd = bd_ref[...]
    cos, sa, sb = cos_ref[...], sa_ref[...], sb_ref[...]
    q = _qk_norm_rope(_dot(hb, w_ref[:, 0:d]), qg_ref[...], bd, cos, sa, sb)
    q_ref[...] = _bf(q * Q_SCALE)
    k = _qk_norm_rope(_dot(hb, w_ref[:, d:d + LANES]), kg_ref[...], bd, cos, sa, sb)
    k_ref[...] = _bf(_kv_rep(k))
    vt_ref[0] = _bf(_dot_nt(wvt_ref[...], hb))
    _gather_drain(t, pl.num_programs(0), fs_hbm, fbuf, fsem)


def _proj_odd(x_all, f_sorted, pos, rt, mod_prev, mod, gain, w, wvt, qg, kg, tables, bd, geom):
    n, d = x_all.shape
    n_lat_tiles, tiles_per_batch, _ = geom
    tt = TOK_TILE
    cos, sa, sb = tables
    tok = lambda w_: pl.BlockSpec((tt, w_), lambda t: (t, 0))
    rope = pl.BlockSpec((tt, LANES), lambda t: (_rope_block(t, n_lat_tiles, tiles_per_batch), 0))
    outs = [(d, F32), (d, BF16), (512, BF16)]
    return pl.pallas_call(
        functools.partial(_proj_odd_kernel, geom),
        out_shape=[jax.ShapeDtypeStruct((n, w_), dt) for w_, dt in outs]
        + [jax.ShapeDtypeStruct((n // tt, LANES, tt), BF16)],
        grid=(n // tt,),
        in_specs=[tok(d)] + _gather_specs(n // tt)
        + [_full(mod_prev.shape), _full(mod.shape), _full(gain.shape), _full(w.shape),
           _full(wvt.shape), _full(qg.shape), _full(kg.shape), rope, rope, rope, _full(bd.shape)],
        out_specs=[tok(w_) for w_, _ in outs] + [_vt_spec()],
        scratch_shapes=_gather_scratch(d),
        compiler_params=_cparams(("arbitrary",)),
        name="proj_odd",
    )(x_all, pos, pos, rt, f_sorted, mod_prev, mod, gain, w, wvt, qg, kg, cos, sa, sb, bd)


N_LEVELS = 6
GLA_MX_ROWS = (N_LEVELS + 2) * GLA_CHUNK


def _gla_constants():
    c = GLA_CHUNK
    mx = np.zeros((2, GLA_MX_ROWS, c), np.float32)
    pat = np.zeros((2, N_LEVELS + 1, c, GLA_HEADS * c), np.float32)
    r = np.arange(c)
    for lvl in range(N_LEVELS):
        h = 1 << lvl
        ref = (r // (2 * h)) * 2 * h + h - 1
        upper = (r % (2 * h)) >= h
        m = np.zeros((c, c), np.float32)
        for i in range(c):
            if upper[i]:
                m[i, ref[i] + 1:i + 1] = 1.0
            else:
                m[i, i + 1:ref[i] + 1] = 1.0
        mx[0, lvl * c:(lvl + 1) * c] = m
        same = (r[:, None] // (2 * h)) == (r[None, :] // (2 * h))
        p = same & upper[:, None] & (~upper)[None, :]
        pat[0, lvl] = np.tile(p.astype(np.float32), (1, GLA_HEADS))
    mx[0, N_LEVELS * c:(N_LEVELS + 1) * c] = (r[None, :] <= r[:, None])
    mx[0, (N_LEVELS + 1) * c:(N_LEVELS + 2) * c] = (r[None, :] > r[:, None])
    pat[0, N_LEVELS] = np.tile(np.eye(c, dtype=np.float32), (1, GLA_HEADS))
    for k in range(N_LEVELS + 2):
        mx[1, k * c:(k + 1) * c] = mx[0, k * c:(k + 1) * c][::-1, ::-1]
    for k in range(N_LEVELS + 1):
        pat[1, k] = np.tile(pat[0, k, :, 0:c][::-1, ::-1], (1, GLA_HEADS))
    return mx, pat


GLA_BLOCK = 1024
GLA_GROUP = 4


def _gla_chunks(chunks, mx, pat_ref, s_ref):
    c = GLA_CHUNK
    lane_head = lax.broadcasted_iota(jnp.int32, (c, GLA_HEADS * GLA_DK), 1) // GLA_DK

    def stack_heads(a):
        return jnp.concatenate([_bf(jnp.where(lane_head == h, a, 0.0)) for h in range(GLA_HEADS)], axis=0)

    xs = [_dot(mx, jnp.concatenate(_split2(g), axis=0)) for _, _, _, g in chunks]
    atts = [jnp.where(pat_ref[0, N_LEVELS] > 0.0, _dot_nt(_bf(q), stack_heads(k)), 0.0)
            for q, k, _, _ in chunks]
    for lvl in range(N_LEVELS):
        for j, (q, k, _, _) in enumerate(chunks):
            e = jnp.exp(xs[j][lvl * c:(lvl + 1) * c])
            atts[j] = atts[j] + jnp.where(pat_ref[0, lvl] > 0.0, _dot_nt(_bf(q * e), stack_heads(k * e)), 0.0)
    outs, qes, news, a_cols = [], [], [], []
    for j, (q, k, v, _) in enumerate(chunks):
        vhead = lax.broadcasted_iota(jnp.int32, v.shape, 1) // GLA_DV
        v_bd = jnp.concatenate([jnp.where(vhead == h, v, jnp.zeros_like(v)) for h in range(GLA_HEADS)], axis=0)
        outs.append(_dot(_bf(atts[j]), v_bd))
        bcum = xs[j][N_LEVELS * c:(N_LEVELS + 1) * c]
        brem = xs[j][(N_LEVELS + 1) * c:(N_LEVELS + 2) * c]
        qes.append(stack_heads(q * jnp.exp(bcum)))
        kt = jnp.transpose(k * jnp.exp(brem))
        news.append(jnp.concatenate(
            [_dot(_bf(kt[h * GLA_DK:(h + 1) * GLA_DK]), v[:, h * GLA_DV:(h + 1) * GLA_DV])
             for h in range(GLA_HEADS)], axis=0))
        tot = bcum[0:1] + brem[0:1]
        a_cols.append(jnp.transpose(jnp.exp(jnp.broadcast_to(tot, (8, tot.shape[1]))))[:, 0:1])
    s = s_ref[...]
    for j in range(len(chunks)):
        o_inter = _dot(qes[j], _bf(s))
        outs[j] = outs[j] + jnp.concatenate([o_inter[h * c:(h + 1) * c] for h in range(GLA_HEADS)], axis=-1)
        s = a_cols[j] * s + news[j]
    s_ref[...] = s
    return outs


def _gla_kernel(n_lat_chunks, n_ctx_chunks, ql_ref, kl_ref, vl_ref, gl_ref, qc_ref, kc_ref, vc_ref, gc_ref,
                mx_ref, pat_ref, ol_ref, oc_ref, s_ref):
    d = pl.program_id(1)
    c = GLA_CHUNK
    mx = mx_ref[0]

    def run(n_chunks, q_ref, k_ref, v_ref, g_ref, o_ref):
        group = min(GLA_GROUP, n_chunks)

        def body(i, carry):
            rows = []
            for j in range(group):
                step = i * group + j
                ci = jnp.where(d == 0, step, n_chunks - 1 - step)
                rows.append(pl.ds(pl.multiple_of(ci * c, c), c))
            outs = _gla_chunks([(q_ref[r, :], k_ref[r, :], v_ref[r, :], g_ref[r, :]) for r in rows],
                               mx, pat_ref, s_ref)
            for r, o in zip(rows, outs):
                o_ref[0, r, :] = o
            return carry
        lax.fori_loop(0, n_chunks // group, body, 0)

    @pl.when(pl.program_id(2) == 0)
    def _():
        s_ref[...] = jnp.zeros_like(s_ref)
        run(n_ctx_chunks, qc_ref, kc_ref, vc_ref, gc_ref, oc_ref)

    run(n_lat_chunks, ql_ref, kl_ref, vl_ref, gl_ref, ol_ref)


def _gla(gq, gk, gv, g, n_batch, seq, lc):
    mx_np, pat_np = _gla_constants()
    mx = jnp.asarray(np.concatenate([mx_np, mx_np], axis=2), BF16)
    pat = jnp.asarray(pat_np, F32)
    ctx0 = n_batch * seq // lc
    hk, hv = GLA_HEADS * GLA_DK, GLA_HEADS * GLA_DV
    blk = min(GLA_BLOCK, seq)
    nb = seq // blk
    for rows_ in (blk, lc):
        assert rows_ % (GLA_CHUNK * min(GLA_GROUP, rows_ // GLA_CHUNK)) == 0
    assert seq % blk == 0
    row = lambda b, d_, i: b * nb + jnp.where(d_ == 0, i, nb - 1 - i)
    lat = lambda w_, col: pl.BlockSpec((blk, w_), lambda b, d_, i: (row(b, d_, i), col(d_)))
    ctx = lambda w_, col: pl.BlockSpec((lc, w_), lambda b, d_, i: (ctx0 + b, col(d_)))
    zero = lambda d_: 0
    same = lambda d_: d_
    return pl.pallas_call(
        functools.partial(_gla_kernel, blk // GLA_CHUNK, lc // GLA_CHUNK),
        out_shape=[jax.ShapeDtypeStruct((2, n_batch * seq, hv), F32),
                   jax.ShapeDtypeStruct((2, n_batch * lc, hv), F32)],
        grid=(n_batch, 2, nb),
        in_specs=[lat(hk, zero), lat(hk, zero), lat(hv, zero), lat(hk, same),
                  ctx(hk, zero), ctx(hk, zero), ctx(hv, zero), ctx(hk, same),
                  pl.BlockSpec((1,) + mx.shape[1:], lambda b, d_, i: (d_, 0, 0)),
                  pl.BlockSpec((1,) + pat.shape[1:], lambda b, d_, i: (d_, 0, 0, 0))],
        out_specs=[pl.BlockSpec((1, blk, hv), lambda b, d_, i: (d_, row(b, d_, i), 0)),
                   pl.BlockSpec((1, lc, hv), lambda b, d_, i: (d_, b, 0))],
        scratch_shapes=[pltpu.VMEM((GLA_HEADS * GLA_DK, GLA_DV), F32)],
        compiler_params=_cparams(("arbitrary", "arbitrary", "arbitrary")),
        name="gla_scan",
    )(gq, gk, gv, g, gq, gk, gv, g, mx, pat)


def _stack_heads(q):
    lane_head = lax.broadcasted_iota(jnp.int32, q.shape, 1) // HEAD_DIM
    return jnp.concatenate([jnp.where(lane_head == h, q, jnp.zeros_like(q)) for h in range(4)], axis=0)


SAFE_SCORE_BOUND = 60.0
ONES_ROWS = 16


def _with_ones(vt):
    return jnp.concatenate([vt, jnp.ones((ONES_ROWS, vt.shape[1]), vt.dtype)], axis=0)


def _attn_store(acc, l, o_ref, u, tq):
    out = acc * (1.0 / l)
    out = jnp.concatenate([out[:, h * tq:(h + 1) * tq] for h in range(4)], axis=0)
    o_ref[u * tq:(u + 1) * tq, :] = _bf(jnp.transpose(out))


def _attn_dense_kernel(n_chunks, n_sub, *refs):
    if n_chunks:
        bound_ref, q_ref, kl_ref, vtl_ref, kc_ref, vtc_ref, o_ref = refs
    else:
        bound_ref, q_ref, kc_ref, vtc_ref, o_ref = refs
    tq = q_ref.shape[0] // n_sub
    cols = 4 * tq
    q4 = [_stack_heads(q_ref[u * tq:(u + 1) * tq, :]) for u in range(n_sub)]

    def scores(c, u):
        if c < n_chunks:
            return _dot_nt(kl_ref[c * ATT_K_CHUNK:(c + 1) * ATT_K_CHUNK, :], q4[u])
        return _dot_nt(kc_ref[...], q4[u])

    def values(c):
        if c < n_chunks:
            tile, off = divmod(c * ATT_K_CHUNK, TOK_TILE)
            return vtl_ref[tile][:, off:off + ATT_K_CHUNK]
        return vtc_ref[0]

    def update(carry, st, vt_aug):
        m, acc = carry
        m_new = jnp.maximum(m, jnp.max(st, axis=0, keepdims=True))
        acc = jnp.exp2(m - m_new) * acc + _dot(vt_aug, _bf(jnp.exp2(st - m_new)))
        return m_new, acc

    def run(fixed_ref):
        if fixed_ref is None:
            carry = [(jnp.full((1, cols), NEG_BIG, F32), jnp.zeros((HEAD_DIM + ONES_ROWS, cols), F32))
                     for _ in range(n_sub)]
        else:
            carry = [jnp.zeros((HEAD_DIM + ONES_ROWS, cols), F32) for _ in range(n_sub)]
        st = [scores(0, u) for u in range(n_sub)]
        for c in range(n_chunks + 1):
            st_next = [scores(c + 1, u) for u in range(n_sub)] if c < n_chunks else None
            vt_aug = _with_ones(values(c))
            if fixed_ref is None:
                carry = [update(carry[u], st[u], vt_aug) for u in range(n_sub)]
            else:
                carry = [carry[u] + _dot(vt_aug, _bf(jnp.exp2(st[u] - fixed_ref))) for u in range(n_sub)]
            st = st_next
        for u in range(n_sub):
            acc = carry[u][1] if fixed_ref is None else carry[u]
            _attn_store(acc[0:HEAD_DIM], acc[HEAD_DIM:HEAD_DIM + 1], o_ref, u, tq)

    bound = bound_ref[0]

    @pl.when(bound <= SAFE_SCORE_BOUND)
    def _():
        run(bound)

    @pl.when(bound > SAFE_SCORE_BOUND)
    def _():
        run(None)


def _attn_window_kernel(seq, bound_ref, q_ref, *refs):
    nk = WIN_SUB + 2
    k_refs, v_refs = refs[0:nk], refs[nk:2 * nk]
    kc_ref, vtc_ref, sink_ref, o_ref = refs[2 * nk:]
    tq = SWA_WINDOW
    i = pl.program_id(2)
    kb = jnp.concatenate([r[...] for r in k_refs], axis=0)
    vtb = _with_ones(jnp.concatenate([r[0] for r in v_refs], axis=1))
    kc, vtc, sink = kc_ref[...], _with_ones(vtc_ref[0]), sink_ref[0]
    span = 3 * tq
    r_i = lax.broadcasted_iota(jnp.int32, (tq, tq), 0)
    c_i = lax.broadcasted_iota(jnp.int32, (tq, tq), 1)
    band_lo = jnp.where(r_i >= c_i, 0.0, NEG_BIG)
    band_hi = jnp.where(r_i <= c_i, 0.0, NEG_BIG)
    n_blocks = seq // tq

    def run(use_bound):
        sb, sc = [], []
        for u in range(WIN_SUB):
            q4 = _stack_heads(q_ref[u * tq:(u + 1) * tq, :])
            first = i * WIN_SUB + u - 1
            bias_lo = jnp.where(first < 0, NEG_BIG, band_lo)
            bias_hi = jnp.where(first + 2 >= n_blocks, NEG_BIG, band_hi)
            s = _dot_nt(kb[u * tq:u * tq + span], q4)
            sb.append(jnp.concatenate([s[0:tq] + jnp.concatenate([bias_lo] * 4, axis=1), s[tq:2 * tq],
                                       s[2 * tq:span] + jnp.concatenate([bias_hi] * 4, axis=1)], axis=0))
            sc.append(_dot_nt(kc, q4))
        ms, pbs, pcs = [], [], []
        for u in range(WIN_SUB):
            if use_bound:
                m = jnp.maximum(bound_ref[0], sink)
            else:
                m = jnp.maximum(jnp.maximum(jnp.max(sb[u], axis=0, keepdims=True),
                                            jnp.max(sc[u], axis=0, keepdims=True)), sink)
            ms.append(m)
            pbs.append(_bf(jnp.exp2(sb[u] - m)))
            pcs.append(_bf(jnp.exp2(sc[u] - m)))
        for u in range(WIN_SUB):
            acc = _dot(vtb[:, u * tq:u * tq + span], pbs[u]) + _dot(vtc, pcs[u])
            l = acc[HEAD_DIM:HEAD_DIM + 1] + jnp.exp2(sink - ms[u])
            _attn_store(acc[0:HEAD_DIM], l, o_ref, u, tq)

    @pl.when(bound_ref[0] <= SAFE_SCORE_BOUND)
    def _():
        run(True)

    @pl.when(bound_ref[0] > SAFE_SCORE_BOUND)
    def _():
        run(False)


def _score_bound(q_gain, k_gain):
    return (HEAD_DIM * Q_SCALE * 1.02 * jnp.max(jnp.abs(q_gain)) * jnp.max(jnp.abs(k_gain))).reshape(1)


def _attention(mode, q, k_rep, vt, bound, n_batch, seq, lc, n_heads, n_kv, sink=None):
    ncol = n_heads * HEAD_DIM // ATT_COL
    col_per_kv = ncol // n_kv
    tq = ATT_Q_TILE
    tpb = seq // TOK_TILE
    n_lat_tiles = n_batch * tpb
    ctx_per_tile = TOK_TILE // lc
    ctx0 = n_batch * seq // lc
    assert TOK_TILE % ATT_K_CHUNK == 0 and TOK_TILE % lc == 0 and tq == SWA_WINDOW
    kv = lambda j: j // col_per_kv
    k_ctx = pl.BlockSpec((lc, ATT_COL), lambda b, j, i: (ctx0 + b, kv(j)))
    vt_ctx = pl.BlockSpec((1, HEAD_DIM, lc), lambda b, j, i: (n_lat_tiles + b // ctx_per_tile, kv(j), b % ctx_per_tile))
    if mode == "ctx":
        nq = lc // tq
        q0 = n_batch * seq // tq
        kern = functools.partial(_attn_dense_kernel, 0, 1)
        args = (q, k_rep, vt)
        in_specs = [pl.BlockSpec((tq, ATT_COL), lambda b, j, i: (q0 + b * nq + i, j)), k_ctx, vt_ctx]
    elif mode == "dense":
        tq = DENSE_SUB * ATT_Q_TILE
        nq = seq // tq
        kern = functools.partial(_attn_dense_kernel, seq // ATT_K_CHUNK, DENSE_SUB)
        args = (q, k_rep, vt, k_rep, vt)
        in_specs = [pl.BlockSpec((tq, ATT_COL), lambda b, j, i: (b * nq + i, j)),
                    pl.BlockSpec((seq, ATT_COL), lambda b, j, i: (b, kv(j))),
                    pl.BlockSpec((tpb, HEAD_DIM, TOK_TILE), lambda b, j, i: (b, kv(j), 0)), k_ctx, vt_ctx]
    else:
        wb = SWA_WINDOW
        tq = WIN_SUB * wb
        nq = seq // tq
        nkb = seq // wb
        per_tile = TOK_TILE // wb
        kern = functools.partial(_attn_window_kernel, seq)
        nb = lambda i, o: jnp.clip(i * WIN_SUB + o, 0, nkb - 1)
        k_nb = lambda o: pl.BlockSpec((wb, ATT_COL), lambda b, j, i: (b * nkb + nb(i, o), kv(j)))
        v_nb = lambda o: pl.BlockSpec(
            (1, HEAD_DIM, wb), lambda b, j, i: (b * tpb + nb(i, o) // per_tile, kv(j), nb(i, o) % per_tile))
        offs = range(-1, WIN_SUB + 1)
        sink_row = jnp.repeat(sink.reshape(ncol, 1, 4), wb, axis=2) * LOG2E
        args = (q,) + (k_rep,) * len(offs) + (vt,) * len(offs) + (k_rep, vt, sink_row)
        in_specs = ([pl.BlockSpec((tq, ATT_COL), lambda b, j, i: (b * nq + i, j))]
                    + [k_nb(o) for o in offs] + [v_nb(o) for o in offs]
                    + [k_ctx, vt_ctx, pl.BlockSpec((1, 1, 4 * wb), lambda b, j, i: (j, 0, 0))])
    args = (bound,) + args
    in_specs = [pl.BlockSpec(memory_space=pltpu.SMEM)] + in_specs
    return pl.pallas_call(
        kern,
        out_shape=jax.ShapeDtypeStruct((n_batch * nq * tq, n_heads * HEAD_DIM), BF16),
        grid=(n_batch, ncol, nq),
        in_specs=in_specs,
        out_specs=pl.BlockSpec((tq, ATT_COL), lambda b, j, i: (b * nq + i, j)),
        compiler_params=_cparams(("arbitrary", "arbitrary", "arbitrary")),
        name="attention_" + mode,
    )(*args)


ROUTE_ROWS = 8
TOKEN_TILE = (8, LANES)
OUT_TOKEN_TILE = (16, LANES)
OUT_TOKEN_DTYPE = BF16


def _to_token_tiles(x, tile):
    return x.reshape((x.shape[0],) + tile)


def _from_token_tiles(tiles):
    return tiles.reshape(tiles.shape[0], tiles.shape[1] * tiles.shape[2])


def _route(h, wrt_ref, wrt_hi_ref, brt_ref):
    hh, hl = _split2(h)
    a = _dot_nt(wrt_ref[...], hh)
    lt = a[0:LANES] + a[LANES:2 * LANES] + _dot_nt(wrt_hi_ref[...], hl) + brt_ref[...]
    col = lambda i: lt[i:i + 1, :]
    gl = [col(i) for i in range(MOE_GROUPS)]
    gmax = functools.reduce(jnp.maximum, gl)
    gi = jnp.where(gl[0] == gmax, 0, jnp.where(gl[1] == gmax, 1, jnp.where(gl[2] == gmax, 2, 3)))
    g_weight = 1.0 / functools.reduce(lambda a, b: a + b, [jnp.exp(x - gmax) for x in gl])
    el = []
    for j in range(MOE_EPG):
        cand = [col(MOE_GROUPS + g * MOE_EPG + j) for g in range(MOE_GROUPS)]
        el.append(jnp.where(gi == 0, cand[0], jnp.where(gi == 1, cand[1], jnp.where(gi == 2, cand[2], cand[3]))))
    m1 = functools.reduce(jnp.maximum, el)
    i1 = jnp.where(el[0] == m1, 0, jnp.where(el[1] == m1, 1, jnp.where(el[2] == m1, 2, 3)))
    rest = [jnp.where(i1 == j, -jnp.inf, el[j]) for j in range(MOE_EPG)]
    m2 = functools.reduce(jnp.maximum, rest)
    i2 = jnp.where(rest[0] == m2, 0, jnp.where(rest[1] == m2, 1, jnp.where(rest[2] == m2, 2, 3)))
    e2 = jnp.exp(m2 - m1)
    w1 = g_weight / (1.0 + e2)
    w2 = g_weight * e2 / (1.0 + e2)
    lo = jnp.minimum(i1, i2)
    hi = jnp.maximum(i1, i2)
    w_lo = jnp.where(i1 == lo, w1, w2)
    w_hi = jnp.where(i1 == lo, w2, w1)
    pair = jnp.where(lo == 0, hi - 1, jnp.where(lo == 1, hi + 1, N_PAIRS - 1))
    return w_lo, w_hi, gi * N_PAIRS + pair


def _out_tail(geom, m, x, mod_ref, gain_ffn_ref, wrt_ref, wrt_hi_ref, brt_ref, triu_ref,
              x_new_ref, hrow_ref, rt_ref, counts_ref, run_ref):
    t = pl.program_id(0)
    row = _mod_row(t, *geom)
    tt, d = x.shape

    @pl.when(t == 0)
    def _():
        run_ref[...] = jnp.zeros_like(run_ref)

    x_new = x + mod_ref[pl.ds(row, 1), 2 * d:3 * d] * m
    x_new_ref[...] = x_new
    h = _modulated(x_new, gain_ffn_ref[...], mod_ref, row, 3, 4)
    w_lo, w_hi, bucket = _route(h, wrt_ref, wrt_hi_ref, brt_ref)
    onehot = lax.broadcasted_iota(jnp.int32, (LANES, tt), 0) == bucket
    ones = jnp.where(onehot, 1.0, 0.0)
    before = _dot(_bf(ones), triu_ref[...]) + run_ref[...]
    rank = jnp.sum(jnp.where(onehot, before, 0.0), axis=0, keepdims=True)
    run = run_ref[...] + jnp.sum(ones, axis=1, keepdims=True)
    run_ref[...] = run
    counts_ref[...] = jnp.broadcast_to(run, counts_ref.shape)
    rec = jnp.concatenate([w_lo, w_hi, bucket.astype(F32), rank, jnp.zeros((ROUTE_ROWS - 4, tt), F32)], axis=0)
    rt_ref[0] = rec
    hrow_ref[...] = _to_token_tiles(h, TOKEN_TILE)


def _pick(t, n_lat_tiles, lat_ref, ctx_ref):
    return jnp.where(t < n_lat_tiles, lat_ref[...], ctx_ref[...])


def _out_even_kernel(geom, ol_ref, oc_ref, r_ref, attl_ref, attc_ref, gn_ref, w_ref, xl_ref, xc_ref, mod_ref,
                     gain_ffn_ref, wrt_ref, wrt_hi_ref, brt_ref, triu_ref,
                     x_new_ref, hrow_ref, rt_ref, counts_ref, run_ref):
    t = pl.program_id(0)
    o2 = _pick(t, geom[0], ol_ref, oc_ref)
    o = o2[0] + o2[1]
    r = r_ref[...]
    parts = []
    for h in range(GLA_HEADS):
        sl = slice(h * GLA_DV, (h + 1) * GLA_DV)
        parts.append(_rms(o[:, sl]) * gn_ref[...] * _silu(r[:, sl]))
    a = _bf(jnp.concatenate(parts, axis=-1))
    half = a.shape[-1]
    m = _dot(a, w_ref[0:half, :]) + _dot(_pick(t, geom[0], attl_ref, attc_ref), w_ref[half:, :])
    _out_tail(geom, m, _pick(t, geom[0], xl_ref, xc_ref), mod_ref, gain_ffn_ref, wrt_ref, wrt_hi_ref, brt_ref,
              triu_ref, x_new_ref, hrow_ref, rt_ref, counts_ref, run_ref)


def _out_odd_kernel(geom, att_ref, w_ref, x_ref, mod_ref, gain_ffn_ref, wrt_ref, wrt_hi_ref, brt_ref, triu_ref,
                    x_new_ref, hrow_ref, rt_ref, counts_ref, run_ref):
    m = _dot(att_ref[...], w_ref[...])
    _out_tail(geom, m, x_ref[...], mod_ref, gain_ffn_ref, wrt_ref, wrt_hi_ref, brt_ref, triu_ref,
              x_new_ref, hrow_ref, rt_ref, counts_ref, run_ref)


def _out_proj(kernel, lead_args, lead_specs, d, mod, gain_ffn, router, n_rows, name):
    tt = TOK_TILE
    tok = lambda w_: pl.BlockSpec((tt, w_), lambda t: (t, 0))
    r = np.arange(tt)
    triu = jnp.asarray((r[:, None] < r[None, :]).astype(np.float32), BF16)
    wrt, wrt_hi, br = router
    brt = jnp.broadcast_to(br.reshape(LANES, 1), (LANES, tt))
    return pl.pallas_call(
        kernel,
        out_shape=[jax.ShapeDtypeStruct((n_rows, d), F32),
                   jax.ShapeDtypeStruct((n_rows,) + TOKEN_TILE, F32),
                   jax.ShapeDtypeStruct((n_rows // tt, ROUTE_ROWS, tt), F32),
                   jax.ShapeDtypeStruct((LANES, LANES), F32)],
        grid=(n_rows // tt,),
        in_specs=lead_specs + [_full(mod.shape), _full(gain_ffn.shape), _full(wrt.shape), _full(wrt_hi.shape),
                               _full(brt.shape), _full(triu.shape)],
        out_specs=[tok(d), pl.BlockSpec((tt,) + TOKEN_TILE, lambda t: (t, 0, 0)),
                   pl.BlockSpec((1, ROUTE_ROWS, tt), lambda t: (t, 0, 0)), _full((LANES, LANES))],
        scratch_shapes=[pltpu.VMEM((LANES, 1), F32)],
        compiler_params=_cparams(("arbitrary",)),
        name=name,
    )(*lead_args, mod, gain_ffn, wrt, wrt_hi, brt, triu)


SUBLANES = 8


def _for_each_row(n_rows, start_row_copy):
    def body(g, c):
        base = g * SUBLANES
        for j in range(SUBLANES):
            start_row_copy(base + j, j % 2)
        return c
    lax.fori_loop(0, n_rows // SUBLANES, body, 0)


def _pos_kernel(start_ref, rt_ref, pos_ref):
    bucket = rt_ref[:, 2, :].astype(jnp.int32)
    base = jnp.zeros_like(bucket)
    for b in range(N_BUCKETS):
        base = jnp.where(bucket == b, start_ref[b], base)
    pos_ref[:, 0, :] = base + rt_ref[:, 3, :].astype(jnp.int32)


def _dispatch_kernel(last_ref, pos_ref, x_ref, xs_ref, zbuf, sem, zsem):
    tt = x_ref.shape[0]
    tm = zbuf.shape[0]

    @pl.when(pl.program_id(0) == 0)
    def _():
        zbuf[...] = jnp.zeros_like(zbuf)

        def zero_copy(b):
            return pltpu.make_async_copy(zbuf, xs_ref.at[pl.ds(jnp.maximum(last_ref[b], 0) * tm, tm)], zsem)

        for b in range(2 * N_BUCKETS):
            @pl.when(last_ref[b] >= 0)
            def _():
                zero_copy(b).start()
        for b in range(2 * N_BUCKETS):
            @pl.when(last_ref[b] >= 0)
            def _():
                zero_copy(b).wait()

    _for_each_row(tt, lambda r, prio: pltpu.make_async_copy(
        x_ref.at[r], xs_ref.at[pos_ref[0, 0, r]], sem).start(priority=prio))
    pltpu.make_async_copy(x_ref, xs_ref.at[pl.ds(0, tt)], sem).wait()


def _moe_mlp_kernel(tlo_ref, thi_ref, nused_ref, xa_ref, xb_ref, *refs):
    del tlo_ref, thi_ref
    x_refs, w_sets, f_ref = (xa_ref, xb_ref), (refs[0:6], refs[6:12]), refs[12]
    tm = xa_ref.shape[0]
    first = MOE_PAIR * pl.program_id(0)
    n_used = nused_ref[0]

    def experts(tiles):
        xs = [_bf(_from_token_tiles(x_refs[j][...])) for j in tiles]
        ups = [[_dot(x, w_sets[j][k][0]) for k in (0, 1, 3, 4)] for x, j in zip(xs, tiles)]
        hids = [(_bf(_silu(g_lo) * u_lo), _bf(_silu(g_hi) * u_hi)) for g_lo, u_lo, g_hi, u_hi in ups]
        for (h_lo, h_hi), j in zip(hids, tiles):
            f2 = jnp.concatenate([_dot(h_lo, w_sets[j][2][0]), _dot(h_hi, w_sets[j][5][0])], axis=1)
            f_ref[j * tm:(j + 1) * tm] = _to_token_tiles(f2.astype(OUT_TOKEN_DTYPE), OUT_TOKEN_TILE)

    @pl.when(first + 1 < n_used)
    def _():
        experts((0, 1))

    @pl.when(first + 1 == n_used)
    def _():
        experts((0,))
        f_ref[tm:2 * tm] = jnp.zeros((tm,) + OUT_TOKEN_TILE, OUT_TOKEN_DTYPE)

    @pl.when(first >= n_used)
    def _():
        f_ref[...] = jnp.zeros_like(f_ref)


def _moe(hrow, rt, counts, experts, layer):
    n = hrow.shape[0]
    d = TOKEN_TILE[0] * TOKEN_TILE[1]
    tm, tt = MOE_TILE, TOK_TILE
    n_tiles = n // tm + N_BUCKETS
    p = n_tiles * tm
    cnt = counts[0:N_BUCKETS, 0].astype(jnp.int32)
    tiles_b = (cnt + tm - 1) // tm
    tile_end = jnp.cumsum(tiles_b)
    start_b = (tile_end - tiles_b) * tm
    n_used = tile_end[-1].reshape(1)
    spare = n_used[0] + jnp.arange(N_BUCKETS, dtype=jnp.int32)
    last_tile = jnp.concatenate([jnp.where(tiles_b > 0, tile_end - 1, -1), jnp.where(spare < n_tiles, spare, -1)])
    pos = pl.pallas_call(
        _pos_kernel,
        out_shape=jax.ShapeDtypeStruct((n // tt, 1, tt), jnp.int32),
        grid_spec=pltpu.PrefetchScalarGridSpec(
            num_scalar_prefetch=1, grid=(1,),
            in_specs=[pl.BlockSpec(rt.shape, lambda i, s: (0, 0, 0))],
            out_specs=pl.BlockSpec((n // tt, 1, tt), lambda i, s: (0, 0, 0))),
        compiler_params=_cparams(("arbitrary",)),
        name="moe_positions",
    )(start_b, rt)
    tile_ids = jnp.arange(n_tiles, dtype=jnp.int32)
    tile_bucket = jnp.sum((tile_ids[:, None] >= tile_end[None, :]).astype(jnp.int32), axis=1)
    tile_bucket = jnp.minimum(tile_bucket, jnp.sum((n_used[0] - 1 >= tile_end).astype(jnp.int32)))
    tile_bucket = jnp.minimum(tile_bucket, N_BUCKETS - 1)
    pair_lo = jnp.asarray([0, 0, 0, 1, 1, 2], jnp.int32)
    pair_hi = jnp.asarray([1, 2, 3, 2, 3, 3], jnp.int32)
    grp = tile_bucket // N_PAIRS
    w_gate, w_up, w_down = experts
    first = layer * MOE_GROUPS * MOE_EPG
    t_lo = first + grp * MOE_EPG + pair_lo[tile_bucket % N_PAIRS]
    t_hi = first + grp * MOE_EPG + pair_hi[tile_bucket % N_PAIRS]

    dt = DISPATCH_TILE if n % DISPATCH_TILE == 0 else tt
    xs = pl.pallas_call(
        _dispatch_kernel,
        out_shape=jax.ShapeDtypeStruct((p,) + TOKEN_TILE, F32),
        grid_spec=pltpu.PrefetchScalarGridSpec(
            num_scalar_prefetch=1, grid=(n // dt,),
            in_specs=[pl.BlockSpec((1, 1, dt), lambda t, s: (t, 0, 0), memory_space=pltpu.SMEM),
                      pl.BlockSpec((dt,) + TOKEN_TILE, lambda t, s: (t, 0, 0))],
            out_specs=pl.BlockSpec(memory_space=pl.ANY),
            scratch_shapes=[pltpu.VMEM((tm,) + TOKEN_TILE, F32), pltpu.SemaphoreType.DMA(()),
                            pltpu.SemaphoreType.DMA(())]),
        compiler_params=_cparams(("arbitrary",)),
        name="moe_dispatch",
    )(last_tile, pos.reshape(n // dt, 1, dt), hrow)

    f = D_EXPERT
    assert n_tiles % MOE_PAIR == 0

    def tile_specs(j):
        tile = lambda t: MOE_PAIR * t + j
        up_lo = pl.BlockSpec((1, d, f), lambda t, lo, hi, nu: (lo[tile(t)], 0, 0))
        up_hi = pl.BlockSpec((1, d, f), lambda t, lo, hi, nu: (hi[tile(t)], 0, 0))
        dn_lo = pl.BlockSpec((1, f, d), lambda t, lo, hi, nu: (lo[tile(t)], 0, 0))
        dn_hi = pl.BlockSpec((1, f, d), lambda t, lo, hi, nu: (hi[tile(t)], 0, 0))
        rows = pl.BlockSpec((tm,) + TOKEN_TILE, lambda t, lo, hi, nu: (jnp.minimum(tile(t), nu[0] - 1), 0, 0))
        return rows, [up_lo, up_lo, dn_lo, up_hi, up_hi, dn_hi]

    (rows_a, w_a), (rows_b, w_b) = tile_specs(0), tile_specs(1)
    grid_spec = pltpu.PrefetchScalarGridSpec(
        num_scalar_prefetch=3,
        grid=(n_tiles // MOE_PAIR,),
        in_specs=[rows_a, rows_b] + w_a + w_b,
        out_specs=pl.BlockSpec((MOE_PAIR * tm,) + OUT_TOKEN_TILE, lambda t, *_: (t, 0, 0)),
    )
    weights = (w_gate, w_up, w_down, w_gate, w_up, w_down)
    f_sorted = pl.pallas_call(
        _moe_mlp_kernel,
        out_shape=jax.ShapeDtypeStruct((p,) + OUT_TOKEN_TILE, OUT_TOKEN_DTYPE),
        grid_spec=grid_spec,
        compiler_params=_cparams(("arbitrary",)),
        name="moe_experts",
    )(t_lo, t_hi, n_used, xs, xs, *weights, *weights)
    return f_sorted, pos


def _gather_tile(t, n_t, pos_ref, pos_next_ref, rt_ref, src_hbm, buf, sem, inline_prefetch=False):
    tt = buf.shape[1]

    def start(p_ref, slot):
        _for_each_row(tt, lambda r, prio: pltpu.make_async_copy(
            src_hbm.at[p_ref[0, 0, r]], buf.at[slot].at[r], sem.at[slot]).start(priority=prio))

    slot = t % 2

    def wait(s):
        pltpu.make_async_copy(src_hbm.at[pl.ds(0, tt)], buf.at[s], sem.at[s]).wait()

    @pl.when(t == 0)
    def _():
        start(pos_ref, 0)

    wait(slot)
    if inline_prefetch:
        for r in range(tt):
            pltpu.make_async_copy(src_hbm.at[pos_next_ref[0, 0, r]], buf.at[1 - slot].at[r],
                                  sem.at[1 - slot]).start(priority=r % 2)
    else:
        @pl.when(t + 1 < n_t)
        def _():
            start(pos_next_ref, 1 - slot)

    f2 = _from_token_tiles(buf[slot]).astype(F32)
    d = f2.shape[1] // 2
    w = jnp.transpose(rt_ref[0])
    return w[:, 0:1] * f2[:, 0:d] + w[:, 1:2] * f2[:, d:2 * d]


def _gather_drain(t, n_t, src_hbm, buf, sem):
    tt = buf.shape[1]

    @pl.when(t == n_t - 1)
    def _():
        pltpu.make_async_copy(src_hbm.at[pl.ds(0, tt)], buf.at[1 - t % 2], sem.at[1 - t % 2]).wait()


def _gather_specs(n_t):
    tt = TOK_TILE
    return [pl.BlockSpec((1, 1, tt), lambda t: (t, 0, 0), memory_space=pltpu.SMEM),
            pl.BlockSpec((1, 1, tt), lambda t: (jnp.minimum(t + 1, n_t - 1), 0, 0), memory_space=pltpu.SMEM),
            pl.BlockSpec((1, ROUTE_ROWS, tt), lambda t: (t, 0, 0)),
            pl.BlockSpec(memory_space=pl.ANY)]


def _gather_scratch(d):
    assert 2 * d == OUT_TOKEN_TILE[0] * OUT_TOKEN_TILE[1]
    return [pltpu.VMEM((2, TOK_TILE) + OUT_TOKEN_TILE, OUT_TOKEN_DTYPE), pltpu.SemaphoreType.DMA((2,))]


def _final_kernel(tiles_per_batch, x_ref, pos_ref, pos_next_ref, rt_ref, fs_hbm, mod_ref, o_ref, fbuf, fsem):
    t = pl.program_id(0)
    row = t // tiles_per_batch
    d = x_ref.shape[-1]
    f = _gather_tile(t, pl.num_programs(0), pos_ref, pos_next_ref, rt_ref, fs_hbm, fbuf, fsem)
    o_ref[...] = x_ref[...] + mod_ref[pl.ds(row, 1), 5 * d:6 * d] * f


def _final(x_lat, f_sorted, pos, rt, mod, tiles_per_batch):
    n, d = x_lat.shape
    tok = pl.BlockSpec((TOK_TILE, d), lambda t: (t, 0))
    return pl.pallas_call(
        functools.partial(_final_kernel, tiles_per_batch),
        out_shape=jax.ShapeDtypeStruct((n, d), F32),
        grid=(n // TOK_TILE,),
        in_specs=[tok] + _gather_specs(n // TOK_TILE) + [_full(mod.shape)],
        out_specs=tok,
        scratch_shapes=_gather_scratch(d),
        compiler_params=_cparams(("arbitrary",)),
        name="final_residual",
    )(x_lat, pos, pos, rt, f_sorted, mod)


def _block_diag_ones():
    r = np.arange(MXU_DIM) // HEAD_DIM
    return jnp.asarray((r[:, None] == r[None, :]).astype(np.float32), BF16)


def _router_weights(wg, bg, we, be):
    d = wg.shape[0]
    n = MOE_GROUPS + MOE_GROUPS * MOE_EPG
    wt = jnp.concatenate([wg, we, jnp.zeros((d, LANES - n), F32)], axis=1).T
    b = jnp.concatenate([bg, be, jnp.zeros((LANES - n,), F32)])
    hi = _bf(wt)
    lo = _bf(wt - hi.astype(F32))
    return jnp.concatenate([hi, lo], axis=0), hi, b


def _expert_weights(w_gate, w_up, w_down):
    l, g, e, d, f = w_gate.shape
    n = l * g * e
    return _bf(w_gate).reshape(n, d, f), _bf(w_up).reshape(n, d, f), _bf(w_down).reshape(n, f, d)


def kernel(x, c, ctx, c_ctx, mod_w, mod_b, norm_mix, norm_ffn, ev_w_in, ev_w_out, gla_gate_w, gla_gate_b,
           gla_out_norm, att_q_norm, att_k_norm, od_w_in, od_w_out, swa_sink, swa_q_norm, swa_k_norm,
           router_group_w, router_group_b, router_expert_w, router_expert_b, exp_w_gate, exp_w_up, exp_w_down):
    n_batch, seq, d = x.shape
    lc = ctx.shape[1]
    depth = mod_w.shape[0]
    n_lat = n_batch * seq
    tiles_per_batch = seq // TOK_TILE
    geom = (n_lat // TOK_TILE, tiles_per_batch, n_batch)
    assert depth == 2 and seq % TOK_TILE == 0 and (n_batch * lc) % TOK_TILE == 0 and n_batch < 16

    x_lat, x_ctx = x.reshape(n_lat, d), ctx.reshape(n_batch * lc, d)
    n_all = n_lat + n_batch * lc
    c_rows = jnp.zeros((16, d), F32).at[:n_batch].set(c).at[n_batch].set(c_ctx)
    mod = _modulation(c_rows, mod_w, mod_b)
    tables = _rope_tables(seq)
    bd = _block_diag_ones()
    row2 = lambda v: v.reshape(1, -1)
    tile_gain = lambda gvec, reps: jnp.tile(gvec, reps).reshape(1, -1)

    w0 = ev_w_in[0]
    seg = np.cumsum([0, 256, 256, 512, 512, 32, 512, 128, 128])
    cols = lambda i: w0[:, seg[i]:seg[i + 1]]
    w_even = _bf(jnp.concatenate([cols(0), cols(1), cols(2), cols(3), cols(5), cols(6), cols(4),
                                  jnp.zeros((d, EV_END - EV_LR - 2 * GLA_GATE_RANK), F32)], axis=1))
    wvt_even = _bf(cols(7).T)
    hk = GLA_HEADS * GLA_DK
    gw = jnp.zeros((LANES, 2 * hk), F32)
    gw = gw.at[0:GLA_GATE_RANK, 0:hk].set(gla_gate_w[0, 0])
    gw = gw.at[GLA_GATE_RANK:2 * GLA_GATE_RANK, hk:2 * hk].set(gla_gate_w[0, 1])
    gb = gla_gate_b[0].reshape(1, 2 * hk)
    gq, gk, gv, gr, g, aq, ak, avt = _proj_even(
        x_lat, x_ctx, mod[0], row2(norm_mix[0]), w_even, wvt_even, _bf(gw), gb,
        tile_gain(att_q_norm[0], ATT_HEADS), tile_gain(att_k_norm[0], ATT_KV_HEADS), tables, bd, geom)
    o_lat, o_ctx = _gla(gq, gk, gv, g, n_batch, seq, lc)
    bound0 = _score_bound(att_q_norm[0], att_k_norm[0])
    att_lat = _attention("dense", aq, ak, avt, bound0, n_batch, seq, lc, ATT_HEADS, ATT_KV_HEADS)
    att_ctx = _attention("ctx", aq, ak, avt, bound0, n_batch, seq, lc, ATT_HEADS, ATT_KV_HEADS)
    router = _router_weights(router_group_w[0], router_group_b[0], router_expert_w[0], router_expert_b[0])
    tt = TOK_TILE
    nlt = geom[0]
    gn = row2(gla_out_norm[0])
    w_out0 = _bf(ev_w_out[0])
    hv, ha = GLA_HEADS * GLA_DV, ATT_HEADS * HEAD_DIM
    x_mid, hrow, rt, counts = _out_proj(
        functools.partial(_out_even_kernel, geom),
        (o_lat, o_ctx, gr, att_lat, att_ctx, gn, w_out0, x_lat, x_ctx),
        _lat_ctx_specs((tt, hv), nlt, lead=(2,)) + [pl.BlockSpec((tt, hv), lambda t: (t, 0))]
        + _lat_ctx_specs((tt, ha), nlt) + [_full(gn.shape), _full(w_out0.shape)] + _lat_ctx_specs((tt, d), nlt),
        d, mod[0], row2(norm_ffn[0]), router, n_all, "out_even")
    experts = _expert_weights(exp_w_gate, exp_w_up, exp_w_down)
    f0, pos0 = _moe(hrow, rt, counts, experts, 0)

    w_odd = od_w_in[0]
    x1, q1, k1, v1t = _proj_odd(
        x_mid, f0, pos0, rt, mod[0], mod[1], row2(norm_mix[1]), _bf(w_odd[:, 0:d + LANES]), _bf(w_odd[:, d + LANES:].T),
        tile_gain(swa_q_norm[0], SWA_HEADS), tile_gain(swa_k_norm[0], SWA_KV_HEADS), tables, bd, geom)
    att1 = _attention("window", q1, k1, v1t, _score_bound(swa_q_norm[0], swa_k_norm[0]), n_batch, seq, lc,
                      SWA_HEADS, SWA_KV_HEADS, sink=swa_sink[0])
    router = _router_weights(router_group_w[1], router_group_b[1], router_expert_w[1], router_expert_b[1])
    w_out1 = _bf(od_w_out[0])
    tok = lambda w_: pl.BlockSpec((tt, w_), lambda t: (t, 0))
    x2, hrow1, rt1, counts1 = _out_proj(
        functools.partial(_out_odd_kernel, geom), (att1, w_out1, x1),
        [tok(SWA_HEADS * HEAD_DIM), _full(w_out1.shape), tok(d)],
        d, mod[1], row2(norm_ffn[1]), router, n_lat, "out_odd")
    f1, pos1 = _moe(hrow1, rt1, counts1, experts, 1)
    out = _final(x2, f1, pos1, rt1, mod[1], tiles_per_batch)
    return out.reshape(n_batch, seq, d)
```

```python
import functools

import numpy as np
import jax
import jax.numpy as jnp
from jax import lax
from jax.experimental import pallas as pl
from jax.experimental.pallas import tpu as pltpu

F32 = jnp.float32
BF16 = jnp.bfloat16

GRID_W = 64
HEAD_DIM = 64
AXIS_DIM = HEAD_DIM // 2
ROPE_THETA = 10000.0
EPS = 1e-6
GLA_HEADS = 4
GLA_DK = 64
GLA_DV = 128
GLA_GATE_RANK = 16
GLA_GATE_NORM = 16.0
GLA_CHUNK = 64
ATT_HEADS = 8
ATT_KV_HEADS = 2
SWA_HEADS = 16
SWA_KV_HEADS = 2
SWA_WINDOW = 128
MOE_GROUPS = 4
MOE_EPG = 4
D_EXPERT = 256
N_PAIRS = 6
N_BUCKETS = MOE_GROUPS * N_PAIRS

LANES = 128
MXU_DIM = 256
TOK_TILE = 512
ATT_Q_TILE = 128
ATT_K_CHUNK = 512
ATT_COL = 4 * HEAD_DIM
WIN_SUB = 8
DENSE_SUB = 4
MOE_TILE = 256
MOE_PAIR = 4
DISPATCH_TILE = 1024
N_ZERO_TILES = 2 * N_BUCKETS + MOE_PAIR
VMEM_LIMIT = 56 * 1024 * 1024
NEG_BIG = -1e30
LOG2E = 1.4426950408889634
Q_SCALE = HEAD_DIM ** -0.5 * LOG2E


def _bf(x):
    return x.astype(BF16)


def _split2(x):
    hi = _bf(x)
    lo = _bf(x - hi.astype(F32))
    return hi, lo


def _dot(a, b):
    return jnp.dot(a, b, preferred_element_type=F32)


def _dot_nt(a, b):
    return lax.dot_general(a, b, (((1,), (1,)), ((), ())), preferred_element_type=F32)


def _silu(x):
    return x / (1.0 + jnp.exp(-x))


def _rms(x):
    return x * lax.rsqrt(jnp.mean(x * x, axis=-1, keepdims=True) + EPS)


def _cparams(sem):
    return pltpu.CompilerParams(dimension_semantics=sem, vmem_limit_bytes=VMEM_LIMIT)


def _full(shape):
    n = len(shape)
    return pl.BlockSpec(shape, lambda *_: (0,) * n)


def _mod_kernel(c_ref, w_ref, b_ref, o_ref):
    c = c_ref[...]
    ch, cl = _split2(_silu(c))
    wh, wl = _split2(w_ref[0])
    o_ref[0] = _dot(ch, wh) + _dot(ch, wl) + _dot(cl, wh) + b_ref[0]


def _modulation(c_rows, mod_w, mod_b):
    depth, d, n = mod_w.shape
    tn = n // 4
    return pl.pallas_call(
        _mod_kernel,
        out_shape=jax.ShapeDtypeStruct((depth, 16, n), F32),
        grid=(depth, n // tn),
        in_specs=[pl.BlockSpec((16, d), lambda i, j: (0, 0)),
                  pl.BlockSpec((1, d, tn), lambda i, j: (i, 0, j)),
                  pl.BlockSpec((1, 1, tn), lambda i, j: (i, 0, j))],
        out_specs=pl.BlockSpec((1, 16, tn), lambda i, j: (i, 0, j)),
        compiler_params=_cparams(("arbitrary", "arbitrary")),
        name="modulation",
    )(c_rows, mod_w, mod_b.reshape(depth, 1, n))


def _mod_row(t, n_lat_tiles, tiles_per_batch, n_batch):
    return jnp.where(t < n_lat_tiles, t // tiles_per_batch, n_batch)


def _modulated(x, gain, mod_ref, row, k_shift, k_scale):
    d = x.shape[-1]
    shift = mod_ref[pl.ds(row, 1), k_shift * d:(k_shift + 1) * d]
    scale = mod_ref[pl.ds(row, 1), k_scale * d:(k_scale + 1) * d]
    return _rms(x) * gain * (1.0 + scale) + shift


def _rope_tables(seq):
    rows = seq // GRID_W
    row = np.repeat(np.arange(rows), GRID_W)
    col = np.tile(np.arange(GRID_W), rows)
    inv_freq = ROPE_THETA ** (-np.arange(0, AXIS_DIM, 2, dtype=np.float64) / AXIS_DIM)
    ang = np.stack([row[:, None] * inv_freq, col[:, None] * inv_freq], axis=1)
    cos, sin = np.cos(ang), np.sin(ang)
    zero = np.zeros_like(sin)
    cos64 = np.concatenate([cos[:, 0], cos[:, 0], cos[:, 1], cos[:, 1]], axis=-1)
    sa64 = np.concatenate([-sin[:, 0], zero[:, 0], -sin[:, 1], zero[:, 1]], axis=-1)
    sb64 = np.concatenate([zero[:, 0], sin[:, 0], zero[:, 1], sin[:, 1]], axis=-1)

    def widen(t, fill):
        t = np.concatenate([t, t], axis=-1)
        return jnp.asarray(np.concatenate([t, np.full((TOK_TILE, LANES), fill)], axis=0), F32)

    return widen(cos64, 1.0), widen(sa64, 0.0), widen(sb64, 0.0)


def _head_sumsq(y, bd):
    w = y.shape[-1]
    outs = []
    for s in range(0, w, MXU_DIM):
        e = min(s + MXU_DIM, w)
        hi, lo = _split2(y[:, s:e])
        b = bd[0:e - s, 0:e - s]
        outs.append(_dot(hi, b) + _dot(lo, b))
    return outs[0] if len(outs) == 1 else jnp.concatenate(outs, axis=-1)


def _qk_norm_rope(z, gain, bd, cos, sa, sb):
    w = z.shape[-1]
    rep = w // LANES
    ss = _head_sumsq(z * z, bd)
    y = z * lax.rsqrt(ss * (1.0 / HEAD_DIM) + EPS) * gain

    def wide(t):
        return t if rep == 1 else jnp.concatenate([t] * rep, axis=-1)

    return (y * wide(cos) + pltpu.roll(y, w - AXIS_DIM // 2, 1) * wide(sa)
            + pltpu.roll(y, AXIS_DIM // 2, 1) * wide(sb))


def _kv_rep(kv128):
    lane = lax.broadcasted_iota(jnp.int32, kv128.shape, 1)
    sw = pltpu.roll(kv128, HEAD_DIM, 1)
    a0 = jnp.where(lane < HEAD_DIM, kv128, sw)
    a1 = jnp.where(lane < HEAD_DIM, sw, kv128)
    return jnp.concatenate([a0, a0, a1, a1], axis=-1)


def _rope_block(t, n_lat_tiles, tiles_per_batch):
    return jnp.where(t < n_lat_tiles, t % tiles_per_batch, tiles_per_batch)


EV_GQ, EV_GK, EV_GV, EV_GR, EV_AQ, EV_AK, EV_LR, EV_END = 0, 256, 512, 1024, 1536, 2048, 2176, 2304


def _proj_even_kernel(geom, xl_ref, xc_ref, mod_ref, gain_ref, w_ref, wvt_ref, gw_ref, gb_ref, qg_ref, kg_ref,
                      cos_ref, sa_ref, sb_ref, bd_ref,
                      gq_ref, gk_ref, gv_ref, gr_ref, g_ref, aq_ref, ak_ref, avt_ref):
    t = pl.program_id(0)
    row = _mod_row(t, *geom)
    x = jnp.where(t < geom[0], xl_ref[...], xc_ref[...])
    hb = _bf(_modulated(x, gain_ref[...], mod_ref, row, 0, 1))

    def seg(a, b):
        return _dot(hb, w_ref[:, a:b])

    gq_ref[...] = seg(EV_GQ, EV_GK) * (GLA_DK ** -0.5)
    gk_ref[...] = seg(EV_GK, EV_GV)
    gv_ref[...] = _bf(seg(EV_GV, EV_GR))
    gr_ref[...] = seg(EV_GR, EV_AQ)
    zg = _dot(_bf(seg(EV_LR, EV_END)), gw_ref[...]) + gb_ref[...]
    g_ref[...] = -(jnp.maximum(-zg, 0.0) + jnp.log1p(jnp.exp(-jnp.abs(zg)))) * (1.0 / GLA_GATE_NORM)
    bd = bd_ref[...]
    cos, sa, sb = cos_ref[...], sa_ref[...], sb_ref[...]
    aq = _qk_norm_rope(seg(EV_AQ, EV_AK), qg_ref[...], bd, cos, sa, sb)
    aq_ref[...] = _bf(aq * Q_SCALE)
    ak = _qk_norm_rope(seg(EV_AK, EV_LR), kg_ref[...], bd, cos, sa, sb)
    ak_ref[...] = _bf(_kv_rep(ak))
    avt_ref[0] = _bf(_dot_nt(wvt_ref[...], hb))


def _vt_spec():
    return pl.BlockSpec((1, LANES, TOK_TILE), lambda t: (t, 0, 0))


def _lat_ctx_specs(block, n_lat_tiles, lead=()):
    z = (0,) * len(lead)
    lat = pl.BlockSpec(lead + block, lambda t: z + (jnp.minimum(t, n_lat_tiles - 1), 0))
    ctx = pl.BlockSpec(lead + block, lambda t: z + (jnp.maximum(t - n_lat_tiles, 0), 0))
    return [lat, ctx]


def _proj_even(x_lat, x_ctx, mod, gain, w, wvt, gw, gb, qg, kg, tables, bd, geom):
    d = x_lat.shape[1]
    n = x_lat.shape[0] + x_ctx.shape[0]
    n_lat_tiles, tiles_per_batch, _ = geom
    tt = TOK_TILE
    cos, sa, sb = tables
    tok = lambda w_: pl.BlockSpec((tt, w_), lambda t: (t, 0))
    rope = pl.BlockSpec((tt, LANES), lambda t: (_rope_block(t, n_lat_tiles, tiles_per_batch), 0))
    outs = [(256, F32), (256, F32), (512, BF16), (512, F32), (512, F32), (512, BF16), (512, BF16)]
    return pl.pallas_call(
        functools.partial(_proj_even_kernel, geom),
        out_shape=[jax.ShapeDtypeStruct((n, w_), dt) for w_, dt in outs]
        + [jax.ShapeDtypeStruct((n // tt, LANES, tt), BF16)],
        grid=(n // tt,),
        in_specs=_lat_ctx_specs((tt, d), n_lat_tiles)
        + [_full(mod.shape), _full(gain.shape), _full(w.shape), _full(wvt.shape), _full(gw.shape),
           _full(gb.shape), _full(qg.shape), _full(kg.shape), rope, rope, rope, _full(bd.shape)],
        out_specs=[tok(w_) for w_, _ in outs] + [_vt_spec()],
        compiler_params=_cparams(("arbitrary",)),
        name="proj_even",
    )(x_lat, x_ctx, mod, gain, w, wvt, gw, gb, qg, kg, cos, sa, sb, bd)


def _proj_odd_kernel(geom, x_ref, pos_ref, pos_next_ref, rt_ref, fs_hbm, mod_prev_ref, mod_ref, gain_ref, w_ref,
                     wvt_ref, qg_ref, kg_ref, cos_ref, sa_ref, sb_ref, bd_ref, x1_ref, q_ref, k_ref, vt_ref,
                     fbuf, fsem):
    t = pl.program_id(0)
    row = _mod_row(t, *geom)
    d = x_ref.shape[-1]
    gate = mod_prev_ref[pl.ds(row, 1), 5 * d:6 * d]
    f = _gather_tile(t, pl.num_programs(0), pos_ref, pos_next_ref, rt_ref, fs_hbm, fbuf, fsem,
                     inline_prefetch=True)
    x1 = x_ref[...] + gate * f
    x1_ref[...] = x1
    hb = _bf(_modulated(x1, gain_ref[...], mod_ref, row, 0, 1))
    bd = bd_ref[...]
    cos, sa, sb = cos_ref[...], sa_ref[...], sb_ref[...]
    q = _qk_norm_rope(_dot(hb, w_ref[:, 0:d]), qg_ref[...], bd, cos, sa, sb)
    q_ref[...] = _bf(q * Q_SCALE)
    k = _qk_norm_rope(_dot(hb, w_ref[:, d:d + LANES]), kg_ref[...], bd, cos, sa, sb)
    k_ref[...] = _bf(_kv_rep(k))
    vt_ref[0] = _bf(_dot_nt(wvt_ref[...], hb))
    _gather_drain(t, pl.num_programs(0), fs_hbm, fbuf, fsem)


def _proj_odd(x_all, f_sorted, pos, rt, mod_prev, mod, gain, w, wvt, qg, kg, tables, bd, geom):
    n, d = x_all.shape
    n_lat_tiles, tiles_per_batch, _ = geom
    tt = TOK_TILE
    cos, sa, sb = tables
    tok = lambda w_: pl.BlockSpec((tt, w_), lambda t: (t, 0))
    rope = pl.BlockSpec((tt, LANES), lambda t: (_rope_block(t, n_lat_tiles, tiles_per_batch), 0))
    outs = [(d, F32), (d, BF16), (512, BF16)]
    return pl.pallas_call(
        functools.partial(_proj_odd_kernel, geom),
        out_shape=[jax.ShapeDtypeStruct((n, w_), dt) for w_, dt in outs]
        + [jax.ShapeDtypeStruct((n // tt, LANES, tt), BF16)],
        grid=(n // tt,),
        in_specs=[tok(d)] + _gather_specs(n // tt)
        + [_full(mod_prev.shape), _full(mod.shape), _full(gain.shape), _full(w.shape),
           _full(wvt.shape), _full(qg.shape), _full(kg.shape), rope, rope, rope, _full(bd.shape)],
        out_specs=[tok(w_) for w_, _ in outs] + [_vt_spec()],
        scratch_shapes=_gather_scratch(d),
        compiler_params=_cparams(("arbitrary",)),
        name="proj_odd",
    )(x_all, pos, pos, rt, f_sorted, mod_prev, mod, gain, w, wvt, qg, kg, cos, sa, sb, bd)


N_LEVELS = 6
GLA_MX_ROWS = (N_LEVELS + 2) * GLA_CHUNK


def _gla_constants():
    c = GLA_CHUNK
    mx = np.zeros((2, GLA_MX_ROWS, c), np.float32)
    pat = np.zeros((2, N_LEVELS + 1, c, GLA_HEADS * c), np.float32)
    r = np.arange(c)
    for lvl in range(N_LEVELS):
        h = 1 << lvl
        ref = (r // (2 * h)) * 2 * h + h - 1
        upper = (r % (2 * h)) >= h
        m = np.zeros((c, c), np.float32)
        for i in range(c):
            if upper[i]:
                m[i, ref[i] + 1:i + 1] = 1.0
            else:
                m[i, i + 1:ref[i] + 1] = 1.0
        mx[0, lvl * c:(lvl + 1) * c] = m
        same = (r[:, None] // (2 * h)) == (r[None, :] // (2 * h))
        p = same & upper[:, None] & (~upper)[None, :]
        pat[0, lvl] = np.tile(p.astype(np.float32), (1, GLA_HEADS))
    mx[0, N_LEVELS * c:(N_LEVELS + 1) * c] = (r[None, :] <= r[:, None])
    mx[0, (N_LEVELS + 1) * c:(N_LEVELS + 2) * c] = (r[None, :] > r[:, None])
    pat[0, N_LEVELS] = np.tile(np.eye(c, dtype=np.float32), (1, GLA_HEADS))
    for k in range(N_LEVELS + 2):
        mx[1, k * c:(k + 1) * c] = mx[0, k * c:(k + 1) * c][::-1, ::-1]
    for k in range(N_LEVELS + 1):
        pat[1, k] = np.tile(pat[0, k, :, 0:c][::-1, ::-1], (1, GLA_HEADS))
    return mx, pat


GLA_BLOCK = 1024
GLA_GROUP = 4


def _gla_chunks(chunks, mx, pat_ref, s_ref):
    c = GLA_CHUNK
    lane_head = lax.broadcasted_iota(jnp.int32, (c, GLA_HEADS * GLA_DK), 1) // GLA_DK

    def stack_heads(a):
        return jnp.concatenate([_bf(jnp.where(lane_head == h, a, 0.0)) for h in range(GLA_HEADS)], axis=0)

    xs = [_dot(mx, jnp.concatenate(_split2(g), axis=0)) for _, _, _, g in chunks]
    atts = [jnp.where(pat_ref[0, N_LEVELS] > 0.0, _dot_nt(_bf(q), stack_heads(k)), 0.0)
            for q, k, _, _ in chunks]
    for lvl in range(N_LEVELS):
        for j, (q, k, _, _) in enumerate(chunks):
            e = jnp.exp(xs[j][lvl * c:(lvl + 1) * c])
            atts[j] = atts[j] + jnp.where(pat_ref[0, lvl] > 0.0, _dot_nt(_bf(q * e), stack_heads(k * e)), 0.0)
    outs, qes, news, a_cols = [], [], [], []
    for j, (q, k, v, _) in enumerate(chunks):
        vhead = lax.broadcasted_iota(jnp.int32, v.shape, 1) // GLA_DV
        v_bd = jnp.concatenate([jnp.where(vhead == h, v, jnp.zeros_like(v)) for h in range(GLA_HEADS)], axis=0)
        outs.append(_dot(_bf(atts[j]), v_bd))
        bcum = xs[j][N_LEVELS * c:(N_LEVELS + 1) * c]
        brem = xs[j][(N_LEVELS + 1) * c:(N_LEVELS + 2) * c]
        qes.append(stack_heads(q * jnp.exp(bcum)))
        kt = jnp.transpose(k * jnp.exp(brem))
        news.append(jnp.concatenate(
            [_dot(_bf(kt[h * GLA_DK:(h + 1) * GLA_DK]), v[:, h * GLA_DV:(h + 1) * GLA_DV])
             for h in range(GLA_HEADS)], axis=0))
        tot = bcum[0:1] + brem[0:1]
        a_cols.append(jnp.transpose(jnp.exp(jnp.broadcast_to(tot, (8, tot.shape[1]))))[:, 0:1])
    s = s_ref[...]
    for j in range(len(chunks)):
        o_inter = _dot(qes[j], _bf(s))
        outs[j] = outs[j] + jnp.concatenate([o_inter[h * c:(h + 1) * c] for h in range(GLA_HEADS)], axis=-1)
        s = a_cols[j] * s + news[j]
    s_ref[...] = s
    return outs


def _gla_kernel(n_lat_chunks, n_ctx_chunks, ql_ref, kl_ref, vl_ref, gl_ref, qc_ref, kc_ref, vc_ref, gc_ref,
                mx_ref, pat_ref, ol_ref, oc_ref, s_ref):
    d = pl.program_id(1)
    c = GLA_CHUNK
    mx = mx_ref[0]

    def run(n_chunks, q_ref, k_ref, v_ref, g_ref, o_ref):
        group = min(GLA_GROUP, n_chunks)

        def body(i, carry):
            rows = []
            for j in range(group):
                step = i * group + j
                ci = jnp.where(d == 0, step, n_chunks - 1 - step)
                rows.append(pl.ds(pl.multiple_of(ci * c, c), c))
            outs = _gla_chunks([(q_ref[r, :], k_ref[r, :], v_ref[r, :], g_ref[r, :]) for r in rows],
                               mx, pat_ref, s_ref)
            for r, o in zip(rows, outs):
                o_ref[0, r, :] = o
            return carry
        lax.fori_loop(0, n_chunks // group, body, 0)

    @pl.when(pl.program_id(2) == 0)
    def _():
        s_ref[...] = jnp.zeros_like(s_ref)
        run(n_ctx_chunks, qc_ref, kc_ref, vc_ref, gc_ref, oc_ref)

    run(n_lat_chunks, ql_ref, kl_ref, vl_ref, gl_ref, ol_ref)


def _gla(gq, gk, gv, g, n_batch, seq, lc):
    mx_np, pat_np = _gla_constants()
    mx = jnp.asarray(np.concatenate([mx_np, mx_np], axis=2), BF16)
    pat = jnp.asarray(pat_np, F32)
    ctx0 = n_batch * seq // lc
    hk, hv = GLA_HEADS * GLA_DK, GLA_HEADS * GLA_DV
    blk = min(GLA_BLOCK, seq)
    nb = seq // blk
    for rows_ in (blk, lc):
        assert rows_ % (GLA_CHUNK * min(GLA_GROUP, rows_ // GLA_CHUNK)) == 0
    assert seq % blk == 0
    row = lambda b, d_, i: b * nb + jnp.where(d_ == 0, i, nb - 1 - i)
    lat = lambda w_, col: pl.BlockSpec((blk, w_), lambda b, d_, i: (row(b, d_, i), col(d_)))
    ctx = lambda w_, col: pl.BlockSpec((lc, w_), lambda b, d_, i: (ctx0 + b, col(d_)))
    zero = lambda d_: 0
    same = lambda d_: d_
    return pl.pallas_call(
        functools.partial(_gla_kernel, blk // GLA_CHUNK, lc // GLA_CHUNK),
        out_shape=[jax.ShapeDtypeStruct((2, n_batch * seq, hv), F32),
                   jax.ShapeDtypeStruct((2, n_batch * lc, hv), F32)],
        grid=(n_batch, 2, nb),
        in_specs=[lat(hk, zero), lat(hk, zero), lat(hv, zero), lat(hk, same),
                  ctx(hk, zero), ctx(hk, zero), ctx(hv, zero), ctx(hk, same),
                  pl.BlockSpec((1,) + mx.shape[1:], lambda b, d_, i: (d_, 0, 0)),
                  pl.BlockSpec((1,) + pat.shape[1:], lambda b, d_, i: (d_, 0, 0, 0))],
        out_specs=[pl.BlockSpec((1, blk, hv), lambda b, d_, i: (d_, row(b, d_, i), 0)),
                   pl.BlockSpec((1, lc, hv), lambda b, d_, i: (d_, b, 0))],
        scratch_shapes=[pltpu.VMEM((GLA_HEADS * GLA_DK, GLA_DV), F32)],
        compiler_params=_cparams(("arbitrary", "arbitrary", "arbitrary")),
        name="gla_scan",
    )(gq, gk, gv, g, gq, gk, gv, g, mx, pat)


def _stack_heads(q):
    lane_head = lax.broadcasted_iota(jnp.int32, q.shape, 1) // HEAD_DIM
    return jnp.concatenate([jnp.where(lane_head == h, q, jnp.zeros_like(q)) for h in range(4)], axis=0)


SAFE_SCORE_BOUND = 60.0
ONES_ROWS = 16


def _with_ones(vt):
    return jnp.concatenate([vt, jnp.ones((ONES_ROWS, vt.shape[1]), vt.dtype)], axis=0)


def _attn_store(acc, l, o_ref, u, tq):
    out = acc * (1.0 / l)
    out = jnp.concatenate([out[:, h * tq:(h + 1) * tq] for h in range(4)], axis=0)
    o_ref[u * tq:(u + 1) * tq, :] = _bf(jnp.transpose(out))


def _attn_dense_kernel(n_chunks, n_sub, *refs):
    if n_chunks:
        bound_ref, q_ref, kl_ref, vtl_ref, kc_ref, vtc_ref, o_ref = refs
    else:
        bound_ref, q_ref, kc_ref, vtc_ref, o_ref = refs
    tq = q_ref.shape[0] // n_sub
    cols = 4 * tq
    q4 = [_stack_heads(q_ref[u * tq:(u + 1) * tq, :]) for u in range(n_sub)]

    def scores(c, u):
        if c < n_chunks:
            return _dot_nt(kl_ref[c * ATT_K_CHUNK:(c + 1) * ATT_K_CHUNK, :], q4[u])
        return _dot_nt(kc_ref[...], q4[u])

    def values(c):
        if c < n_chunks:
            tile, off = divmod(c * ATT_K_CHUNK, TOK_TILE)
            return vtl_ref[tile][:, off:off + ATT_K_CHUNK]
        return vtc_ref[0]

    def update(carry, st, vt_aug):
        m, acc = carry
        m_new = jnp.maximum(m, jnp.max(st, axis=0, keepdims=True))
        acc = jnp.exp2(m - m_new) * acc + _dot(vt_aug, _bf(jnp.exp2(st - m_new)))
        return m_new, acc

    def run(fixed_ref):
        if fixed_ref is None:
            carry = [(jnp.full((1, cols), NEG_BIG, F32), jnp.zeros((HEAD_DIM + ONES_ROWS, cols), F32))
                     for _ in range(n_sub)]
        else:
            carry = [jnp.zeros((HEAD_DIM + ONES_ROWS, cols), F32) for _ in range(n_sub)]
        st = [scores(0, u) for u in range(n_sub)]
        for c in range(n_chunks + 1):
            st_next = [scores(c + 1, u) for u in range(n_sub)] if c < n_chunks else None
            vt_aug = _with_ones(values(c))
            if fixed_ref is None:
                carry = [update(carry[u], st[u], vt_aug) for u in range(n_sub)]
            else:
                carry = [carry[u] + _dot(vt_aug, _bf(jnp.exp2(st[u] - fixed_ref))) for u in range(n_sub)]
            st = st_next
        for u in range(n_sub):
            acc = carry[u][1] if fixed_ref is None else carry[u]
            _attn_store(acc[0:HEAD_DIM], acc[HEAD_DIM:HEAD_DIM + 1], o_ref, u, tq)

    bound = bound_ref[0]

    @pl.when(bound <= SAFE_SCORE_BOUND)
    def _():
        run(bound)

    @pl.when(bound > SAFE_SCORE_BOUND)
    def _():
        run(None)


def _attn_window_kernel(seq, bound_ref, q_ref, *refs):
    nk = WIN_SUB + 2
    k_refs, v_refs = refs[0:nk], refs[nk:2 * nk]
    kc_ref, vtc_ref, sink_ref, o_ref = refs[2 * nk:]
    tq = SWA_WINDOW
    i = pl.program_id(2)
    kb = jnp.concatenate([r[...] for r in k_refs], axis=0)
    vtb = _with_ones(jnp.concatenate([r[0] for r in v_refs], axis=1))
    kc, vtc, sink = kc_ref[...], _with_ones(vtc_ref[0]), sink_ref[0]
    span = 3 * tq
    r_i = lax.broadcasted_iota(jnp.int32, (tq, tq), 0)
    c_i = lax.broadcasted_iota(jnp.int32, (tq, tq), 1)
    band_lo = jnp.where(r_i >= c_i, 0.0, NEG_BIG)
    band_hi = jnp.where(r_i <= c_i, 0.0, NEG_BIG)
    n_blocks = seq // tq

    def run(use_bound):
        sb, sc = [], []
        for u in range(WIN_SUB):
            q4 = _stack_heads(q_ref[u * tq:(u + 1) * tq, :])
            first = i * WIN_SUB + u - 1
            bias_lo = jnp.where(first < 0, NEG_BIG, band_lo)
            bias_hi = jnp.where(first + 2 >= n_blocks, NEG_BIG, band_hi)
            s = _dot_nt(kb[u * tq:u * tq + span], q4)
            sb.append(jnp.concatenate([s[0:tq] + jnp.concatenate([bias_lo] * 4, axis=1), s[tq:2 * tq],
                                       s[2 * tq:span] + jnp.concatenate([bias_hi] * 4, axis=1)], axis=0))
            sc.append(_dot_nt(kc, q4))
        ms, pbs, pcs = [], [], []
        for u in range(WIN_SUB):
            if use_bound:
                m = jnp.maximum(bound_ref[0], sink)
            else:
                m = jnp.maximum(jnp.maximum(jnp.max(sb[u], axis=0, keepdims=True),
                                            jnp.max(sc[u], axis=0, keepdims=True)), sink)
            ms.append(m)
            pbs.append(_bf(jnp.exp2(sb[u] - m)))
            pcs.append(_bf(jnp.exp2(sc[u] - m)))
        for u in range(WIN_SUB):
            acc = _dot(vtb[:, u * tq:u * tq + span], pbs[u]) + _dot(vtc, pcs[u])
            l = acc[HEAD_DIM:HEAD_DIM + 1] + jnp.exp2(sink - ms[u])
            _attn_store(acc[0:HEAD_DIM], l, o_ref, u, tq)

    @pl.when(bound_ref[0] <= SAFE_SCORE_BOUND)
    def _():
        run(True)

    @pl.when(bound_ref[0] > SAFE_SCORE_BOUND)
    def _():
        run(False)


def _score_bound(q_gain, k_gain):
    return (HEAD_DIM * Q_SCALE * 1.02 * jnp.max(jnp.abs(q_gain)) * jnp.max(jnp.abs(k_gain))).reshape(1)


def _attention(mode, q, k_rep, vt, bound, n_batch, seq, lc, n_heads, n_kv, sink=None):
    ncol = n_heads * HEAD_DIM // ATT_COL
    col_per_kv = ncol // n_kv
    tq = ATT_Q_TILE
    tpb = seq // TOK_TILE
    n_lat_tiles = n_batch * tpb
    ctx_per_tile = TOK_TILE // lc
    ctx0 = n_batch * seq // lc
    assert TOK_TILE % ATT_K_CHUNK == 0 and TOK_TILE % lc == 0 and tq == SWA_WINDOW
    kv = lambda j: j // col_per_kv
    k_ctx = pl.BlockSpec((lc, ATT_COL), lambda b, j, i: (ctx0 + b, kv(j)))
    vt_ctx = pl.BlockSpec((1, HEAD_DIM, lc), lambda b, j, i: (n_lat_tiles + b // ctx_per_tile, kv(j), b % ctx_per_tile))
    if mode == "ctx":
        nq = lc // tq
        q0 = n_batch * seq // tq
        kern = functools.partial(_attn_dense_kernel, 0, 1)
        args = (q, k_rep, vt)
        in_specs = [pl.BlockSpec((tq, ATT_COL), lambda b, j, i: (q0 + b * nq + i, j)), k_ctx, vt_ctx]
    elif mode == "dense":
        tq = DENSE_SUB * ATT_Q_TILE
        nq = seq // tq
        kern = functools.partial(_attn_dense_kernel, seq // ATT_K_CHUNK, DENSE_SUB)
        args = (q, k_rep, vt, k_rep, vt)
        in_specs = [pl.BlockSpec((tq, ATT_COL), lambda b, j, i: (b * nq + i, j)),
                    pl.BlockSpec((seq, ATT_COL), lambda b, j, i: (b, kv(j))),
                    pl.BlockSpec((tpb, HEAD_DIM, TOK_TILE), lambda b, j, i: (b, kv(j), 0)), k_ctx, vt_ctx]
    else:
        wb = SWA_WINDOW
        tq = WIN_SUB * wb
        nq = seq // tq
        nkb = seq // wb
        per_tile = TOK_TILE // wb
        kern = functools.partial(_attn_window_kernel, seq)
        nb = lambda i, o: jnp.clip(i * WIN_SUB + o, 0, nkb - 1)
        k_nb = lambda o: pl.BlockSpec((wb, ATT_COL), lambda b, j, i: (b * nkb + nb(i, o), kv(j)))
        v_nb = lambda o: pl.BlockSpec(
            (1, HEAD_DIM, wb), lambda b, j, i: (b * tpb + nb(i, o) // per_tile, kv(j), nb(i, o) % per_tile))
        offs = range(-1, WIN_SUB + 1)
        sink_row = jnp.repeat(sink.reshape(ncol, 1, 4), wb, axis=2) * LOG2E
        args = (q,) + (k_rep,) * len(offs) + (vt,) * len(offs) + (k_rep, vt, sink_row)
        in_specs = ([pl.BlockSpec((tq, ATT_COL), lambda b, j, i: (b * nq + i, j))]
                    + [k_nb(o) for o in offs] + [v_nb(o) for o in offs]
                    + [k_ctx, vt_ctx, pl.BlockSpec((1, 1, 4 * wb), lambda b, j, i: (j, 0, 0))])
    args = (bound,) + args
    in_specs = [pl.BlockSpec(memory_space=pltpu.SMEM)] + in_specs
    return pl.pallas_call(
        kern,
        out_shape=jax.ShapeDtypeStruct((n_batch * nq * tq, n_heads * HEAD_DIM), BF16),
        grid=(n_batch, ncol, nq),
        in_specs=in_specs,
        out_specs=pl.BlockSpec((tq, ATT_COL), lambda b, j, i: (b * nq + i, j)),
        compiler_params=_cparams(("arbitrary", "arbitrary", "arbitrary")),
        name="attention_" + mode,
    )(*args)


ROUTE_ROWS = 8
TOKEN_TILE = (8, LANES)
OUT_TOKEN_TILE = (16, LANES)
OUT_TOKEN_DTYPE = BF16


def _to_token_tiles(x, tile):
    return x.reshape((x.shape[0],) + tile)


def _from_token_tiles(tiles):
    return tiles.reshape(tiles.shape[0], tiles.shape[1] * tiles.shape[2])


def _route(h, wrt_ref, wrt_hi_ref, brt_ref):
    hh, hl = _split2(h)
    a = _dot_nt(wrt_ref[...], hh)
    lt = a[0:LANES] + a[LANES:2 * LANES] + _dot_nt(wrt_hi_ref[...], hl) + brt_ref[...]
    col = lambda i: lt[i:i + 1, :]
    gl = [col(i) for i in range(MOE_GROUPS)]
    gmax = functools.reduce(jnp.maximum, gl)
    gi = jnp.where(gl[0] == gmax, 0, jnp.where(gl[1] == gmax, 1, jnp.where(gl[2] == gmax, 2, 3)))
    g_weight = 1.0 / functools.reduce(lambda a, b: a + b, [jnp.exp(x - gmax) for x in gl])
    el = []
    for j in range(MOE_EPG):
        cand = [col(MOE_GROUPS + g * MOE_EPG + j) for g in range(MOE_GROUPS)]
        el.append(jnp.where(gi == 0, cand[0], jnp.where(gi == 1, cand[1], jnp.where(gi == 2, cand[2], cand[3]))))
    m1 = functools.reduce(jnp.maximum, el)
    i1 = jnp.where(el[0] == m1, 0, jnp.where(el[1] == m1, 1, jnp.where(el[2] == m1, 2, 3)))
    rest = [jnp.where(i1 == j, -jnp.inf, el[j]) for j in range(MOE_EPG)]
    m2 = functools.reduce(jnp.maximum, rest)
    i2 = jnp.where(rest[0] == m2, 0, jnp.where(rest[1] == m2, 1, jnp.where(rest[2] == m2, 2, 3)))
    e2 = jnp.exp(m2 - m1)
    w1 = g_weight / (1.0 + e2)
    w2 = g_weight * e2 / (1.0 + e2)
    lo = jnp.minimum(i1, i2)
    hi = jnp.maximum(i1, i2)
    w_lo = jnp.where(i1 == lo, w1, w2)
    w_hi = jnp.where(i1 == lo, w2, w1)
    pair = jnp.where(lo == 0, hi - 1, jnp.where(lo == 1, hi + 1, N_PAIRS - 1))
    return w_lo, w_hi, gi * N_PAIRS + pair


def _out_tail(geom, m, x, mod_ref, gain_ffn_ref, wrt_ref, wrt_hi_ref, brt_ref, triu_ref,
              x_new_ref, hrow_ref, rt_ref, counts_ref, run_ref):
    t = pl.program_id(0)
    row = _mod_row(t, *geom)
    tt, d = x.shape

    @pl.when(t == 0)
    def _():
        run_ref[...] = jnp.zeros_like(run_ref)

    x_new = x + mod_ref[pl.ds(row, 1), 2 * d:3 * d] * m
    x_new_ref[...] = x_new
    h = _modulated(x_new, gain_ffn_ref[...], mod_ref, row, 3, 4)
    w_lo, w_hi, bucket = _route(h, wrt_ref, wrt_hi_ref, brt_ref)
    onehot = lax.broadcasted_iota(jnp.int32, (LANES, tt), 0) == bucket
    ones = jnp.where(onehot, 1.0, 0.0)
    before = _dot(_bf(ones), triu_ref[...]) + run_ref[...]
    rank = jnp.sum(jnp.where(onehot, before, 0.0), axis=0, keepdims=True)
    run = run_ref[...] + jnp.sum(ones, axis=1, keepdims=True)
    run_ref[...] = run
    counts_ref[...] = jnp.broadcast_to(run, counts_ref.shape)
    rec = jnp.concatenate([w_lo, w_hi, bucket.astype(F32), rank, jnp.zeros((ROUTE_ROWS - 4, tt), F32)], axis=0)
    rt_ref[0] = rec
    hrow_ref[...] = _to_token_tiles(h, TOKEN_TILE)


def _pick(t, n_lat_tiles, lat_ref, ctx_ref):
    return jnp.where(t < n_lat_tiles, lat_ref[...], ctx_ref[...])


def _out_even_kernel(geom, ol_ref, oc_ref, r_ref, attl_ref, attc_ref, gn_ref, w_ref, xl_ref, xc_ref, mod_ref,
                     gain_ffn_ref, wrt_ref, wrt_hi_ref, brt_ref, triu_ref,
                     x_new_ref, hrow_ref, rt_ref, counts_ref, run_ref):
    t = pl.program_id(0)
    o2 = _pick(t, geom[0], ol_ref, oc_ref)
    o = o2[0] + o2[1]
    r = r_ref[...]
    parts = []
    for h in range(GLA_HEADS):
        sl = slice(h * GLA_DV, (h + 1) * GLA_DV)
        parts.append(_rms(o[:, sl]) * gn_ref[...] * _silu(r[:, sl]))
    a = _bf(jnp.concatenate(parts, axis=-1))
    half = a.shape[-1]
    m = _dot(a, w_ref[0:half, :]) + _dot(_pick(t, geom[0], attl_ref, attc_ref), w_ref[half:, :])
    _out_tail(geom, m, _pick(t, geom[0], xl_ref, xc_ref), mod_ref, gain_ffn_ref, wrt_ref, wrt_hi_ref, brt_ref,
              triu_ref, x_new_ref, hrow_ref, rt_ref, counts_ref, run_ref)


def _out_odd_kernel(geom, att_ref, w_ref, x_ref, mod_ref, gain_ffn_ref, wrt_ref, wrt_hi_ref, brt_ref, triu_ref,
                    x_new_ref, hrow_ref, rt_ref, counts_ref, run_ref):
    m = _dot(att_ref[...], w_ref[...])
    _out_tail(geom, m, x_ref[...], mod_ref, gain_ffn_ref, wrt_ref, wrt_hi_ref, brt_ref, triu_ref,
              x_new_ref, hrow_ref, rt_ref, counts_ref, run_ref)


def _out_proj(kernel, lead_args, lead_specs, d, mod, gain_ffn, router, n_rows, name):
    tt = TOK_TILE
    tok = lambda w_: pl.BlockSpec((tt, w_), lambda t: (t, 0))
    r = np.arange(tt)
    triu = jnp.asarray((r[:, None] < r[None, :]).astype(np.float32), BF16)
    wrt, wrt_hi, br = router
    brt = jnp.broadcast_to(br.reshape(LANES, 1), (LANES, tt))
    return pl.pallas_call(
        kernel,
        out_shape=[jax.ShapeDtypeStruct((n_rows, d), F32),
                   jax.ShapeDtypeStruct((n_rows,) + TOKEN_TILE, F32),
                   jax.ShapeDtypeStruct((n_rows // tt, ROUTE_ROWS, tt), F32),
                   jax.ShapeDtypeStruct((LANES, LANES), F32)],
        grid=(n_rows // tt,),
        in_specs=lead_specs + [_full(mod.shape), _full(gain_ffn.shape), _full(wrt.shape), _full(wrt_hi.shape),
                               _full(brt.shape), _full(triu.shape)],
        out_specs=[tok(d), pl.BlockSpec((tt,) + TOKEN_TILE, lambda t: (t, 0, 0)),
                   pl.BlockSpec((1, ROUTE_ROWS, tt), lambda t: (t, 0, 0)), _full((LANES, LANES))],
        scratch_shapes=[pltpu.VMEM((LANES, 1), F32)],
        compiler_params=_cparams(("arbitrary",)),
        name=name,
    )(*lead_args, mod, gain_ffn, wrt, wrt_hi, brt, triu)


SUBLANES = 8


def _for_each_row(n_rows, start_row_copy):
    def body(g, c):
        base = g * SUBLANES
        for j in range(SUBLANES):
            start_row_copy(base + j, j % 2)
        return c
    lax.fori_loop(0, n_rows // SUBLANES, body, 0)


def _pos_kernel(start_ref, rt_ref, pos_ref):
    bucket = rt_ref[:, 2, :].astype(jnp.int32)
    base = jnp.zeros_like(bucket)
    for b in range(N_BUCKETS):
        base = jnp.where(bucket == b, start_ref[b], base)
    pos_ref[:, 0, :] = base + rt_ref[:, 3, :].astype(jnp.int32)


def _dispatch_kernel(last_ref, pos_ref, x_ref, xs_ref, zbuf, sem, zsem):
    tt = x_ref.shape[0]
    tm = zbuf.shape[0]

    @pl.when(pl.program_id(0) == 0)
    def _():
        zbuf[...] = jnp.zeros_like(zbuf)

        def zero_copy(b):
            return pltpu.make_async_copy(zbuf, xs_ref.at[pl.ds(jnp.maximum(last_ref[b], 0) * tm, tm)], zsem)

        for b in range(N_ZERO_TILES):
            @pl.when(last_ref[b] >= 0)
            def _():
                zero_copy(b).start()
        for b in range(N_ZERO_TILES):
            @pl.when(last_ref[b] >= 0)
            def _():
                zero_copy(b).wait()

    _for_each_row(tt, lambda r, prio: pltpu.make_async_copy(
        x_ref.at[r], xs_ref.at[pos_ref[0, 0, r]], sem).start(priority=prio))
    pltpu.make_async_copy(x_ref, xs_ref.at[pl.ds(0, tt)], sem).wait()


def _moe_mlp_kernel(tlo_ref, thi_ref, nused_ref, *refs):
    del tlo_ref, thi_ref
    x_refs = refs[0:MOE_PAIR]
    w_sets = [refs[MOE_PAIR + 6 * j:MOE_PAIR + 6 * (j + 1)] for j in range(MOE_PAIR)]
    f_ref = refs[7 * MOE_PAIR]
    tm = x_refs[0].shape[0]
    first = MOE_PAIR * pl.program_id(0)
    n_used = nused_ref[0]

    def experts(tiles):
        xs = [_bf(_from_token_tiles(x_refs[j][...])) for j in tiles]
        ups = [[_dot(x, w_sets[j][k][0]) for k in (0, 1, 3, 4)] for x, j in zip(xs, tiles)]
        hids = [(_bf(_silu(g_lo) * u_lo), _bf(_silu(g_hi) * u_hi)) for g_lo, u_lo, g_hi, u_hi in ups]
        for (h_lo, h_hi), j in zip(hids, tiles):
            f2 = jnp.concatenate([_dot(h_lo, w_sets[j][2][0]), _dot(h_hi, w_sets[j][5][0])], axis=1)
            f_ref[j * tm:(j + 1) * tm] = _to_token_tiles(f2.astype(OUT_TOKEN_DTYPE), OUT_TOKEN_TILE)

    live = jnp.clip(n_used - first, 0, MOE_PAIR)
    for k in range(MOE_PAIR + 1):
        @pl.when(live == k)
        def _(k=k):
            if k:
                experts(tuple(range(k)))
            if k < MOE_PAIR:
                f_ref[k * tm:MOE_PAIR * tm] = jnp.zeros(((MOE_PAIR - k) * tm,) + OUT_TOKEN_TILE, OUT_TOKEN_DTYPE)


def _moe(hrow, rt, counts, experts, layer):
    n = hrow.shape[0]
    d = TOKEN_TILE[0] * TOKEN_TILE[1]
    tm, tt = MOE_TILE, TOK_TILE
    n_tiles = -(-(n // tm + N_BUCKETS) // MOE_PAIR) * MOE_PAIR
    p = n_tiles * tm
    cnt = counts[0:N_BUCKETS, 0].astype(jnp.int32)
    tiles_b = (cnt + tm - 1) // tm
    tile_end = jnp.cumsum(tiles_b)
    start_b = (tile_end - tiles_b) * tm
    n_used = tile_end[-1].reshape(1)
    spare = n_used[0] + jnp.arange(N_ZERO_TILES - N_BUCKETS, dtype=jnp.int32)
    last_tile = jnp.concatenate([jnp.where(tiles_b > 0, tile_end - 1, -1), jnp.where(spare < n_tiles, spare, -1)])
    pos = pl.pallas_call(
        _pos_kernel,
        out_shape=jax.ShapeDtypeStruct((n // tt, 1, tt), jnp.int32),
        grid_spec=pltpu.PrefetchScalarGridSpec(
            num_scalar_prefetch=1, grid=(1,),
            in_specs=[pl.BlockSpec(rt.shape, lambda i, s: (0, 0, 0))],
            out_specs=pl.BlockSpec((n // tt, 1, tt), lambda i, s: (0, 0, 0))),
        compiler_params=_cparams(("arbitrary",)),
        name="moe_positions",
    )(start_b, rt)
    tile_ids = jnp.arange(n_tiles, dtype=jnp.int32)
    tile_bucket = jnp.sum((tile_ids[:, None] >= tile_end[None, :]).astype(jnp.int32), axis=1)
    tile_bucket = jnp.minimum(tile_bucket, jnp.sum((n_used[0] - 1 >= tile_end).astype(jnp.int32)))
    tile_bucket = jnp.minimum(tile_bucket, N_BUCKETS - 1)
    pair_lo = jnp.asarray([0, 0, 0, 1, 1, 2], jnp.int32)
    pair_hi = jnp.asarray([1, 2, 3, 2, 3, 3], jnp.int32)
    grp = tile_bucket // N_PAIRS
    w_gate, w_up, w_down = experts
    first = layer * MOE_GROUPS * MOE_EPG
    t_lo = first + grp * MOE_EPG + pair_lo[tile_bucket % N_PAIRS]
    t_hi = first + grp * MOE_EPG + pair_hi[tile_bucket % N_PAIRS]

    dt = DISPATCH_TILE if n % DISPATCH_TILE == 0 else tt
    xs = pl.pallas_call(
        _dispatch_kernel,
        out_shape=jax.ShapeDtypeStruct((p,) + TOKEN_TILE, F32),
        grid_spec=pltpu.PrefetchScalarGridSpec(
            num_scalar_prefetch=1, grid=(n // dt,),
            in_specs=[pl.BlockSpec((1, 1, dt), lambda t, s: (t, 0, 0), memory_space=pltpu.SMEM),
                      pl.BlockSpec((dt,) + TOKEN_TILE, lambda t, s: (t, 0, 0))],
            out_specs=pl.BlockSpec(memory_space=pl.ANY),
            scratch_shapes=[pltpu.VMEM((tm,) + TOKEN_TILE, F32), pltpu.SemaphoreType.DMA(()),
                            pltpu.SemaphoreType.DMA(())]),
        compiler_params=_cparams(("arbitrary",)),
        name="moe_dispatch",
    )(last_tile, pos.reshape(n // dt, 1, dt), hrow)

    f = D_EXPERT
    assert n_tiles % MOE_PAIR == 0

    def tile_specs(j):
        tile = lambda t: MOE_PAIR * t + j
        up_lo = pl.BlockSpec((1, d, f), lambda t, lo, hi, nu: (lo[tile(t)], 0, 0))
        up_hi = pl.BlockSpec((1, d, f), lambda t, lo, hi, nu: (hi[tile(t)], 0, 0))
        dn_lo = pl.BlockSpec((1, f, d), lambda t, lo, hi, nu: (lo[tile(t)], 0, 0))
        dn_hi = pl.BlockSpec((1, f, d), lambda t, lo, hi, nu: (hi[tile(t)], 0, 0))
        rows = pl.BlockSpec((tm,) + TOKEN_TILE, lambda t, lo, hi, nu: (jnp.minimum(tile(t), nu[0] - 1), 0, 0))
        return rows, [up_lo, up_lo, dn_lo, up_hi, up_hi, dn_hi]

    specs = [tile_specs(j) for j in range(MOE_PAIR)]
    grid_spec = pltpu.PrefetchScalarGridSpec(
        num_scalar_prefetch=3,
        grid=(n_tiles // MOE_PAIR,),
        in_specs=[rows for rows, _ in specs] + [w for _, ws in specs for w in ws],
        out_specs=pl.BlockSpec((MOE_PAIR * tm,) + OUT_TOKEN_TILE, lambda t, *_: (t, 0, 0)),
    )
    weights = (w_gate, w_up, w_down, w_gate, w_up, w_down)
    f_sorted = pl.pallas_call(
        _moe_mlp_kernel,
        out_shape=jax.ShapeDtypeStruct((p,) + OUT_TOKEN_TILE, OUT_TOKEN_DTYPE),
        grid_spec=grid_spec,
        compiler_params=_cparams(("arbitrary",)),
        name="moe_experts",
    )(t_lo, t_hi, n_used, *((xs,) * MOE_PAIR), *(weights * MOE_PAIR))
    return f_sorted, pos


def _gather_tile(t, n_t, pos_ref, pos_next_ref, rt_ref, src_hbm, buf, sem, inline_prefetch=False):
    tt = buf.shape[1]

    def start(p_ref, slot):
        _for_each_row(tt, lambda r, prio: pltpu.make_async_copy(
            src_hbm.at[p_ref[0, 0, r]], buf.at[slot].at[r], sem.at[slot]).start(priority=prio))

    slot = t % 2

    def wait(s):
        pltpu.make_async_copy(src_hbm.at[pl.ds(0, tt)], buf.at[s], sem.at[s]).wait()

    @pl.when(t == 0)
    def _():
        start(pos_ref, 0)

    wait(slot)
    if inline_prefetch:
        for r in range(tt):
            pltpu.make_async_copy(src_hbm.at[pos_next_ref[0, 0, r]], buf.at[1 - slot].at[r],
                                  sem.at[1 - slot]).start(priority=r % 2)
    else:
        @pl.when(t + 1 < n_t)
        def _():
            start(pos_next_ref, 1 - slot)

    f2 = _from_token_tiles(buf[slot]).astype(F32)
    d = f2.shape[1] // 2
    w = jnp.transpose(rt_ref[0])
    return w[:, 0:1] * f2[:, 0:d] + w[:, 1:2] * f2[:, d:2 * d]


def _gather_drain(t, n_t, src_hbm, buf, sem):
    tt = buf.shape[1]

    @pl.when(t == n_t - 1)
    def _():
        pltpu.make_async_copy(src_hbm.at[pl.ds(0, tt)], buf.at[1 - t % 2], sem.at[1 - t % 2]).wait()


def _gather_specs(n_t):
    tt = TOK_TILE
    return [pl.BlockSpec((1, 1, tt), lambda t: (t, 0, 0), memory_space=pltpu.SMEM),
            pl.BlockSpec((1, 1, tt), lambda t: (jnp.minimum(t + 1, n_t - 1), 0, 0), memory_space=pltpu.SMEM),
            pl.BlockSpec((1, ROUTE_ROWS, tt), lambda t: (t, 0, 0)),
            pl.BlockSpec(memory_space=pl.ANY)]


def _gather_scratch(d):
    assert 2 * d == OUT_TOKEN_TILE[0] * OUT_TOKEN_TILE[1]
    return [pltpu.VMEM((2, TOK_TILE) + OUT_TOKEN_TILE, OUT_TOKEN_DTYPE), pltpu.SemaphoreType.DMA((2,))]


def _final_kernel(tiles_per_batch, x_ref, pos_ref, pos_next_ref, rt_ref, fs_hbm, mod_ref, o_ref, fbuf, fsem):
    t = pl.program_id(0)
    row = t // tiles_per_batch
    d = x_ref.shape[-1]
    f = _gather_tile(t, pl.num_programs(0), pos_ref, pos_next_ref, rt_ref, fs_hbm, fbuf, fsem)
    o_ref[...] = x_ref[...] + mod_ref[pl.ds(row, 1), 5 * d:6 * d] * f


def _final(x_lat, f_sorted, pos, rt, mod, tiles_per_batch):
    n, d = x_lat.shape
    tok = pl.BlockSpec((TOK_TILE, d), lambda t: (t, 0))
    return pl.pallas_call(
        functools.partial(_final_kernel, tiles_per_batch),
        out_shape=jax.ShapeDtypeStruct((n, d), F32),
        grid=(n // TOK_TILE,),
        in_specs=[tok] + _gather_specs(n // TOK_TILE) + [_full(mod.shape)],
        out_specs=tok,
        scratch_shapes=_gather_scratch(d),
        compiler_params=_cparams(("arbitrary",)),
        name="final_residual",
    )(x_lat, pos, pos, rt, f_sorted, mod)


def _block_diag_ones():
    r = np.arange(MXU_DIM) // HEAD_DIM
    return jnp.asarray((r[:, None] == r[None, :]).astype(np.float32), BF16)


def _router_weights(wg, bg, we, be):
    d = wg.shape[0]
    n = MOE_GROUPS + MOE_GROUPS * MOE_EPG
    wt = jnp.concatenate([wg, we, jnp.zeros((d, LANES - n), F32)], axis=1).T
    b = jnp.concatenate([bg, be, jnp.zeros((LANES - n,), F32)])
    hi = _bf(wt)
    lo = _bf(wt - hi.astype(F32))
    return jnp.concatenate([hi, lo], axis=0), hi, b


def _expert_weights(w_gate, w_up, w_down):
    l, g, e, d, f = w_gate.shape
    n = l * g * e
    return _bf(w_gate).reshape(n, d, f), _bf(w_up).reshape(n, d, f), _bf(w_down).reshape(n, f, d)


def kernel(x, c, ctx, c_ctx, mod_w, mod_b, norm_mix, norm_ffn, ev_w_in, ev_w_out, gla_gate_w, gla_gate_b,
           gla_out_norm, att_q_norm, att_k_norm, od_w_in, od_w_out, swa_sink, swa_q_norm, swa_k_norm,
           router_group_w, router_group_b, router_expert_w, router_expert_b, exp_w_gate, exp_w_up, exp_w_down):
    n_batch, seq, d = x.shape
    lc = ctx.shape[1]
    depth = mod_w.shape[0]
    n_lat = n_batch * seq
    tiles_per_batch = seq // TOK_TILE
    geom = (n_lat // TOK_TILE, tiles_per_batch, n_batch)
    assert depth == 2 and seq % TOK_TILE == 0 and (n_batch * lc) % TOK_TILE == 0 and n_batch < 16

    x_lat, x_ctx = x.reshape(n_lat, d), ctx.reshape(n_batch * lc, d)
    n_all = n_lat + n_batch * lc
    c_rows = jnp.zeros((16, d), F32).at[:n_batch].set(c).at[n_batch].set(c_ctx)
    mod = _modulation(c_rows, mod_w, mod_b)
    tables = _rope_tables(seq)
    bd = _block_diag_ones()
    row2 = lambda v: v.reshape(1, -1)
    tile_gain = lambda gvec, reps: jnp.tile(gvec, reps).reshape(1, -1)

    w0 = ev_w_in[0]
    seg = np.cumsum([0, 256, 256, 512, 512, 32, 512, 128, 128])
    cols = lambda i: w0[:, seg[i]:seg[i + 1]]
    w_even = _bf(jnp.concatenate([cols(0), cols(1), cols(2), cols(3), cols(5), cols(6), cols(4),
                                  jnp.zeros((d, EV_END - EV_LR - 2 * GLA_GATE_RANK), F32)], axis=1))
    wvt_even = _bf(cols(7).T)
    hk = GLA_HEADS * GLA_DK
    gw = jnp.zeros((LANES, 2 * hk), F32)
    gw = gw.at[0:GLA_GATE_RANK, 0:hk].set(gla_gate_w[0, 0])
    gw = gw.at[GLA_GATE_RANK:2 * GLA_GATE_RANK, hk:2 * hk].set(gla_gate_w[0, 1])
    gb = gla_gate_b[0].reshape(1, 2 * hk)
    gq, gk, gv, gr, g, aq, ak, avt = _proj_even(
        x_lat, x_ctx, mod[0], row2(norm_mix[0]), w_even, wvt_even, _bf(gw), gb,
        tile_gain(att_q_norm[0], ATT_HEADS), tile_gain(att_k_norm[0], ATT_KV_HEADS), tables, bd, geom)
    o_lat, o_ctx = _gla(gq, gk, gv, g, n_batch, seq, lc)
    bound0 = _score_bound(att_q_norm[0], att_k_norm[0])
    att_lat = _attention("dense", aq, ak, avt, bound0, n_batch, seq, lc, ATT_HEADS, ATT_KV_HEADS)
    att_ctx = _attention("ctx", aq, ak, avt, bound0, n_batch, seq, lc, ATT_HEADS, ATT_KV_HEADS)
    router = _router_weights(router_group_w[0], router_group_b[0], router_expert_w[0], router_expert_b[0])
    tt = TOK_TILE
    nlt = geom[0]
    gn = row2(gla_out_norm[0])
    w_out0 = _bf(ev_w_out[0])
    hv, ha = GLA_HEADS * GLA_DV, ATT_HEADS * HEAD_DIM
    x_mid, hrow, rt, counts = _out_proj(
        functools.partial(_out_even_kernel, geom),
        (o_lat, o_ctx, gr, att_lat, att_ctx, gn, w_out0, x_lat, x_ctx),
        _lat_ctx_specs((tt, hv), nlt, lead=(2,)) + [pl.BlockSpec((tt, hv), lambda t: (t, 0))]
        + _lat_ctx_specs((tt, ha), nlt) + [_full(gn.shape), _full(w_out0.shape)] + _lat_ctx_specs((tt, d), nlt),
        d, mod[0], row2(norm_ffn[0]), router, n_all, "out_even")
    experts = _expert_weights(exp_w_gate, exp_w_up, exp_w_down)
    f0, pos0 = _moe(hrow, rt, counts, experts, 0)

    w_odd = od_w_in[0]
    x1, q1, k1, v1t = _proj_odd(
        x_mid, f0, pos0, rt, mod[0], mod[1], row2(norm_mix[1]), _bf(w_odd[:, 0:d + LANES]), _bf(w_odd[:, d + LANES:].T),
        tile_gain(swa_q_norm[0], SWA_HEADS), tile_gain(swa_k_norm[0], SWA_KV_HEADS), tables, bd, geom)
    att1 = _attention("window", q1, k1, v1t, _score_bound(swa_q_norm[0], swa_k_norm[0]), n_batch, seq, lc,
                      SWA_HEADS, SWA_KV_HEADS, sink=swa_sink[0])
    router = _router_weights(router_group_w[1], router_group_b[1], router_expert_w[1], router_expert_b[1])
    w_out1 = _bf(od_w_out[0])
    tok = lambda w_: pl.BlockSpec((tt, w_), lambda t: (t, 0))
    x2, hrow1, rt1, counts1 = _out_proj(
        functools.partial(_out_odd_kernel, geom), (att1, w_out1, x1),
        [tok(SWA_HEADS * HEAD_DIM), _full(w_out1.shape), tok(d)],
        d, mod[1], row2(norm_ffn[1]), router, n_lat, "out_odd")
    f1, pos1 = _moe(hrow1, rt1, counts1, experts, 1)
    out = _final(x2, f1, pos1, rt1, mod[1], tiles_per_batch)
    return out.reshape(n_batch, seq, d)
```

```python
import functools

import numpy as np
import jax
import jax.numpy as jnp
from jax import lax
from jax.experimental import pallas as pl
from jax.experimental.pallas import tpu as pltpu

F32 = jnp.float32
BF16 = jnp.bfloat16

GRID_W = 64
HEAD_DIM = 64
AXIS_DIM = HEAD_DIM // 2
ROPE_THETA = 10000.0
EPS = 1e-6
GLA_HEADS = 4
GLA_DK = 64
GLA_DV = 128
GLA_GATE_RANK = 16
GLA_GATE_NORM = 16.0
GLA_CHUNK = 64
ATT_HEADS = 8
ATT_KV_HEADS = 2
SWA_HEADS = 16
SWA_KV_HEADS = 2
SWA_WINDOW = 128
MOE_GROUPS = 4
MOE_EPG = 4
D_EXPERT = 256
N_PAIRS = 6
N_BUCKETS = MOE_GROUPS * N_PAIRS

LANES = 128
MXU_DIM = 256
TOK_TILE = 512
ATT_Q_TILE = 128
ATT_K_CHUNK = 512
ATT_COL = 4 * HEAD_DIM
WIN_SUB = 8
DENSE_SUB = 4
MOE_TILE = 256
MOE_PAIR = 2
DISPATCH_TILE = 1024
OUT_ODD_TILES = 2
VMEM_LIMIT = 56 * 1024 * 1024
NEG_BIG = -1e30
LOG2E = 1.4426950408889634
Q_SCALE = HEAD_DIM ** -0.5 * LOG2E


def _bf(x):
    return x.astype(BF16)


def _split2(x):
    hi = _bf(x)
    lo = _bf(x - hi.astype(F32))
    return hi, lo


def _dot(a, b):
    return jnp.dot(a, b, preferred_element_type=F32)


def _dot_nt(a, b):
    return lax.dot_general(a, b, (((1,), (1,)), ((), ())), preferred_element_type=F32)


def _silu(x):
    return x / (1.0 + jnp.exp(-x))


def _rms(x):
    return x * lax.rsqrt(jnp.mean(x * x, axis=-1, keepdims=True) + EPS)


def _cparams(sem):
    return pltpu.CompilerParams(dimension_semantics=sem, vmem_limit_bytes=VMEM_LIMIT)


def _full(shape):
    n = len(shape)
    return pl.BlockSpec(shape, lambda *_: (0,) * n)


def _mod_kernel(c_ref, w_ref, b_ref, o_ref):
    c = c_ref[...]
    ch, cl = _split2(_silu(c))
    wh, wl = _split2(w_ref[0])
    o_ref[0] = _dot(ch, wh) + _dot(ch, wl) + _dot(cl, wh) + b_ref[0]


def _modulation(c_rows, mod_w, mod_b):
    depth, d, n = mod_w.shape
    tn = n // 4
    return pl.pallas_call(
        _mod_kernel,
        out_shape=jax.ShapeDtypeStruct((depth, 16, n), F32),
        grid=(depth, n // tn),
        in_specs=[pl.BlockSpec((16, d), lambda i, j: (0, 0)),
                  pl.BlockSpec((1, d, tn), lambda i, j: (i, 0, j)),
                  pl.BlockSpec((1, 1, tn), lambda i, j: (i, 0, j))],
        out_specs=pl.BlockSpec((1, 16, tn), lambda i, j: (i, 0, j)),
        compiler_params=_cparams(("arbitrary", "arbitrary")),
        name="modulation",
    )(c_rows, mod_w, mod_b.reshape(depth, 1, n))


def _mod_row(t, n_lat_tiles, tiles_per_batch, n_batch):
    return jnp.where(t < n_lat_tiles, t // tiles_per_batch, n_batch)


def _modulated(x, gain, mod_ref, row, k_shift, k_scale):
    d = x.shape[-1]
    shift = mod_ref[pl.ds(row, 1), k_shift * d:(k_shift + 1) * d]
    scale = mod_ref[pl.ds(row, 1), k_scale * d:(k_scale + 1) * d]
    return _rms(x) * gain * (1.0 + scale) + shift


def _rope_tables(seq):
    rows = seq // GRID_W
    row = np.repeat(np.arange(rows), GRID_W)
    col = np.tile(np.arange(GRID_W), rows)
    inv_freq = ROPE_THETA ** (-np.arange(0, AXIS_DIM, 2, dtype=np.float64) / AXIS_DIM)
    ang = np.stack([row[:, None] * inv_freq, col[:, None] * inv_freq], axis=1)
    cos, sin = np.cos(ang), np.sin(ang)
    zero = np.zeros_like(sin)
    cos64 = np.concatenate([cos[:, 0], cos[:, 0], cos[:, 1], cos[:, 1]], axis=-1)
    sa64 = np.concatenate([-sin[:, 0], zero[:, 0], -sin[:, 1], zero[:, 1]], axis=-1)
    sb64 = np.concatenate([zero[:, 0], sin[:, 0], zero[:, 1], sin[:, 1]], axis=-1)

    def widen(t, fill):
        t = np.concatenate([t, t], axis=-1)
        return jnp.asarray(np.concatenate([t, np.full((TOK_TILE, LANES), fill)], axis=0), F32)

    return widen(cos64, 1.0), widen(sa64, 0.0), widen(sb64, 0.0)


def _head_sumsq(y, bd):
    w = y.shape[-1]
    outs = []
    for s in range(0, w, MXU_DIM):
        e = min(s + MXU_DIM, w)
        hi, lo = _split2(y[:, s:e])
        b = bd[0:e - s, 0:e - s]
        outs.append(_dot(hi, b) + _dot(lo, b))
    return outs[0] if len(outs) == 1 else jnp.concatenate(outs, axis=-1)


def _qk_norm_rope(z, gain, bd, cos, sa, sb):
    w = z.shape[-1]
    rep = w // LANES
    ss = _head_sumsq(z * z, bd)
    y = z * lax.rsqrt(ss * (1.0 / HEAD_DIM) + EPS) * gain

    def wide(t):
        return t if rep == 1 else jnp.concatenate([t] * rep, axis=-1)

    return (y * wide(cos) + pltpu.roll(y, w - AXIS_DIM // 2, 1) * wide(sa)
            + pltpu.roll(y, AXIS_DIM // 2, 1) * wide(sb))


def _kv_rep(kv128):
    lane = lax.broadcasted_iota(jnp.int32, kv128.shape, 1)
    sw = pltpu.roll(kv128, HEAD_DIM, 1)
    a0 = jnp.where(lane < HEAD_DIM, kv128, sw)
    a1 = jnp.where(lane < HEAD_DIM, sw, kv128)
    return jnp.concatenate([a0, a0, a1, a1], axis=-1)


def _rope_block(t, n_lat_tiles, tiles_per_batch):
    return jnp.where(t < n_lat_tiles, t % tiles_per_batch, tiles_per_batch)


EV_GQ, EV_GK, EV_GV, EV_GR, EV_AQ, EV_AK, EV_LR, EV_END = 0, 256, 512, 1024, 1536, 2048, 2176, 2304


def _proj_even_kernel(geom, xl_ref, xc_ref, mod_ref, gain_ref, w_ref, wvt_ref, gw_ref, gb_ref, qg_ref, kg_ref,
                      cos_ref, sa_ref, sb_ref, bd_ref,
                      gq_ref, gk_ref, gv_ref, gr_ref, g_ref, aq_ref, ak_ref, avt_ref):
    t = pl.program_id(0)
    row = _mod_row(t, *geom)
    x = jnp.where(t < geom[0], xl_ref[...], xc_ref[...])
    hb = _bf(_modulated(x, gain_ref[...], mod_ref, row, 0, 1))

    def seg(a, b):
        return _dot(hb, w_ref[:, a:b])

    gq_ref[...] = seg(EV_GQ, EV_GK) * (GLA_DK ** -0.5)
    gk_ref[...] = seg(EV_GK, EV_GV)
    gv_ref[...] = _bf(seg(EV_GV, EV_GR))
    gr_ref[...] = seg(EV_GR, EV_AQ)
    zg = _dot(_bf(seg(EV_LR, EV_END)), gw_ref[...]) + gb_ref[...]
    g_ref[...] = -(jnp.maximum(-zg, 0.0) + jnp.log1p(jnp.exp(-jnp.abs(zg)))) * (1.0 / GLA_GATE_NORM)
    bd = bd_ref[...]
    cos, sa, sb = cos_ref[...], sa_ref[...], sb_ref[...]
    aq = _qk_norm_rope(seg(EV_AQ, EV_AK), qg_ref[...], bd, cos, sa, sb)
    aq_ref[...] = _bf(aq * Q_SCALE)
    ak = _qk_norm_rope(seg(EV_AK, EV_LR), kg_ref[...], bd, cos, sa, sb)
    ak_ref[...] = _bf(_kv_rep(ak))
    avt_ref[0] = _bf(_dot_nt(wvt_ref[...], hb))


def _vt_spec():
    return pl.BlockSpec((1, LANES, TOK_TILE), lambda t: (t, 0, 0))


def _lat_ctx_specs(block, n_lat_tiles, lead=()):
    z = (0,) * len(lead)
    lat = pl.BlockSpec(lead + block, lambda t: z + (jnp.minimum(t, n_lat_tiles - 1), 0))
    ctx = pl.BlockSpec(lead + block, lambda t: z + (jnp.maximum(t - n_lat_tiles, 0), 0))
    return [lat, ctx]


def _proj_even(x_lat, x_ctx, mod, gain, w, wvt, gw, gb, qg, kg, tables, bd, geom):
    d = x_lat.shape[1]
    n = x_lat.shape[0] + x_ctx.shape[0]
    n_lat_tiles, tiles_per_batch, _ = geom
    tt = TOK_TILE
    cos, sa, sb = tables
    tok = lambda w_: pl.BlockSpec((tt, w_), lambda t: (t, 0))
    rope = pl.BlockSpec((tt, LANES), lambda t: (_rope_block(t, n_lat_tiles, tiles_per_batch), 0))
    outs = [(256, F32), (256, F32), (512, BF16), (512, F32), (512, F32), (512, BF16), (512, BF16)]
    return pl.pallas_call(
        functools.partial(_proj_even_kernel, geom),
        out_shape=[jax.ShapeDtypeStruct((n, w_), dt) for w_, dt in outs]
        + [jax.ShapeDtypeStruct((n // tt, LANES, tt), BF16)],
        grid=(n // tt,),
        in_specs=_lat_ctx_specs((tt, d), n_lat_tiles)
        + [_full(mod.shape), _full(gain.shape), _full(w.shape), _full(wvt.shape), _full(gw.shape),
           _full(gb.shape), _full(qg.shape), _full(kg.shape), rope, rope, rope, _full(bd.shape)],
        out_specs=[tok(w_) for w_, _ in outs] + [_vt_spec()],
        compiler_params=_cparams(("arbitrary",)),
        name="proj_even",
    )(x_lat, x_ctx, mod, gain, w, wvt, gw, gb, qg, kg, cos, sa, sb, bd)


def _proj_odd_kernel(geom, x_ref, pos_ref, pos_next_ref, rt_ref, fs_hbm, mod_prev_ref, mod_ref, gain_ref, w_ref,
                     wvt_ref, qg_ref, kg_ref, cos_ref, sa_ref, sb_ref, bd_ref, x1_ref, q_ref, k_ref, vt_ref,
                     fbuf, fsem):
    t = pl.program_id(0)
    row = _mod_row(t, *geom)
    d = x_ref.shape[-1]
    gate = mod_prev_ref[pl.ds(row, 1), 5 * d:6 * d]
    f = _gather_tile(t, pl.num_programs(0), pos_ref, pos_next_ref, rt_ref, fs_hbm, fbuf, fsem,
                     inline_prefetch=True)
    x1 = x_ref[...] + gate * f
    x1_ref[...] = x1
    hb = _bf(_modulated(x1, gain_ref[...], mod_ref, row, 0, 1))
    bd = bd_ref[...]
    cos, sa, sb = cos_ref[...], sa_ref[...], sb_ref[...]
    q = _qk_norm_rope(_dot(hb, w_ref[:, 0:d]), qg_ref[...], bd, cos, sa, sb)
    q_ref[...] = _bf(q * Q_SCALE)
    k = _qk_norm_rope(_dot(hb, w_ref[:, d:d + LANES]), kg_ref[...], bd, cos, sa, sb)
    k_ref[...] = _bf(_kv_rep(k))
    vt_ref[0] = _bf(_dot_nt(wvt_ref[...], hb))
    _gather_drain(t, pl.num_programs(0), fs_hbm, fbuf, fsem)


def _proj_odd(x_all, f_sorted, pos, rt, mod_prev, mod, gain, w, wvt, qg, kg, tables, bd, geom):
    n, d = x_all.shape
    n_lat_tiles, tiles_per_batch, _ = geom
    tt = TOK_TILE
    cos, sa, sb = tables
    tok = lambda w_: pl.BlockSpec((tt, w_), lambda t: (t, 0))
    rope = pl.BlockSpec((tt, LANES), lambda t: (_rope_block(t, n_lat_tiles, tiles_per_batch), 0))
    outs = [(d, F32), (d, BF16), (512, BF16)]
    return pl.pallas_call(
        functools.partial(_proj_odd_kernel, geom),
        out_shape=[jax.ShapeDtypeStruct((n, w_), dt) for w_, dt in outs]
        + [jax.ShapeDtypeStruct((n // tt, LANES, tt), BF16)],
        grid=(n // tt,),
        in_specs=[tok(d)] + _gather_specs(n // tt)
        + [_full(mod_prev.shape), _full(mod.shape), _full(gain.shape), _full(w.shape),
           _full(wvt.shape), _full(qg.shape), _full(kg.shape), rope, rope, rope, _full(bd.shape)],
        out_specs=[tok(w_) for w_, _ in outs] + [_vt_spec()],
        scratch_shapes=_gather_scratch(d),
        compiler_params=_cparams(("arbitrary",)),
        name="proj_odd",
    )(x_all, pos, pos, rt, f_sorted, mod_prev, mod, gain, w, wvt, qg, kg, cos, sa, sb, bd)


N_LEVELS = 6
GLA_MX_ROWS = (N_LEVELS + 2) * GLA_CHUNK


def _gla_constants():
    c = GLA_CHUNK
    mx = np.zeros((2, GLA_MX_ROWS, c), np.float32)
    pat = np.zeros((2, N_LEVELS + 1, c, GLA_HEADS * c), np.float32)
    r = np.arange(c)
    for lvl in range(N_LEVELS):
        h = 1 << lvl
        ref = (r // (2 * h)) * 2 * h + h - 1
        upper = (r % (2 * h)) >= h
        m = np.zeros((c, c), np.float32)
        for i in range(c):
            if upper[i]:
                m[i, ref[i] + 1:i + 1] = 1.0
            else:
                m[i, i + 1:ref[i] + 1] = 1.0
        mx[0, lvl * c:(lvl + 1) * c] = m
        same = (r[:, None] // (2 * h)) == (r[None, :] // (2 * h))
        p = same & upper[:, None] & (~upper)[None, :]
        pat[0, lvl] = np.tile(p.astype(np.float32), (1, GLA_HEADS))
    mx[0, N_LEVELS * c:(N_LEVELS + 1) * c] = (r[None, :] <= r[:, None])
    mx[0, (N_LEVELS + 1) * c:(N_LEVELS + 2) * c] = (r[None, :] > r[:, None])
    pat[0, N_LEVELS] = np.tile(np.eye(c, dtype=np.float32), (1, GLA_HEADS))
    for k in range(N_LEVELS + 2):
        mx[1, k * c:(k + 1) * c] = mx[0, k * c:(k + 1) * c][::-1, ::-1]
    for k in range(N_LEVELS + 1):
        pat[1, k] = np.tile(pat[0, k, :, 0:c][::-1, ::-1], (1, GLA_HEADS))
    return mx, pat


GLA_BLOCK = 1024
GLA_GROUP = 4


def _gla_chunks(chunks, mx, pat_ref, s_ref):
    c = GLA_CHUNK
    lane_head = lax.broadcasted_iota(jnp.int32, (c, GLA_HEADS * GLA_DK), 1) // GLA_DK

    def stack_heads(a):
        return jnp.concatenate([_bf(jnp.where(lane_head == h, a, 0.0)) for h in range(GLA_HEADS)], axis=0)

    xs = [_dot(mx, jnp.concatenate(_split2(g), axis=0)) for _, _, _, g in chunks]
    atts = [jnp.where(pat_ref[0, N_LEVELS] > 0.0, _dot_nt(_bf(q), stack_heads(k)), 0.0)
            for q, k, _, _ in chunks]
    for lvl in range(N_LEVELS):
        for j, (q, k, _, _) in enumerate(chunks):
            e = jnp.exp(xs[j][lvl * c:(lvl + 1) * c])
            atts[j] = atts[j] + jnp.where(pat_ref[0, lvl] > 0.0, _dot_nt(_bf(q * e), stack_heads(k * e)), 0.0)
    outs, qes, news, a_cols = [], [], [], []
    for j, (q, k, v, _) in enumerate(chunks):
        vhead = lax.broadcasted_iota(jnp.int32, v.shape, 1) // GLA_DV
        v_bd = jnp.concatenate([jnp.where(vhead == h, v, jnp.zeros_like(v)) for h in range(GLA_HEADS)], axis=0)
        outs.append(_dot(_bf(atts[j]), v_bd))
        bcum = xs[j][N_LEVELS * c:(N_LEVELS + 1) * c]
        brem = xs[j][(N_LEVELS + 1) * c:(N_LEVELS + 2) * c]
        qes.append(stack_heads(q * jnp.exp(bcum)))
        kt = jnp.transpose(k * jnp.exp(brem))
        news.append(jnp.concatenate(
            [_dot(_bf(kt[h * GLA_DK:(h + 1) * GLA_DK]), v[:, h * GLA_DV:(h + 1) * GLA_DV])
             for h in range(GLA_HEADS)], axis=0))
        tot = bcum[0:1] + brem[0:1]
        a_cols.append(jnp.transpose(jnp.exp(jnp.broadcast_to(tot, (8, tot.shape[1]))))[:, 0:1])
    s = s_ref[...]
    for j in range(len(chunks)):
        o_inter = _dot(qes[j], _bf(s))
        outs[j] = outs[j] + jnp.concatenate([o_inter[h * c:(h + 1) * c] for h in range(GLA_HEADS)], axis=-1)
        s = a_cols[j] * s + news[j]
    s_ref[...] = s
    return outs


def _gla_kernel(n_lat_chunks, n_ctx_chunks, ql_ref, kl_ref, vl_ref, gl_ref, qc_ref, kc_ref, vc_ref, gc_ref,
                mx_ref, pat_ref, ol_ref, oc_ref, s_ref):
    d = pl.program_id(1)
    c = GLA_CHUNK
    mx = mx_ref[0]

    def run(n_chunks, q_ref, k_ref, v_ref, g_ref, o_ref):
        group = min(GLA_GROUP, n_chunks)

        def body(i, carry):
            rows = []
            for j in range(group):
                step = i * group + j
                ci = jnp.where(d == 0, step, n_chunks - 1 - step)
                rows.append(pl.ds(pl.multiple_of(ci * c, c), c))
            outs = _gla_chunks([(q_ref[r, :], k_ref[r, :], v_ref[r, :], g_ref[r, :]) for r in rows],
                               mx, pat_ref, s_ref)
            for r, o in zip(rows, outs):
                o_ref[0, r, :] = o
            return carry
        lax.fori_loop(0, n_chunks // group, body, 0)

    @pl.when(pl.program_id(2) == 0)
    def _():
        s_ref[...] = jnp.zeros_like(s_ref)
        run(n_ctx_chunks, qc_ref, kc_ref, vc_ref, gc_ref, oc_ref)

    run(n_lat_chunks, ql_ref, kl_ref, vl_ref, gl_ref, ol_ref)


def _gla(gq, gk, gv, g, n_batch, seq, lc):
    mx_np, pat_np = _gla_constants()
    mx = jnp.asarray(np.concatenate([mx_np, mx_np], axis=2), BF16)
    pat = jnp.asarray(pat_np, F32)
    ctx0 = n_batch * seq // lc
    hk, hv = GLA_HEADS * GLA_DK, GLA_HEADS * GLA_DV
    blk = min(GLA_BLOCK, seq)
    nb = seq // blk
    for rows_ in (blk, lc):
        assert rows_ % (GLA_CHUNK * min(GLA_GROUP, rows_ // GLA_CHUNK)) == 0
    assert seq % blk == 0
    row = lambda b, d_, i: b * nb + jnp.where(d_ == 0, i, nb - 1 - i)
    lat = lambda w_, col: pl.BlockSpec((blk, w_), lambda b, d_, i: (row(b, d_, i), col(d_)))
    ctx = lambda w_, col: pl.BlockSpec((lc, w_), lambda b, d_, i: (ctx0 + b, col(d_)))
    zero = lambda d_: 0
    same = lambda d_: d_
    return pl.pallas_call(
        functools.partial(_gla_kernel, blk // GLA_CHUNK, lc // GLA_CHUNK),
        out_shape=[jax.ShapeDtypeStruct((2, n_batch * seq, hv), F32),
                   jax.ShapeDtypeStruct((2, n_batch * lc, hv), F32)],
        grid=(n_batch, 2, nb),
        in_specs=[lat(hk, zero), lat(hk, zero), lat(hv, zero), lat(hk, same),
                  ctx(hk, zero), ctx(hk, zero), ctx(hv, zero), ctx(hk, same),
                  pl.BlockSpec((1,) + mx.shape[1:], lambda b, d_, i: (d_, 0, 0)),
                  pl.BlockSpec((1,) + pat.shape[1:], lambda b, d_, i: (d_, 0, 0, 0))],
        out_specs=[pl.BlockSpec((1, blk, hv), lambda b, d_, i: (d_, row(b, d_, i), 0)),
                   pl.BlockSpec((1, lc, hv), lambda b, d_, i: (d_, b, 0))],
        scratch_shapes=[pltpu.VMEM((GLA_HEADS * GLA_DK, GLA_DV), F32)],
        compiler_params=_cparams(("arbitrary", "arbitrary", "arbitrary")),
        name="gla_scan",
    )(gq, gk, gv, g, gq, gk, gv, g, mx, pat)


def _stack_heads(q):
    lane_head = lax.broadcasted_iota(jnp.int32, q.shape, 1) // HEAD_DIM
    return jnp.concatenate([jnp.where(lane_head == h, q, jnp.zeros_like(q)) for h in range(4)], axis=0)


SAFE_SCORE_BOUND = 60.0
ONES_ROWS = 16


def _with_ones(vt):
    return jnp.concatenate([vt, jnp.ones((ONES_ROWS, vt.shape[1]), vt.dtype)], axis=0)


def _attn_store(acc, l, o_ref, u, tq):
    out = acc * (1.0 / l)
    out = jnp.concatenate([out[:, h * tq:(h + 1) * tq] for h in range(4)], axis=0)
    o_ref[u * tq:(u + 1) * tq, :] = _bf(jnp.transpose(out))


def _attn_dense_kernel(n_chunks, n_sub, *refs):
    if n_chunks:
        bound_ref, q_ref, kl_ref, vtl_ref, kc_ref, vtc_ref, o_ref = refs
    else:
        bound_ref, q_ref, kc_ref, vtc_ref, o_ref = refs
    tq = q_ref.shape[0] // n_sub
    cols = 4 * tq
    q4 = [_stack_heads(q_ref[u * tq:(u + 1) * tq, :]) for u in range(n_sub)]

    def scores(c, u):
        if c < n_chunks:
            return _dot_nt(kl_ref[c * ATT_K_CHUNK:(c + 1) * ATT_K_CHUNK, :], q4[u])
        return _dot_nt(kc_ref[...], q4[u])

    def values(c):
        if c < n_chunks:
            tile, off = divmod(c * ATT_K_CHUNK, TOK_TILE)
            return vtl_ref[tile][:, off:off + ATT_K_CHUNK]
        return vtc_ref[0]

    def update(carry, st, vt_aug):
        m, acc = carry
        m_new = jnp.maximum(m, jnp.max(st, axis=0, keepdims=True))
        acc = jnp.exp2(m - m_new) * acc + _dot(vt_aug, _bf(jnp.exp2(st - m_new)))
        return m_new, acc

    def run(fixed_ref):
        if fixed_ref is None:
            carry = [(jnp.full((1, cols), NEG_BIG, F32), jnp.zeros((HEAD_DIM + ONES_ROWS, cols), F32))
                     for _ in range(n_sub)]
        else:
            carry = [jnp.zeros((HEAD_DIM + ONES_ROWS, cols), F32) for _ in range(n_sub)]
        st = [scores(0, u) for u in range(n_sub)]
        for c in range(n_chunks + 1):
            st_next = [scores(c + 1, u) for u in range(n_sub)] if c < n_chunks else None
            vt_aug = _with_ones(values(c))
            if fixed_ref is None:
                carry = [update(carry[u], st[u], vt_aug) for u in range(n_sub)]
            else:
                carry = [carry[u] + _dot(vt_aug, _bf(jnp.exp2(st[u] - fixed_ref))) for u in range(n_sub)]
            st = st_next
        for u in range(n_sub):
            acc = carry[u][1] if fixed_ref is None else carry[u]
            _attn_store(acc[0:HEAD_DIM], acc[HEAD_DIM:HEAD_DIM + 1], o_ref, u, tq)

    bound = bound_ref[0]

    @pl.when(bound <= SAFE_SCORE_BOUND)
    def _():
        run(bound)

    @pl.when(bound > SAFE_SCORE_BOUND)
    def _():
        run(None)


def _attn_window_kernel(seq, bound_ref, q_ref, *refs):
    nk = WIN_SUB + 2
    k_refs, v_refs = refs[0:nk], refs[nk:2 * nk]
    kc_ref, vtc_ref, sink_ref, o_ref = refs[2 * nk:]
    tq = SWA_WINDOW
    i = pl.program_id(2)
    kb = jnp.concatenate([r[...] for r in k_refs], axis=0)
    vtb = _with_ones(jnp.concatenate([r[0] for r in v_refs], axis=1))
    kc, vtc, sink = kc_ref[...], _with_ones(vtc_ref[0]), sink_ref[0]
    span = 3 * tq
    r_i = lax.broadcasted_iota(jnp.int32, (tq, tq), 0)
    c_i = lax.broadcasted_iota(jnp.int32, (tq, tq), 1)
    band_lo = jnp.where(r_i >= c_i, 0.0, NEG_BIG)
    band_hi = jnp.where(r_i <= c_i, 0.0, NEG_BIG)
    n_blocks = seq // tq

    def run(use_bound):
        sb, sc = [], []
        for u in range(WIN_SUB):
            q4 = _stack_heads(q_ref[u * tq:(u + 1) * tq, :])
            first = i * WIN_SUB + u - 1
            bias_lo = jnp.where(first < 0, NEG_BIG, band_lo)
            bias_hi = jnp.where(first + 2 >= n_blocks, NEG_BIG, band_hi)
            s = _dot_nt(kb[u * tq:u * tq + span], q4)
            sb.append(jnp.concatenate([s[0:tq] + jnp.concatenate([bias_lo] * 4, axis=1), s[tq:2 * tq],
                                       s[2 * tq:span] + jnp.concatenate([bias_hi] * 4, axis=1)], axis=0))
            sc.append(_dot_nt(kc, q4))
        ms, pbs, pcs = [], [], []
        for u in range(WIN_SUB):
            if use_bound:
                m = jnp.maximum(bound_ref[0], sink)
            else:
                m = jnp.maximum(jnp.maximum(jnp.max(sb[u], axis=0, keepdims=True),
                                            jnp.max(sc[u], axis=0, keepdims=True)), sink)
            ms.append(m)
            pbs.append(_bf(jnp.exp2(sb[u] - m)))
            pcs.append(_bf(jnp.exp2(sc[u] - m)))
        for u in range(WIN_SUB):
            acc = _dot(vtb[:, u * tq:u * tq + span], pbs[u]) + _dot(vtc, pcs[u])
            l = acc[HEAD_DIM:HEAD_DIM + 1] + jnp.exp2(sink - ms[u])
            _attn_store(acc[0:HEAD_DIM], l, o_ref, u, tq)

    @pl.when(bound_ref[0] <= SAFE_SCORE_BOUND)
    def _():
        run(True)

    @pl.when(bound_ref[0] > SAFE_SCORE_BOUND)
    def _():
        run(False)


def _score_bound(q_gain, k_gain):
    return (HEAD_DIM * Q_SCALE * 1.02 * jnp.max(jnp.abs(q_gain)) * jnp.max(jnp.abs(k_gain))).reshape(1)


def _attention(mode, q, k_rep, vt, bound, n_batch, seq, lc, n_heads, n_kv, sink=None):
    ncol = n_heads * HEAD_DIM // ATT_COL
    col_per_kv = ncol // n_kv
    tq = ATT_Q_TILE
    tpb = seq // TOK_TILE
    n_lat_tiles = n_batch * tpb
    ctx_per_tile = TOK_TILE // lc
    ctx0 = n_batch * seq // lc
    assert TOK_TILE % ATT_K_CHUNK == 0 and TOK_TILE % lc == 0 and tq == SWA_WINDOW
    kv = lambda j: j // col_per_kv
    k_ctx = pl.BlockSpec((lc, ATT_COL), lambda b, j, i: (ctx0 + b, kv(j)))
    vt_ctx = pl.BlockSpec((1, HEAD_DIM, lc), lambda b, j, i: (n_lat_tiles + b // ctx_per_tile, kv(j), b % ctx_per_tile))
    if mode == "ctx":
        nq = lc // tq
        q0 = n_batch * seq // tq
        kern = functools.partial(_attn_dense_kernel, 0, 1)
        args = (q, k_rep, vt)
        in_specs = [pl.BlockSpec((tq, ATT_COL), lambda b, j, i: (q0 + b * nq + i, j)), k_ctx, vt_ctx]
    elif mode == "dense":
        tq = DENSE_SUB * ATT_Q_TILE
        nq = seq // tq
        kern = functools.partial(_attn_dense_kernel, seq // ATT_K_CHUNK, DENSE_SUB)
        args = (q, k_rep, vt, k_rep, vt)
        in_specs = [pl.BlockSpec((tq, ATT_COL), lambda b, j, i: (b * nq + i, j)),
                    pl.BlockSpec((seq, ATT_COL), lambda b, j, i: (b, kv(j))),
                    pl.BlockSpec((tpb, HEAD_DIM, TOK_TILE), lambda b, j, i: (b, kv(j), 0)), k_ctx, vt_ctx]
    else:
        wb = SWA_WINDOW
        tq = WIN_SUB * wb
        nq = seq // tq
        nkb = seq // wb
        per_tile = TOK_TILE // wb
        kern = functools.partial(_attn_window_kernel, seq)
        nb = lambda i, o: jnp.clip(i * WIN_SUB + o, 0, nkb - 1)
        k_nb = lambda o: pl.BlockSpec((wb, ATT_COL), lambda b, j, i: (b * nkb + nb(i, o), kv(j)))
        v_nb = lambda o: pl.BlockSpec(
            (1, HEAD_DIM, wb), lambda b, j, i: (b * tpb + nb(i, o) // per_tile, kv(j), nb(i, o) % per_tile))
        offs = range(-1, WIN_SUB + 1)
        sink_row = jnp.repeat(sink.reshape(ncol, 1, 4), wb, axis=2) * LOG2E
        args = (q,) + (k_rep,) * len(offs) + (vt,) * len(offs) + (k_rep, vt, sink_row)
        in_specs = ([pl.BlockSpec((tq, ATT_COL), lambda b, j, i: (b * nq + i, j))]
                    + [k_nb(o) for o in offs] + [v_nb(o) for o in offs]
                    + [k_ctx, vt_ctx, pl.BlockSpec((1, 1, 4 * wb), lambda b, j, i: (j, 0, 0))])
    args = (bound,) + args
    in_specs = [pl.BlockSpec(memory_space=pltpu.SMEM)] + in_specs
    return pl.pallas_call(
        kern,
        out_shape=jax.ShapeDtypeStruct((n_batch * nq * tq, n_heads * HEAD_DIM), BF16),
        grid=(n_batch, ncol, nq),
        in_specs=in_specs,
        out_specs=pl.BlockSpec((tq, ATT_COL), lambda b, j, i: (b * nq + i, j)),
        compiler_params=_cparams(("arbitrary", "arbitrary", "arbitrary")),
        name="attention_" + mode,
    )(*args)


ROUTE_ROWS = 8
TOKEN_TILE = (8, LANES)
OUT_TOKEN_TILE = (16, LANES)
OUT_TOKEN_DTYPE = BF16


def _to_token_tiles(x, tile):
    return x.reshape((x.shape[0],) + tile)


def _from_token_tiles(tiles):
    return tiles.reshape(tiles.shape[0], tiles.shape[1] * tiles.shape[2])


def _route(h, wrt_ref, wrt_hi_ref, brt_ref):
    hh, hl = _split2(h)
    a = _dot_nt(wrt_ref[...], hh)
    lt = a[0:LANES] + a[LANES:2 * LANES] + _dot_nt(wrt_hi_ref[...], hl) + brt_ref[...]
    col = lambda i: lt[i:i + 1, :]
    gl = [col(i) for i in range(MOE_GROUPS)]
    gmax = functools.reduce(jnp.maximum, gl)
    gi = jnp.where(gl[0] == gmax, 0, jnp.where(gl[1] == gmax, 1, jnp.where(gl[2] == gmax, 2, 3)))
    g_weight = 1.0 / functools.reduce(lambda a, b: a + b, [jnp.exp(x - gmax) for x in gl])
    el = []
    for j in range(MOE_EPG):
        cand = [col(MOE_GROUPS + g * MOE_EPG + j) for g in range(MOE_GROUPS)]
        el.append(jnp.where(gi == 0, cand[0], jnp.where(gi == 1, cand[1], jnp.where(gi == 2, cand[2], cand[3]))))
    m1 = functools.reduce(jnp.maximum, el)
    i1 = jnp.where(el[0] == m1, 0, jnp.where(el[1] == m1, 1, jnp.where(el[2] == m1, 2, 3)))
    rest = [jnp.where(i1 == j, -jnp.inf, el[j]) for j in range(MOE_EPG)]
    m2 = functools.reduce(jnp.maximum, rest)
    i2 = jnp.where(rest[0] == m2, 0, jnp.where(rest[1] == m2, 1, jnp.where(rest[2] == m2, 2, 3)))
    e2 = jnp.exp(m2 - m1)
    w1 = g_weight / (1.0 + e2)
    w2 = g_weight * e2 / (1.0 + e2)
    lo = jnp.minimum(i1, i2)
    hi = jnp.maximum(i1, i2)
    w_lo = jnp.where(i1 == lo, w1, w2)
    w_hi = jnp.where(i1 == lo, w2, w1)
    pair = jnp.where(lo == 0, hi - 1, jnp.where(lo == 1, hi + 1, N_PAIRS - 1))
    return w_lo, w_hi, gi * N_PAIRS + pair


def _out_tail(geom, m, x, mod_ref, gain_ffn_ref, wrt_ref, wrt_hi_ref, brt_ref, triu_ref,
              x_new_ref, hrow_ref, rt_ref, counts_ref, run_ref, tile=None):
    t = pl.program_id(0) if tile is None else tile
    row = _mod_row(t, *geom)
    tt, d = x.shape

    @pl.when(t == 0)
    def _():
        run_ref[...] = jnp.zeros_like(run_ref)

    x_new = x + mod_ref[pl.ds(row, 1), 2 * d:3 * d] * m
    x_new_ref[...] = x_new
    h = _modulated(x_new, gain_ffn_ref[...], mod_ref, row, 3, 4)
    w_lo, w_hi, bucket = _route(h, wrt_ref, wrt_hi_ref, brt_ref)
    onehot = lax.broadcasted_iota(jnp.int32, (LANES, tt), 0) == bucket
    ones = jnp.where(onehot, 1.0, 0.0)
    before = _dot(_bf(ones), triu_ref[...]) + run_ref[...]
    rank = jnp.sum(jnp.where(onehot, before, 0.0), axis=0, keepdims=True)
    run = run_ref[...] + jnp.sum(ones, axis=1, keepdims=True)
    run_ref[...] = run
    counts_ref[...] = jnp.broadcast_to(run, counts_ref.shape)
    rec = jnp.concatenate([w_lo, w_hi, bucket.astype(F32), rank, jnp.zeros((ROUTE_ROWS - 4, tt), F32)], axis=0)
    rt_ref[0] = rec
    hrow_ref[...] = _to_token_tiles(h, TOKEN_TILE)


def _pick(t, n_lat_tiles, lat_ref, ctx_ref):
    return jnp.where(t < n_lat_tiles, lat_ref[...], ctx_ref[...])


def _out_even_kernel(geom, ol_ref, oc_ref, r_ref, attl_ref, attc_ref, gn_ref, w_ref, xl_ref, xc_ref, mod_ref,
                     gain_ffn_ref, wrt_ref, wrt_hi_ref, brt_ref, triu_ref,
                     x_new_ref, hrow_ref, rt_ref, counts_ref, run_ref):
    t = pl.program_id(0)
    o2 = _pick(t, geom[0], ol_ref, oc_ref)
    o = o2[0] + o2[1]
    r = r_ref[...]
    parts = []
    for h in range(GLA_HEADS):
        sl = slice(h * GLA_DV, (h + 1) * GLA_DV)
        parts.append(_rms(o[:, sl]) * gn_ref[...] * _silu(r[:, sl]))
    a = _bf(jnp.concatenate(parts, axis=-1))
    half = a.shape[-1]
    m = _dot(a, w_ref[0:half, :]) + _dot(_pick(t, geom[0], attl_ref, attc_ref), w_ref[half:, :])
    _out_tail(geom, m, _pick(t, geom[0], xl_ref, xc_ref), mod_ref, gain_ffn_ref, wrt_ref, wrt_hi_ref, brt_ref,
              triu_ref, x_new_ref, hrow_ref, rt_ref, counts_ref, run_ref)


def _out_odd_kernel(geom, att_ref, w_ref, x_ref, mod_ref, gain_ffn_ref, wrt_ref, wrt_hi_ref, brt_ref, triu_ref,
                    x_new_ref, hrow_ref, rt_ref, counts_ref, run_ref):
    tt = TOK_TILE
    halves = range(x_ref.shape[0] // tt)
    ms = [_dot(att_ref[j * tt:(j + 1) * tt, :], w_ref[...]) for j in halves]
    for j in halves:
        rows = pl.ds(j * tt, tt)
        _out_tail(geom, ms[j], x_ref[rows, :], mod_ref, gain_ffn_ref, wrt_ref, wrt_hi_ref, brt_ref, triu_ref,
                  x_new_ref.at[rows], hrow_ref.at[rows], rt_ref.at[pl.ds(j, 1)], counts_ref, run_ref,
                  tile=len(halves) * pl.program_id(0) + j)


def _out_proj(kernel, lead_args, lead_specs, d, mod, gain_ffn, router, n_rows, name, tiles_per_step=1):
    tt = TOK_TILE
    tps = tiles_per_step
    tok = lambda w_: pl.BlockSpec((tps * tt, w_), lambda t: (t, 0))
    r = np.arange(tt)
    triu = jnp.asarray((r[:, None] < r[None, :]).astype(np.float32), BF16)
    wrt, wrt_hi, br = router
    brt = jnp.broadcast_to(br.reshape(LANES, 1), (LANES, tt))
    return pl.pallas_call(
        kernel,
        out_shape=[jax.ShapeDtypeStruct((n_rows, d), F32),
                   jax.ShapeDtypeStruct((n_rows,) + TOKEN_TILE, F32),
                   jax.ShapeDtypeStruct((n_rows // tt, ROUTE_ROWS, tt), F32),
                   jax.ShapeDtypeStruct((LANES, LANES), F32)],
        grid=(n_rows // (tps * tt),),
        in_specs=lead_specs + [_full(mod.shape), _full(gain_ffn.shape), _full(wrt.shape), _full(wrt_hi.shape),
                               _full(brt.shape), _full(triu.shape)],
        out_specs=[tok(d), pl.BlockSpec((tps * tt,) + TOKEN_TILE, lambda t: (t, 0, 0)),
                   pl.BlockSpec((tps, ROUTE_ROWS, tt), lambda t: (t, 0, 0)), _full((LANES, LANES))],
        scratch_shapes=[pltpu.VMEM((LANES, 1), F32)],
        compiler_params=_cparams(("arbitrary",)),
        name=name,
    )(*lead_args, mod, gain_ffn, wrt, wrt_hi, brt, triu)


SUBLANES = 8


def _for_each_row(n_rows, start_row_copy):
    def body(g, c):
        base = g * SUBLANES
        for j in range(SUBLANES):
            start_row_copy(base + j, j % 2)
        return c
    lax.fori_loop(0, n_rows // SUBLANES, body, 0)


def _pos_kernel(start_ref, rt_ref, pos_ref):
    bucket = rt_ref[:, 2, :].astype(jnp.int32)
    base = jnp.zeros_like(bucket)
    for b in range(N_BUCKETS):
        base = jnp.where(bucket == b, start_ref[b], base)
    pos_ref[:, 0, :] = base + rt_ref[:, 3, :].astype(jnp.int32)


def _dispatch_kernel(last_ref, pos_ref, x_ref, xs_ref, zbuf, sem, zsem):
    tt = x_ref.shape[0]
    tm = zbuf.shape[0]

    @pl.when(pl.program_id(0) == 0)
    def _():
        zbuf[...] = jnp.zeros_like(zbuf)

        def zero_copy(b):
            return pltpu.make_async_copy(zbuf, xs_ref.at[pl.ds(jnp.maximum(last_ref[b], 0) * tm, tm)], zsem)

        for b in range(2 * N_BUCKETS):
            @pl.when(last_ref[b] >= 0)
            def _():
                zero_copy(b).start()
        for b in range(2 * N_BUCKETS):
            @pl.when(last_ref[b] >= 0)
            def _():
                zero_copy(b).wait()

    _for_each_row(tt, lambda r, prio: pltpu.make_async_copy(
        x_ref.at[r], xs_ref.at[pos_ref[0, 0, r]], sem).start(priority=prio))
    pltpu.make_async_copy(x_ref, xs_ref.at[pl.ds(0, tt)], sem).wait()


def _moe_mlp_kernel(tlo_ref, thi_ref, nused_ref, xa_ref, xb_ref, *refs):
    del tlo_ref, thi_ref
    x_refs, w_sets, f_ref = (xa_ref, xb_ref), (refs[0:6], refs[6:12]), refs[12]
    tm = xa_ref.shape[0]
    first = MOE_PAIR * pl.program_id(0)
    n_used = nused_ref[0]

    def experts(tiles):
        xs = [_bf(_from_token_tiles(x_refs[j][...])) for j in tiles]
        ups = [[_dot(x, w_sets[j][k][0]) for k in (0, 1, 3, 4)] for x, j in zip(xs, tiles)]
        hids = [(_bf(_silu(g_lo) * u_lo), _bf(_silu(g_hi) * u_hi)) for g_lo, u_lo, g_hi, u_hi in ups]
        for (h_lo, h_hi), j in zip(hids, tiles):
            f2 = jnp.concatenate([_dot(h_lo, w_sets[j][2][0]), _dot(h_hi, w_sets[j][5][0])], axis=1)
            f_ref[j * tm:(j + 1) * tm] = _to_token_tiles(f2.astype(OUT_TOKEN_DTYPE), OUT_TOKEN_TILE)

    @pl.when(first + 1 < n_used)
    def _():
        experts((0, 1))

    @pl.when(first + 1 == n_used)
    def _():
        experts((0,))
        f_ref[tm:2 * tm] = jnp.zeros((tm,) + OUT_TOKEN_TILE, OUT_TOKEN_DTYPE)

    @pl.when(first >= n_used)
    def _():
        f_ref[...] = jnp.zeros_like(f_ref)


def _moe(hrow, rt, counts, experts, layer):
    n = hrow.shape[0]
    d = TOKEN_TILE[0] * TOKEN_TILE[1]
    tm, tt = MOE_TILE, TOK_TILE
    n_tiles = n // tm + N_BUCKETS
    p = n_tiles * tm
    cnt = counts[0:N_BUCKETS, 0].astype(jnp.int32)
    tiles_b = (cnt + tm - 1) // tm
    tile_end = jnp.cumsum(tiles_b)
    start_b = (tile_end - tiles_b) * tm
    n_used = tile_end[-1].reshape(1)
    spare = n_used[0] + jnp.arange(N_BUCKETS, dtype=jnp.int32)
    last_tile = jnp.concatenate([jnp.where(tiles_b > 0, tile_end - 1, -1), jnp.where(spare < n_tiles, spare, -1)])
    pos = pl.pallas_call(
        _pos_kernel,
        out_shape=jax.ShapeDtypeStruct((n // tt, 1, tt), jnp.int32),
        grid_spec=pltpu.PrefetchScalarGridSpec(
            num_scalar_prefetch=1, grid=(1,),
            in_specs=[pl.BlockSpec(rt.shape, lambda i, s: (0, 0, 0))],
            out_specs=pl.BlockSpec((n // tt, 1, tt), lambda i, s: (0, 0, 0))),
        compiler_params=_cparams(("arbitrary",)),
        name="moe_positions",
    )(start_b, rt)
    tile_ids = jnp.arange(n_tiles, dtype=jnp.int32)
    tile_bucket = jnp.sum((tile_ids[:, None] >= tile_end[None, :]).astype(jnp.int32), axis=1)
    tile_bucket = jnp.minimum(tile_bucket, jnp.sum((n_used[0] - 1 >= tile_end).astype(jnp.int32)))
    tile_bucket = jnp.minimum(tile_bucket, N_BUCKETS - 1)
    pair_lo = jnp.asarray([0, 0, 0, 1, 1, 2], jnp.int32)
    pair_hi = jnp.asarray([1, 2, 3, 2, 3, 3], jnp.int32)
    grp = tile_bucket // N_PAIRS
    w_gate, w_up, w_down = experts
    first = layer * MOE_GROUPS * MOE_EPG
    t_lo = first + grp * MOE_EPG + pair_lo[tile_bucket % N_PAIRS]
    t_hi = first + grp * MOE_EPG + pair_hi[tile_bucket % N_PAIRS]

    dt = DISPATCH_TILE if n % DISPATCH_TILE == 0 else tt
    xs = pl.pallas_call(
        _dispatch_kernel,
        out_shape=jax.ShapeDtypeStruct((p,) + TOKEN_TILE, F32),
        grid_spec=pltpu.PrefetchScalarGridSpec(
            num_scalar_prefetch=1, grid=(n // dt,),
            in_specs=[pl.BlockSpec((1, 1, dt), lambda t, s: (t, 0, 0), memory_space=pltpu.SMEM),
                      pl.BlockSpec((dt,) + TOKEN_TILE, lambda t, s: (t, 0, 0))],
            out_specs=pl.BlockSpec(memory_space=pl.ANY),
            scratch_shapes=[pltpu.VMEM((tm,) + TOKEN_TILE, F32), pltpu.SemaphoreType.DMA(()),
                            pltpu.SemaphoreType.DMA(())]),
        compiler_params=_cparams(("arbitrary",)),
        name="moe_dispatch",
    )(last_tile, pos.reshape(n // dt, 1, dt), hrow)

    f = D_EXPERT
    assert n_tiles % MOE_PAIR == 0

    def tile_specs(j):
        tile = lambda t: MOE_PAIR * t + j
        up_lo = pl.BlockSpec((1, d, f), lambda t, lo, hi, nu: (lo[tile(t)], 0, 0))
        up_hi = pl.BlockSpec((1, d, f), lambda t, lo, hi, nu: (hi[tile(t)], 0, 0))
        dn_lo = pl.BlockSpec((1, f, d), lambda t, lo, hi, nu: (lo[tile(t)], 0, 0))
        dn_hi = pl.BlockSpec((1, f, d), lambda t, lo, hi, nu: (hi[tile(t)], 0, 0))
        rows = pl.BlockSpec((tm,) + TOKEN_TILE, lambda t, lo, hi, nu: (jnp.minimum(tile(t), nu[0] - 1), 0, 0))
        return rows, [up_lo, up_lo, dn_lo, up_hi, up_hi, dn_hi]

    (rows_a, w_a), (rows_b, w_b) = tile_specs(0), tile_specs(1)
    grid_spec = pltpu.PrefetchScalarGridSpec(
        num_scalar_prefetch=3,
        grid=(n_tiles // MOE_PAIR,),
        in_specs=[rows_a, rows_b] + w_a + w_b,
        out_specs=pl.BlockSpec((MOE_PAIR * tm,) + OUT_TOKEN_TILE, lambda t, *_: (t, 0, 0)),
    )
    weights = (w_gate, w_up, w_down, w_gate, w_up, w_down)
    f_sorted = pl.pallas_call(
        _moe_mlp_kernel,
        out_shape=jax.ShapeDtypeStruct((p,) + OUT_TOKEN_TILE, OUT_TOKEN_DTYPE),
        grid_spec=grid_spec,
        compiler_params=_cparams(("arbitrary",)),
        name="moe_experts",
    )(t_lo, t_hi, n_used, xs, xs, *weights, *weights)
    return f_sorted, pos


def _gather_tile(t, n_t, pos_ref, pos_next_ref, rt_ref, src_hbm, buf, sem, inline_prefetch=False):
    tt = buf.shape[1]

    def start(p_ref, slot):
        _for_each_row(tt, lambda r, prio: pltpu.make_async_copy(
            src_hbm.at[p_ref[0, 0, r]], buf.at[slot].at[r], sem.at[slot]).start(priority=prio))

    slot = t % 2

    def wait(s):
        pltpu.make_async_copy(src_hbm.at[pl.ds(0, tt)], buf.at[s], sem.at[s]).wait()

    @pl.when(t == 0)
    def _():
        start(pos_ref, 0)

    wait(slot)
    if inline_prefetch:
        for r in range(tt):
            pltpu.make_async_copy(src_hbm.at[pos_next_ref[0, 0, r]], buf.at[1 - slot].at[r],
                                  sem.at[1 - slot]).start(priority=r % 2)
    else:
        @pl.when(t + 1 < n_t)
        def _():
            start(pos_next_ref, 1 - slot)

    f2 = _from_token_tiles(buf[slot]).astype(F32)
    d = f2.shape[1] // 2
    w = jnp.transpose(rt_ref[0])
    return w[:, 0:1] * f2[:, 0:d] + w[:, 1:2] * f2[:, d:2 * d]


def _gather_drain(t, n_t, src_hbm, buf, sem):
    tt = buf.shape[1]

    @pl.when(t == n_t - 1)
    def _():
        pltpu.make_async_copy(src_hbm.at[pl.ds(0, tt)], buf.at[1 - t % 2], sem.at[1 - t % 2]).wait()


def _gather_specs(n_t):
    tt = TOK_TILE
    return [pl.BlockSpec((1, 1, tt), lambda t: (t, 0, 0), memory_space=pltpu.SMEM),
            pl.BlockSpec((1, 1, tt), lambda t: (jnp.minimum(t + 1, n_t - 1), 0, 0), memory_space=pltpu.SMEM),
            pl.BlockSpec((1, ROUTE_ROWS, tt), lambda t: (t, 0, 0)),
            pl.BlockSpec(memory_space=pl.ANY)]


def _gather_scratch(d):
    assert 2 * d == OUT_TOKEN_TILE[0] * OUT_TOKEN_TILE[1]
    return [pltpu.VMEM((2, TOK_TILE) + OUT_TOKEN_TILE, OUT_TOKEN_DTYPE), pltpu.SemaphoreType.DMA((2,))]


def _final_kernel(tiles_per_batch, x_ref, pos_ref, pos_next_ref, rt_ref, fs_hbm, mod_ref, o_ref, fbuf, fsem):
    t = pl.program_id(0)
    row = t // tiles_per_batch
    d = x_ref.shape[-1]
    f = _gather_tile(t, pl.num_programs(0), pos_ref, pos_next_ref, rt_ref, fs_hbm, fbuf, fsem)
    o_ref[...] = x_ref[...] + mod_ref[pl.ds(row, 1), 5 * d:6 * d] * f


def _final(x_lat, f_sorted, pos, rt, mod, tiles_per_batch):
    n, d = x_lat.shape
    tok = pl.BlockSpec((TOK_TILE, d), lambda t: (t, 0))
    return pl.pallas_call(
        functools.partial(_final_kernel, tiles_per_batch),
        out_shape=jax.ShapeDtypeStruct((n, d), F32),
        grid=(n // TOK_TILE,),
        in_specs=[tok] + _gather_specs(n // TOK_TILE) + [_full(mod.shape)],
        out_specs=tok,
        scratch_shapes=_gather_scratch(d),
        compiler_params=_cparams(("arbitrary",)),
        name="final_residual",
    )(x_lat, pos, pos, rt, f_sorted, mod)


def _block_diag_ones():
    r = np.arange(MXU_DIM) // HEAD_DIM
    return jnp.asarray((r[:, None] == r[None, :]).astype(np.float32), BF16)


def _router_weights(wg, bg, we, be):
    d = wg.shape[0]
    n = MOE_GROUPS + MOE_GROUPS * MOE_EPG
    wt = jnp.concatenate([wg, we, jnp.zeros((d, LANES - n), F32)], axis=1).T
    b = jnp.concatenate([bg, be, jnp.zeros((LANES - n,), F32)])
    hi = _bf(wt)
    lo = _bf(wt - hi.astype(F32))
    return jnp.concatenate([hi, lo], axis=0), hi, b


def _expert_weights(w_gate, w_up, w_down):
    l, g, e, d, f = w_gate.shape
    n = l * g * e
    return _bf(w_gate).reshape(n, d, f), _bf(w_up).reshape(n, d, f), _bf(w_down).reshape(n, f, d)


def kernel(x, c, ctx, c_ctx, mod_w, mod_b, norm_mix, norm_ffn, ev_w_in, ev_w_out, gla_gate_w, gla_gate_b,
           gla_out_norm, att_q_norm, att_k_norm, od_w_in, od_w_out, swa_sink, swa_q_norm, swa_k_norm,
           router_group_w, router_group_b, router_expert_w, router_expert_b, exp_w_gate, exp_w_up, exp_w_down):
    n_batch, seq, d = x.shape
    lc = ctx.shape[1]
    depth = mod_w.shape[0]
    n_lat = n_batch * seq
    tiles_per_batch = seq // TOK_TILE
    geom = (n_lat // TOK_TILE, tiles_per_batch, n_batch)
    assert depth == 2 and seq % TOK_TILE == 0 and (n_batch * lc) % TOK_TILE == 0 and n_batch < 16

    x_lat, x_ctx = x.reshape(n_lat, d), ctx.reshape(n_batch * lc, d)
    n_all = n_lat + n_batch * lc
    c_rows = jnp.zeros((16, d), F32).at[:n_batch].set(c).at[n_batch].set(c_ctx)
    mod = _modulation(c_rows, mod_w, mod_b)
    tables = _rope_tables(seq)
    bd = _block_diag_ones()
    row2 = lambda v: v.reshape(1, -1)
    tile_gain = lambda gvec, reps: jnp.tile(gvec, reps).reshape(1, -1)

    w0 = ev_w_in[0]
    seg = np.cumsum([0, 256, 256, 512, 512, 32, 512, 128, 128])
    cols = lambda i: w0[:, seg[i]:seg[i + 1]]
    w_even = _bf(jnp.concatenate([cols(0), cols(1), cols(2), cols(3), cols(5), cols(6), cols(4),
                                  jnp.zeros((d, EV_END - EV_LR - 2 * GLA_GATE_RANK), F32)], axis=1))
    wvt_even = _bf(cols(7).T)
    hk = GLA_HEADS * GLA_DK
    gw = jnp.zeros((LANES, 2 * hk), F32)
    gw = gw.at[0:GLA_GATE_RANK, 0:hk].set(gla_gate_w[0, 0])
    gw = gw.at[GLA_GATE_RANK:2 * GLA_GATE_RANK, hk:2 * hk].set(gla_gate_w[0, 1])
    gb = gla_gate_b[0].reshape(1, 2 * hk)
    gq, gk, gv, gr, g, aq, ak, avt = _proj_even(
        x_lat, x_ctx, mod[0], row2(norm_mix[0]), w_even, wvt_even, _bf(gw), gb,
        tile_gain(att_q_norm[0], ATT_HEADS), tile_gain(att_k_norm[0], ATT_KV_HEADS), tables, bd, geom)
    o_lat, o_ctx = _gla(gq, gk, gv, g, n_batch, seq, lc)
    bound0 = _score_bound(att_q_norm[0], att_k_norm[0])
    att_lat = _attention("dense", aq, ak, avt, bound0, n_batch, seq, lc, ATT_HEADS, ATT_KV_HEADS)
    att_ctx = _attention("ctx", aq, ak, avt, bound0, n_batch, seq, lc, ATT_HEADS, ATT_KV_HEADS)
    router = _router_weights(router_group_w[0], router_group_b[0], router_expert_w[0], router_expert_b[0])
    tt = TOK_TILE
    nlt = geom[0]
    gn = row2(gla_out_norm[0])
    w_out0 = _bf(ev_w_out[0])
    hv, ha = GLA_HEADS * GLA_DV, ATT_HEADS * HEAD_DIM
    x_mid, hrow, rt, counts = _out_proj(
        functools.partial(_out_even_kernel, geom),
        (o_lat, o_ctx, gr, att_lat, att_ctx, gn, w_out0, x_lat, x_ctx),
        _lat_ctx_specs((tt, hv), nlt, lead=(2,)) + [pl.BlockSpec((tt, hv), lambda t: (t, 0))]
        + _lat_ctx_specs((tt, ha), nlt) + [_full(gn.shape), _full(w_out0.shape)] + _lat_ctx_specs((tt, d), nlt),
        d, mod[0], row2(norm_ffn[0]), router, n_all, "out_even")
    experts = _expert_weights(exp_w_gate, exp_w_up, exp_w_down)
    f0, pos0 = _moe(hrow, rt, counts, experts, 0)

    w_odd = od_w_in[0]
    x1, q1, k1, v1t = _proj_odd(
        x_mid, f0, pos0, rt, mod[0], mod[1], row2(norm_mix[1]), _bf(w_odd[:, 0:d + LANES]), _bf(w_odd[:, d + LANES:].T),
        tile_gain(swa_q_norm[0], SWA_HEADS), tile_gain(swa_k_norm[0], SWA_KV_HEADS), tables, bd, geom)
    att1 = _attention("window", q1, k1, v1t, _score_bound(swa_q_norm[0], swa_k_norm[0]), n_batch, seq, lc,
                      SWA_HEADS, SWA_KV_HEADS, sink=swa_sink[0])
    router = _router_weights(router_group_w[1], router_group_b[1], router_expert_w[1], router_expert_b[1])
    w_out1 = _bf(od_w_out[0])
    tok = lambda w_: pl.BlockSpec((tt, w_), lambda t: (t, 0))
    tok2 = lambda w_: pl.BlockSpec((OUT_ODD_TILES * tt, w_), lambda t: (t, 0))
    x2, hrow1, rt1, counts1 = _out_proj(
        functools.partial(_out_odd_kernel, geom), (att1, w_out1, x1),
        [tok2(SWA_HEADS * HEAD_DIM), _full(w_out1.shape), tok2(d)],
        d, mod[1], row2(norm_ffn[1]), router, n_lat, "out_odd", tiles_per_step=OUT_ODD_TILES)
    f1, pos1 = _moe(hrow1, rt1, counts1, experts, 1)
    out = _final(x2, f1, pos1, rt1, mod[1], tiles_per_batch)
    return out.reshape(n_batch, seq, d)
```
